```python
import jax, jax.numpy as jnp
from jax import lax
import numpy as np

D_MODEL = 1024
BATCH = 8
SEQ = 2048
DEPTH = 2
DEC_BATCH = 32
DEC_SEQ = 1
PAST_LEN = 8192
PAGE_SIZE = 128

N_EVEN = (DEPTH + 1) // 2
N_ODD = DEPTH // 2
D_A = D_MODEL // 2
CONV_W = 31
H_B = 8
HD_B = 64
D_B = H_B * HD_B
KV_H = 2
GRP = H_B // KV_H
L_CMP = 32
L_SEL = 64
K_SEL = 16
WINDOW = 512
Q_BLK_SEL = 64
Q_BLK_WIN = 128
FORCE = 1e4
HD_C = 64
H_C = D_MODEL // HD_C
D_C = H_C * HD_C
LORA_W = 64
LORA_A = 64
GN_EPS = 64e-5
ALPHA = (2 * DEPTH) ** 0.25
BETA = (8 * DEPTH) ** -0.25
LN_EPS = 1e-5
NEG = -1e30
D_IN_EVEN = 3 * D_A + D_B + 6 * KV_H * HD_B + 3 * H_B + D_B

kernel_name = 'hybrid_conformer_nsa_rwkv7_deepnorm_step'


def _layer_norm(x, g, b, eps=LN_EPS):
    xf = x.astype(jnp.float32)
    mu = xf.mean(-1, keepdims=True)
    var = jnp.square(xf - mu).mean(-1, keepdims=True)
    return ((xf - mu) * lax.rsqrt(var + eps) * g + b).astype(x.dtype)


def _masked_softmax(s, mask):
    p = jax.nn.softmax(jnp.where(mask, s, NEG), axis=-1)
    return jnp.where(mask, p, 0.0)


def _alibi_slopes():
    return jnp.exp2(-8.0 * (jnp.arange(H_B, dtype=jnp.float32) + 1.0) / H_B).reshape(KV_H, GRP)


def nsa_cmp_sel(qf, q_pos, kv_cmp, kv_sel, wk_cmp, wv_cmp, slopes):
    f32 = jnp.float32
    B, Tq = qf.shape[:2]
    Lk = kv_cmp.shape[1]
    n_cmp = Lk // L_CMP
    blk = kv_cmp[:, :n_cmp * L_CMP].reshape(B, n_cmp, L_CMP, 2, KV_H, HD_B).astype(f32)
    kc = jnp.einsum('bnlgd,lg->bngd', blk[:, :, :, 0], wk_cmp.astype(f32))
    vc = jnp.einsum('bnlgd,lg->bngd', blk[:, :, :, 1], wv_cmp.astype(f32))
    c_end = jnp.arange(n_cmp) * L_CMP + (L_CMP - 1)
    dist_c = q_pos[:, None] - c_end[None, :]
    s_c = (jnp.einsum('btgrd,bngd->btgrn', qf, kc)
           - slopes[None, None, :, :, None] * dist_c[None, :, None, None, :].astype(f32))
    p_c = _masked_softmax(s_c, (dist_c >= 0)[None, :, None, None, :])
    o_c = jnp.einsum('btgrn,bngd->btgrd', p_c, vc)
    n_sel = -(-Lk // L_SEL)
    ratio = L_SEL // L_CMP
    imp = jnp.pad(p_c.sum(axis=3), ((0, 0), (0, 0), (0, 0), (0, n_sel * ratio - n_cmp)))
    imp = imp.reshape(B, Tq, KV_H, n_sel, ratio).sum(-1)
    sb = jnp.arange(n_sel)[None, :]
    tb = (q_pos // L_SEL)[:, None]
    forced = (sb == 0) | (sb == tb) | (sb == tb - 1)
    imp = jnp.where(forced[None, :, None, :], FORCE, imp)
    imp = jnp.where((sb <= tb)[None, :, None, :], imp, -jnp.inf)
    k_top = min(K_SEL, n_sel)
    _, idx = lax.top_k(imp, k_top)
    pad = n_sel * L_SEL - Lk
    ks = jnp.pad(kv_sel, ((0, 0), (0, pad), (0, 0), (0, 0), (0, 0)))
    ks = ks.reshape(B, n_sel, L_SEL, 2, KV_H, HD_B).transpose(0, 4, 1, 3, 2, 5)
    bi = jnp.arange(B)[:, None, None, None]
    gi = jnp.arange(KV_H)[None, :, None, None]
    lpos = jnp.arange(L_SEL)

    def sel_block(args):
        qb, pb, ib = args
        it = ib.transpose(0, 2, 1, 3)
        kvg = ks[bi, gi, it].astype(f32)
        kpos = it[..., None] * L_SEL + lpos
        dist = pb[None, None, :, None, None] - kpos
        s = (jnp.einsum('bqgrd,bgqkld->bgqrkl', qb, kvg[:, :, :, :, 0])
             - slopes[None, :, None, :, None, None] * dist[:, :, :, None].astype(f32))
        sh = s.shape
        p = _masked_softmax(s.reshape(sh[:4] + (-1,)), (dist >= 0)[:, :, :, None].reshape(sh[:3] + (1, -1)))
        return jnp.einsum('bgqrkl,bgqkld->bqgrd', p.reshape(sh), kvg[:, :, :, :, 1])

    qbs = Q_BLK_SEL if Tq % Q_BLK_SEL == 0 else Tq
    nb = Tq // qbs
    o_s = lax.map(sel_block, (qf.reshape(B, nb, qbs, KV_H, GRP, HD_B).swapaxes(0, 1),
                              q_pos.reshape(nb, qbs),
                              idx.reshape(B, nb, qbs, KV_H, k_top).swapaxes(0, 1)))
    o_s = o_s.swapaxes(0, 1).reshape(B, Tq, KV_H, GRP, HD_B)
    return o_c, o_s


def window_banded(qf, kv_w, slopes):
    f32 = jnp.float32
    B, T = qf.shape[:2]
    qbs = min(Q_BLK_WIN, T)
    nb = T // qbs
    kvp = jnp.pad(kv_w, ((0, 0), (WINDOW, 0), (0, 0), (0, 0), (0, 0)))
    kidx = jnp.arange(nb)[:, None] * qbs + jnp.arange(qbs + WINDOW)[None, :]
    kvb = kvp[:, kidx].astype(f32)
    kpos = kidx - WINDOW
    qpos = jnp.arange(T).reshape(nb, qbs)
    dist = qpos[:, :, None] - kpos[:, None, :]
    mask = (dist >= 0) & (dist < WINDOW) & (kpos[:, None, :] >= 0)
    qb = qf.reshape(B, nb, qbs, KV_H, GRP, HD_B)
    s = (jnp.einsum('bnqgrd,bnkgd->bngrqk', qb, kvb[:, :, :, 0])
         - slopes[None, None, :, :, None, None] * dist[None, :, None, None].astype(f32))
    p = _masked_softmax(s, mask[None, :, None, None])
    o = jnp.einsum('bngrqk,bnkgd->bnqgrd', p, kvb[:, :, :, 1])
    return o.reshape(B, T, KV_H, GRP, HD_B)


def window_dense(qf, q_pos, kv_ctx, k_pos0, slopes):
    f32 = jnp.float32
    kvf = kv_ctx.astype(f32)
    kpos = k_pos0 + jnp.arange(kv_ctx.shape[1])
    dist = q_pos[:, None] - kpos[None, :]
    mask = (dist >= 0) & (dist < WINDOW)
    s = (jnp.einsum('btgrd,bkgd->bgrtk', qf, kvf[:, :, 0])
         - slopes[None, :, :, None, None] * dist[None, None, None].astype(f32))
    p = _masked_softmax(s, mask[None, None, None])
    return jnp.einsum('bgrtk,bkgd->btgrd', p, kvf[:, :, 1])


def even_mixer(x, q_pos, conv_prev, past_cmp, past_sel, win_prev, w_in, conv_w, conv_b, cln_g, cln_b,
               wk_cmp, wv_cmp, w_out, slopes, banded):
    f32 = jnp.float32
    B, T, _ = x.shape
    h = x @ w_in
    cuts = [int(c) for c in np.cumsum([D_A, D_A, D_A, D_B, 6 * KV_H * HD_B, 3 * H_B])]
    a_val, a_glu, z_a, q, kv6, g3, z_b = jnp.split(h, cuts, axis=-1)
    u = a_val * jax.nn.sigmoid(a_glu)
    ext = jnp.concatenate([conv_prev.astype(u.dtype), u], axis=1)
    c = lax.conv_general_dilated(ext, conv_w.astype(u.dtype), (1,), 'VALID',
                                 dimension_numbers=('NWC', 'WIO', 'NWC'), feature_group_count=D_A) + conv_b
    ya = jax.nn.silu(_layer_norm(c, cln_g, cln_b)) * jax.nn.silu(z_a)
    conv_new = ext[:, ext.shape[1] - (CONV_W - 1):]
    qf = q.reshape(B, T, KV_H, GRP, HD_B).astype(f32) * (HD_B ** -0.5)
    kv6 = kv6.reshape(B, T, 3, 2, KV_H, HD_B)
    cmp_new, sel_new, win_new = kv6[:, :, 0], kv6[:, :, 1], kv6[:, :, 2]
    kv_cmp = jnp.concatenate([past_cmp.astype(x.dtype), cmp_new], axis=1)
    kv_sel = jnp.concatenate([past_sel.astype(x.dtype), sel_new], axis=1)
    o_c, o_s = nsa_cmp_sel(qf, q_pos, kv_cmp, kv_sel, wk_cmp, wv_cmp, slopes)
    if banded:
        o_w = window_banded(qf, win_new, slopes)
        win_state = win_new[:, T - min(WINDOW, T):]
    else:
        ctx = jnp.concatenate([win_prev.astype(x.dtype), win_new], axis=1)
        o_w = window_dense(qf, q_pos, ctx, q_pos[0] - win_prev.shape[1], slopes)
        win_state = ctx[:, ctx.shape[1] - min(WINDOW, ctx.shape[1]):]
    g = jax.nn.sigmoid(g3.astype(f32)).reshape(B, T, KV_H, GRP, 3)
    o = g[..., 0:1] * o_c + g[..., 1:2] * o_s + g[..., 2:3] * o_w
    yb = o.reshape(B, T, D_B).astype(x.dtype) * jax.nn.silu(z_b)
    y = jnp.concatenate([ya, yb], axis=-1) @ w_out
    return y, cmp_new, sel_new, win_state, conv_new


def rwkv_mixer(x, shift_prev, S0, mu, w_rkvz, w0, w1, w2, a0, a1, a2, k_k, k_a, r_k, gn_g, gn_b, w_out):
    f32 = jnp.float32
    B, T, _ = x.shape
    x_prev = jnp.concatenate([shift_prev[:, None].astype(x.dtype), x[:, :-1]], axis=1)
    xsh = x[None] + (x_prev - x)[None] * mu[:, None, None, :]
    r, k, v, z = jnp.einsum('nbtd,nde->nbte', xsh[:4], w_rkvz)
    r, k, v = r.astype(f32), k.astype(f32), v.astype(f32)
    w_raw = -jax.nn.softplus(-(w0 + jnp.tanh(xsh[4] @ w1) @ w2).astype(f32)) - 0.5
    decay = jnp.exp(-jnp.exp(w_raw))
    a = jax.nn.sigmoid((a0 + (xsh[5] @ a1) @ a2).astype(f32))
    kk = (k * k_k).reshape(B, T, H_C, HD_C)
    kk = kk / jnp.maximum(jnp.sqrt(jnp.sum(kk * kk, axis=-1, keepdims=True)), 1e-12)
    k = k * (1.0 + (a - 1.0) * k_a)
    hs = lambda t: t.reshape(B, T, H_C, HD_C)
    r, decay, k, v, a = hs(r), hs(decay), hs(k), hs(v), hs(a)

    def step(S, inp):
        r_t, w_t, k_t, v_t, kk_t, a_t = inp
        sa = jnp.einsum('bhvk,bhk->bhv', S, kk_t)
        S = S * w_t[:, :, None, :] - sa[..., None] * (kk_t * a_t)[:, :, None, :] + v_t[..., None] * k_t[:, :, None, :]
        return S, jnp.einsum('bhvk,bhk->bhv', S, r_t)

    tm = lambda t: jnp.swapaxes(t, 0, 1)
    S_fin, y = lax.scan(step, S0.astype(f32), (tm(r), tm(decay), tm(k), tm(v), tm(kk), tm(a)))
    y = jnp.swapaxes(y, 0, 1)
    mu_y = y.mean(-1, keepdims=True)
    var_y = jnp.square(y - mu_y).mean(-1, keepdims=True)
    y = (y - mu_y) * lax.rsqrt(var_y + GN_EPS) * gn_g.reshape(H_C, HD_C) + gn_b.reshape(H_C, HD_C)
    y = y + jnp.sum(r * k * r_k, axis=-1, keepdims=True) * v
    out = (y.reshape(B, T, D_C).astype(x.dtype) * jax.nn.silu(z)) @ w_out
    return out, S_fin, x[:, -1]


def setup_inputs(seed: int = 0) -> dict:
    key = jax.random.key(seed)
    ks = iter(jax.random.split(key, 48))
    nrm = lambda shape, scale: scale * jax.random.normal(next(ks), shape, jnp.float32)
    n_pages = PAST_LEN // PAGE_SIZE
    used = DEC_BATCH * n_pages
    n_pool = used + max(1, used // 4)
    w_buf = min(WINDOW, PAST_LEN)
    return {
        'x_prompt': nrm((BATCH, SEQ, D_MODEL), 1.0),
        'x_sample': nrm((DEC_BATCH, DEC_SEQ, D_MODEL), 1.0),
        'cache_cmp_kv': nrm((N_EVEN, n_pool, PAGE_SIZE, 2, KV_H, HD_B), 1.0),
        'cache_sel_kv': nrm((N_EVEN, n_pool, PAGE_SIZE, 2, KV_H, HD_B), 1.0),
        'cache_win_kv': nrm((N_EVEN, DEC_BATCH, w_buf, 2, KV_H, HD_B), 1.0),
        'state_conv': nrm((N_EVEN, DEC_BATCH, CONV_W - 1, D_A), 0.5),
        'state_wkv': nrm((N_ODD, DEC_BATCH, H_C, HD_C, HD_C), 0.5),
        'state_shift': nrm((N_ODD, DEC_BATCH, D_MODEL), 1.0),
        'page_table': jax.random.permutation(next(ks), n_pool)[:used].reshape(DEC_BATCH, n_pages).astype(jnp.int32),
        'w_in_even': nrm((N_EVEN, D_MODEL, D_IN_EVEN), D_MODEL ** -0.5),
        'conv_w': nrm((N_EVEN, CONV_W, 1, D_A), CONV_W ** -0.5),
        'conv_b': nrm((N_EVEN, D_A), 0.02),
        'conv_ln_g': 1.0 + nrm((N_EVEN, D_A), 0.02),
        'conv_ln_b': nrm((N_EVEN, D_A), 0.02),
        'wk_cmp': (1.0 + nrm((N_EVEN, L_CMP, KV_H), 0.1)) / L_CMP,
        'wv_cmp': (1.0 + nrm((N_EVEN, L_CMP, KV_H), 0.1)) / L_CMP,
        'w_out_even': nrm((N_EVEN, D_A + D_B, D_MODEL), BETA * (D_A + D_B) ** -0.5),
        'mu_c': jax.random.uniform(next(ks), (N_ODD, 6, D_MODEL), jnp.float32),
        'w_rkvz': nrm((N_ODD, 4, D_MODEL, D_C), D_MODEL ** -0.5),
        'w0': -0.6 + nrm((N_ODD, D_C), 0.5),
        'w1': nrm((N_ODD, D_MODEL, LORA_W), D_MODEL ** -0.5),
        'w2': nrm((N_ODD, LORA_W, D_C), 0.5 * LORA_W ** -0.5),
        'a0': nrm((N_ODD, D_C), 0.3),
        'a1': nrm((N_ODD, D_MODEL, LORA_A), D_MODEL ** -0.5),
        'a2': nrm((N_ODD, LORA_A, D_C), 0.5 * LORA_A ** -0.5),
        'k_k': 0.85 + nrm((N_ODD, D_C), 0.05),
        'k_a': 1.0 + nrm((N_ODD, D_C), 0.05),
        'r_k': nrm((N_ODD, H_C, HD_C), 0.1),
        'gn_g': 1.0 + nrm((N_ODD, D_C), 0.02),
        'gn_b': nrm((N_ODD, D_C), 0.02),
        'w_out_odd': nrm((N_ODD, D_C, D_MODEL), BETA * D_C ** -0.5),
        'ln_g': 1.0 + nrm((DEPTH, D_MODEL), 0.02),
        'ln_b': nrm((DEPTH, D_MODEL), 0.02),
    }


def reference(x_prompt, x_sample, cache_cmp_kv, cache_sel_kv, cache_win_kv, state_conv, state_wkv, state_shift,
              page_table, w_in_even, conv_w, conv_b, conv_ln_g, conv_ln_b, wk_cmp, wv_cmp, w_out_even,
              mu_c, w_rkvz, w0, w1, w2, a0, a1, a2, k_k, k_a, r_k, gn_g, gn_b, w_out_odd, ln_g, ln_b):
    slopes = _alibi_slopes()
    Bp, Tp, _ = x_prompt.shape
    Bs, Ts, _ = x_sample.shape
    past_len = page_table.shape[1] * PAGE_SIZE
    pos_p = jnp.arange(Tp, dtype=jnp.int32)
    pos_s = past_len + jnp.arange(Ts, dtype=jnp.int32)
    xp, xq = x_prompt, x_sample
    cmp_p, cmp_s, sel_p, sel_s, win_p, win_s, conv_p, conv_s = [], [], [], [], [], [], [], []
    wkv_p, wkv_s, sh_p, sh_s = [], [], [], []
    for l in range(DEPTH):
        if l % 2 == 0:
            e = l // 2
            pe = (w_in_even[e], conv_w[e], conv_b[e], conv_ln_g[e], conv_ln_b[e], wk_cmp[e], wv_cmp[e], w_out_even[e], slopes)
            empty = jnp.zeros((Bp, 0, 2, KV_H, HD_B), xp.dtype)
            yp, c1, s1, w1_, v1 = even_mixer(xp, pos_p, jnp.zeros((Bp, CONV_W - 1, D_A), xp.dtype),
                                             empty, empty, empty, *pe, banded=True)
            past_c = cache_cmp_kv[e][page_table].reshape(Bs, past_len, 2, KV_H, HD_B)
            past_s = cache_sel_kv[e][page_table].reshape(Bs, past_len, 2, KV_H, HD_B)
            ys, c2, s2, w2_, v2 = even_mixer(xq, pos_s, state_conv[e], past_c, past_s, cache_win_kv[e],
                                             *pe, banded=False)
            cmp_p.append(c1); cmp_s.append(c2); sel_p.append(s1); sel_s.append(s2)
            win_p.append(w1_); win_s.append(w2_); conv_p.append(v1); conv_s.append(v2)
        else:
            o = l // 2
            po = (mu_c[o], w_rkvz[o], w0[o], w1[o], w2[o], a0[o], a1[o], a2[o], k_k[o], k_a[o], r_k[o],
                  gn_g[o], gn_b[o], w_out_odd[o])
            yp, S1, h1 = rwkv_mixer(xp, jnp.zeros((Bp, D_MODEL), xp.dtype),
                                    jnp.zeros((Bp, H_C, HD_C, HD_C), jnp.float32), *po)
            ys, S2, h2 = rwkv_mixer(xq, state_shift[o], state_wkv[o], *po)
            wkv_p.append(S1); wkv_s.append(S2); sh_p.append(h1); sh_s.append(h2)
        xp = _layer_norm(ALPHA * xp + yp, ln_g[l], ln_b[l])
        xq = _layer_norm(ALPHA * xq + ys, ln_g[l], ln_b[l])
    return (xp, xq,
            jnp.stack(cmp_p), jnp.stack(cmp_s), jnp.stack(sel_p), jnp.stack(sel_s),
            jnp.stack(win_p), jnp.stack(win_s), jnp.stack(conv_p), jnp.stack(conv_s),
            jnp.stack(wkv_p), jnp.stack(wkv_s), jnp.stack(sh_p), jnp.stack(sh_s))
```

```python
import functools

import jax
import jax.numpy as jnp
from jax import lax
from jax.experimental import pallas as pl
from jax.experimental.pallas import tpu as pltpu

F32 = jnp.float32
BF16 = jnp.bfloat16

H_B = 8
L_SEL = 64
K_SEL = 16
WINDOW = 512
FORCE = 1e4
HD_C = 64
GN_EPS = 64e-5
LN_EPS = 1e-5
NEG = -1e30

LANES = 128
SUBLANES = 8
VMEM_LIMIT = 56 * 1024 * 1024

NT = (((1,), (1,)), ((), ()))
TN = (((0,), (0,)), ((), ()))
NN = (((1,), (0,)), ((), ()))


def _cparams(n_axes):
    return pltpu.CompilerParams(dimension_semantics=("arbitrary",) * n_axes,
                                vmem_limit_bytes=VMEM_LIMIT)


def _sigmoid(x):
    return 1.0 / (1.0 + jnp.exp(-x))


def _silu(x):
    return x * _sigmoid(x)


def _split_bf16(x, n):
    parts, rem = [], x
    for i in range(n):
        p = rem.astype(BF16)
        parts.append(p)
        if i + 1 < n:
            rem = rem - p.astype(F32)
    return parts


def _mm(a, b, dims=NN, pa=1, pb=1):
    aa = [a] if a.dtype == BF16 else _split_bf16(a, pa)
    bb = [b] if b.dtype == BF16 else _split_bf16(b, pb)
    keep = max(len(aa), len(bb))
    out = None
    for i, ai in enumerate(aa):
        for j, bj in enumerate(bb):
            if i + j < keep:
                t = lax.dot_general(ai, bj, dims, preferred_element_type=F32)
                out = t if out is None else out + t
    return out


def _layer_norm_rows(h, g, b, eps):
    mu = jnp.mean(h, axis=-1, keepdims=True)
    d = h - mu
    var = jnp.mean(d * d, axis=-1, keepdims=True)
    return d * lax.rsqrt(var + eps) * g + b


def _even_in_proj_kernel(x_ref, w_ref, u_ref, sza_ref, q_ref, cmp_ref, sel_ref, win_ref, szb_ref, gate_ref,
                         *, d_a, d_b, d_kv, q_scale):
    xb = x_ref[...].astype(BF16)

    def seg(lo, n):
        return jnp.dot(xb, w_ref[:, lo:lo + n], preferred_element_type=F32)

    o = 0
    a_val = seg(o, d_a); o += d_a
    a_glu = seg(o, d_a); o += d_a
    u_ref[...] = a_val * _sigmoid(a_glu)
    sza_ref[...] = _silu(seg(o, d_a)); o += d_a
    q_ref[...] = (seg(o, d_b) * q_scale).astype(BF16); o += d_b
    for ref in (cmp_ref, sel_ref, win_ref):
        ref[...] = seg(o, d_kv); o += d_kv
    szb_ref[...] = _silu(seg(o, d_b)); o += d_b
    gate_ref[...] = _sigmoid(seg(o, gate_ref.shape[-1]))


def _prep_w_in(w, d_a, d_b, kv_h, hd):
    grp = H_B // kv_h
    c_kv6 = 3 * d_a + d_b
    c_g3 = c_kv6 + 6 * kv_h * hd
    c_zb = c_g3 + 3 * H_B
    gate_blocks = []
    for g in range(kv_h):
        blk = w[:, c_g3 + g * grp * 3: c_g3 + (g + 1) * grp * 3]
        gate_blocks.append(jnp.pad(blk, ((0, 0), (0, LANES - grp * 3))))
    wn = jnp.concatenate([w[:, :c_g3], w[:, c_zb:c_zb + d_b]] + gate_blocks, axis=1)
    return wn.astype(BF16)


def _even_in_proj(x2, w_bf, d_a, d_b, d_kv, kv_h, hd):
    m, d = x2.shape
    tm = min(512, m)
    assert m % tm == 0
    n_gate = kv_h * LANES
    row = lambda n: pl.BlockSpec((tm, n), lambda i: (i, 0))
    out_shape = (jax.ShapeDtypeStruct((m, d_a), F32), jax.ShapeDtypeStruct((m, d_a), F32),
                 jax.ShapeDtypeStruct((m, d_b), BF16),
                 jax.ShapeDtypeStruct((m, d_kv), F32), jax.ShapeDtypeStruct((m, d_kv), F32),
                 jax.ShapeDtypeStruct((m, d_kv), F32),
                 jax.ShapeDtypeStruct((m, d_b), F32), jax.ShapeDtypeStruct((m, n_gate), F32))
    return pl.pallas_call(
        functools.partial(_even_in_proj_kernel, d_a=d_a, d_b=d_b, d_kv=d_kv, q_scale=hd ** -0.5),
        grid=(m // tm,),
        in_specs=[row(d), pl.BlockSpec(w_bf.shape, lambda i: (0, 0))],
        out_specs=(row(d_a), row(d_a), row(d_b), row(d_kv), row(d_kv), row(d_kv), row(d_b), row(n_gate)),
        out_shape=out_shape,
        compiler_params=_cparams(1),
        name="even_in_proj",
    )(x2, w_bf)


CONV_HIST = 32


def _conv_kernel(hist_ref, u_ref, sza_ref, w_ref, cb_ref, g_ref, b_ref, ya_ref, win_ref, sh_ref,
                 *, tq, rb, n_taps):
    i = pl.program_id(1)
    base = pl.multiple_of(i * tq, SUBLANES)
    win_ref[CONV_HIST:, :] = u_ref[0, pl.ds(base, tq), :]

    @pl.when(i == 0)
    def _():
        win_ref[:CONV_HIST, :] = hist_ref[0]

    @pl.when(i > 0)
    def _():
        win_ref[:CONV_HIST, :] = u_ref[0, pl.ds(base - CONV_HIST, CONV_HIST), :]

    w = win_ref[...]
    n = tq + CONV_HIST
    sh_ref[0] = w
    for s in range(1, SUBLANES):
        sh_ref[s] = pltpu.roll(w, n - s, 0)

    first = CONV_HIST - (n_taps - 1)
    cb = cb_ref[...]
    g = g_ref[...]
    b = b_ref[...]

    def block(k, carry):
        r0 = pl.multiple_of(k * rb, SUBLANES)
        acc = jnp.zeros((rb, u_ref.shape[-1]), F32) + cb
        for j in range(n_taps):
            a, s = divmod(first + j, SUBLANES)
            acc = acc + sh_ref[s, pl.ds(r0 + SUBLANES * a, rb), :] * w_ref[j:j + 1, :]
        y = _silu(_layer_norm_rows(acc, g, b, LN_EPS)) * sza_ref[0, pl.ds(r0, rb), :]
        ya_ref[0, pl.ds(r0, rb), :] = y.astype(ya_ref.dtype)
        return carry

    lax.fori_loop(0, tq // rb, block, 0)


def _conv_branch(hist, u3, sza3, conv_w, conv_b, ln_g, ln_b):
    bsz, t, d = u3.shape
    n_taps = conv_w.shape[0]
    assert n_taps - 1 <= CONV_HIST and t % SUBLANES == 0
    tq = min(256, t)
    rb = min(32, tq)
    assert t % tq == 0 and tq % rb == 0
    vec = lambda a: a.reshape(1, d)
    full = lambda shp: pl.BlockSpec(shp, lambda b, i: (0,) * len(shp))
    return pl.pallas_call(
        functools.partial(_conv_kernel, tq=tq, rb=rb, n_taps=n_taps),
        grid=(bsz, t // tq),
        in_specs=[pl.BlockSpec((1, CONV_HIST, d), lambda b, i: (b, 0, 0)),
                  pl.BlockSpec((1, t, d), lambda b, i: (b, 0, 0)),
                  pl.BlockSpec((1, tq, d), lambda b, i: (b, i, 0)),
                  full((n_taps, d)), full((1, d)), full((1, d)), full((1, d))],
        out_specs=pl.BlockSpec((1, tq, d), lambda b, i: (b, i, 0)),
        out_shape=jax.ShapeDtypeStruct((bsz, t, d), BF16),
        scratch_shapes=[pltpu.VMEM((tq + CONV_HIST, d), F32),
                        pltpu.VMEM((SUBLANES, tq + CONV_HIST, d), F32)],
        compiler_params=_cparams(2),
        name="conv_branch",
    )(hist, u3, sza3, conv_w, vec(conv_b), vec(ln_g), vec(ln_b))


def _nsa_prompt_kernel(q_ref, kcmp_ref, ksel_ref, kwin_ref, gate_ref, szb_ref, wpool_ref, o_ref,
                       kc_s, vc_s, ks_s, vs_s, kw_s, vw_s, *, tq, t_len, l_cmp, hd, grp, kv_h):
    i = pl.program_id(1)
    g = pl.program_id(2)
    n_cmp = t_len // l_cmp
    n_sel = t_len // L_SEL
    half = n_cmp // 2
    gw = kv_h * hd

    @pl.when((i == 0) & (g == 0))
    def _prep():
        x3 = kcmp_ref[0].reshape(half, 2 * l_cmp, 2 * gw)
        pooled = jnp.concatenate([jnp.sum(x3 * wpool_ref[0][None], axis=1),
                                  jnp.sum(x3 * wpool_ref[1][None], axis=1)], axis=0)
        for gg in range(kv_h):
            kc_s[gg] = pooled[:, gg * hd:(gg + 1) * hd].astype(BF16)
            vc_s[gg] = pooled[:, gw + gg * hd: gw + (gg + 1) * hd].astype(BF16)
            ks_s[gg] = ksel_ref[0, :, gg * hd:(gg + 1) * hd].astype(BF16)
            vs_s[gg] = ksel_ref[0, :, gw + gg * hd: gw + (gg + 1) * hd].astype(BF16)
            kw_s[gg] = kwin_ref[0, :, gg * hd:(gg + 1) * hd].astype(BF16)
            vw_s[gg] = kwin_ref[0, :, gw + gg * hd: gw + (gg + 1) * hd].astype(BF16)

    t0 = i * tq
    gs = jnp.float32(1.0)
    for gg in range(1, kv_h):
        gs = jnp.where(g == gg, 2.0 ** -(gg * grp), gs)
    slopes = [gs * (2.0 ** -(r + 1)) for r in range(grp)]
    qt = q_ref[0]
    qs = jnp.concatenate([qt[:, r * hd:(r + 1) * hd] for r in range(grp)], axis=0)

    s_t = lax.dot_general(kc_s[g], qs, NT, preferred_element_type=F32)
    n_io = lax.broadcasted_iota(jnp.int32, (n_cmp, tq), 0)
    t_io = lax.broadcasted_iota(jnp.int32, (n_cmp, tq), 1) + t0
    c_end = jnp.where(n_io < half, 2 * l_cmp * n_io + (l_cmp - 1), 2 * l_cmp * (n_io - half) + (2 * l_cmp - 1))
    dist_c = t_io - c_end
    valid_c = dist_c >= 0
    dist_cf = dist_c.astype(F32)
    imp = jnp.zeros((n_sel, tq), F32)
    p_list = []
    for r in range(grp):
        s = jnp.where(valid_c, s_t[:, r * tq:(r + 1) * tq] - slopes[r] * dist_cf, NEG)
        e = jnp.exp(s - jnp.max(s, axis=0, keepdims=True))
        p = jnp.where(valid_c, e / jnp.sum(e, axis=0, keepdims=True), 0.0)
        imp = imp + (p[:half] + p[half:])
        p_list.append(p.astype(BF16))
    p_t = jnp.concatenate(p_list, axis=1)
    o_c = lax.dot_general(p_t, vc_s[g], TN, preferred_element_type=F32)

    sb = lax.broadcasted_iota(jnp.int32, (n_sel, tq), 0)
    tb = (lax.broadcasted_iota(jnp.int32, (n_sel, tq), 1) + t0) // L_SEL
    forced = (sb == 0) | (sb == tb) | (sb == tb - 1)
    imp = jnp.where(forced, FORCE, imp)
    imp = jnp.where(sb <= tb, imp, -jnp.inf)
    rank = jnp.zeros((n_sel, tq), F32)
    for j in range(n_sel):
        vj = imp[j:j + 1, :]
        beats = jnp.where(imp > vj, 1.0, jnp.where((imp == vj) & (sb < j), 1.0, 0.0))
        rank = jnp.where(sb == j, jnp.sum(beats, axis=0, keepdims=True), rank)
    sel_t = jnp.where(rank < min(K_SEL, n_sel), 1.0, 0.0).astype(BF16)

    d0 = (lax.broadcasted_iota(jnp.int32, (tq, tq), 0) - lax.broadcasted_iota(jnp.int32, (tq, tq), 1))

    def attend(k_s, v_s, lo, hi, mask_fn):
        def body(kt, carry):
            m, l, acc = carry
            k0 = pl.multiple_of(kt * tq, tq)
            kk = k_s[g, pl.ds(k0, tq), :]
            vv = v_s[g, pl.ds(k0, tq), :]
            s = lax.dot_general(qs, kk, NT, preferred_element_type=F32)
            d = d0 + (t0 - k0)
            mask = mask_fn(k0, d)
            df = d.astype(F32)
            m_new, l_new, acc_new = [], [], []
            p_rows = []
            for r in range(grp):
                sr = jnp.where(mask, s[r * tq:(r + 1) * tq] - slopes[r] * df, NEG)
                mr = jnp.maximum(m[r], jnp.max(sr, axis=-1, keepdims=True))
                al = jnp.exp(m[r] - mr)
                p = jnp.exp(sr - mr)
                m_new.append(mr)
                l_new.append(al * l[r] + jnp.sum(p, axis=-1, keepdims=True))
                acc_new.append(al)
                p_rows.append(p.astype(BF16))
            pv = jnp.dot(jnp.concatenate(p_rows, axis=0), vv, preferred_element_type=F32)
            acc = jnp.concatenate(acc_new, axis=0) * acc + pv
            return tuple(m_new), tuple(l_new), acc

        init = (tuple(jnp.full((tq, 1), NEG, F32) for _ in range(grp)),
                tuple(jnp.zeros((tq, 1), F32) for _ in range(grp)),
                jnp.zeros((grp * tq, hd), F32))
        m, l, acc = lax.fori_loop(lo, hi, body, init)
        return [acc[r * tq:(r + 1) * tq] / l[r] for r in range(grp)]

    def sel_mask(k0, d):
        jb = lax.broadcasted_iota(jnp.int32, (n_sel, tq), 0)
        kb = (lax.broadcasted_iota(jnp.int32, (n_sel, tq), 1) + k0) // L_SEL
        expand = jnp.where(jb == kb, 1.0, 0.0).astype(BF16)
        chosen = lax.dot_general(sel_t, expand, TN, preferred_element_type=F32)
        return (chosen > 0.5) & (d >= 0)

    def win_mask(k0, d):
        return (d >= 0) & (d < WINDOW)

    o_s = attend(ks_s, vs_s, 0, i + 1, sel_mask)
    w_tiles = -(-(WINDOW - 1) // tq)
    o_w = attend(kw_s, vw_s, jnp.maximum(i - w_tiles, 0), i + 1, win_mask)

    gate = gate_ref[0]
    outs = []
    for r in range(grp):
        outs.append(gate[:, 3 * r:3 * r + 1] * o_c[r * tq:(r + 1) * tq]
                    + gate[:, 3 * r + 1:3 * r + 2] * o_s[r]
                    + gate[:, 3 * r + 2:3 * r + 3] * o_w[r])
    o_ref[0] = (jnp.concatenate(outs, axis=1) * szb_ref[0]).astype(o_ref.dtype)


def _pool_weights(wk, wv, hd, halves):
    l_cmp, kv_h = wk.shape
    row = jnp.concatenate([jnp.repeat(wk, hd, axis=1), jnp.repeat(wv, hd, axis=1)], axis=1)
    if not halves:
        return row
    z = jnp.zeros_like(row)
    return jnp.stack([jnp.concatenate([row, z], axis=0), jnp.concatenate([z, row], axis=0)])


def _nsa_prompt(q3, cmp3, sel3, win3, gate3, szb3, wk, wv):
    bsz, t, d_b = q3.shape
    l_cmp, kv_h = wk.shape
    grp = H_B // kv_h
    hd = d_b // H_B
    gw = kv_h * hd
    tq = min(256, t)
    assert t % tq == 0 and t % (2 * l_cmp) == 0 and L_SEL == 2 * l_cmp and tq % L_SEL == 0
    n_cmp = t // l_cmp
    wpool = _pool_weights(wk, wv, hd, halves=True)
    kv_spec = pl.BlockSpec((1, t, 2 * gw), lambda b, i, g: (b, 0, 0))
    return pl.pallas_call(
        functools.partial(_nsa_prompt_kernel, tq=tq, t_len=t, l_cmp=l_cmp, hd=hd, grp=grp, kv_h=kv_h),
        grid=(bsz, t // tq, kv_h),
        in_specs=[pl.BlockSpec((1, tq, grp * hd), lambda b, i, g: (b, i, g)),
                  kv_spec, kv_spec, kv_spec,
                  pl.BlockSpec((1, tq, LANES), lambda b, i, g: (b, i, g)),
                  pl.BlockSpec((1, tq, grp * hd), lambda b, i, g: (b, i, g)),
                  pl.BlockSpec(wpool.shape, lambda b, i, g: (0, 0, 0))],
        out_specs=pl.BlockSpec((1, tq, grp * hd), lambda b, i, g: (b, i, g)),
        out_shape=jax.ShapeDtypeStruct((bsz, t, d_b), BF16),
        scratch_shapes=[pltpu.VMEM((kv_h, n_cmp, hd), BF16), pltpu.VMEM((kv_h, n_cmp, hd), BF16)]
                       + [pltpu.VMEM((kv_h, t, hd), BF16) for _ in range(4)],
        compiler_params=_cparams(3),
        name="nsa_prompt",
    )(q3, cmp3, sel3, win3, gate3, szb3, wpool)


def _nsa_decode_kernel(pt_ref, *refs, pg, n_pages, page, l_cmp, hd, grp, kv_h, w_buf):
    del pt_ref
    q_ref = refs[0]
    cmp_refs = refs[1:1 + pg]
    sel_refs = refs[1 + pg:1 + 2 * pg]
    (kwin_ref, seln_ref, winn_ref, gate_ref, szb_ref, wtile_ref, o_ref,
     kcv_s, selm_s, oc_s, m_s, l_s, acc_s) = refs[1 + 2 * pg:]
    p = pl.program_id(1)
    n_steps = n_pages // pg
    past = n_pages * page
    n_cmp = past // l_cmp
    n_sel = past // L_SEL
    per_page = page // l_cmp
    gw = kv_h * hd
    nc_pad = selm_s.shape[-1]
    qf = q_ref[0]
    q = qf.astype(BF16)
    head = lax.broadcasted_iota(jnp.int32, (H_B, 1), 0)
    slope = jnp.zeros((H_B, 1), F32)
    for hh in range(H_B):
        slope = jnp.where(head == hh, 2.0 ** -(hh + 1), slope)

    @pl.when(p < n_steps)
    def _pool():
        rows = []
        for j in range(pg):
            x3 = (cmp_refs[j][0] * wtile_ref[...]).reshape(per_page, l_cmp, 2 * gw)
            rows.append(jnp.sum(x3, axis=1))
        kcv_s[pl.ds(pl.multiple_of(p * (pg * per_page), SUBLANES), pg * per_page), :] = jnp.concatenate(rows, axis=0)

    @pl.when(p == n_steps - 1)
    def _compressed():
        kcv = kcv_s[...]
        n_io = lax.broadcasted_iota(jnp.int32, (1, n_cmp), 1)
        dist = (past - (l_cmp * n_io + (l_cmp - 1))).astype(F32)
        pair = jnp.where(lax.broadcasted_iota(jnp.int32, (n_cmp, nc_pad), 0) // (L_SEL // l_cmp)
                         == lax.broadcasted_iota(jnp.int32, (n_cmp, nc_pad), 1), 1.0, 0.0).astype(BF16)
        c_io = lax.broadcasted_iota(jnp.int32, (1, nc_pad), 1)
        i_r = lax.broadcasted_iota(jnp.int32, (nc_pad, nc_pad), 1)
        i_c = lax.broadcasted_iota(jnp.int32, (nc_pad, nc_pad), 0)
        selm_s[...] = jnp.zeros(selm_s.shape, F32)
        for g in range(kv_h):
            kc = kcv[:, g * hd:(g + 1) * hd].astype(BF16)
            vc = kcv[:, gw + g * hd: gw + (g + 1) * hd].astype(BF16)
            s = lax.dot_general(q, kc, NT, preferred_element_type=F32) - slope * dist
            e = jnp.exp(s - jnp.max(s, axis=-1, keepdims=True))
            pc = e / jnp.sum(e, axis=-1, keepdims=True)
            oc_s[g] = jnp.dot(pc.astype(BF16), vc, preferred_element_type=F32)
            in_grp = (head >= g * grp) & (head < (g + 1) * grp)
            imp = _mm(jnp.sum(jnp.where(in_grp, pc, 0.0), axis=0, keepdims=True), pair, NN, pa=3)
            forced = (c_io == 0) | (c_io == n_sel - 1)
            imp = jnp.where(forced, FORCE, imp)
            imp = jnp.where(c_io < n_sel, imp, -jnp.inf)
            v_r = jnp.broadcast_to(imp, (nc_pad, nc_pad))
            v_c = v_r.T
            beats = jnp.where(v_c > v_r, 1.0, jnp.where((v_c == v_r) & (i_c < i_r), 1.0, 0.0))
            rank = jnp.sum(beats, axis=0, keepdims=True) + jnp.where(imp < FORCE, 1.0, 0.0)
            chosen = (rank < min(K_SEL, n_sel + 1)) & (c_io < n_sel)
            selm_s[g:g + 1, :] = jnp.where(chosen, 1.0, 0.0)
        m_s[...] = jnp.full(m_s.shape, NEG, F32)
        l_s[...] = jnp.zeros(l_s.shape, F32)
        acc_s[...] = jnp.zeros(acc_s.shape, F32)

    @pl.when(p >= n_steps)
    def _selected():
        step = p - n_steps
        nk = pg * page
        x = jnp.concatenate([sel_refs[j][0] for j in range(pg)], axis=0)
        kpos = lax.broadcasted_iota(jnp.int32, (1, nk), 1) + step * nk
        dist = (past - kpos).astype(F32)
        expand = jnp.where(lax.broadcasted_iota(jnp.int32, (nc_pad, nk), 0)
                           == (lax.broadcasted_iota(jnp.int32, (nc_pad, nk), 1) + step * nk) // L_SEL,
                           1.0, 0.0).astype(BF16)
        chosen = jnp.dot(selm_s[...].astype(BF16), expand, preferred_element_type=F32)
        for g in range(kv_h):
            kk = x[:, g * hd:(g + 1) * hd].astype(BF16)
            vv = x[:, gw + g * hd: gw + (g + 1) * hd].astype(BF16)
            s = lax.dot_general(q, kk, NT, preferred_element_type=F32) - slope * dist
            s = jnp.where(chosen[g:g + 1, :] > 0.5, s, NEG)
            m_old = m_s[g]
            m_new = jnp.maximum(m_old, jnp.max(s, axis=-1, keepdims=True))
            al = jnp.exp(m_old - m_new)
            pe = jnp.exp(s - m_new)
            m_s[g] = m_new
            l_s[g] = al * l_s[g] + jnp.sum(pe, axis=-1, keepdims=True)
            acc_s[g] = al * acc_s[g] + jnp.dot(pe.astype(BF16), vv, preferred_element_type=F32)

    @pl.when(p == 2 * n_steps - 1)
    def _finish():
        gate = gate_ref[0]
        xw = kwin_ref[0]
        i_io = lax.broadcasted_iota(jnp.int32, (1, w_buf), 1)
        dist_w = w_buf - i_io
        valid_w = dist_w < WINDOW
        o = jnp.zeros((H_B, hd), F32)
        for g in range(kv_h):
            kn = seln_ref[0][:, g * hd:(g + 1) * hd]
            vn = seln_ref[0][:, gw + g * hd: gw + (g + 1) * hd]
            s_n = jnp.sum(qf * kn, axis=-1, keepdims=True)
            m_old = m_s[g]
            m_new = jnp.maximum(m_old, s_n)
            al = jnp.exp(m_old - m_new)
            pn = jnp.exp(s_n - m_new)
            o_s = (al * acc_s[g] + pn * vn) / (al * l_s[g] + pn)
            kw = xw[:, g * hd:(g + 1) * hd].astype(BF16)
            vw = xw[:, gw + g * hd: gw + (g + 1) * hd].astype(BF16)
            s = lax.dot_general(q, kw, NT, preferred_element_type=F32) - slope * dist_w.astype(F32)
            s = jnp.where(valid_w, s, NEG)
            kwn = winn_ref[0][:, g * hd:(g + 1) * hd]
            vwn = winn_ref[0][:, gw + g * hd: gw + (g + 1) * hd]
            s_wn = jnp.sum(qf * kwn, axis=-1, keepdims=True)
            mw = jnp.maximum(jnp.max(s, axis=-1, keepdims=True), s_wn)
            pw = jnp.where(valid_w, jnp.exp(s - mw), 0.0)
            pwn = jnp.exp(s_wn - mw)
            o_w = ((jnp.dot(pw.astype(BF16), vw, preferred_element_type=F32) + pwn * vwn)
                   / (jnp.sum(pw, axis=-1, keepdims=True) + pwn))
            gcol = []
            for c in range(3):
                col = jnp.zeros((H_B, 1), F32)
                for r in range(grp):
                    lane = g * LANES + 3 * r + c
                    col = jnp.where(head == g * grp + r, gate[:, lane:lane + 1], col)
                gcol.append(col)
            og = gcol[0] * oc_s[g] + gcol[1] * o_s + gcol[2] * o_w
            o = jnp.where((head >= g * grp) & (head < (g + 1) * grp), og, o)
        o_ref[0] = o * szb_ref[0]


def _nsa_decode(q3, cache_cmp, cache_sel, cache_win, page_table, sel_new, win_new, gate, szb3, wk, wv):
    bs, _, hd = q3.shape
    l_cmp, kv_h = wk.shape
    grp = H_B // kv_h
    gw = kv_h * hd
    _, page, _ = cache_cmp.shape
    n_pages = page_table.shape[1]
    w_buf = cache_win.shape[1]
    pg = 8 if n_pages % 8 == 0 else 1
    n_steps = n_pages // pg
    per_page = page // l_cmp
    assert page % L_SEL == 0 and page % l_cmp == 0 and (pg * per_page) % SUBLANES == 0
    n_sel = n_pages * page // L_SEL
    nc_pad = -(-n_sel // LANES) * LANES
    wtile = jnp.tile(_pool_weights(wk, wv, hd, halves=False), (per_page, 1))

    def cmp_map(j):
        return lambda b, p, pt: (pt[b, jnp.minimum(p, n_steps - 1) * pg + j], 0, 0)

    def sel_map(j):
        return lambda b, p, pt: (pt[b, jnp.maximum(p - n_steps, 0) * pg + j], 0, 0)

    per_b = lambda shp: pl.BlockSpec((1,) + shp, lambda b, p, pt: (b, 0, 0))
    grid_spec = pltpu.PrefetchScalarGridSpec(
        num_scalar_prefetch=1,
        grid=(bs, 2 * n_steps),
        in_specs=([per_b((H_B, hd))]
                  + [pl.BlockSpec((1, page, 2 * gw), cmp_map(j)) for j in range(pg)]
                  + [pl.BlockSpec((1, page, 2 * gw), sel_map(j)) for j in range(pg)]
                  + [per_b((w_buf, 2 * gw)), per_b((1, 2 * gw)), per_b((1, 2 * gw)),
                     per_b((1, kv_h * LANES)), per_b((H_B, hd)),
                     pl.BlockSpec((page, 2 * gw), lambda b, p, pt: (0, 0))]),
        out_specs=per_b((H_B, hd)),
        scratch_shapes=[pltpu.VMEM((n_pages * per_page, 2 * gw), F32),
                        pltpu.VMEM((SUBLANES, nc_pad), F32),
                        pltpu.VMEM((kv_h, H_B, hd), F32), pltpu.VMEM((kv_h, H_B, 1), F32),
                        pltpu.VMEM((kv_h, H_B, 1), F32), pltpu.VMEM((kv_h, H_B, hd), F32)])
    return pl.pallas_call(
        functools.partial(_nsa_decode_kernel, pg=pg, n_pages=n_pages, page=page, l_cmp=l_cmp, hd=hd, grp=grp,
                          kv_h=kv_h, w_buf=w_buf),
        grid_spec=grid_spec,
        out_shape=jax.ShapeDtypeStruct((bs, H_B, hd), F32),
        compiler_params=_cparams(2),
        name="nsa_decode",
    )(page_table, q3, *([cache_cmp] * pg), *([cache_sel] * pg), cache_win, sel_new, win_new, gate, szb3, wtile)


def _out_proj_ln_kernel(*refs, n_in, alpha):
    a_refs, w_refs = refs[:n_in], refs[n_in:2 * n_in]
    x_ref, g_ref, b_ref, o_ref = refs[2 * n_in:]
    y = None
    for a_ref, w_ref in zip(a_refs, w_refs):
        t = jnp.dot(a_ref[...].astype(BF16), w_ref[...], preferred_element_type=F32)
        y = t if y is None else y + t
    o_ref[...] = _layer_norm_rows(alpha * x_ref[...] + y, g_ref[...], b_ref[...], LN_EPS)


def _out_proj_ln(a_list, w_list, x2, ln_g, ln_b, alpha):
    m, d = x2.shape
    tm = min(512, m)
    assert m % tm == 0
    n_in = len(a_list)
    return pl.pallas_call(
        functools.partial(_out_proj_ln_kernel, n_in=n_in, alpha=alpha),
        grid=(m // tm,),
        in_specs=([pl.BlockSpec((tm, a.shape[1]), lambda i: (i, 0)) for a in a_list]
                  + [pl.BlockSpec(w.shape, lambda i: (0, 0)) for w in w_list]
                  + [pl.BlockSpec((tm, d), lambda i: (i, 0)),
                     pl.BlockSpec((1, d), lambda i: (0, 0)), pl.BlockSpec((1, d), lambda i: (0, 0))]),
        out_specs=pl.BlockSpec((tm, d), lambda i: (i, 0)),
        out_shape=jax.ShapeDtypeStruct((m, d), F32),
        compiler_params=_cparams(1),
        name="out_proj_ln",
    )(*a_list, *w_list, x2, ln_g.reshape(1, d), ln_b.reshape(1, d))


def _rwkv_proj_kernel(*refs, seq, tiles_per_seq):
    if seq:
        x_ref, tail_ref, shift_ref = refs[:3]
        rest = refs[3:]
    else:
        x_ref, shift_ref = refs[:2]
        rest = refs[2:]
    (mu_ref, w_ref, w1_ref, w2_ref, a1_ref, a2_ref, w0_ref, a0_ref,
     r_ref, k_ref, v_ref, lw_ref, a_ref, sz_ref) = rest
    x = x_ref[...]
    if seq:
        i = pl.program_id(0)
        first = jnp.where(i % tiles_per_seq == 0, shift_ref[0], tail_ref[SUBLANES - 1:SUBLANES, :])
        row = lax.broadcasted_iota(jnp.int32, x.shape, 0)
        x_prev = jnp.where(row == 0, first, pltpu.roll(x, 1, 0))
    else:
        x_prev = shift_ref[...]
    dx = x_prev - x
    mix = lambda n: (x + dx * mu_ref[n:n + 1, :]).astype(BF16)
    r_ref[...] = jnp.dot(mix(0), w_ref[0], preferred_element_type=F32)
    k_ref[...] = jnp.dot(mix(1), w_ref[1], preferred_element_type=F32)
    v_ref[...] = jnp.dot(mix(2), w_ref[2], preferred_element_type=F32)
    sz_ref[...] = _silu(jnp.dot(mix(3), w_ref[3], preferred_element_type=F32))
    hw = jnp.tanh(jnp.dot(mix(4), w1_ref[...], preferred_element_type=F32)).astype(BF16)
    y = -(w0_ref[...] + jnp.dot(hw, w2_ref[...], preferred_element_type=F32))
    softplus = jnp.maximum(y, 0.0) + jnp.log(1.0 + jnp.exp(-jnp.abs(y)))
    lw_ref[...] = -jnp.exp(-softplus - 0.5)
    ha = jnp.dot(mix(5), a1_ref[...], preferred_element_type=F32).astype(BF16)
    a_ref[...] = _sigmoid(a0_ref[...] + jnp.dot(ha, a2_ref[...], preferred_element_type=F32))


def _rwkv_proj(x2, shift, t_len, mu, w_rkvz, w0, w1, w2, a0, a1, a2):
    m, d = x2.shape
    seq = t_len > 1
    tm = min(256, t_len) if seq else m
    assert m % tm == 0 and (not seq or (t_len % tm == 0 and tm % SUBLANES == 0))
    tiles_per_seq = t_len // tm if seq else 1
    full = lambda a: pl.BlockSpec(a.shape, lambda i: (0,) * a.ndim)
    row = pl.BlockSpec((tm, d), lambda i: (i, 0))
    if seq:
        blk = tm // SUBLANES
        lead = [x2, x2, shift.reshape(-1, 1, d)]
        lead_specs = [row, pl.BlockSpec((SUBLANES, d), lambda i: (jnp.maximum(i * blk - 1, 0), 0)),
                      pl.BlockSpec((1, 1, d), lambda i: (i // tiles_per_seq, 0, 0))]
    else:
        lead = [x2, shift]
        lead_specs = [row, row]
    ws = [mu, w_rkvz.astype(BF16), w1.astype(BF16), w2.astype(BF16), a1.astype(BF16), a2.astype(BF16),
          w0.reshape(1, d), a0.reshape(1, d)]
    return pl.pallas_call(
        functools.partial(_rwkv_proj_kernel, seq=seq, tiles_per_seq=tiles_per_seq),
        grid=(m // tm,),
        in_specs=lead_specs + [full(a) for a in ws],
        out_specs=(row,) * 6,
        out_shape=(jax.ShapeDtypeStruct((m, d), F32),) * 6,
        compiler_params=_cparams(1),
        name="rwkv_proj",
    )(*lead, *ws)


WKV_CHUNK = 64
WKV_PASSES = 2


def _wkv_kernel(r_ref, k_ref, v_ref, lw_ref, a_ref, sz_ref, s0_ref, kk_ref, ka_ref, rk_ref, gg_ref, gb_ref,
                yz_ref, sfin_ref, h_s, *, c_len, n_heads, hd):
    c = pl.program_id(1)
    ps = WKV_PASSES
    mm = functools.partial(_mm, pa=ps, pb=ps)

    @pl.when(c == 0)
    def _():
        for h in range(n_heads):
            h_s[h] = s0_ref[0, h].T

    r = r_ref[0]
    k = k_ref[0]
    v = v_ref[0]
    lw = lw_ref[0]
    a = a_ref[0]
    row = lax.broadcasted_iota(jnp.int32, (c_len, c_len), 0)
    col = lax.broadcasted_iota(jnp.int32, (c_len, c_len), 1)
    low_inc = row >= col
    low_exc = row > col
    eye = jnp.where(row == col, 1.0, 0.0)
    cum = _mm(jnp.where(low_inc, 1.0, 0.0).astype(BF16), lw, NN, pb=3)
    p_inc = jnp.exp(cum)
    p_exc = jnp.exp(cum - lw)
    p_inv = jnp.exp(-cum)
    cum_end = cum[c_len - 1:c_len, :]
    p_rest = jnp.exp(cum_end - cum)
    p_end = jnp.exp(cum_end)
    kk = k * kk_ref[...]
    kmod = k * (1.0 + (a - 1.0) * ka_ref[...])
    r_t = r * p_inc
    k_t = kmod * p_inv
    k_e = kmod * p_rest
    rkr = r * kmod * rk_ref[...]
    hr = lax.broadcasted_iota(jnp.int32, (hd, hd), 0)
    hc = lax.broadcasted_iota(jnp.int32, (hd, hd), 1)
    levels = max((c_len - 1).bit_length() - 1, 0)

    outs = []
    for h in range(n_heads):
        hs = slice(h * hd, (h + 1) * hd)
        kkh = kk[:, hs]
        kap = kkh / jnp.maximum(jnp.sqrt(jnp.sum(kkh * kkh, axis=-1, keepdims=True)), 1e-12)
        ka_h = kap * a[:, hs]
        lhs = jnp.concatenate([kap * p_exc[:, hs], r_t[:, hs]], axis=0)
        rhs = jnp.concatenate([ka_h * p_inv[:, hs], k_t[:, hs]], axis=0)
        vh = v[:, hs]
        am = mm(lhs, rhs, NT)
        l_b = jnp.where(low_exc, am[:c_len, :c_len], 0.0)
        l_k = jnp.where(low_exc, am[:c_len, c_len:], 0.0)
        m_r = jnp.concatenate([jnp.where(low_inc, am[c_len:, :c_len], 0.0),
                               jnp.where(low_inc, am[c_len:, c_len:], 0.0)], axis=1)
        h_in = h_s[h]
        gh = mm(lhs, h_in, NN)
        t_inv = eye - l_b
        pw = l_b
        for _ in range(levels):
            pw = mm(pw, pw, NN)
            t_inv = t_inv + mm(t_inv, pw, NN)
        u = -mm(t_inv, gh[:c_len] + mm(l_k, vh, NN), NN)
        uv = jnp.concatenate([u, vh], axis=0)
        y = gh[c_len:] + mm(m_r, uv, NN)
        pe = p_end[:, hs]
        lhs2 = jnp.concatenate([ka_h * p_rest[:, hs], k_e[:, hs], jnp.where(hr == hc, pe, 0.0)], axis=0)
        h_out = mm(lhs2, jnp.concatenate([uv, h_in], axis=0), TN)
        h_s[h] = h_out

        @pl.when(c == pl.num_programs(1) - 1)
        def _():
            sfin_ref[0, h] = h_out.T

        mu = jnp.mean(y, axis=-1, keepdims=True)
        dy = y - mu
        var = jnp.mean(dy * dy, axis=-1, keepdims=True)
        yn = dy * lax.rsqrt(var + GN_EPS) * gg_ref[:, hs] + gb_ref[:, hs]
        outs.append(yn + jnp.sum(rkr[:, hs], axis=-1, keepdims=True) * vh)
    yz_ref[0] = (jnp.concatenate(outs, axis=1) * sz_ref[0]).astype(yz_ref.dtype)


def _wkv_scan(r3, k3, v3, lw3, a3, sz3, s0, k_k, k_a, r_k, gn_g, gn_b):
    bsz, t, d = r3.shape
    n_heads = d // HD_C
    c_len = WKV_CHUNK
    assert t % c_len == 0
    vec = lambda x: x.reshape(1, d)
    seq = pl.BlockSpec((1, c_len, d), lambda b, c: (b, c, 0))
    st = pl.BlockSpec((1, n_heads, HD_C, HD_C), lambda b, c: (b, 0, 0, 0))
    par = pl.BlockSpec((1, d), lambda b, c: (0, 0))
    return pl.pallas_call(
        functools.partial(_wkv_kernel, c_len=c_len, n_heads=n_heads, hd=HD_C),
        grid=(bsz, t // c_len),
        in_specs=[seq] * 6 + [st] + [par] * 5,
        out_specs=(seq, st),
        out_shape=(jax.ShapeDtypeStruct((bsz, t, d), BF16),
                   jax.ShapeDtypeStruct((bsz, n_heads, HD_C, HD_C), F32)),
        scratch_shapes=[pltpu.VMEM((n_heads, HD_C, HD_C), F32)],
        compiler_params=_cparams(2),
        name="wkv_scan",
    )(r3, k3, v3, lw3, a3, sz3, s0, vec(k_k), vec(k_a), vec(r_k), vec(gn_g), vec(gn_b))


def _even_layer(xp, xs, cache_cmp, cache_sel, cache_win, state_conv, page_table,
                w_in, conv_w, conv_b, cln_g, cln_b, wk, wv, w_out, ln_g, ln_b, alpha):
    bp, tp, d = xp.shape
    bs, ts, _ = xs.shape
    assert ts == 1
    d_a = conv_w.shape[-1]
    n_taps = conv_w.shape[0]
    l_cmp, kv_h = wk.shape
    hd = cache_cmp.shape[-1]
    d_b = H_B * hd
    d_kv = 2 * kv_h * hd
    w_bf = _prep_w_in(w_in, d_a, d_b, kv_h, hd)
    cw = conv_w.reshape(n_taps, d_a)
    w_out_bf = w_out.astype(BF16)
    kv_shape = lambda b, t: (b, t, 2, kv_h, hd)

    u, sza, q, cmp_n, sel_n, win_n, szb, gate = _even_in_proj(xp.reshape(bp * tp, d), w_bf, d_a, d_b, d_kv, kv_h, hd)
    r3 = lambda a: a.reshape(bp, tp, a.shape[-1])
    u3 = r3(u)
    ya = _conv_branch(jnp.zeros((bp, CONV_HIST, d_a), F32), u3, r3(sza), cw, conv_b, cln_g, cln_b)
    yb = _nsa_prompt(r3(q), r3(cmp_n), r3(sel_n), r3(win_n), r3(gate), r3(szb), wk, wv)
    yp = _out_proj_ln([ya.reshape(bp * tp, d_a), yb.reshape(bp * tp, d_b)], [w_out_bf[:d_a], w_out_bf[d_a:]],
                      xp.reshape(bp * tp, d), ln_g, ln_b, alpha).reshape(bp, tp, d)
    w_keep = min(WINDOW, tp)
    outs_p = (cmp_n.reshape(kv_shape(bp, tp)), sel_n.reshape(kv_shape(bp, tp)),
              win_n.reshape(kv_shape(bp, tp))[:, tp - w_keep:], u3[:, tp - (n_taps - 1):])

    u, sza, q, cmp_s, sel_s, win_s, szb, gate = _even_in_proj(xs.reshape(bs, d), w_bf, d_a, d_b, d_kv, kv_h, hd)
    ext = jnp.concatenate([state_conv, u[:, None, :]], axis=1)
    hist = jnp.pad(state_conv, ((0, 0), (CONV_HIST - (n_taps - 1), 0), (0, 0)))
    pad_rows = lambda a: jnp.pad(a[:, None, :], ((0, 0), (0, SUBLANES - 1), (0, 0)))
    ya = _conv_branch(hist, pad_rows(u), pad_rows(sza), cw, conv_b, cln_g, cln_b)[:, 0]
    n_pool, page = cache_cmp.shape[:2]
    w_buf = cache_win.shape[1]
    yb = _nsa_decode(q.astype(F32).reshape(bs, H_B, hd), cache_cmp.reshape(n_pool, page, d_kv),
                     cache_sel.reshape(n_pool, page, d_kv), cache_win.reshape(bs, w_buf, d_kv), page_table,
                     sel_s[:, None, :], win_s[:, None, :], gate[:, None, :], szb.reshape(bs, H_B, hd), wk, wv)
    ys = _out_proj_ln([ya, yb.reshape(bs, d_b)], [w_out_bf[:d_a], w_out_bf[d_a:]], xs.reshape(bs, d),
                      ln_g, ln_b, alpha).reshape(bs, 1, d)
    ctx = jnp.concatenate([cache_win, win_s.reshape(kv_shape(bs, 1))], axis=1)
    outs_s = (cmp_s.reshape(kv_shape(bs, 1)), sel_s.reshape(kv_shape(bs, 1)),
              ctx[:, ctx.shape[1] - min(WINDOW, ctx.shape[1]):], ext[:, 1:])
    return yp, ys, outs_p, outs_s


def _odd_group(x3, shift, s0, mu, w_rkvz, w0, w1, w2, a0, a1, a2, k_k, k_a, r_k, gn_g, gn_b, w_out_bf,
               ln_g, ln_b, alpha):
    bsz, t, d = x3.shape
    x2 = x3.reshape(bsz * t, d)
    r, k, v, lw, a, sz = _rwkv_proj(x2, shift, t, mu, w_rkvz, w0, w1, w2, a0, a1, a2)
    t_pad = -(-t // WKV_CHUNK) * WKV_CHUNK
    r3 = lambda z: jnp.pad(z.reshape(bsz, t, d), ((0, 0), (0, t_pad - t), (0, 0)))
    yz, s_fin = _wkv_scan(r3(r), r3(k), r3(v), r3(lw), r3(a), r3(sz), s0, k_k, k_a, r_k.reshape(-1), gn_g, gn_b)
    y = _out_proj_ln([yz[:, :t].reshape(bsz * t, d)], [w_out_bf], x2, ln_g, ln_b, alpha).reshape(bsz, t, d)
    return y, s_fin, x3[:, -1]


def kernel(x_prompt, x_sample, cache_cmp_kv, cache_sel_kv, cache_win_kv, state_conv, state_wkv, state_shift,
           page_table, w_in_even, conv_w, conv_b, conv_ln_g, conv_ln_b, wk_cmp, wv_cmp, w_out_even, mu_c, w_rkvz,
           w0, w1, w2, a0, a1, a2, k_k, k_a, r_k, gn_g, gn_b, w_out_odd, ln_g, ln_b):
    depth = ln_g.shape[0]
    alpha = (2 * depth) ** 0.25
    bp, _, d = x_prompt.shape
    n_heads = d // HD_C
    xp, xs = x_prompt, x_sample
    even_p, even_s, odd_p, odd_s = [], [], [], []
    for l in range(depth):
        if l % 2 == 0:
            e = l // 2
            xp, xs, o_p, o_s = _even_layer(
                xp, xs, cache_cmp_kv[e], cache_sel_kv[e], cache_win_kv[e], state_conv[e], page_table,
                w_in_even[e], conv_w[e], conv_b[e], conv_ln_g[e], conv_ln_b[e], wk_cmp[e], wv_cmp[e],
                w_out_even[e], ln_g[l], ln_b[l], alpha)
            even_p.append(o_p)
            even_s.append(o_s)
        else:
            o = l // 2
            po = (mu_c[o], w_rkvz[o], w0[o], w1[o], w2[o], a0[o], a1[o], a2[o], k_k[o], k_a[o], r_k[o],
                  gn_g[o], gn_b[o], w_out_odd[o].astype(BF16), ln_g[l], ln_b[l], alpha)
            xp, s_p, h_p = _odd_group(xp, jnp.zeros((bp, d), F32), jnp.zeros((bp, n_heads, HD_C, HD_C), F32), *po)
            xs, s_s, h_s = _odd_group(xs, state_shift[o], state_wkv[o], *po)
            odd_p.append((s_p, h_p))
            odd_s.append((s_s, h_s))
    stack = lambda items, j: jnp.stack([it[j] for it in items])
    return (xp, xs,
            stack(even_p, 0), stack(even_s, 0), stack(even_p, 1), stack(even_s, 1),
            stack(even_p, 2), stack(even_s, 2), stack(even_p, 3), stack(even_s, 3),
            stack(odd_p, 0), stack(odd_s, 0), stack(odd_p, 1), stack(odd_s, 1))
```

```python
import functools

import jax
import jax.numpy as jnp
from jax import lax
from jax.experimental import pallas as pl
from jax.experimental.pallas import tpu as pltpu

F32 = jnp.float32
BF16 = jnp.bfloat16

H_B = 8
L_SEL = 64
K_SEL = 16
WINDOW = 512
FORCE = 1e4
HD_C = 64
GN_EPS = 64e-5
LN_EPS = 1e-5
NEG = -1e30

LANES = 128
SUBLANES = 8
VMEM_LIMIT = 56 * 1024 * 1024

NT = (((1,), (1,)), ((), ()))
TN = (((0,), (0,)), ((), ()))
NN = (((1,), (0,)), ((), ()))


def _cparams(n_axes):
    return pltpu.CompilerParams(dimension_semantics=("arbitrary",) * n_axes,
                                vmem_limit_bytes=VMEM_LIMIT)


def _sigmoid(x):
    return 1.0 / (1.0 + jnp.exp(-x))


def _silu(x):
    return x * _sigmoid(x)


def _split_bf16(x, n):
    parts, rem = [], x
    for i in range(n):
        p = rem.astype(BF16)
        parts.append(p)
        if i + 1 < n:
            rem = rem - p.astype(F32)
    return parts


def _mm(a, b, dims=NN, pa=1, pb=1):
    aa = [a] if a.dtype == BF16 else _split_bf16(a, pa)
    bb = [b] if b.dtype == BF16 else _split_bf16(b, pb)
    keep = max(len(aa), len(bb))
    out = None
    for i, ai in enumerate(aa):
        for j, bj in enumerate(bb):
            if i + j < keep:
                t = lax.dot_general(ai, bj, dims, preferred_element_type=F32)
                out = t if out is None else out + t
    return out


def _layer_norm_rows(h, g, b, eps):
    mu = jnp.mean(h, axis=-1, keepdims=True)
    d = h - mu
    var = jnp.mean(d * d, axis=-1, keepdims=True)
    return d * lax.rsqrt(var + eps) * g + b


def _even_in_proj_kernel(x_ref, w_ref, u_ref, sza_ref, q_ref, cmp_ref, sel_ref, win_ref, szb_ref, gate_ref,
                         *, d_a, d_b, d_kv, q_scale):
    xb = x_ref[...].astype(BF16)

    def seg(lo, n):
        return jnp.dot(xb, w_ref[:, lo:lo + n], preferred_element_type=F32)

    o = 0
    a_val = seg(o, d_a); o += d_a
    a_glu = seg(o, d_a); o += d_a
    u_ref[...] = a_val * _sigmoid(a_glu)
    sza_ref[...] = _silu(seg(o, d_a)); o += d_a
    q_ref[...] = (seg(o, d_b) * q_scale).astype(BF16); o += d_b
    for ref in (cmp_ref, sel_ref, win_ref):
        ref[...] = seg(o, d_kv); o += d_kv
    szb_ref[...] = _silu(seg(o, d_b)); o += d_b
    gate_ref[...] = _sigmoid(seg(o, gate_ref.shape[-1]))


def _prep_w_in(w, d_a, d_b, kv_h, hd):
    grp = H_B // kv_h
    c_kv6 = 3 * d_a + d_b
    c_g3 = c_kv6 + 6 * kv_h * hd
    c_zb = c_g3 + 3 * H_B
    gate_blocks = []
    for g in range(kv_h):
        blk = w[:, c_g3 + g * grp * 3: c_g3 + (g + 1) * grp * 3]
        gate_blocks.append(jnp.pad(blk, ((0, 0), (0, LANES - grp * 3))))
    wn = jnp.concatenate([w[:, :c_g3], w[:, c_zb:c_zb + d_b]] + gate_blocks, axis=1)
    return wn.astype(BF16)


def _even_in_proj(x2, w_bf, d_a, d_b, d_kv, kv_h, hd):
    m, d = x2.shape
    tm = min(512, m)
    assert m % tm == 0
    n_gate = kv_h * LANES
    row = lambda n: pl.BlockSpec((tm, n), lambda i: (i, 0))
    out_shape = (jax.ShapeDtypeStruct((m, d_a), F32), jax.ShapeDtypeStruct((m, d_a), F32),
                 jax.ShapeDtypeStruct((m, d_b), BF16),
                 jax.ShapeDtypeStruct((m, d_kv), F32), jax.ShapeDtypeStruct((m, d_kv), F32),
                 jax.ShapeDtypeStruct((m, d_kv), F32),
                 jax.ShapeDtypeStruct((m, d_b), F32), jax.ShapeDtypeStruct((m, n_gate), F32))
    return pl.pallas_call(
        functools.partial(_even_in_proj_kernel, d_a=d_a, d_b=d_b, d_kv=d_kv, q_scale=hd ** -0.5),
        grid=(m // tm,),
        in_specs=[row(d), pl.BlockSpec(w_bf.shape, lambda i: (0, 0))],
        out_specs=(row(d_a), row(d_a), row(d_b), row(d_kv), row(d_kv), row(d_kv), row(d_b), row(n_gate)),
        out_shape=out_shape,
        compiler_params=_cparams(1),
        name="even_in_proj",
    )(x2, w_bf)


CONV_HIST = 32


def _conv_kernel(hist_ref, u_ref, sza_ref, w_ref, cb_ref, g_ref, b_ref, ya_ref, win_ref, sh_ref,
                 *, tq, rb, n_taps):
    i = pl.program_id(1)
    base = pl.multiple_of(i * tq, SUBLANES)
    win_ref[CONV_HIST:, :] = u_ref[0, pl.ds(base, tq), :]

    @pl.when(i == 0)
    def _():
        win_ref[:CONV_HIST, :] = hist_ref[0]

    @pl.when(i > 0)
    def _():
        win_ref[:CONV_HIST, :] = u_ref[0, pl.ds(base - CONV_HIST, CONV_HIST), :]

    w = win_ref[...]
    n = tq + CONV_HIST
    sh_ref[0] = w
    for s in range(1, SUBLANES):
        sh_ref[s] = pltpu.roll(w, n - s, 0)

    first = CONV_HIST - (n_taps - 1)
    cb = cb_ref[...]
    g = g_ref[...]
    b = b_ref[...]

    def block(k, carry):
        r0 = pl.multiple_of(k * rb, SUBLANES)
        acc = jnp.zeros((rb, u_ref.shape[-1]), F32) + cb
        for j in range(n_taps):
            a, s = divmod(first + j, SUBLANES)
            acc = acc + sh_ref[s, pl.ds(r0 + SUBLANES * a, rb), :] * w_ref[j:j + 1, :]
        y = _silu(_layer_norm_rows(acc, g, b, LN_EPS)) * sza_ref[0, pl.ds(r0, rb), :]
        ya_ref[0, pl.ds(r0, rb), :] = y.astype(ya_ref.dtype)
        return carry

    lax.fori_loop(0, tq // rb, block, 0)


def _conv_branch(hist, u3, sza3, conv_w, conv_b, ln_g, ln_b):
    bsz, t, d = u3.shape
    n_taps = conv_w.shape[0]
    assert n_taps - 1 <= CONV_HIST and t % SUBLANES == 0
    tq = min(256, t)
    rb = min(32, tq)
    assert t % tq == 0 and tq % rb == 0
    vec = lambda a: a.reshape(1, d)
    full = lambda shp: pl.BlockSpec(shp, lambda b, i: (0,) * len(shp))
    return pl.pallas_call(
        functools.partial(_conv_kernel, tq=tq, rb=rb, n_taps=n_taps),
        grid=(bsz, t // tq),
        in_specs=[pl.BlockSpec((1, CONV_HIST, d), lambda b, i: (b, 0, 0)),
                  pl.BlockSpec((1, t, d), lambda b, i: (b, 0, 0)),
                  pl.BlockSpec((1, tq, d), lambda b, i: (b, i, 0)),
                  full((n_taps, d)), full((1, d)), full((1, d)), full((1, d))],
        out_specs=pl.BlockSpec((1, tq, d), lambda b, i: (b, i, 0)),
        out_shape=jax.ShapeDtypeStruct((bsz, t, d), BF16),
        scratch_shapes=[pltpu.VMEM((tq + CONV_HIST, d), F32),
                        pltpu.VMEM((SUBLANES, tq + CONV_HIST, d), F32)],
        compiler_params=_cparams(2),
        name="conv_branch",
    )(hist, u3, sza3, conv_w, vec(conv_b), vec(ln_g), vec(ln_b))


SEL_OFF = -(2.0 ** 100)


def _nsa_prompt_kernel(q_ref, kcmp_ref, ksel_ref, kwin_ref, gate_ref, szb_ref, wpool_ref, o_ref,
                       kc_s, vc_s, ks_s, vs_s, kw_s, vw_s, m_s, l_s, acc_s,
                       *, tq, t_len, l_cmp, hd, grp, kv_h):
    i = pl.program_id(1)
    g = pl.program_id(2)
    n_cmp = t_len // l_cmp
    n_sel = t_len // L_SEL
    half = n_cmp // 2
    gw = kv_h * hd

    @pl.when((i == 0) & (g == 0))
    def _prep():
        x3 = kcmp_ref[0].reshape(half, 2 * l_cmp, 2 * gw)
        pooled = jnp.concatenate([jnp.sum(x3 * wpool_ref[0][None], axis=1),
                                  jnp.sum(x3 * wpool_ref[1][None], axis=1)], axis=0)
        pos = lax.broadcasted_iota(jnp.int32, (t_len, LANES), 0)
        lane = lax.broadcasted_iota(jnp.int32, (t_len, LANES), 1)
        blk = pos // L_SEL
        feat = jnp.where(lane == blk, 1.0, 0.0)
        feat = jnp.where(lane == n_sel, blk.astype(F32), feat)
        feat = jnp.where(lane == n_sel + 1, (pos - blk * L_SEL).astype(F32), feat)
        for src, dst in ((ksel_ref, ks_s), (kwin_ref, kw_s)):
            k01 = src[0, :, :gw]
            dst[0] = jnp.where(lane < hd, feat, pltpu.roll(k01, hd, 1)).astype(BF16)
            dst[1] = jnp.where(lane < hd, feat, k01).astype(BF16)
        vc_pad = jnp.concatenate([pooled[:, gw:], jnp.zeros((LANES - n_cmp, gw), F32)], axis=0) \
            if n_cmp < LANES else pooled[:, gw:]
        vc_t = vc_pad.T
        vs_t = ksel_ref[0, :, gw:].T
        vw_t = kwin_ref[0, :, gw:].T
        for gg in range(kv_h):
            kc_s[gg] = pooled[:, gg * hd:(gg + 1) * hd].astype(BF16)
            vc_s[gg] = vc_t[gg * hd:(gg + 1) * hd, :n_cmp].astype(BF16)
            vs_s[gg] = vs_t[gg * hd:(gg + 1) * hd].astype(BF16)
            vw_s[gg] = vw_t[gg * hd:(gg + 1) * hd].astype(BF16)

    t0 = i * tq
    gs = jnp.float32(1.0)
    for gg in range(1, kv_h):
        gs = jnp.where(g == gg, 2.0 ** -(gg * grp), gs)
    slopes = [gs * (2.0 ** -(r + 1)) for r in range(grp)]
    qt = q_ref[0]
    qs = jnp.concatenate([qt[:, r * hd:(r + 1) * hd] for r in range(grp)], axis=0)

    s_t = lax.dot_general(kc_s[g], qs, NT, preferred_element_type=F32)
    n_io = lax.broadcasted_iota(jnp.int32, (n_cmp, tq), 0)
    t_io = lax.broadcasted_iota(jnp.int32, (n_cmp, tq), 1) + t0
    c_end = jnp.where(n_io < half, 2 * l_cmp * n_io + (l_cmp - 1), 2 * l_cmp * (n_io - half) + (2 * l_cmp - 1))
    dist_c = t_io - c_end
    valid_c = dist_c >= 0
    dist_cf = dist_c.astype(F32)
    imp = jnp.zeros((n_sel, tq), F32)
    o_c = []
    for r in range(grp):
        s = jnp.where(valid_c, s_t[:, r * tq:(r + 1) * tq] - slopes[r] * dist_cf, NEG)
        e = jnp.exp(s - jnp.max(s, axis=0, keepdims=True))
        p = jnp.where(valid_c, e / jnp.sum(e, axis=0, keepdims=True), 0.0)
        imp = imp + (p[:half] + p[half:])
        o_c.append(jnp.dot(vc_s[g], p.astype(BF16), preferred_element_type=F32))

    sb = lax.broadcasted_iota(jnp.int32, (n_sel, tq), 0)
    tb = (lax.broadcasted_iota(jnp.int32, (n_sel, tq), 1) + t0) // L_SEL
    forced = (sb == 0) | (sb == tb) | (sb == tb - 1)
    imp = jnp.where(forced, FORCE, imp)
    imp = jnp.where(sb <= tb, imp, -jnp.inf)
    rank = jnp.zeros((n_sel, tq), F32)
    for j in range(n_sel):
        vj = imp[j:j + 1, :]
        beats = jnp.where(imp > vj, 1.0, jnp.where((imp == vj) & (sb < j), 1.0, 0.0))
        rank = jnp.where(sb == j, jnp.sum(beats, axis=0, keepdims=True), rank)
    off_t = jnp.where(rank < min(K_SEL, n_sel), 0.0, SEL_OFF)
    feat_sel = jnp.concatenate([off_t, jnp.zeros((LANES - n_sel, tq), F32)], axis=0).T

    qf = q_ref[0].astype(F32)
    lane_q = lax.broadcasted_iota(jnp.int32, (tq, LANES), 1)

    def q_aug(r, feat):
        tile = qf[:, (r // 2) * LANES:(r // 2 + 1) * LANES]
        if r % 2 == 0:
            tile = pltpu.roll(tile, hd, 1)
        x = jnp.where(lane_q < hd, feat, tile)
        x = jnp.where(lane_q == n_sel, slopes[r] * L_SEL, x)
        return jnp.where(lane_q == n_sel + 1, slopes[r], x).astype(BF16)

    d0 = (lax.broadcasted_iota(jnp.int32, (tq, tq), 1) - lax.broadcasted_iota(jnp.int32, (tq, tq), 0))

    def reset():
        m_s[...] = jnp.full(m_s.shape, NEG, F32)
        l_s[...] = jnp.zeros(l_s.shape, F32)
        acc_s[...] = jnp.zeros(acc_s.shape, F32)

    def tile_update(q_list, k_s, v_s, kt, mask):
        k0 = pl.multiple_of(kt * tq, tq)
        kk = k_s[g, pl.ds(k0, tq), :]
        vt = v_s[g, :, pl.ds(k0, tq)]
        heads = range(grp)
        s = [lax.dot_general(kk, q_list[r], NT, preferred_element_type=F32) for r in heads]
        if mask is not None:
            s = [jnp.where(mask, s[r], NEG) for r in heads]
        m_old = [m_s[r] for r in heads]
        m_new = [jnp.maximum(m_old[r], jnp.max(s[r], axis=0, keepdims=True)) for r in heads]
        al = [jnp.exp(m_old[r] - m_new[r]) for r in heads]
        p = [jnp.exp(s[r] - m_new[r]) for r in heads]
        for r in heads:
            m_s[r] = m_new[r]
            l_s[r] = al[r] * l_s[r] + jnp.sum(p[r], axis=0, keepdims=True)
            acc_s[r] = al[r] * acc_s[r] + jnp.dot(vt, p[r].astype(BF16), preferred_element_type=F32)

    q_sel = [q_aug(r, feat_sel) for r in range(grp)]
    reset()

    def sel_body(kt, carry):
        tile_update(q_sel, ks_s, vs_s, kt, None)
        return carry

    lax.fori_loop(0, i, sel_body, 0)
    tile_update(q_sel, ks_s, vs_s, i, d0 >= 0)
    o_s = [acc_s[r] / l_s[r] for r in range(grp)]

    q_win = [q_aug(r, 0.0) for r in range(grp)]
    reset()
    for j in range(-(-(WINDOW - 1) // tq), 0, -1):
        d_j = d0 + j * tq
        inside = j * tq - (tq - 1) >= 0 and j * tq + (tq - 1) < WINDOW
        mask_j = None if inside else (d_j >= 0) & (d_j < WINDOW)

        @pl.when(i >= j)
        def _(j=j, mask_j=mask_j):
            tile_update(q_win, kw_s, vw_s, i - j, mask_j)

    tile_update(q_win, kw_s, vw_s, i, (d0 >= 0) & (d0 < WINDOW))
    o_w = [acc_s[r] / l_s[r] for r in range(grp)]

    gate_t = gate_ref[0].T
    outs = []
    for r in range(grp):
        outs.append(gate_t[3 * r:3 * r + 1] * o_c[r]
                    + gate_t[3 * r + 1:3 * r + 2] * o_s[r]
                    + gate_t[3 * r + 2:3 * r + 3] * o_w[r])
    o_ref[0] = (jnp.concatenate(outs, axis=0).T * szb_ref[0]).astype(o_ref.dtype)


def _pool_weights(wk, wv, hd, halves):
    l_cmp, kv_h = wk.shape
    row = jnp.concatenate([jnp.repeat(wk, hd, axis=1), jnp.repeat(wv, hd, axis=1)], axis=1)
    if not halves:
        return row
    z = jnp.zeros_like(row)
    return jnp.stack([jnp.concatenate([row, z], axis=0), jnp.concatenate([z, row], axis=0)])


def _nsa_prompt(q3, cmp3, sel3, win3, gate3, szb3, wk, wv):
    bsz, t, d_b = q3.shape
    l_cmp, kv_h = wk.shape
    grp = H_B // kv_h
    hd = d_b // H_B
    gw = kv_h * hd
    tq = min(256, t)
    assert t % tq == 0 and t % (2 * l_cmp) == 0 and L_SEL == 2 * l_cmp and tq % L_SEL == 0
    n_cmp = t // l_cmp
    n_sel = t // L_SEL
    assert gw == LANES and 2 * hd == LANES and n_sel + 2 <= hd and n_sel % SUBLANES == 0 and n_cmp <= LANES
    wpool = _pool_weights(wk, wv, hd, halves=True)
    kv_spec = pl.BlockSpec((1, t, 2 * gw), lambda b, i, g: (b, 0, 0))
    return pl.pallas_call(
        functools.partial(_nsa_prompt_kernel, tq=tq, t_len=t, l_cmp=l_cmp, hd=hd, grp=grp, kv_h=kv_h),
        grid=(bsz, t // tq, kv_h),
        in_specs=[pl.BlockSpec((1, tq, grp * hd), lambda b, i, g: (b, i, g)),
                  kv_spec, kv_spec, kv_spec,
                  pl.BlockSpec((1, tq, LANES), lambda b, i, g: (b, i, g)),
                  pl.BlockSpec((1, tq, grp * hd), lambda b, i, g: (b, i, g)),
                  pl.BlockSpec(wpool.shape, lambda b, i, g: (0, 0, 0))],
        out_specs=pl.BlockSpec((1, tq, grp * hd), lambda b, i, g: (b, i, g)),
        out_shape=jax.ShapeDtypeStruct((bsz, t, d_b), BF16),
        scratch_shapes=[pltpu.VMEM((kv_h, n_cmp, hd), BF16), pltpu.VMEM((kv_h, hd, n_cmp), BF16),
                        pltpu.VMEM((kv_h, t, LANES), BF16), pltpu.VMEM((kv_h, hd, t), BF16),
                        pltpu.VMEM((kv_h, t, LANES), BF16), pltpu.VMEM((kv_h, hd, t), BF16),
                        pltpu.VMEM((grp, 1, tq), F32), pltpu.VMEM((grp, 1, tq), F32),
                        pltpu.VMEM((grp, hd, tq), F32)],
        compiler_params=_cparams(3),
        name="nsa_prompt",
    )(q3, cmp3, sel3, win3, gate3, szb3, wpool)


def _nsa_decode_kernel(pt_ref, *refs, pg, n_pages, page, l_cmp, hd, grp, kv_h, w_buf):
    del pt_ref
    q_ref = refs[0]
    cmp_refs = refs[1:1 + pg]
    sel_refs = refs[1 + pg:1 + 2 * pg]
    (kwin_ref, seln_ref, winn_ref, gate_ref, szb_ref, wtile_ref, o_ref,
     kcv_s, selm_s, oc_s, m_s, l_s, acc_s) = refs[1 + 2 * pg:]
    p = pl.program_id(1)
    n_steps = n_pages // pg
    past = n_pages * page
    n_cmp = past // l_cmp
    n_sel = past // L_SEL
    per_page = page // l_cmp
    gw = kv_h * hd
    nc_pad = selm_s.shape[-1]
    qf = q_ref[0]
    q = qf.astype(BF16)
    head = lax.broadcasted_iota(jnp.int32, (H_B, 1), 0)
    slope = jnp.zeros((H_B, 1), F32)
    for hh in range(H_B):
        slope = jnp.where(head == hh, 2.0 ** -(hh + 1), slope)

    @pl.when(p < n_steps)
    def _pool():
        rows = []
        for j in range(pg):
            x3 = (cmp_refs[j][0] * wtile_ref[...]).reshape(per_page, l_cmp, 2 * gw)
            rows.append(jnp.sum(x3, axis=1))
        kcv_s[pl.ds(pl.multiple_of(p * (pg * per_page), SUBLANES), pg * per_page), :] = jnp.concatenate(rows, axis=0)

    @pl.when(p == n_steps - 1)
    def _compressed():
        kcv = kcv_s[...]
        n_io = lax.broadcasted_iota(jnp.int32, (1, n_cmp), 1)
        dist = (past - (l_cmp * n_io + (l_cmp - 1))).astype(F32)
        pair = jnp.where(lax.broadcasted_iota(jnp.int32, (n_cmp, nc_pad), 0) // (L_SEL // l_cmp)
                         == lax.broadcasted_iota(jnp.int32, (n_cmp, nc_pad), 1), 1.0, 0.0).astype(BF16)
        c_io = lax.broadcasted_iota(jnp.int32, (1, nc_pad), 1)
        i_r = lax.broadcasted_iota(jnp.int32, (nc_pad, nc_pad), 1)
        i_c = lax.broadcasted_iota(jnp.int32, (nc_pad, nc_pad), 0)
        selm_s[...] = jnp.zeros(selm_s.shape, F32)
        for g in range(kv_h):
            kc = kcv[:, g * hd:(g + 1) * hd].astype(BF16)
            vc = kcv[:, gw + g * hd: gw + (g + 1) * hd].astype(BF16)
            s = lax.dot_general(q, kc, NT, preferred_element_type=F32) - slope * dist
            e = jnp.exp(s - jnp.max(s, axis=-1, keepdims=True))
            pc = e / jnp.sum(e, axis=-1, keepdims=True)
            oc_s[g] = jnp.dot(pc.astype(BF16), vc, preferred_element_type=F32)
            in_grp = (head >= g * grp) & (head < (g + 1) * grp)
            imp = _mm(jnp.sum(jnp.where(in_grp, pc, 0.0), axis=0, keepdims=True), pair, NN, pa=3)
            forced = (c_io == 0) | (c_io == n_sel - 1)
            imp = jnp.where(forced, FORCE, imp)
            imp = jnp.where(c_io < n_sel, imp, -jnp.inf)
            v_r = jnp.broadcast_to(imp, (nc_pad, nc_pad))
            v_c = v_r.T
            beats = jnp.where(v_c > v_r, 1.0, jnp.where((v_c == v_r) & (i_c < i_r), 1.0, 0.0))
            rank = jnp.sum(beats, axis=0, keepdims=True) + jnp.where(imp < FORCE, 1.0, 0.0)
            chosen = (rank < min(K_SEL, n_sel + 1)) & (c_io < n_sel)
            selm_s[g:g + 1, :] = jnp.where(chosen, 1.0, 0.0)
        m_s[...] = jnp.full(m_s.shape, NEG, F32)
        l_s[...] = jnp.zeros(l_s.shape, F32)
        acc_s[...] = jnp.zeros(acc_s.shape, F32)

    @pl.when(p >= n_steps)
    def _selected():
        step = p - n_steps
        nk = pg * page
        x = jnp.concatenate([sel_refs[j][0] for j in range(pg)], axis=0)
        kpos = lax.broadcasted_iota(jnp.int32, (1, nk), 1) + step * nk
        dist = (past - kpos).astype(F32)
        expand = jnp.where(lax.broadcasted_iota(jnp.int32, (nc_pad, nk), 0)
                           == (lax.broadcasted_iota(jnp.int32, (nc_pad, nk), 1) + step * nk) // L_SEL,
                           1.0, 0.0).astype(BF16)
        chosen = jnp.dot(selm_s[...].astype(BF16), expand, preferred_element_type=F32)
        for g in range(kv_h):
            kk = x[:, g * hd:(g + 1) * hd].astype(BF16)
            vv = x[:, gw + g * hd: gw + (g + 1) * hd].astype(BF16)
            s = lax.dot_general(q, kk, NT, preferred_element_type=F32) - slope * dist
            s = jnp.where(chosen[g:g + 1, :] > 0.5, s, NEG)
            m_old = m_s[g]
            m_new = jnp.maximum(m_old, jnp.max(s, axis=-1, keepdims=True))
            al = jnp.exp(m_old - m_new)
            pe = jnp.exp(s - m_new)
            m_s[g] = m_new
            l_s[g] = al * l_s[g] + jnp.sum(pe, axis=-1, keepdims=True)
            acc_s[g] = al * acc_s[g] + jnp.dot(pe.astype(BF16), vv, preferred_element_type=F32)

    @pl.when(p == 2 * n_steps - 1)
    def _finish():
        gate = gate_ref[0]
        xw = kwin_ref[0]
        i_io = lax.broadcasted_iota(jnp.int32, (1, w_buf), 1)
        dist_w = w_buf - i_io
        valid_w = dist_w < WINDOW
        o = jnp.zeros((H_B, hd), F32)
        for g in range(kv_h):
            kn = seln_ref[0][:, g * hd:(g + 1) * hd]
            vn = seln_ref[0][:, gw + g * hd: gw + (g + 1) * hd]
            s_n = jnp.sum(qf * kn, axis=-1, keepdims=True)
            m_old = m_s[g]
            m_new = jnp.maximum(m_old, s_n)
            al = jnp.exp(m_old - m_new)
            pn = jnp.exp(s_n - m_new)
            o_s = (al * acc_s[g] + pn * vn) / (al * l_s[g] + pn)
            kw = xw[:, g * hd:(g + 1) * hd].astype(BF16)
            vw = xw[:, gw + g * hd: gw + (g + 1) * hd].astype(BF16)
            s = lax.dot_general(q, kw, NT, preferred_element_type=F32) - slope * dist_w.astype(F32)
            s = jnp.where(valid_w, s, NEG)
            kwn = winn_ref[0][:, g * hd:(g + 1) * hd]
            vwn = winn_ref[0][:, gw + g * hd: gw + (g + 1) * hd]
            s_wn = jnp.sum(qf * kwn, axis=-1, keepdims=True)
            mw = jnp.maximum(jnp.max(s, axis=-1, keepdims=True), s_wn)
            pw = jnp.where(valid_w, jnp.exp(s - mw), 0.0)
            pwn = jnp.exp(s_wn - mw)
            o_w = ((jnp.dot(pw.astype(BF16), vw, preferred_element_type=F32) + pwn * vwn)
                   / (jnp.sum(pw, axis=-1, keepdims=True) + pwn))
            gcol = []
            for c in range(3):
                col = jnp.zeros((H_B, 1), F32)
                for r in range(grp):
                    lane = g * LANES + 3 * r + c
                    col = jnp.where(head == g * grp + r, gate[:, lane:lane + 1], col)
                gcol.append(col)
            og = gcol[0] * oc_s[g] + gcol[1] * o_s + gcol[2] * o_w
            o = jnp.where((head >= g * grp) & (head < (g + 1) * grp), og, o)
        o_ref[0] = o * szb_ref[0]


def _nsa_decode(q3, cache_cmp, cache_sel, cache_win, page_table, sel_new, win_new, gate, szb3, wk, wv):
    bs, _, hd = q3.shape
    l_cmp, kv_h = wk.shape
    grp = H_B // kv_h
    gw = kv_h * hd
    _, page, _ = cache_cmp.shape
    n_pages = page_table.shape[1]
    w_buf = cache_win.shape[1]
    pg = 8 if n_pages % 8 == 0 else 1
    n_steps = n_pages // pg
    per_page = page // l_cmp
    assert page % L_SEL == 0 and page % l_cmp == 0 and (pg * per_page) % SUBLANES == 0
    n_sel = n_pages * page // L_SEL
    nc_pad = -(-n_sel // LANES) * LANES
    wtile = jnp.tile(_pool_weights(wk, wv, hd, halves=False), (per_page, 1))

    def cmp_map(j):
        return lambda b, p, pt: (pt[b, jnp.minimum(p, n_steps - 1) * pg + j], 0, 0)

    def sel_map(j):
        return lambda b, p, pt: (pt[b, jnp.maximum(p - n_steps, 0) * pg + j], 0, 0)

    per_b = lambda shp: pl.BlockSpec((1,) + shp, lambda b, p, pt: (b, 0, 0))
    grid_spec = pltpu.PrefetchScalarGridSpec(
        num_scalar_prefetch=1,
        grid=(bs, 2 * n_steps),
        in_specs=([per_b((H_B, hd))]
                  + [pl.BlockSpec((1, page, 2 * gw), cmp_map(j)) for j in range(pg)]
                  + [pl.BlockSpec((1, page, 2 * gw), sel_map(j)) for j in range(pg)]
                  + [per_b((w_buf, 2 * gw)), per_b((1, 2 * gw)), per_b((1, 2 * gw)),
                     per_b((1, kv_h * LANES)), per_b((H_B, hd)),
                     pl.BlockSpec((page, 2 * gw), lambda b, p, pt: (0, 0))]),
        out_specs=per_b((H_B, hd)),
        scratch_shapes=[pltpu.VMEM((n_pages * per_page, 2 * gw), F32),
                        pltpu.VMEM((SUBLANES, nc_pad), F32),
                        pltpu.VMEM((kv_h, H_B, hd), F32), pltpu.VMEM((kv_h, H_B, 1), F32),
                        pltpu.VMEM((kv_h, H_B, 1), F32), pltpu.VMEM((kv_h, H_B, hd), F32)])
    return pl.pallas_call(
        functools.partial(_nsa_decode_kernel, pg=pg, n_pages=n_pages, page=page, l_cmp=l_cmp, hd=hd, grp=grp,
                          kv_h=kv_h, w_buf=w_buf),
        grid_spec=grid_spec,
        out_shape=jax.ShapeDtypeStruct((bs, H_B, hd), F32),
        compiler_params=_cparams(2),
        name="nsa_decode",
    )(page_table, q3, *([cache_cmp] * pg), *([cache_sel] * pg), cache_win, sel_new, win_new, gate, szb3, wtile)


def _out_proj_ln_kernel(*refs, n_in, alpha):
    a_refs, w_refs = refs[:n_in], refs[n_in:2 * n_in]
    x_ref, g_ref, b_ref, o_ref = refs[2 * n_in:]
    y = None
    for a_ref, w_ref in zip(a_refs, w_refs):
        t = jnp.dot(a_ref[...].astype(BF16), w_ref[...], preferred_element_type=F32)
        y = t if y is None else y + t
    o_ref[...] = _layer_norm_rows(alpha * x_ref[...] + y, g_ref[...], b_ref[...], LN_EPS)


def _out_proj_ln(a_list, w_list, x2, ln_g, ln_b, alpha):
    m, d = x2.shape
    tm = min(512, m)
    assert m % tm == 0
    n_in = len(a_list)
    return pl.pallas_call(
        functools.partial(_out_proj_ln_kernel, n_in=n_in, alpha=alpha),
        grid=(m // tm,),
        in_specs=([pl.BlockSpec((tm, a.shape[1]), lambda i: (i, 0)) for a in a_list]
                  + [pl.BlockSpec(w.shape, lambda i: (0, 0)) for w in w_list]
                  + [pl.BlockSpec((tm, d), lambda i: (i, 0)),
                     pl.BlockSpec((1, d), lambda i: (0, 0)), pl.BlockSpec((1, d), lambda i: (0, 0))]),
        out_specs=pl.BlockSpec((tm, d), lambda i: (i, 0)),
        out_shape=jax.ShapeDtypeStruct((m, d), F32),
        compiler_params=_cparams(1),
        name="out_proj_ln",
    )(*a_list, *w_list, x2, ln_g.reshape(1, d), ln_b.reshape(1, d))


def _rwkv_proj_kernel(*refs, seq, tiles_per_seq):
    if seq:
        x_ref, tail_ref, shift_ref = refs[:3]
        rest = refs[3:]
    else:
        x_ref, shift_ref = refs[:2]
        rest = refs[2:]
    (mu_ref, w_ref, w1_ref, w2_ref, a1_ref, a2_ref, w0_ref, a0_ref,
     r_ref, k_ref, v_ref, lw_ref, a_ref, sz_ref) = rest
    x = x_ref[...]
    if seq:
        i = pl.program_id(0)
        first = jnp.where(i % tiles_per_seq == 0, shift_ref[0], tail_ref[SUBLANES - 1:SUBLANES, :])
        row = lax.broadcasted_iota(jnp.int32, x.shape, 0)
        x_prev = jnp.where(row == 0, first, pltpu.roll(x, 1, 0))
    else:
        x_prev = shift_ref[...]
    dx = x_prev - x
    mix = lambda n: (x + dx * mu_ref[n:n + 1, :]).astype(BF16)
    r_ref[...] = jnp.dot(mix(0), w_ref[0], preferred_element_type=F32)
    k_ref[...] = jnp.dot(mix(1), w_ref[1], preferred_element_type=F32)
    v_ref[...] = jnp.dot(mix(2), w_ref[2], preferred_element_type=F32)
    sz_ref[...] = _silu(jnp.dot(mix(3), w_ref[3], preferred_element_type=F32))
    hw = jnp.tanh(jnp.dot(mix(4), w1_ref[...], preferred_element_type=F32)).astype(BF16)
    y = -(w0_ref[...] + jnp.dot(hw, w2_ref[...], preferred_element_type=F32))
    softplus = jnp.maximum(y, 0.0) + jnp.log(1.0 + jnp.exp(-jnp.abs(y)))
    lw_ref[...] = -jnp.exp(-softplus - 0.5)
    ha = jnp.dot(mix(5), a1_ref[...], preferred_element_type=F32).astype(BF16)
    a_ref[...] = _sigmoid(a0_ref[...] + jnp.dot(ha, a2_ref[...], preferred_element_type=F32))


def _rwkv_proj(x2, shift, t_len, mu, w_rkvz, w0, w1, w2, a0, a1, a2):
    m, d = x2.shape
    seq = t_len > 1
    tm = min(256, t_len) if seq else m
    assert m % tm == 0 and (not seq or (t_len % tm == 0 and tm % SUBLANES == 0))
    tiles_per_seq = t_len // tm if seq else 1
    full = lambda a: pl.BlockSpec(a.shape, lambda i: (0,) * a.ndim)
    row = pl.BlockSpec((tm, d), lambda i: (i, 0))
    if seq:
        blk = tm // SUBLANES
        lead = [x2, x2, shift.reshape(-1, 1, d)]
        lead_specs = [row, pl.BlockSpec((SUBLANES, d), lambda i: (jnp.maximum(i * blk - 1, 0), 0)),
                      pl.BlockSpec((1, 1, d), lambda i: (i // tiles_per_seq, 0, 0))]
    else:
        lead = [x2, shift]
        lead_specs = [row, row]
    ws = [mu, w_rkvz.astype(BF16), w1.astype(BF16), w2.astype(BF16), a1.astype(BF16), a2.astype(BF16),
          w0.reshape(1, d), a0.reshape(1, d)]
    return pl.pallas_call(
        functools.partial(_rwkv_proj_kernel, seq=seq, tiles_per_seq=tiles_per_seq),
        grid=(m // tm,),
        in_specs=lead_specs + [full(a) for a in ws],
        out_specs=(row,) * 6,
        out_shape=(jax.ShapeDtypeStruct((m, d), F32),) * 6,
        compiler_params=_cparams(1),
        name="rwkv_proj",
    )(*lead, *ws)


WKV_CHUNK = 64
WKV_PASSES = 1


def _wkv_kernel(r_ref, k_ref, v_ref, lw_ref, a_ref, sz_ref, s0_ref, kk_ref, ka_ref, rk_ref, gg_ref, gb_ref,
                yz_ref, sfin_ref, h_s, *, c_len, n_real, n_heads, hd):
    c = pl.program_id(1)
    ps = WKV_PASSES
    mm = functools.partial(_mm, pa=ps, pb=ps)

    @pl.when(c == 0)
    def _():
        for h in range(n_heads):
            h_s[h] = s0_ref[0, h].T

    r = r_ref[0]
    k = k_ref[0]
    v = v_ref[0]
    lw = lw_ref[0]
    a = a_ref[0]
    row = lax.broadcasted_iota(jnp.int32, (c_len, c_len), 0)
    col = lax.broadcasted_iota(jnp.int32, (c_len, c_len), 1)
    low_inc = row >= col
    low_exc = row > col
    eye = jnp.where(row == col, 1.0, 0.0)
    cum = _mm(jnp.where(low_inc, 1.0, 0.0).astype(BF16), lw, NN, pb=3)
    p_inc = jnp.exp(cum)
    p_exc = jnp.exp(cum - lw)
    p_inv = jnp.exp(-cum)
    cum_end = cum[c_len - 1:c_len, :]
    p_rest = jnp.exp(cum_end - cum)
    p_end = jnp.exp(cum_end)
    kk = k * kk_ref[...]
    kmod = k * (1.0 + (a - 1.0) * ka_ref[...])
    r_t = r * p_inc
    k_t = kmod * p_inv
    k_e = kmod * p_rest
    rkr = r * kmod * rk_ref[...]
    hr = lax.broadcasted_iota(jnp.int32, (hd, hd), 0)
    hc = lax.broadcasted_iota(jnp.int32, (hd, hd), 1)
    levels = max((n_real - 1).bit_length() - 1, 0)

    heads = range(n_heads)
    hsl = [slice(h * hd, (h + 1) * hd) for h in heads]
    ka_h, lhs, rhs, vh = [], [], [], []
    for h in heads:
        hs = hsl[h]
        kkh = kk[:, hs]
        kp = kkh / jnp.maximum(jnp.sqrt(jnp.sum(kkh * kkh, axis=-1, keepdims=True)), 1e-12)
        ka_h.append(kp * a[:, hs])
        lhs.append(jnp.concatenate([kp * p_exc[:, hs], r_t[:, hs]], axis=0).astype(BF16))
        rhs.append(jnp.concatenate([ka_h[h] * p_inv[:, hs], k_t[:, hs]], axis=0).astype(BF16))
        vh.append(v[:, hs])
    h_in = [h_s[h] for h in heads]
    am = [mm(lhs[h], rhs[h], NT) for h in heads]
    gh = [mm(lhs[h], h_in[h], NN) for h in heads]
    l_b = [jnp.where(low_exc, am[h][:c_len, :c_len], 0.0) for h in heads]
    l_k = [jnp.where(low_exc, am[h][:c_len, c_len:], 0.0).astype(BF16) for h in heads]
    m_r = [jnp.concatenate([jnp.where(low_inc, am[h][c_len:, :c_len], 0.0),
                            jnp.where(low_inc, am[h][c_len:, c_len:], 0.0)], axis=1).astype(BF16)
           for h in heads]
    vb = [vh[h].astype(BF16) for h in heads]
    rhs_u = [gh[h][:c_len] + mm(l_k[h], vb[h], NN) for h in heads]
    t_inv = [eye - l_b[h] for h in heads]
    pw = l_b
    for _ in range(levels):
        pw = [mm(pw[h], pw[h], NN) for h in heads]
        t_inv = [t_inv[h] + mm(t_inv[h], pw[h], NN) for h in heads]
    u = [-mm(t_inv[h], rhs_u[h], NN) for h in heads]
    uv = [jnp.concatenate([u[h].astype(BF16), vb[h]], axis=0) for h in heads]
    y = [gh[h][c_len:] + mm(m_r[h], uv[h], NN) for h in heads]
    for h in heads:
        hs = hsl[h]
        lhs2 = jnp.concatenate([ka_h[h] * p_rest[:, hs], k_e[:, hs], jnp.where(hr == hc, p_end[:, hs], 0.0)],
                               axis=0)
        h_s[h] = mm(lhs2, jnp.concatenate([uv[h], h_in[h].astype(BF16)], axis=0), TN)
    outs = []
    for h in heads:
        hs = hsl[h]
        mu = jnp.mean(y[h], axis=-1, keepdims=True)
        dy = y[h] - mu
        var = jnp.mean(dy * dy, axis=-1, keepdims=True)
        yn = dy * lax.rsqrt(var + GN_EPS) * gg_ref[:, hs] + gb_ref[:, hs]
        outs.append(yn + jnp.sum(rkr[:, hs], axis=-1, keepdims=True) * vh[h])
    yz_ref[0] = (jnp.concatenate(outs, axis=1) * sz_ref[0]).astype(yz_ref.dtype)

    @pl.when(c == pl.num_programs(1) - 1)
    def _():
        for h in range(n_heads):
            sfin_ref[0, h] = h_s[h].T


def _wkv_scan(r3, k3, v3, lw3, a3, sz3, s0, k_k, k_a, r_k, gn_g, gn_b, c_len, t_real):
    bsz, t, d = r3.shape
    n_heads = d // HD_C
    assert t % c_len == 0
    vec = lambda x: x.reshape(1, d)
    seq = pl.BlockSpec((1, c_len, d), lambda b, c: (b, c, 0))
    st = pl.BlockSpec((1, n_heads, HD_C, HD_C), lambda b, c: (b, 0, 0, 0))
    par = pl.BlockSpec((1, d), lambda b, c: (0, 0))
    return pl.pallas_call(
        functools.partial(_wkv_kernel, c_len=c_len, n_real=min(t_real, c_len), n_heads=n_heads, hd=HD_C),
        grid=(bsz, t // c_len),
        in_specs=[seq] * 6 + [st] + [par] * 5,
        out_specs=(seq, st),
        out_shape=(jax.ShapeDtypeStruct((bsz, t, d), BF16),
                   jax.ShapeDtypeStruct((bsz, n_heads, HD_C, HD_C), F32)),
        scratch_shapes=[pltpu.VMEM((n_heads, HD_C, HD_C), F32)],
        compiler_params=_cparams(2),
        name="wkv_scan",
    )(r3, k3, v3, lw3, a3, sz3, s0, vec(k_k), vec(k_a), vec(r_k), vec(gn_g), vec(gn_b))


def _even_layer(xp, xs, cache_cmp, cache_sel, cache_win, state_conv, page_table,
                w_in, conv_w, conv_b, cln_g, cln_b, wk, wv, w_out, ln_g, ln_b, alpha):
    bp, tp, d = xp.shape
    bs, ts, _ = xs.shape
    assert ts == 1
    d_a = conv_w.shape[-1]
    n_taps = conv_w.shape[0]
    l_cmp, kv_h = wk.shape
    hd = cache_cmp.shape[-1]
    d_b = H_B * hd
    d_kv = 2 * kv_h * hd
    w_bf = _prep_w_in(w_in, d_a, d_b, kv_h, hd)
    cw = conv_w.reshape(n_taps, d_a)
    w_out_bf = w_out.astype(BF16)
    kv_shape = lambda b, t: (b, t, 2, kv_h, hd)

    u, sza, q, cmp_n, sel_n, win_n, szb, gate = _even_in_proj(xp.reshape(bp * tp, d), w_bf, d_a, d_b, d_kv, kv_h, hd)
    r3 = lambda a: a.reshape(bp, tp, a.shape[-1])
    u3 = r3(u)
    ya = _conv_branch(jnp.zeros((bp, CONV_HIST, d_a), F32), u3, r3(sza), cw, conv_b, cln_g, cln_b)
    yb = _nsa_prompt(r3(q), r3(cmp_n), r3(sel_n), r3(win_n), r3(gate), r3(szb), wk, wv)
    yp = _out_proj_ln([ya.reshape(bp * tp, d_a), yb.reshape(bp * tp, d_b)], [w_out_bf[:d_a], w_out_bf[d_a:]],
                      xp.reshape(bp * tp, d), ln_g, ln_b, alpha).reshape(bp, tp, d)
    w_keep = min(WINDOW, tp)
    outs_p = (cmp_n.reshape(kv_shape(bp, tp)), sel_n.reshape(kv_shape(bp, tp)),
              win_n.reshape(kv_shape(bp, tp))[:, tp - w_keep:], u3[:, tp - (n_taps - 1):])

    u, sza, q, cmp_s, sel_s, win_s, szb, gate = _even_in_proj(xs.reshape(bs, d), w_bf, d_a, d_b, d_kv, kv_h, hd)
    ext = jnp.concatenate([state_conv, u[:, None, :]], axis=1)
    hist = jnp.pad(state_conv, ((0, 0), (CONV_HIST - (n_taps - 1), 0), (0, 0)))
    pad_rows = lambda a: jnp.pad(a[:, None, :], ((0, 0), (0, SUBLANES - 1), (0, 0)))
    ya = _conv_branch(hist, pad_rows(u), pad_rows(sza), cw, conv_b, cln_g, cln_b)[:, 0]
    n_pool, page = cache_cmp.shape[:2]
    w_buf = cache_win.shape[1]
    yb = _nsa_decode(q.astype(F32).reshape(bs, H_B, hd), cache_cmp.reshape(n_pool, page, d_kv),
                     cache_sel.reshape(n_pool, page, d_kv), cache_win.reshape(bs, w_buf, d_kv), page_table,
                     sel_s[:, None, :], win_s[:, None, :], gate[:, None, :], szb.reshape(bs, H_B, hd), wk, wv)
    ys = _out_proj_ln([ya, yb.reshape(bs, d_b)], [w_out_bf[:d_a], w_out_bf[d_a:]], xs.reshape(bs, d),
                      ln_g, ln_b, alpha).reshape(bs, 1, d)
    ctx = jnp.concatenate([cache_win, win_s.reshape(kv_shape(bs, 1))], axis=1)
    outs_s = (cmp_s.reshape(kv_shape(bs, 1)), sel_s.reshape(kv_shape(bs, 1)),
              ctx[:, ctx.shape[1] - min(WINDOW, ctx.shape[1]):], ext[:, 1:])
    return yp, ys, outs_p, outs_s


def _odd_group(x3, shift, s0, mu, w_rkvz, w0, w1, w2, a0, a1, a2, k_k, k_a, r_k, gn_g, gn_b, w_out_bf,
               ln_g, ln_b, alpha):
    bsz, t, d = x3.shape
    x2 = x3.reshape(bsz * t, d)
    r, k, v, lw, a, sz = _rwkv_proj(x2, shift, t, mu, w_rkvz, w0, w1, w2, a0, a1, a2)
    c_len = WKV_CHUNK if t >= WKV_CHUNK else -(-t // SUBLANES) * SUBLANES
    t_pad = -(-t // c_len) * c_len
    r3 = lambda z: jnp.pad(z.reshape(bsz, t, d), ((0, 0), (0, t_pad - t), (0, 0)))
    yz, s_fin = _wkv_scan(r3(r), r3(k), r3(v), r3(lw), r3(a), r3(sz), s0, k_k, k_a, r_k.reshape(-1), gn_g, gn_b,
                          c_len, t)
    y = _out_proj_ln([yz[:, :t].reshape(bsz * t, d)], [w_out_bf], x2, ln_g, ln_b, alpha).reshape(bsz, t, d)
    return y, s_fin, x3[:, -1]


def kernel(x_prompt, x_sample, cache_cmp_kv, cache_sel_kv, cache_win_kv, state_conv, state_wkv, state_shift,
           page_table, w_in_even, conv_w, conv_b, conv_ln_g, conv_ln_b, wk_cmp, wv_cmp, w_out_even, mu_c, w_rkvz,
           w0, w1, w2, a0, a1, a2, k_k, k_a, r_k, gn_g, gn_b, w_out_odd, ln_g, ln_b):
    depth = ln_g.shape[0]
    alpha = (2 * depth) ** 0.25
    bp, _, d = x_prompt.shape
    n_heads = d // HD_C
    xp, xs = x_prompt, x_sample
    even_p, even_s, odd_p, odd_s = [], [], [], []
    for l in range(depth):
        if l % 2 == 0:
            e = l // 2
            xp, xs, o_p, o_s = _even_layer(
                xp, xs, cache_cmp_kv[e], cache_sel_kv[e], cache_win_kv[e], state_conv[e], page_table,
                w_in_even[e], conv_w[e], conv_b[e], conv_ln_g[e], conv_ln_b[e], wk_cmp[e], wv_cmp[e],
                w_out_even[e], ln_g[l], ln_b[l], alpha)
            even_p.append(o_p)
            even_s.append(o_s)
        else:
            o = l // 2
            po = (mu_c[o], w_rkvz[o], w0[o], w1[o], w2[o], a0[o], a1[o], a2[o], k_k[o], k_a[o], r_k[o],
                  gn_g[o], gn_b[o], w_out_odd[o].astype(BF16), ln_g[l], ln_b[l], alpha)
            xp, s_p, h_p = _odd_group(xp, jnp.zeros((bp, d), F32), jnp.zeros((bp, n_heads, HD_C, HD_C), F32), *po)
            xs, s_s, h_s = _odd_group(xs, state_shift[o], state_wkv[o], *po)
            odd_p.append((s_p, h_p))
            odd_s.append((s_s, h_s))
    stack = lambda items, j: jnp.stack([it[j] for it in items])
    return (xp, xs,
            stack(even_p, 0), stack(even_s, 0), stack(even_p, 1), stack(even_s, 1),
            stack(even_p, 2), stack(even_s, 2), stack(even_p, 3), stack(even_s, 3),
            stack(odd_p, 0), stack(odd_s, 0), stack(odd_p, 1), stack(odd_s, 1))
```

```python
import functools

import jax
import jax.numpy as jnp
from jax import lax
from jax.experimental import pallas as pl
from jax.experimental.pallas import tpu as pltpu

F32 = jnp.float32
BF16 = jnp.bfloat16

H_B = 8
L_SEL = 64
K_SEL = 16
WINDOW = 512
FORCE = 1e4
HD_C = 64
GN_EPS = 64e-5
LN_EPS = 1e-5
NEG = -1e30

LANES = 128
SUBLANES = 8
VMEM_LIMIT = 56 * 1024 * 1024

NT = (((1,), (1,)), ((), ()))
TN = (((0,), (0,)), ((), ()))
NN = (((1,), (0,)), ((), ()))


def _cparams(n_axes):
    return pltpu.CompilerParams(dimension_semantics=("arbitrary",) * n_axes,
                                vmem_limit_bytes=VMEM_LIMIT)


def _sigmoid(x):
    return 1.0 / (1.0 + jnp.exp(-x))


def _silu(x):
    return x * _sigmoid(x)


def _split_bf16(x, n):
    parts, rem = [], x
    for i in range(n):
        p = rem.astype(BF16)
        parts.append(p)
        if i + 1 < n:
            rem = rem - p.astype(F32)
    return parts


def _mm(a, b, dims=NN, pa=1, pb=1):
    aa = [a] if a.dtype == BF16 else _split_bf16(a, pa)
    bb = [b] if b.dtype == BF16 else _split_bf16(b, pb)
    keep = max(len(aa), len(bb))
    out = None
    for i, ai in enumerate(aa):
        for j, bj in enumerate(bb):
            if i + j < keep:
                t = lax.dot_general(ai, bj, dims, preferred_element_type=F32)
                out = t if out is None else out + t
    return out


def _layer_norm_rows(h, g, b, eps):
    mu = jnp.mean(h, axis=-1, keepdims=True)
    d = h - mu
    var = jnp.mean(d * d, axis=-1, keepdims=True)
    return d * lax.rsqrt(var + eps) * g + b


def _even_in_proj_kernel(x_ref, w_ref, u_ref, sza_ref, q_ref, cmp_ref, sel_ref, win_ref, szb_ref, gate_ref,
                         *, d_a, d_b, d_kv, q_scale):
    xb = x_ref[...].astype(BF16)

    def seg(lo, n):
        return jnp.dot(xb, w_ref[:, lo:lo + n], preferred_element_type=F32)

    o = 0
    a_val = seg(o, d_a); o += d_a
    a_glu = seg(o, d_a); o += d_a
    u_ref[...] = a_val * _sigmoid(a_glu)
    sza_ref[...] = _silu(seg(o, d_a)); o += d_a
    q_ref[...] = (seg(o, d_b) * q_scale).astype(BF16); o += d_b
    for ref in (cmp_ref, sel_ref, win_ref):
        ref[...] = seg(o, d_kv); o += d_kv
    szb_ref[...] = _silu(seg(o, d_b)); o += d_b
    gate_ref[...] = _sigmoid(seg(o, gate_ref.shape[-1]))


def _prep_w_in(w, d_a, d_b, kv_h, hd):
    grp = H_B // kv_h
    c_kv6 = 3 * d_a + d_b
    c_g3 = c_kv6 + 6 * kv_h * hd
    c_zb = c_g3 + 3 * H_B
    gate_blocks = []
    for g in range(kv_h):
        blk = w[:, c_g3 + g * grp * 3: c_g3 + (g + 1) * grp * 3]
        gate_blocks.append(jnp.pad(blk, ((0, 0), (0, LANES - grp * 3))))
    wn = jnp.concatenate([w[:, :c_g3], w[:, c_zb:c_zb + d_b]] + gate_blocks, axis=1)
    return wn.astype(BF16)


def _even_in_proj(x2, w_bf, d_a, d_b, d_kv, kv_h, hd):
    m, d = x2.shape
    tm = min(512, m)
    assert m % tm == 0
    n_gate = kv_h * LANES
    row = lambda n: pl.BlockSpec((tm, n), lambda i: (i, 0))
    out_shape = (jax.ShapeDtypeStruct((m, d_a), F32), jax.ShapeDtypeStruct((m, d_a), F32),
                 jax.ShapeDtypeStruct((m, d_b), BF16),
                 jax.ShapeDtypeStruct((m, d_kv), F32), jax.ShapeDtypeStruct((m, d_kv), F32),
                 jax.ShapeDtypeStruct((m, d_kv), F32),
                 jax.ShapeDtypeStruct((m, d_b), F32), jax.ShapeDtypeStruct((m, n_gate), F32))
    return pl.pallas_call(
        functools.partial(_even_in_proj_kernel, d_a=d_a, d_b=d_b, d_kv=d_kv, q_scale=hd ** -0.5),
        grid=(m // tm,),
        in_specs=[row(d), pl.BlockSpec(w_bf.shape, lambda i: (0, 0))],
        out_specs=(row(d_a), row(d_a), row(d_b), row(d_kv), row(d_kv), row(d_kv), row(d_b), row(n_gate)),
        out_shape=out_shape,
        compiler_params=_cparams(1),
        name="even_in_proj",
    )(x2, w_bf)


CONV_HIST = 32


def _conv_kernel(hist_ref, u_ref, sza_ref, w_ref, cb_ref, g_ref, b_ref, ya_ref, win_ref, sh_ref,
                 *, tq, rb, n_taps):
    i = pl.program_id(1)
    base = pl.multiple_of(i * tq, SUBLANES)
    win_ref[CONV_HIST:, :] = u_ref[0, pl.ds(base, tq), :]

    @pl.when(i == 0)
    def _():
        win_ref[:CONV_HIST, :] = hist_ref[0]

    @pl.when(i > 0)
    def _():
        win_ref[:CONV_HIST, :] = u_ref[0, pl.ds(base - CONV_HIST, CONV_HIST), :]

    w = win_ref[...]
    n = tq + CONV_HIST
    sh_ref[0] = w
    for s in range(1, SUBLANES):
        sh_ref[s] = pltpu.roll(w, n - s, 0)

    first = CONV_HIST - (n_taps - 1)
    cb = cb_ref[...]
    g = g_ref[...]
    b = b_ref[...]

    def block(k, carry):
        r0 = pl.multiple_of(k * rb, SUBLANES)
        acc = jnp.zeros((rb, u_ref.shape[-1]), F32) + cb
        for j in range(n_taps):
            a, s = divmod(first + j, SUBLANES)
            acc = acc + sh_ref[s, pl.ds(r0 + SUBLANES * a, rb), :] * w_ref[j:j + 1, :]
        y = _silu(_layer_norm_rows(acc, g, b, LN_EPS)) * sza_ref[0, pl.ds(r0, rb), :]
        ya_ref[0, pl.ds(r0, rb), :] = y.astype(ya_ref.dtype)
        return carry

    lax.fori_loop(0, tq // rb, block, 0)


def _conv_branch(hist, u3, sza3, conv_w, conv_b, ln_g, ln_b):
    bsz, t, d = u3.shape
    n_taps = conv_w.shape[0]
    assert n_taps - 1 <= CONV_HIST and t % SUBLANES == 0
    tq = min(256, t)
    rb = min(32, tq)
    assert t % tq == 0 and tq % rb == 0
    vec = lambda a: a.reshape(1, d)
    full = lambda shp: pl.BlockSpec(shp, lambda b, i: (0,) * len(shp))
    return pl.pallas_call(
        functools.partial(_conv_kernel, tq=tq, rb=rb, n_taps=n_taps),
        grid=(bsz, t // tq),
        in_specs=[pl.BlockSpec((1, CONV_HIST, d), lambda b, i: (b, 0, 0)),
                  pl.BlockSpec((1, t, d), lambda b, i: (b, 0, 0)),
                  pl.BlockSpec((1, tq, d), lambda b, i: (b, i, 0)),
                  full((n_taps, d)), full((1, d)), full((1, d)), full((1, d))],
        out_specs=pl.BlockSpec((1, tq, d), lambda b, i: (b, i, 0)),
        out_shape=jax.ShapeDtypeStruct((bsz, t, d), BF16),
        scratch_shapes=[pltpu.VMEM((tq + CONV_HIST, d), F32),
                        pltpu.VMEM((SUBLANES, tq + CONV_HIST, d), F32)],
        compiler_params=_cparams(2),
        name="conv_branch",
    )(hist, u3, sza3, conv_w, vec(conv_b), vec(ln_g), vec(ln_b))


SEL_OFF = -(2.0 ** 100)


def _nsa_prompt_kernel(q_ref, kcmp_ref, ksel_ref, kwin_ref, gate_ref, szb_ref, wpool_ref, o_ref,
                       kc_s, vc_s, ks_s, vs_s, kw_s, vw_s, m_s, l_s, acc_s,
                       *, tq, t_len, l_cmp, hd, grp, kv_h):
    i = pl.program_id(1)
    g = pl.program_id(2)
    n_cmp = t_len // l_cmp
    n_sel = t_len // L_SEL
    half = n_cmp // 2
    gw = kv_h * hd

    @pl.when((i == 0) & (g == 0))
    def _prep():
        x3 = kcmp_ref[0].reshape(half, 2 * l_cmp, 2 * gw)
        pooled = jnp.concatenate([jnp.sum(x3 * wpool_ref[0][None], axis=1),
                                  jnp.sum(x3 * wpool_ref[1][None], axis=1)], axis=0)
        pos = lax.broadcasted_iota(jnp.int32, (t_len, LANES), 0)
        lane = lax.broadcasted_iota(jnp.int32, (t_len, LANES), 1)
        blk = pos // L_SEL
        feat = jnp.where(lane == blk, 1.0, 0.0)
        feat = jnp.where(lane == n_sel, blk.astype(F32), feat)
        feat = jnp.where(lane == n_sel + 1, (pos - blk * L_SEL).astype(F32), feat)
        for src, dst in ((ksel_ref, ks_s), (kwin_ref, kw_s)):
            k01 = src[0, :, :gw]
            dst[0] = jnp.where(lane < hd, feat, pltpu.roll(k01, hd, 1)).astype(BF16)
            dst[1] = jnp.where(lane < hd, feat, k01).astype(BF16)
        vc_pad = jnp.concatenate([pooled[:, gw:], jnp.zeros((LANES - n_cmp, gw), F32)], axis=0) \
            if n_cmp < LANES else pooled[:, gw:]
        vc_t = vc_pad.T
        vs_t = ksel_ref[0, :, gw:].T
        vw_t = kwin_ref[0, :, gw:].T
        for gg in range(kv_h):
            kc_s[gg] = pooled[:, gg * hd:(gg + 1) * hd].astype(BF16)
            vc_s[gg] = vc_t[gg * hd:(gg + 1) * hd, :n_cmp].astype(BF16)
            vs_s[gg] = vs_t[gg * hd:(gg + 1) * hd].astype(BF16)
            vw_s[gg] = vw_t[gg * hd:(gg + 1) * hd].astype(BF16)

    t0 = i * tq
    gs = jnp.float32(1.0)
    for gg in range(1, kv_h):
        gs = jnp.where(g == gg, 2.0 ** -(gg * grp), gs)
    slopes = [gs * (2.0 ** -(r + 1)) for r in range(grp)]
    qt = q_ref[0]
    qs = jnp.concatenate([qt[:, r * hd:(r + 1) * hd] for r in range(grp)], axis=0)

    s_t = lax.dot_general(kc_s[g], qs, NT, preferred_element_type=F32)
    n_io = lax.broadcasted_iota(jnp.int32, (n_cmp, tq), 0)
    t_io = lax.broadcasted_iota(jnp.int32, (n_cmp, tq), 1) + t0
    c_end = jnp.where(n_io < half, 2 * l_cmp * n_io + (l_cmp - 1), 2 * l_cmp * (n_io - half) + (2 * l_cmp - 1))
    dist_c = t_io - c_end
    valid_c = dist_c >= 0
    dist_cf = dist_c.astype(F32)
    imp = jnp.zeros((n_sel, tq), F32)
    o_c = []
    for r in range(grp):
        s = jnp.where(valid_c, s_t[:, r * tq:(r + 1) * tq] - slopes[r] * dist_cf, NEG)
        e = jnp.exp(s - jnp.max(s, axis=0, keepdims=True))
        p = jnp.where(valid_c, e / jnp.sum(e, axis=0, keepdims=True), 0.0)
        imp = imp + (p[:half] + p[half:])
        o_c.append(jnp.dot(vc_s[g], p.astype(BF16), preferred_element_type=F32))

    sb = lax.broadcasted_iota(jnp.int32, (n_sel, tq), 0)
    tb = (lax.broadcasted_iota(jnp.int32, (n_sel, tq), 1) + t0) // L_SEL
    forced = (sb == 0) | (sb == tb) | (sb == tb - 1)
    imp = jnp.where(forced, FORCE, imp)
    imp = jnp.where(sb <= tb, imp, -jnp.inf)
    rank = jnp.zeros((n_sel, tq), F32)
    for j in range(n_sel):
        vj = imp[j:j + 1, :]
        beats = jnp.where(imp > vj, 1.0, jnp.where((imp == vj) & (sb < j), 1.0, 0.0))
        rank = jnp.where(sb == j, jnp.sum(beats, axis=0, keepdims=True), rank)
    off_t = jnp.where(rank < min(K_SEL, n_sel), 0.0, SEL_OFF)
    feat_sel = jnp.concatenate([off_t, jnp.zeros((LANES - n_sel, tq), F32)], axis=0).T

    qf = q_ref[0].astype(F32)
    lane_q = lax.broadcasted_iota(jnp.int32, (tq, LANES), 1)

    def q_aug(r, feat):
        tile = qf[:, (r // 2) * LANES:(r // 2 + 1) * LANES]
        if r % 2 == 0:
            tile = pltpu.roll(tile, hd, 1)
        x = jnp.where(lane_q < hd, feat, tile)
        x = jnp.where(lane_q == n_sel, slopes[r] * L_SEL, x)
        return jnp.where(lane_q == n_sel + 1, slopes[r], x).astype(BF16)

    d0 = (lax.broadcasted_iota(jnp.int32, (tq, tq), 1) - lax.broadcasted_iota(jnp.int32, (tq, tq), 0))

    def reset():
        m_s[...] = jnp.full(m_s.shape, NEG, F32)
        l_s[...] = jnp.zeros(l_s.shape, F32)
        acc_s[...] = jnp.zeros(acc_s.shape, F32)

    def tile_update(q_list, k_s, v_s, kt, mask):
        k0 = pl.multiple_of(kt * tq, tq)
        kk = k_s[g, pl.ds(k0, tq), :]
        vt = v_s[g, :, pl.ds(k0, tq)]
        heads = range(grp)
        s = [lax.dot_general(kk, q_list[r], NT, preferred_element_type=F32) for r in heads]
        if mask is not None:
            s = [jnp.where(mask, s[r], NEG) for r in heads]
        m_old = [m_s[r] for r in heads]
        m_new = [jnp.maximum(m_old[r], jnp.max(s[r], axis=0, keepdims=True)) for r in heads]
        al = [jnp.exp(m_old[r] - m_new[r]) for r in heads]
        p = [jnp.exp(s[r] - m_new[r]) for r in heads]
        for r in heads:
            m_s[r] = m_new[r]
            l_s[r] = al[r] * l_s[r] + jnp.sum(p[r], axis=0, keepdims=True)
            acc_s[r] = al[r] * acc_s[r] + jnp.dot(vt, p[r].astype(BF16), preferred_element_type=F32)

    q_sel = [q_aug(r, feat_sel) for r in range(grp)]
    reset()

    def sel_body(kt, carry):
        tile_update(q_sel, ks_s, vs_s, kt, None)
        return carry

    lax.fori_loop(0, i, sel_body, 0)
    tile_update(q_sel, ks_s, vs_s, i, d0 >= 0)
    o_s = [acc_s[r] / l_s[r] for r in range(grp)]

    q_win = [q_aug(r, 0.0) for r in range(grp)]
    reset()
    for j in range(-(-(WINDOW - 1) // tq), 0, -1):
        d_j = d0 + j * tq
        inside = j * tq - (tq - 1) >= 0 and j * tq + (tq - 1) < WINDOW
        mask_j = None if inside else (d_j >= 0) & (d_j < WINDOW)

        @pl.when(i >= j)
        def _(j=j, mask_j=mask_j):
            tile_update(q_win, kw_s, vw_s, i - j, mask_j)

    tile_update(q_win, kw_s, vw_s, i, (d0 >= 0) & (d0 < WINDOW))
    o_w = [acc_s[r] / l_s[r] for r in range(grp)]

    gate_t = gate_ref[0].T
    outs = []
    for r in range(grp):
        outs.append(gate_t[3 * r:3 * r + 1] * o_c[r]
                    + gate_t[3 * r + 1:3 * r + 2] * o_s[r]
                    + gate_t[3 * r + 2:3 * r + 3] * o_w[r])
    o_ref[0] = (jnp.concatenate(outs, axis=0).T * szb_ref[0]).astype(o_ref.dtype)


def _pool_weights(wk, wv, hd, halves):
    l_cmp, kv_h = wk.shape
    row = jnp.concatenate([jnp.repeat(wk, hd, axis=1), jnp.repeat(wv, hd, axis=1)], axis=1)
    if not halves:
        return row
    z = jnp.zeros_like(row)
    return jnp.stack([jnp.concatenate([row, z], axis=0), jnp.concatenate([z, row], axis=0)])


def _nsa_prompt(q3, cmp3, sel3, win3, gate3, szb3, wk, wv):
    bsz, t, d_b = q3.shape
    l_cmp, kv_h = wk.shape
    grp = H_B // kv_h
    hd = d_b // H_B
    gw = kv_h * hd
    tq = min(256, t)
    assert t % tq == 0 and t % (2 * l_cmp) == 0 and L_SEL == 2 * l_cmp and tq % L_SEL == 0
    n_cmp = t // l_cmp
    n_sel = t // L_SEL
    assert gw == LANES and 2 * hd == LANES and n_sel + 2 <= hd and n_sel % SUBLANES == 0 and n_cmp <= LANES
    wpool = _pool_weights(wk, wv, hd, halves=True)
    kv_spec = pl.BlockSpec((1, t, 2 * gw), lambda b, i, g: (b, 0, 0))
    return pl.pallas_call(
        functools.partial(_nsa_prompt_kernel, tq=tq, t_len=t, l_cmp=l_cmp, hd=hd, grp=grp, kv_h=kv_h),
        grid=(bsz, t // tq, kv_h),
        in_specs=[pl.BlockSpec((1, tq, grp * hd), lambda b, i, g: (b, i, g)),
                  kv_spec, kv_spec, kv_spec,
                  pl.BlockSpec((1, tq, LANES), lambda b, i, g: (b, i, g)),
                  pl.BlockSpec((1, tq, grp * hd), lambda b, i, g: (b, i, g)),
                  pl.BlockSpec(wpool.shape, lambda b, i, g: (0, 0, 0))],
        out_specs=pl.BlockSpec((1, tq, grp * hd), lambda b, i, g: (b, i, g)),
        out_shape=jax.ShapeDtypeStruct((bsz, t, d_b), BF16),
        scratch_shapes=[pltpu.VMEM((kv_h, n_cmp, hd), BF16), pltpu.VMEM((kv_h, hd, n_cmp), BF16),
                        pltpu.VMEM((kv_h, t, LANES), BF16), pltpu.VMEM((kv_h, hd, t), BF16),
                        pltpu.VMEM((kv_h, t, LANES), BF16), pltpu.VMEM((kv_h, hd, t), BF16),
                        pltpu.VMEM((grp, 1, tq), F32), pltpu.VMEM((grp, 1, tq), F32),
                        pltpu.VMEM((grp, hd, tq), F32)],
        compiler_params=_cparams(3),
        name="nsa_prompt",
    )(q3, cmp3, sel3, win3, gate3, szb3, wpool)


def _nsa_decode_kernel(pt_ref, *refs, pg, n_pages, page, l_cmp, hd, grp, kv_h, w_buf):
    del pt_ref
    q_ref = refs[0]
    cmp_refs = refs[1:1 + pg]
    sel_refs = refs[1 + pg:1 + 2 * pg]
    (kwin_ref, seln_ref, winn_ref, gate_ref, szb_ref, wtile_ref, o_ref,
     kcv_s, selm_s, oc_s, m_s, l_s, acc_s) = refs[1 + 2 * pg:]
    p = pl.program_id(1)
    n_steps = n_pages // pg
    past = n_pages * page
    n_cmp = past // l_cmp
    n_sel = past // L_SEL
    per_page = page // l_cmp
    gw = kv_h * hd
    nc_pad = selm_s.shape[-1]
    qf = q_ref[0]
    q = qf.astype(BF16)
    head = lax.broadcasted_iota(jnp.int32, (H_B, 1), 0)
    slope = jnp.zeros((H_B, 1), F32)
    for hh in range(H_B):
        slope = jnp.where(head == hh, 2.0 ** -(hh + 1), slope)

    @pl.when(p < n_steps)
    def _pool():
        rows = []
        for j in range(pg):
            x3 = (cmp_refs[j][0] * wtile_ref[...]).reshape(per_page, l_cmp, 2 * gw)
            rows.append(jnp.sum(x3, axis=1))
        kcv_s[pl.ds(pl.multiple_of(p * (pg * per_page), SUBLANES), pg * per_page), :] = jnp.concatenate(rows, axis=0)

    @pl.when(p == n_steps - 1)
    def _compressed():
        kcv = kcv_s[...]
        n_io = lax.broadcasted_iota(jnp.int32, (1, n_cmp), 1)
        dist = (past - (l_cmp * n_io + (l_cmp - 1))).astype(F32)
        pair = jnp.where(lax.broadcasted_iota(jnp.int32, (n_cmp, nc_pad), 0) // (L_SEL // l_cmp)
                         == lax.broadcasted_iota(jnp.int32, (n_cmp, nc_pad), 1), 1.0, 0.0).astype(BF16)
        c_io = lax.broadcasted_iota(jnp.int32, (1, nc_pad), 1)
        i_r = lax.broadcasted_iota(jnp.int32, (nc_pad, nc_pad), 1)
        i_c = lax.broadcasted_iota(jnp.int32, (nc_pad, nc_pad), 0)
        selm_s[...] = jnp.zeros(selm_s.shape, F32)
        for g in range(kv_h):
            kc = kcv[:, g * hd:(g + 1) * hd].astype(BF16)
            vc = kcv[:, gw + g * hd: gw + (g + 1) * hd].astype(BF16)
            s = lax.dot_general(q, kc, NT, preferred_element_type=F32) - slope * dist
            e = jnp.exp(s - jnp.max(s, axis=-1, keepdims=True))
            pc = e / jnp.sum(e, axis=-1, keepdims=True)
            oc_s[g] = jnp.dot(pc.astype(BF16), vc, preferred_element_type=F32)
            in_grp = (head >= g * grp) & (head < (g + 1) * grp)
            imp = _mm(jnp.sum(jnp.where(in_grp, pc, 0.0), axis=0, keepdims=True), pair, NN, pa=3)
            forced = (c_io == 0) | (c_io == n_sel - 1)
            imp = jnp.where(forced, FORCE, imp)
            imp = jnp.where(c_io < n_sel, imp, -jnp.inf)
            v_r = jnp.broadcast_to(imp, (nc_pad, nc_pad))
            v_c = v_r.T
            beats = jnp.where(v_c > v_r, 1.0, jnp.where((v_c == v_r) & (i_c < i_r), 1.0, 0.0))
            rank = jnp.sum(beats, axis=0, keepdims=True) + jnp.where(imp < FORCE, 1.0, 0.0)
            chosen = (rank < min(K_SEL, n_sel + 1)) & (c_io < n_sel)
            selm_s[g:g + 1, :] = jnp.where(chosen, 1.0, 0.0)
        m_s[...] = jnp.full(m_s.shape, NEG, F32)
        l_s[...] = jnp.zeros(l_s.shape, F32)
        acc_s[...] = jnp.zeros(acc_s.shape, F32)

    @pl.when(p >= n_steps)
    def _selected():
        step = p - n_steps
        nk = pg * page
        x = jnp.concatenate([sel_refs[j][0] for j in range(pg)], axis=0)
        kpos = lax.broadcasted_iota(jnp.int32, (1, nk), 1) + step * nk
        dist = (past - kpos).astype(F32)
        expand = jnp.where(lax.broadcasted_iota(jnp.int32, (nc_pad, nk), 0)
                           == (lax.broadcasted_iota(jnp.int32, (nc_pad, nk), 1) + step * nk) // L_SEL,
                           1.0, 0.0).astype(BF16)
        chosen = jnp.dot(selm_s[...].astype(BF16), expand, preferred_element_type=F32)
        for g in range(kv_h):
            kk = x[:, g * hd:(g + 1) * hd].astype(BF16)
            vv = x[:, gw + g * hd: gw + (g + 1) * hd].astype(BF16)
            s = lax.dot_general(q, kk, NT, preferred_element_type=F32) - slope * dist
            s = jnp.where(chosen[g:g + 1, :] > 0.5, s, NEG)
            m_old = m_s[g]
            m_new = jnp.maximum(m_old, jnp.max(s, axis=-1, keepdims=True))
            al = jnp.exp(m_old - m_new)
            pe = jnp.exp(s - m_new)
            m_s[g] = m_new
            l_s[g] = al * l_s[g] + jnp.sum(pe, axis=-1, keepdims=True)
            acc_s[g] = al * acc_s[g] + jnp.dot(pe.astype(BF16), vv, preferred_element_type=F32)

    @pl.when(p == 2 * n_steps - 1)
    def _finish():
        gate = gate_ref[0]
        xw = kwin_ref[0]
        i_io = lax.broadcasted_iota(jnp.int32, (1, w_buf), 1)
        dist_w = w_buf - i_io
        valid_w = dist_w < WINDOW
        o = jnp.zeros((H_B, hd), F32)
        for g in range(kv_h):
            kn = seln_ref[0][:, g * hd:(g + 1) * hd]
            vn = seln_ref[0][:, gw + g * hd: gw + (g + 1) * hd]
            s_n = jnp.sum(qf * kn, axis=-1, keepdims=True)
            m_old = m_s[g]
            m_new = jnp.maximum(m_old, s_n)
            al = jnp.exp(m_old - m_new)
            pn = jnp.exp(s_n - m_new)
            o_s = (al * acc_s[g] + pn * vn) / (al * l_s[g] + pn)
            kw = xw[:, g * hd:(g + 1) * hd].astype(BF16)
            vw = xw[:, gw + g * hd: gw + (g + 1) * hd].astype(BF16)
            s = lax.dot_general(q, kw, NT, preferred_element_type=F32) - slope * dist_w.astype(F32)
            s = jnp.where(valid_w, s, NEG)
            kwn = winn_ref[0][:, g * hd:(g + 1) * hd]
            vwn = winn_ref[0][:, gw + g * hd: gw + (g + 1) * hd]
            s_wn = jnp.sum(qf * kwn, axis=-1, keepdims=True)
            mw = jnp.maximum(jnp.max(s, axis=-1, keepdims=True), s_wn)
            pw = jnp.where(valid_w, jnp.exp(s - mw), 0.0)
            pwn = jnp.exp(s_wn - mw)
            o_w = ((jnp.dot(pw.astype(BF16), vw, preferred_element_type=F32) + pwn * vwn)
                   / (jnp.sum(pw, axis=-1, keepdims=True) + pwn))
            gcol = []
            for c in range(3):
                col = jnp.zeros((H_B, 1), F32)
                for r in range(grp):
                    lane = g * LANES + 3 * r + c
                    col = jnp.where(head == g * grp + r, gate[:, lane:lane + 1], col)
                gcol.append(col)
            og = gcol[0] * oc_s[g] + gcol[1] * o_s + gcol[2] * o_w
            o = jnp.where((head >= g * grp) & (head < (g + 1) * grp), og, o)
        o_ref[0] = o * szb_ref[0]


def _nsa_decode(q3, cache_cmp, cache_sel, cache_win, page_table, sel_new, win_new, gate, szb3, wk, wv):
    bs, _, hd = q3.shape
    l_cmp, kv_h = wk.shape
    grp = H_B // kv_h
    gw = kv_h * hd
    _, page, _ = cache_cmp.shape
    n_pages = page_table.shape[1]
    w_buf = cache_win.shape[1]
    pg = 8 if n_pages % 8 == 0 else 1
    n_steps = n_pages // pg
    per_page = page // l_cmp
    assert page % L_SEL == 0 and page % l_cmp == 0 and (pg * per_page) % SUBLANES == 0
    n_sel = n_pages * page // L_SEL
    nc_pad = -(-n_sel // LANES) * LANES
    wtile = jnp.tile(_pool_weights(wk, wv, hd, halves=False), (per_page, 1))

    def cmp_map(j):
        return lambda b, p, pt: (pt[b, jnp.minimum(p, n_steps - 1) * pg + j], 0, 0)

    def sel_map(j):
        return lambda b, p, pt: (pt[b, jnp.maximum(p - n_steps, 0) * pg + j], 0, 0)

    per_b = lambda shp: pl.BlockSpec((1,) + shp, lambda b, p, pt: (b, 0, 0))
    grid_spec = pltpu.PrefetchScalarGridSpec(
        num_scalar_prefetch=1,
        grid=(bs, 2 * n_steps),
        in_specs=([per_b((H_B, hd))]
                  + [pl.BlockSpec((1, page, 2 * gw), cmp_map(j)) for j in range(pg)]
                  + [pl.BlockSpec((1, page, 2 * gw), sel_map(j)) for j in range(pg)]
                  + [per_b((w_buf, 2 * gw)), per_b((1, 2 * gw)), per_b((1, 2 * gw)),
                     per_b((1, kv_h * LANES)), per_b((H_B, hd)),
                     pl.BlockSpec((page, 2 * gw), lambda b, p, pt: (0, 0))]),
        out_specs=per_b((H_B, hd)),
        scratch_shapes=[pltpu.VMEM((n_pages * per_page, 2 * gw), F32),
                        pltpu.VMEM((SUBLANES, nc_pad), F32),
                        pltpu.VMEM((kv_h, H_B, hd), F32), pltpu.VMEM((kv_h, H_B, 1), F32),
                        pltpu.VMEM((kv_h, H_B, 1), F32), pltpu.VMEM((kv_h, H_B, hd), F32)])
    return pl.pallas_call(
        functools.partial(_nsa_decode_kernel, pg=pg, n_pages=n_pages, page=page, l_cmp=l_cmp, hd=hd, grp=grp,
                          kv_h=kv_h, w_buf=w_buf),
        grid_spec=grid_spec,
        out_shape=jax.ShapeDtypeStruct((bs, H_B, hd), F32),
        compiler_params=_cparams(2),
        name="nsa_decode",
    )(page_table, q3, *([cache_cmp] * pg), *([cache_sel] * pg), cache_win, sel_new, win_new, gate, szb3, wtile)


def _nsa_decode_t_kernel(pt_ref, *refs, pg, n_pages, page, l_cmp, hd, grp, kv_h, w_buf):
    del pt_ref
    q_ref = refs[0]
    cmp_refs = refs[1:1 + pg]
    sel_refs = refs[1 + pg:1 + 2 * pg]
    (kwin_ref, seln_ref, winn_ref, gate_ref, szb_ref, wrow_ref, seg_ref, o_ref,
     kcv_s, selm_s, oc_s, m_s, l_s, acc_s) = refs[1 + 2 * pg:]
    p = pl.program_id(1)
    n_steps = n_pages // pg
    past = n_pages * page
    n_cmp = past // l_cmp
    n_sel = past // L_SEL
    per_page = page // l_cmp
    pages_per_chunk = LANES // per_page
    steps_per_chunk = pages_per_chunk // pg
    n_chunks = n_pages // pages_per_chunk
    gw = kv_h * hd
    nc_pad = selm_s.shape[-1]
    qf = q_ref[0]
    q = qf.astype(BF16)
    head = lax.broadcasted_iota(jnp.int32, (H_B, 1), 0)
    slope = jnp.zeros((H_B, 1), F32)
    for hh in range(H_B):
        slope = jnp.where(head == hh, 2.0 ** -(hh + 1), slope)

    @pl.when(p < n_steps)
    def _pool():
        sub = p % steps_per_chunk
        tot = None
        for j in range(pg):
            prod = cmp_refs[j][0] * wrow_ref[...]
            t = _mm(prod, seg_ref[sub * pg + j], NN, pa=2)
            tot = t if tot is None else tot + t
        c = p // steps_per_chunk

        @pl.when(sub == 0)
        def _():
            kcv_s[c] = tot

        @pl.when(sub != 0)
        def _():
            kcv_s[c] = kcv_s[c] + tot

    @pl.when(p == n_steps - 1)
    def _compressed():
        n_io = lax.broadcasted_iota(jnp.int32, (1, n_cmp), 1)
        dist = (past - (l_cmp * n_io + (l_cmp - 1))).astype(F32)
        pair = jnp.where(lax.broadcasted_iota(jnp.int32, (n_cmp, nc_pad), 0) // (L_SEL // l_cmp)
                         == lax.broadcasted_iota(jnp.int32, (n_cmp, nc_pad), 1), 1.0, 0.0).astype(BF16)
        c_io = lax.broadcasted_iota(jnp.int32, (1, nc_pad), 1)
        i_r = lax.broadcasted_iota(jnp.int32, (nc_pad, nc_pad), 1)
        i_c = lax.broadcasted_iota(jnp.int32, (nc_pad, nc_pad), 0)
        selm_s[...] = jnp.zeros(selm_s.shape, F32)
        for g in range(kv_h):
            kc_t = jnp.concatenate([kcv_s[c, g * hd:(g + 1) * hd, :] for c in range(n_chunks)], axis=1)
            vc_t = jnp.concatenate([kcv_s[c, gw + g * hd: gw + (g + 1) * hd, :] for c in range(n_chunks)], axis=1)
            s = jnp.dot(q, kc_t.astype(BF16), preferred_element_type=F32) - slope * dist
            e = jnp.exp(s - jnp.max(s, axis=-1, keepdims=True))
            pc = e / jnp.sum(e, axis=-1, keepdims=True)
            oc_s[g] = lax.dot_general(pc.astype(BF16), vc_t.astype(BF16), NT, preferred_element_type=F32)
            in_grp = (head >= g * grp) & (head < (g + 1) * grp)
            imp = _mm(jnp.sum(jnp.where(in_grp, pc, 0.0), axis=0, keepdims=True), pair, NN, pa=3)
            forced = (c_io == 0) | (c_io == n_sel - 1)
            imp = jnp.where(forced, FORCE, imp)
            imp = jnp.where(c_io < n_sel, imp, -jnp.inf)
            v_r = jnp.broadcast_to(imp, (nc_pad, nc_pad))
            v_c = v_r.T
            beats = jnp.where(v_c > v_r, 1.0, jnp.where((v_c == v_r) & (i_c < i_r), 1.0, 0.0))
            rank = jnp.sum(beats, axis=0, keepdims=True) + jnp.where(imp < FORCE, 1.0, 0.0)
            chosen = (rank < min(K_SEL, n_sel + 1)) & (c_io < n_sel)
            selm_s[g:g + 1, :] = jnp.where(chosen, 1.0, 0.0)
        m_s[...] = jnp.full(m_s.shape, NEG, F32)
        l_s[...] = jnp.zeros(l_s.shape, F32)
        acc_s[...] = jnp.zeros(acc_s.shape, F32)

    @pl.when(p >= n_steps)
    def _selected():
        step = p - n_steps
        nk = pg * page
        kpos = lax.broadcasted_iota(jnp.int32, (1, nk), 1) + step * nk
        dist = (past - kpos).astype(F32)
        expand = jnp.where(lax.broadcasted_iota(jnp.int32, (nc_pad, nk), 0)
                           == (lax.broadcasted_iota(jnp.int32, (nc_pad, nk), 1) + step * nk) // L_SEL,
                           1.0, 0.0).astype(BF16)
        chosen = jnp.dot(selm_s[...].astype(BF16), expand, preferred_element_type=F32)
        for g in range(kv_h):
            k_t = jnp.concatenate([sel_refs[j][0, g * hd:(g + 1) * hd, :] for j in range(pg)], axis=1)
            v_t = jnp.concatenate([sel_refs[j][0, gw + g * hd: gw + (g + 1) * hd, :] for j in range(pg)], axis=1)
            s = jnp.dot(q, k_t.astype(BF16), preferred_element_type=F32) - slope * dist
            s = jnp.where(chosen[g:g + 1, :] > 0.5, s, NEG)
            m_old = m_s[g]
            m_new = jnp.maximum(m_old, jnp.max(s, axis=-1, keepdims=True))
            al = jnp.exp(m_old - m_new)
            pe = jnp.exp(s - m_new)
            m_s[g] = m_new
            l_s[g] = al * l_s[g] + jnp.sum(pe, axis=-1, keepdims=True)
            acc_s[g] = al * acc_s[g] + lax.dot_general(pe.astype(BF16), v_t.astype(BF16), NT,
                                                       preferred_element_type=F32)

    @pl.when(p == 2 * n_steps - 1)
    def _finish():
        gate = gate_ref[0]
        i_io = lax.broadcasted_iota(jnp.int32, (1, w_buf), 1)
        dist_w = w_buf - i_io
        valid_w = dist_w < WINDOW
        o = jnp.zeros((H_B, hd), F32)
        for g in range(kv_h):
            kn = seln_ref[0][:, g * hd:(g + 1) * hd]
            vn = seln_ref[0][:, gw + g * hd: gw + (g + 1) * hd]
            s_n = jnp.sum(qf * kn, axis=-1, keepdims=True)
            m_old = m_s[g]
            m_new = jnp.maximum(m_old, s_n)
            al = jnp.exp(m_old - m_new)
            pn = jnp.exp(s_n - m_new)
            o_s = (al * acc_s[g] + pn * vn) / (al * l_s[g] + pn)
            kw_t = kwin_ref[0, g * hd:(g + 1) * hd, :].astype(BF16)
            vw_t = kwin_ref[0, gw + g * hd: gw + (g + 1) * hd, :].astype(BF16)
            s = jnp.dot(q, kw_t, preferred_element_type=F32) - slope * dist_w.astype(F32)
            s = jnp.where(valid_w, s, NEG)
            kwn = winn_ref[0][:, g * hd:(g + 1) * hd]
            vwn = winn_ref[0][:, gw + g * hd: gw + (g + 1) * hd]
            s_wn = jnp.sum(qf * kwn, axis=-1, keepdims=True)
            mw = jnp.maximum(jnp.max(s, axis=-1, keepdims=True), s_wn)
            pw = jnp.where(valid_w, jnp.exp(s - mw), 0.0)
            pwn = jnp.exp(s_wn - mw)
            o_w = ((lax.dot_general(pw.astype(BF16), vw_t, NT, preferred_element_type=F32) + pwn * vwn)
                   / (jnp.sum(pw, axis=-1, keepdims=True) + pwn))
            gcol = []
            for c in range(3):
                col = jnp.zeros((H_B, 1), F32)
                for r in range(grp):
                    lane = g * LANES + 3 * r + c
                    col = jnp.where(head == g * grp + r, gate[:, lane:lane + 1], col)
                gcol.append(col)
            og = gcol[0] * oc_s[g] + gcol[1] * o_s + gcol[2] * o_w
            o = jnp.where((head >= g * grp) & (head < (g + 1) * grp), og, o)
        o_ref[0] = o * szb_ref[0]


def _nsa_decode_t(q3, cache_cmp_t, cache_sel_t, cache_win_t, page_table, sel_new, win_new, gate, szb3, wk, wv):
    bs, _, hd = q3.shape
    l_cmp, kv_h = wk.shape
    grp = H_B // kv_h
    gw = kv_h * hd
    page = cache_cmp_t.shape[-1]
    n_pages = page_table.shape[1]
    w_buf = cache_win_t.shape[-1]
    per_page = page // l_cmp
    pages_per_chunk = LANES // per_page
    pg = min(16, pages_per_chunk, n_pages)
    assert page == LANES and page % L_SEL == 0 and page % l_cmp == 0
    assert n_pages % pages_per_chunk == 0 and pages_per_chunk % pg == 0
    n_steps = n_pages // pg
    n_sel = n_pages * page // L_SEL
    nc_pad = -(-n_sel // LANES) * LANES
    wrow = jnp.tile(_pool_weights(wk, wv, hd, halves=False), (per_page, 1)).T
    pos = lax.broadcasted_iota(jnp.int32, (pages_per_chunk, page, LANES), 1)
    pj = lax.broadcasted_iota(jnp.int32, (pages_per_chunk, page, LANES), 0)
    ln = lax.broadcasted_iota(jnp.int32, (pages_per_chunk, page, LANES), 2)
    seg = (ln == pj * per_page + pos // l_cmp).astype(BF16)

    def cmp_map(j):
        return lambda b, p, pt: (pt[b, jnp.minimum(p, n_steps - 1) * pg + j], 0, 0)

    def sel_map(j):
        return lambda b, p, pt: (pt[b, jnp.maximum(p - n_steps, 0) * pg + j], 0, 0)

    per_b = lambda shp: pl.BlockSpec((1,) + shp, lambda b, p, pt: (b, 0, 0))
    grid_spec = pltpu.PrefetchScalarGridSpec(
        num_scalar_prefetch=1,
        grid=(bs, 2 * n_steps),
        in_specs=([per_b((H_B, hd))]
                  + [pl.BlockSpec((1, 2 * gw, page), cmp_map(j)) for j in range(pg)]
                  + [pl.BlockSpec((1, 2 * gw, page), sel_map(j)) for j in range(pg)]
                  + [per_b((2 * gw, w_buf)), per_b((1, 2 * gw)), per_b((1, 2 * gw)),
                     per_b((1, kv_h * LANES)), per_b((H_B, hd)),
                     pl.BlockSpec((2 * gw, page), lambda b, p, pt: (0, 0)),
                     pl.BlockSpec(seg.shape, lambda b, p, pt: (0, 0, 0))]),
        out_specs=per_b((H_B, hd)),
        scratch_shapes=[pltpu.VMEM((n_pages // pages_per_chunk, 2 * gw, LANES), F32),
                        pltpu.VMEM((SUBLANES, nc_pad), F32),
                        pltpu.VMEM((kv_h, H_B, hd), F32), pltpu.VMEM((kv_h, H_B, 1), F32),
                        pltpu.VMEM((kv_h, H_B, 1), F32), pltpu.VMEM((kv_h, H_B, hd), F32)])
    return pl.pallas_call(
        functools.partial(_nsa_decode_t_kernel, pg=pg, n_pages=n_pages, page=page, l_cmp=l_cmp, hd=hd, grp=grp,
                          kv_h=kv_h, w_buf=w_buf),
        grid_spec=grid_spec,
        out_shape=jax.ShapeDtypeStruct((bs, H_B, hd), F32),
        compiler_params=_cparams(2),
        name="nsa_decode",
    )(page_table, q3, *([cache_cmp_t] * pg), *([cache_sel_t] * pg), cache_win_t, sel_new, win_new, gate, szb3,
      wrow, seg)


def _out_proj_ln_kernel(*refs, n_in, alpha):
    a_refs, w_refs = refs[:n_in], refs[n_in:2 * n_in]
    x_ref, g_ref, b_ref, o_ref = refs[2 * n_in:]
    y = None
    for a_ref, w_ref in zip(a_refs, w_refs):
        t = jnp.dot(a_ref[...].astype(BF16), w_ref[...], preferred_element_type=F32)
        y = t if y is None else y + t
    o_ref[...] = _layer_norm_rows(alpha * x_ref[...] + y, g_ref[...], b_ref[...], LN_EPS)


def _out_proj_ln(a_list, w_list, x2, ln_g, ln_b, alpha):
    m, d = x2.shape
    tm = min(512, m)
    assert m % tm == 0
    n_in = len(a_list)
    return pl.pallas_call(
        functools.partial(_out_proj_ln_kernel, n_in=n_in, alpha=alpha),
        grid=(m // tm,),
        in_specs=([pl.BlockSpec((tm, a.shape[1]), lambda i: (i, 0)) for a in a_list]
                  + [pl.BlockSpec(w.shape, lambda i: (0, 0)) for w in w_list]
                  + [pl.BlockSpec((tm, d), lambda i: (i, 0)),
                     pl.BlockSpec((1, d), lambda i: (0, 0)), pl.BlockSpec((1, d), lambda i: (0, 0))]),
        out_specs=pl.BlockSpec((tm, d), lambda i: (i, 0)),
        out_shape=jax.ShapeDtypeStruct((m, d), F32),
        compiler_params=_cparams(1),
        name="out_proj_ln",
    )(*a_list, *w_list, x2, ln_g.reshape(1, d), ln_b.reshape(1, d))


def _rwkv_proj_kernel(*refs, seq, tiles_per_seq):
    if seq:
        x_ref, tail_ref, shift_ref = refs[:3]
        rest = refs[3:]
    else:
        x_ref, shift_ref = refs[:2]
        rest = refs[2:]
    (mu_ref, w_ref, w1_ref, w2_ref, a1_ref, a2_ref, w0_ref, a0_ref,
     r_ref, k_ref, v_ref, lw_ref, a_ref, sz_ref) = rest
    x = x_ref[...]
    if seq:
        i = pl.program_id(0)
        first = jnp.where(i % tiles_per_seq == 0, shift_ref[0], tail_ref[SUBLANES - 1:SUBLANES, :])
        row = lax.broadcasted_iota(jnp.int32, x.shape, 0)
        x_prev = jnp.where(row == 0, first, pltpu.roll(x, 1, 0))
    else:
        x_prev = shift_ref[...]
    dx = x_prev - x
    mix = lambda n: (x + dx * mu_ref[n:n + 1, :]).astype(BF16)
    r_ref[...] = jnp.dot(mix(0), w_ref[0], preferred_element_type=F32)
    k_ref[...] = jnp.dot(mix(1), w_ref[1], preferred_element_type=F32)
    v_ref[...] = jnp.dot(mix(2), w_ref[2], preferred_element_type=F32)
    sz_ref[...] = _silu(jnp.dot(mix(3), w_ref[3], preferred_element_type=F32))
    hw = jnp.tanh(jnp.dot(mix(4), w1_ref[...], preferred_element_type=F32)).astype(BF16)
    y = -(w0_ref[...] + jnp.dot(hw, w2_ref[...], preferred_element_type=F32))
    softplus = jnp.maximum(y, 0.0) + jnp.log(1.0 + jnp.exp(-jnp.abs(y)))
    lw_ref[...] = -jnp.exp(-softplus - 0.5)
    ha = jnp.dot(mix(5), a1_ref[...], preferred_element_type=F32).astype(BF16)
    a_ref[...] = _sigmoid(a0_ref[...] + jnp.dot(ha, a2_ref[...], preferred_element_type=F32))


def _rwkv_proj(x2, shift, t_len, mu, w_rkvz, w0, w1, w2, a0, a1, a2):
    m, d = x2.shape
    seq = t_len > 1
    tm = min(256, t_len) if seq else m
    assert m % tm == 0 and (not seq or (t_len % tm == 0 and tm % SUBLANES == 0))
    tiles_per_seq = t_len // tm if seq else 1
    full = lambda a: pl.BlockSpec(a.shape, lambda i: (0,) * a.ndim)
    row = pl.BlockSpec((tm, d), lambda i: (i, 0))
    if seq:
        blk = tm // SUBLANES
        lead = [x2, x2, shift.reshape(-1, 1, d)]
        lead_specs = [row, pl.BlockSpec((SUBLANES, d), lambda i: (jnp.maximum(i * blk - 1, 0), 0)),
                      pl.BlockSpec((1, 1, d), lambda i: (i // tiles_per_seq, 0, 0))]
    else:
        lead = [x2, shift]
        lead_specs = [row, row]
    ws = [mu, w_rkvz.astype(BF16), w1.astype(BF16), w2.astype(BF16), a1.astype(BF16), a2.astype(BF16),
          w0.reshape(1, d), a0.reshape(1, d)]
    return pl.pallas_call(
        functools.partial(_rwkv_proj_kernel, seq=seq, tiles_per_seq=tiles_per_seq),
        grid=(m // tm,),
        in_specs=lead_specs + [full(a) for a in ws],
        out_specs=(row,) * 6,
        out_shape=(jax.ShapeDtypeStruct((m, d), F32),) * 6,
        compiler_params=_cparams(1),
        name="rwkv_proj",
    )(*lead, *ws)


WKV_CHUNK = 64
WKV_PASSES = 1


def _wkv_kernel(r_ref, k_ref, v_ref, lw_ref, a_ref, sz_ref, s0_ref, kk_ref, ka_ref, rk_ref, gg_ref, gb_ref,
                yz_ref, sfin_ref, h_s, *, c_len, n_real, n_heads, hd):
    c = pl.program_id(1)
    ps = WKV_PASSES
    mm = functools.partial(_mm, pa=ps, pb=ps)

    @pl.when(c == 0)
    def _():
        for h in range(n_heads):
            h_s[h] = s0_ref[0, h].T

    r = r_ref[0]
    k = k_ref[0]
    v = v_ref[0]
    lw = lw_ref[0]
    a = a_ref[0]
    row = lax.broadcasted_iota(jnp.int32, (c_len, c_len), 0)
    col = lax.broadcasted_iota(jnp.int32, (c_len, c_len), 1)
    low_inc = row >= col
    low_exc = row > col
    eye = jnp.where(row == col, 1.0, 0.0)
    cum = _mm(jnp.where(low_inc, 1.0, 0.0).astype(BF16), lw, NN, pb=3)
    p_inc = jnp.exp(cum)
    p_exc = jnp.exp(cum - lw)
    p_inv = jnp.exp(-cum)
    cum_end = cum[c_len - 1:c_len, :]
    p_rest = jnp.exp(cum_end - cum)
    p_end = jnp.exp(cum_end)
    kk = k * kk_ref[...]
    kmod = k * (1.0 + (a - 1.0) * ka_ref[...])
    r_t = r * p_inc
    k_t = kmod * p_inv
    k_e = kmod * p_rest
    rkr = r * kmod * rk_ref[...]
    hr = lax.broadcasted_iota(jnp.int32, (hd, hd), 0)
    hc = lax.broadcasted_iota(jnp.int32, (hd, hd), 1)
    levels = max((n_real - 1).bit_length() - 1, 0)

    heads = range(n_heads)
    hsl = [slice(h * hd, (h + 1) * hd) for h in heads]
    ka_h, lhs, rhs, vh = [], [], [], []
    for h in heads:
        hs = hsl[h]
        kkh = kk[:, hs]
        kp = kkh / jnp.maximum(jnp.sqrt(jnp.sum(kkh * kkh, axis=-1, keepdims=True)), 1e-12)
        ka_h.append(kp * a[:, hs])
        lhs.append(jnp.concatenate([kp * p_exc[:, hs], r_t[:, hs]], axis=0).astype(BF16))
        rhs.append(jnp.concatenate([ka_h[h] * p_inv[:, hs], k_t[:, hs]], axis=0).astype(BF16))
        vh.append(v[:, hs])
    h_in = [h_s[h] for h in heads]
    am = [mm(lhs[h], rhs[h], NT) for h in heads]
    gh = [mm(lhs[h], h_in[h], NN) for h in heads]
    l_b = [jnp.where(low_exc, am[h][:c_len, :c_len], 0.0) for h in heads]
    l_k = [jnp.where(low_exc, am[h][:c_len, c_len:], 0.0).astype(BF16) for h in heads]
    m_r = [jnp.concatenate([jnp.where(low_inc, am[h][c_len:, :c_len], 0.0),
                            jnp.where(low_inc, am[h][c_len:, c_len:], 0.0)], axis=1).astype(BF16)
           for h in heads]
    vb = [vh[h].astype(BF16) for h in heads]
    rhs_u = [gh[h][:c_len] + mm(l_k[h], vb[h], NN) for h in heads]
    t_inv = [eye - l_b[h] for h in heads]
    pw = l_b
    for _ in range(levels):
        pw = [mm(pw[h], pw[h], NN) for h in heads]
        t_inv = [t_inv[h] + mm(t_inv[h], pw[h], NN) for h in heads]
    u = [-mm(t_inv[h], rhs_u[h], NN) for h in heads]
    uv = [jnp.concatenate([u[h].astype(BF16), vb[h]], axis=0) for h in heads]
    y = [gh[h][c_len:] + mm(m_r[h], uv[h], NN) for h in heads]
    for h in heads:
        hs = hsl[h]
        lhs2 = jnp.concatenate([ka_h[h] * p_rest[:, hs], k_e[:, hs], jnp.where(hr == hc, p_end[:, hs], 0.0)],
                               axis=0)
        h_s[h] = mm(lhs2, jnp.concatenate([uv[h], h_in[h].astype(BF16)], axis=0), TN)
    outs = []
    for h in heads:
        hs = hsl[h]
        mu = jnp.mean(y[h], axis=-1, keepdims=True)
        dy = y[h] - mu
        var = jnp.mean(dy * dy, axis=-1, keepdims=True)
        yn = dy * lax.rsqrt(var + GN_EPS) * gg_ref[:, hs] + gb_ref[:, hs]
        outs.append(yn + jnp.sum(rkr[:, hs], axis=-1, keepdims=True) * vh[h])
    yz_ref[0] = (jnp.concatenate(outs, axis=1) * sz_ref[0]).astype(yz_ref.dtype)

    @pl.when(c == pl.num_programs(1) - 1)
    def _():
        for h in range(n_heads):
            sfin_ref[0, h] = h_s[h].T


def _wkv_scan(r3, k3, v3, lw3, a3, sz3, s0, k_k, k_a, r_k, gn_g, gn_b, c_len, t_real):
    bsz, t, d = r3.shape
    n_heads = d // HD_C
    assert t % c_len == 0
    vec = lambda x: x.reshape(1, d)
    seq = pl.BlockSpec((1, c_len, d), lambda b, c: (b, c, 0))
    st = pl.BlockSpec((1, n_heads, HD_C, HD_C), lambda b, c: (b, 0, 0, 0))
    par = pl.BlockSpec((1, d), lambda b, c: (0, 0))
    return pl.pallas_call(
        functools.partial(_wkv_kernel, c_len=c_len, n_real=min(t_real, c_len), n_heads=n_heads, hd=HD_C),
        grid=(bsz, t // c_len),
        in_specs=[seq] * 6 + [st] + [par] * 5,
        out_specs=(seq, st),
        out_shape=(jax.ShapeDtypeStruct((bsz, t, d), BF16),
                   jax.ShapeDtypeStruct((bsz, n_heads, HD_C, HD_C), F32)),
        scratch_shapes=[pltpu.VMEM((n_heads, HD_C, HD_C), F32)],
        compiler_params=_cparams(2),
        name="wkv_scan",
    )(r3, k3, v3, lw3, a3, sz3, s0, vec(k_k), vec(k_a), vec(r_k), vec(gn_g), vec(gn_b))


def _even_layer(xp, xs, cache_cmp, cache_sel, cache_win, state_conv, page_table,
                w_in, conv_w, conv_b, cln_g, cln_b, wk, wv, w_out, ln_g, ln_b, alpha):
    bp, tp, d = xp.shape
    bs, ts, _ = xs.shape
    assert ts == 1
    d_a = conv_w.shape[-1]
    n_taps = conv_w.shape[0]
    l_cmp, kv_h = wk.shape
    hd = cache_cmp.shape[-1]
    d_b = H_B * hd
    d_kv = 2 * kv_h * hd
    w_bf = _prep_w_in(w_in, d_a, d_b, kv_h, hd)
    cw = conv_w.reshape(n_taps, d_a)
    w_out_bf = w_out.astype(BF16)
    kv_shape = lambda b, t: (b, t, 2, kv_h, hd)

    u, sza, q, cmp_n, sel_n, win_n, szb, gate = _even_in_proj(xp.reshape(bp * tp, d), w_bf, d_a, d_b, d_kv, kv_h, hd)
    r3 = lambda a: a.reshape(bp, tp, a.shape[-1])
    u3 = r3(u)
    ya = _conv_branch(jnp.zeros((bp, CONV_HIST, d_a), F32), u3, r3(sza), cw, conv_b, cln_g, cln_b)
    yb = _nsa_prompt(r3(q), r3(cmp_n), r3(sel_n), r3(win_n), r3(gate), r3(szb), wk, wv)
    yp = _out_proj_ln([ya.reshape(bp * tp, d_a), yb.reshape(bp * tp, d_b)], [w_out_bf[:d_a], w_out_bf[d_a:]],
                      xp.reshape(bp * tp, d), ln_g, ln_b, alpha).reshape(bp, tp, d)
    w_keep = min(WINDOW, tp)
    outs_p = (cmp_n.reshape(kv_shape(bp, tp)), sel_n.reshape(kv_shape(bp, tp)),
              win_n.reshape(kv_shape(bp, tp))[:, tp - w_keep:], u3[:, tp - (n_taps - 1):])

    u, sza, q, cmp_s, sel_s, win_s, szb, gate = _even_in_proj(xs.reshape(bs, d), w_bf, d_a, d_b, d_kv, kv_h, hd)
    ext = jnp.concatenate([state_conv, u[:, None, :]], axis=1)
    hist = jnp.pad(state_conv, ((0, 0), (CONV_HIST - (n_taps - 1), 0), (0, 0)))
    pad_rows = lambda a: jnp.pad(a[:, None, :], ((0, 0), (0, SUBLANES - 1), (0, 0)))
    ya = _conv_branch(hist, pad_rows(u), pad_rows(sza), cw, conv_b, cln_g, cln_b)[:, 0]
    n_pool, page = cache_cmp.shape[:2]
    w_buf = cache_win.shape[1]
    fm = lambda c: jnp.transpose(c, (0, 2, 3, 4, 1)).reshape(c.shape[0], d_kv, c.shape[1])
    yb = _nsa_decode_t(q.astype(F32).reshape(bs, H_B, hd), fm(cache_cmp), fm(cache_sel), fm(cache_win), page_table,
                       sel_s[:, None, :], win_s[:, None, :], gate[:, None, :], szb.reshape(bs, H_B, hd), wk, wv)
    ys = _out_proj_ln([ya, yb.reshape(bs, d_b)], [w_out_bf[:d_a], w_out_bf[d_a:]], xs.reshape(bs, d),
                      ln_g, ln_b, alpha).reshape(bs, 1, d)
    ctx = jnp.concatenate([cache_win, win_s.reshape(kv_shape(bs, 1))], axis=1)
    outs_s = (cmp_s.reshape(kv_shape(bs, 1)), sel_s.reshape(kv_shape(bs, 1)),
              ctx[:, ctx.shape[1] - min(WINDOW, ctx.shape[1]):], ext[:, 1:])
    return yp, ys, outs_p, outs_s


def _odd_group(x3, shift, s0, mu, w_rkvz, w0, w1, w2, a0, a1, a2, k_k, k_a, r_k, gn_g, gn_b, w_out_bf,
               ln_g, ln_b, alpha):
    bsz, t, d = x3.shape
    x2 = x3.reshape(bsz * t, d)
    r, k, v, lw, a, sz = _rwkv_proj(x2, shift, t, mu, w_rkvz, w0, w1, w2, a0, a1, a2)
    c_len = WKV_CHUNK if t >= WKV_CHUNK else -(-t // SUBLANES) * SUBLANES
    t_pad = -(-t // c_len) * c_len
    r3 = lambda z: jnp.pad(z.reshape(bsz, t, d), ((0, 0), (0, t_pad - t), (0, 0)))
    yz, s_fin = _wkv_scan(r3(r), r3(k), r3(v), r3(lw), r3(a), r3(sz), s0, k_k, k_a, r_k.reshape(-1), gn_g, gn_b,
                          c_len, t)
    y = _out_proj_ln([yz[:, :t].reshape(bsz * t, d)], [w_out_bf], x2, ln_g, ln_b, alpha).reshape(bsz, t, d)
    return y, s_fin, x3[:, -1]


def kernel(x_prompt, x_sample, cache_cmp_kv, cache_sel_kv, cache_win_kv, state_conv, state_wkv, state_shift,
           page_table, w_in_even, conv_w, conv_b, conv_ln_g, conv_ln_b, wk_cmp, wv_cmp, w_out_even, mu_c, w_rkvz,
           w0, w1, w2, a0, a1, a2, k_k, k_a, r_k, gn_g, gn_b, w_out_odd, ln_g, ln_b):
    depth = ln_g.shape[0]
    alpha = (2 * depth) ** 0.25
    bp, _, d = x_prompt.shape
    n_heads = d // HD_C
    xp, xs = x_prompt, x_sample
    even_p, even_s, odd_p, odd_s = [], [], [], []
    for l in range(depth):
        if l % 2 == 0:
            e = l // 2
            xp, xs, o_p, o_s = _even_layer(
                xp, xs, cache_cmp_kv[e], cache_sel_kv[e], cache_win_kv[e], state_conv[e], page_table,
                w_in_even[e], conv_w[e], conv_b[e], conv_ln_g[e], conv_ln_b[e], wk_cmp[e], wv_cmp[e],
                w_out_even[e], ln_g[l], ln_b[l], alpha)
            even_p.append(o_p)
            even_s.append(o_s)
        else:
            o = l // 2
            po = (mu_c[o], w_rkvz[o], w0[o], w1[o], w2[o], a0[o], a1[o], a2[o], k_k[o], k_a[o], r_k[o],
                  gn_g[o], gn_b[o], w_out_odd[o].astype(BF16), ln_g[l], ln_b[l], alpha)
            xp, s_p, h_p = _odd_group(xp, jnp.zeros((bp, d), F32), jnp.zeros((bp, n_heads, HD_C, HD_C), F32), *po)
            xs, s_s, h_s = _odd_group(xs, state_shift[o], state_wkv[o], *po)
            odd_p.append((s_p, h_p))
            odd_s.append((s_s, h_s))
    stack = lambda items, j: jnp.stack([it[j] for it in items])
    return (xp, xs,
            stack(even_p, 0), stack(even_s, 0), stack(even_p, 1), stack(even_s, 1),
            stack(even_p, 2), stack(even_s, 2), stack(even_p, 3), stack(even_s, 3),
            stack(odd_p, 0), stack(odd_s, 0), stack(odd_p, 1), stack(odd_s, 1))
```

```python
import functools

import jax
import jax.numpy as jnp
from jax import lax
from jax.experimental import pallas as pl
from jax.experimental.pallas import tpu as pltpu

F32 = jnp.float32
BF16 = jnp.bfloat16

H_B = 8
L_SEL = 64
K_SEL = 16
WINDOW = 512
FORCE = 1e4
HD_C = 64
GN_EPS = 64e-5
LN_EPS = 1e-5
NEG = -1e30

LANES = 128
SUBLANES = 8
VMEM_LIMIT = 56 * 1024 * 1024

NT = (((1,), (1,)), ((), ()))
TN = (((0,), (0,)), ((), ()))
NN = (((1,), (0,)), ((), ()))


def _cparams(n_axes):
    return pltpu.CompilerParams(dimension_semantics=("arbitrary",) * n_axes,
                                vmem_limit_bytes=VMEM_LIMIT)


def _sigmoid(x):
    return 1.0 / (1.0 + jnp.exp(-x))


def _silu(x):
    return x * _sigmoid(x)


def _split_bf16(x, n):
    parts, rem = [], x
    for i in range(n):
        p = rem.astype(BF16)
        parts.append(p)
        if i + 1 < n:
            rem = rem - p.astype(F32)
    return parts


def _mm(a, b, dims=NN, pa=1, pb=1):
    aa = [a] if a.dtype == BF16 else _split_bf16(a, pa)
    bb = [b] if b.dtype == BF16 else _split_bf16(b, pb)
    keep = max(len(aa), len(bb))
    out = None
    for i, ai in enumerate(aa):
        for j, bj in enumerate(bb):
            if i + j < keep:
                t = lax.dot_general(ai, bj, dims, preferred_element_type=F32)
                out = t if out is None else out + t
    return out


def _layer_norm_rows(h, g, b, eps):
    mu = jnp.mean(h, axis=-1, keepdims=True)
    d = h - mu
    var = jnp.mean(d * d, axis=-1, keepdims=True)
    return d * lax.rsqrt(var + eps) * g + b


def _even_in_proj_kernel(x_ref, w_ref, u_ref, sza_ref, q_ref, cmp_ref, sel_ref, win_ref, szb_ref, gate_ref,
                         *, d_a, d_b, d_kv, q_scale):
    xb = x_ref[...].astype(BF16)

    def seg(lo, n):
        return jnp.dot(xb, w_ref[:, lo:lo + n], preferred_element_type=F32)

    o = 0
    a_val = seg(o, d_a); o += d_a
    a_glu = seg(o, d_a); o += d_a
    u_ref[...] = a_val * _sigmoid(a_glu)
    sza_ref[...] = _silu(seg(o, d_a)); o += d_a
    q_ref[...] = (seg(o, d_b) * q_scale).astype(BF16); o += d_b
    for ref in (cmp_ref, sel_ref, win_ref):
        ref[...] = seg(o, d_kv); o += d_kv
    szb_ref[...] = _silu(seg(o, d_b)); o += d_b
    gate_ref[...] = _sigmoid(seg(o, gate_ref.shape[-1]))


def _prep_w_in(w, d_a, d_b, kv_h, hd):
    grp = H_B // kv_h
    c_kv6 = 3 * d_a + d_b
    c_g3 = c_kv6 + 6 * kv_h * hd
    c_zb = c_g3 + 3 * H_B
    gate_blocks = []
    for g in range(kv_h):
        blk = w[:, c_g3 + g * grp * 3: c_g3 + (g + 1) * grp * 3]
        gate_blocks.append(jnp.pad(blk, ((0, 0), (0, LANES - grp * 3))))
    wn = jnp.concatenate([w[:, :c_g3], w[:, c_zb:c_zb + d_b]] + gate_blocks, axis=1)
    return wn.astype(BF16)


def _even_in_proj(x2, w_bf, d_a, d_b, d_kv, kv_h, hd):
    m, d = x2.shape
    tm = min(512, m)
    assert m % tm == 0
    n_gate = kv_h * LANES
    row = lambda n: pl.BlockSpec((tm, n), lambda i: (i, 0))
    out_shape = (jax.ShapeDtypeStruct((m, d_a), F32), jax.ShapeDtypeStruct((m, d_a), F32),
                 jax.ShapeDtypeStruct((m, d_b), BF16),
                 jax.ShapeDtypeStruct((m, d_kv), F32), jax.ShapeDtypeStruct((m, d_kv), F32),
                 jax.ShapeDtypeStruct((m, d_kv), F32),
                 jax.ShapeDtypeStruct((m, d_b), F32), jax.ShapeDtypeStruct((m, n_gate), F32))
    return pl.pallas_call(
        functools.partial(_even_in_proj_kernel, d_a=d_a, d_b=d_b, d_kv=d_kv, q_scale=hd ** -0.5),
        grid=(m // tm,),
        in_specs=[row(d), pl.BlockSpec(w_bf.shape, lambda i: (0, 0))],
        out_specs=(row(d_a), row(d_a), row(d_b), row(d_kv), row(d_kv), row(d_kv), row(d_b), row(n_gate)),
        out_shape=out_shape,
        compiler_params=_cparams(1),
        name="even_in_proj",
    )(x2, w_bf)


CONV_HIST = 32


def _conv_kernel(hist_ref, u_ref, sza_ref, w_ref, cb_ref, g_ref, b_ref, ya_ref, win_ref, sh_ref,
                 *, tq, rb, n_taps):
    i = pl.program_id(1)
    base = pl.multiple_of(i * tq, SUBLANES)
    win_ref[CONV_HIST:, :] = u_ref[0, pl.ds(base, tq), :]

    @pl.when(i == 0)
    def _():
        win_ref[:CONV_HIST, :] = hist_ref[0]

    @pl.when(i > 0)
    def _():
        win_ref[:CONV_HIST, :] = u_ref[0, pl.ds(base - CONV_HIST, CONV_HIST), :]

    w = win_ref[...]
    n = tq + CONV_HIST
    sh_ref[0] = w
    for s in range(1, SUBLANES):
        sh_ref[s] = pltpu.roll(w, n - s, 0)

    first = CONV_HIST - (n_taps - 1)
    cb = cb_ref[...]
    g = g_ref[...]
    b = b_ref[...]

    def block(k, carry):
        r0 = pl.multiple_of(k * rb, SUBLANES)
        acc = jnp.zeros((rb, u_ref.shape[-1]), F32) + cb
        for j in range(n_taps):
            a, s = divmod(first + j, SUBLANES)
            acc = acc + sh_ref[s, pl.ds(r0 + SUBLANES * a, rb), :] * w_ref[j:j + 1, :]
        y = _silu(_layer_norm_rows(acc, g, b, LN_EPS)) * sza_ref[0, pl.ds(r0, rb), :]
        ya_ref[0, pl.ds(r0, rb), :] = y.astype(ya_ref.dtype)
        return carry

    lax.fori_loop(0, tq // rb, block, 0)


def _conv_branch(hist, u3, sza3, conv_w, conv_b, ln_g, ln_b):
    bsz, t, d = u3.shape
    n_taps = conv_w.shape[0]
    assert n_taps - 1 <= CONV_HIST and t % SUBLANES == 0
    tq = min(256, t)
    rb = min(32, tq)
    assert t % tq == 0 and tq % rb == 0
    vec = lambda a: a.reshape(1, d)
    full = lambda shp: pl.BlockSpec(shp, lambda b, i: (0,) * len(shp))
    return pl.pallas_call(
        functools.partial(_conv_kernel, tq=tq, rb=rb, n_taps=n_taps),
        grid=(bsz, t // tq),
        in_specs=[pl.BlockSpec((1, CONV_HIST, d), lambda b, i: (b, 0, 0)),
                  pl.BlockSpec((1, t, d), lambda b, i: (b, 0, 0)),
                  pl.BlockSpec((1, tq, d), lambda b, i: (b, i, 0)),
                  full((n_taps, d)), full((1, d)), full((1, d)), full((1, d))],
        out_specs=pl.BlockSpec((1, tq, d), lambda b, i: (b, i, 0)),
        out_shape=jax.ShapeDtypeStruct((bsz, t, d), BF16),
        scratch_shapes=[pltpu.VMEM((tq + CONV_HIST, d), F32),
                        pltpu.VMEM((SUBLANES, tq + CONV_HIST, d), F32)],
        compiler_params=_cparams(2),
        name="conv_branch",
    )(hist, u3, sza3, conv_w, vec(conv_b), vec(ln_g), vec(ln_b))


SEL_OFF = -(2.0 ** 100)


def _nsa_prompt_kernel(q_ref, kcmp_ref, ksel_ref, kwin_ref, gate_ref, szb_ref, wpool_ref, o_ref,
                       kc_s, vc_s, ks_s, vs_s, kw_s, vw_s, m_s, l_s, acc_s,
                       *, tq, t_len, l_cmp, hd, grp, kv_h):
    i = pl.program_id(1)
    g = pl.program_id(2)
    n_cmp = t_len // l_cmp
    n_sel = t_len // L_SEL
    half = n_cmp // 2
    gw = kv_h * hd

    @pl.when((i == 0) & (g == 0))
    def _prep():
        x3 = kcmp_ref[0].reshape(half, 2 * l_cmp, 2 * gw)
        pooled = jnp.concatenate([jnp.sum(x3 * wpool_ref[0][None], axis=1),
                                  jnp.sum(x3 * wpool_ref[1][None], axis=1)], axis=0)
        pos = lax.broadcasted_iota(jnp.int32, (t_len, LANES), 0)
        lane = lax.broadcasted_iota(jnp.int32, (t_len, LANES), 1)
        blk = pos // L_SEL
        feat = jnp.where(lane == blk, 1.0, 0.0)
        feat = jnp.where(lane == n_sel, blk.astype(F32), feat)
        feat = jnp.where(lane == n_sel + 1, (pos - blk * L_SEL).astype(F32), feat)
        for src, dst in ((ksel_ref, ks_s), (kwin_ref, kw_s)):
            k01 = src[0, :, :gw]
            dst[0] = jnp.where(lane < hd, feat, pltpu.roll(k01, hd, 1)).astype(BF16)
            dst[1] = jnp.where(lane < hd, feat, k01).astype(BF16)
        vc_pad = jnp.concatenate([pooled[:, gw:], jnp.zeros((LANES - n_cmp, gw), F32)], axis=0) \
            if n_cmp < LANES else pooled[:, gw:]
        vc_t = vc_pad.T
        vs_t = ksel_ref[0, :, gw:].T
        vw_t = kwin_ref[0, :, gw:].T
        for gg in range(kv_h):
            kc_s[gg] = pooled[:, gg * hd:(gg + 1) * hd].astype(BF16)
            vc_s[gg] = vc_t[gg * hd:(gg + 1) * hd, :n_cmp].astype(BF16)
            vs_s[gg] = vs_t[gg * hd:(gg + 1) * hd].astype(BF16)
            vw_s[gg] = vw_t[gg * hd:(gg + 1) * hd].astype(BF16)

    t0 = i * tq
    gs = jnp.float32(1.0)
    for gg in range(1, kv_h):
        gs = jnp.where(g == gg, 2.0 ** -(gg * grp), gs)
    slopes = [gs * (2.0 ** -(r + 1)) for r in range(grp)]
    qt = q_ref[0]
    qs = jnp.concatenate([qt[:, r * hd:(r + 1) * hd] for r in range(grp)], axis=0)

    s_t = lax.dot_general(kc_s[g], qs, NT, preferred_element_type=F32)
    n_io = lax.broadcasted_iota(jnp.int32, (n_cmp, tq), 0)
    t_io = lax.broadcasted_iota(jnp.int32, (n_cmp, tq), 1) + t0
    c_end = jnp.where(n_io < half, 2 * l_cmp * n_io + (l_cmp - 1), 2 * l_cmp * (n_io - half) + (2 * l_cmp - 1))
    dist_c = t_io - c_end
    valid_c = dist_c >= 0
    dist_cf = dist_c.astype(F32)
    imp = jnp.zeros((n_sel, tq), F32)
    o_c = []
    for r in range(grp):
        s = jnp.where(valid_c, s_t[:, r * tq:(r + 1) * tq] - slopes[r] * dist_cf, NEG)
        e = jnp.exp(s - jnp.max(s, axis=0, keepdims=True))
        p = jnp.where(valid_c, e / jnp.sum(e, axis=0, keepdims=True), 0.0)
        imp = imp + (p[:half] + p[half:])
        o_c.append(jnp.dot(vc_s[g], p.astype(BF16), preferred_element_type=F32))

    sb = lax.broadcasted_iota(jnp.int32, (n_sel, tq), 0)
    tb = (lax.broadcasted_iota(jnp.int32, (n_sel, tq), 1) + t0) // L_SEL
    forced = (sb == 0) | (sb == tb) | (sb == tb - 1)
    imp = jnp.where(forced, FORCE, imp)
    imp = jnp.where(sb <= tb, imp, -jnp.inf)
    rank = jnp.zeros((n_sel, tq), F32)
    for j in range(n_sel):
        vj = imp[j:j + 1, :]
        beats = jnp.where(imp > vj, 1.0, jnp.where((imp == vj) & (sb < j), 1.0, 0.0))
        rank = jnp.where(sb == j, jnp.sum(beats, axis=0, keepdims=True), rank)
    off_t = jnp.where(rank < min(K_SEL, n_sel), 0.0, SEL_OFF)
    feat_sel = jnp.concatenate([off_t, jnp.zeros((LANES - n_sel, tq), F32)], axis=0).T

    qf = q_ref[0].astype(F32)
    lane_q = lax.broadcasted_iota(jnp.int32, (tq, LANES), 1)

    def q_aug(r, feat):
        tile = qf[:, (r // 2) * LANES:(r // 2 + 1) * LANES]
        if r % 2 == 0:
            tile = pltpu.roll(tile, hd, 1)
        x = jnp.where(lane_q < hd, feat, tile)
        x = jnp.where(lane_q == n_sel, slopes[r] * L_SEL, x)
        return jnp.where(lane_q == n_sel + 1, slopes[r], x).astype(BF16)

    d0 = (lax.broadcasted_iota(jnp.int32, (tq, tq), 1) - lax.broadcasted_iota(jnp.int32, (tq, tq), 0))

    def reset():
        m_s[...] = jnp.full(m_s.shape, NEG, F32)
        l_s[...] = jnp.zeros(l_s.shape, F32)
        acc_s[...] = jnp.zeros(acc_s.shape, F32)

    def tile_update(q_list, k_s, v_s, kt, mask):
        k0 = pl.multiple_of(kt * tq, tq)
        kk = k_s[g, pl.ds(k0, tq), :]
        vt = v_s[g, :, pl.ds(k0, tq)]
        heads = range(grp)
        s = [lax.dot_general(kk, q_list[r], NT, preferred_element_type=F32) for r in heads]
        if mask is not None:
            s = [jnp.where(mask, s[r], NEG) for r in heads]
        m_old = [m_s[r] for r in heads]
        m_new = [jnp.maximum(m_old[r], jnp.max(s[r], axis=0, keepdims=True)) for r in heads]
        al = [jnp.exp(m_old[r] - m_new[r]) for r in heads]
        p = [jnp.exp(s[r] - m_new[r]) for r in heads]
        for r in heads:
            m_s[r] = m_new[r]
            l_s[r] = al[r] * l_s[r] + jnp.sum(p[r], axis=0, keepdims=True)
            acc_s[r] = al[r] * acc_s[r] + jnp.dot(vt, p[r].astype(BF16), preferred_element_type=F32)

    q_sel = [q_aug(r, feat_sel) for r in range(grp)]
    reset()

    def sel_body(kt, carry):
        tile_update(q_sel, ks_s, vs_s, kt, None)
        return carry

    lax.fori_loop(0, i, sel_body, 0)
    tile_update(q_sel, ks_s, vs_s, i, d0 >= 0)
    o_s = [acc_s[r] / l_s[r] for r in range(grp)]

    q_win = [q_aug(r, 0.0) for r in range(grp)]
    reset()
    for j in range(-(-(WINDOW - 1) // tq), 0, -1):
        d_j = d0 + j * tq
        inside = j * tq - (tq - 1) >= 0 and j * tq + (tq - 1) < WINDOW
        mask_j = None if inside else (d_j >= 0) & (d_j < WINDOW)

        @pl.when(i >= j)
        def _(j=j, mask_j=mask_j):
            tile_update(q_win, kw_s, vw_s, i - j, mask_j)

    tile_update(q_win, kw_s, vw_s, i, (d0 >= 0) & (d0 < WINDOW))
    o_w = [acc_s[r] / l_s[r] for r in range(grp)]

    gate_t = gate_ref[0].T
    outs = []
    for r in range(grp):
        outs.append(gate_t[3 * r:3 * r + 1] * o_c[r]
                    + gate_t[3 * r + 1:3 * r + 2] * o_s[r]
                    + gate_t[3 * r + 2:3 * r + 3] * o_w[r])
    o_ref[0] = (jnp.concatenate(outs, axis=0).T * szb_ref[0]).astype(o_ref.dtype)


def _pool_weights(wk, wv, hd, halves):
    l_cmp, kv_h = wk.shape
    row = jnp.concatenate([jnp.repeat(wk, hd, axis=1), jnp.repeat(wv, hd, axis=1)], axis=1)
    if not halves:
        return row
    z = jnp.zeros_like(row)
    return jnp.stack([jnp.concatenate([row, z], axis=0), jnp.concatenate([z, row], axis=0)])


def _nsa_prompt(q3, cmp3, sel3, win3, gate3, szb3, wk, wv):
    bsz, t, d_b = q3.shape
    l_cmp, kv_h = wk.shape
    grp = H_B // kv_h
    hd = d_b // H_B
    gw = kv_h * hd
    tq = min(256, t)
    assert t % tq == 0 and t % (2 * l_cmp) == 0 and L_SEL == 2 * l_cmp and tq % L_SEL == 0
    n_cmp = t // l_cmp
    n_sel = t // L_SEL
    assert gw == LANES and 2 * hd == LANES and n_sel + 2 <= hd and n_sel % SUBLANES == 0 and n_cmp <= LANES
    wpool = _pool_weights(wk, wv, hd, halves=True)
    kv_spec = pl.BlockSpec((1, t, 2 * gw), lambda b, i, g: (b, 0, 0))
    return pl.pallas_call(
        functools.partial(_nsa_prompt_kernel, tq=tq, t_len=t, l_cmp=l_cmp, hd=hd, grp=grp, kv_h=kv_h),
        grid=(bsz, t // tq, kv_h),
        in_specs=[pl.BlockSpec((1, tq, grp * hd), lambda b, i, g: (b, i, g)),
                  kv_spec, kv_spec, kv_spec,
                  pl.BlockSpec((1, tq, LANES), lambda b, i, g: (b, i, g)),
                  pl.BlockSpec((1, tq, grp * hd), lambda b, i, g: (b, i, g)),
                  pl.BlockSpec(wpool.shape, lambda b, i, g: (0, 0, 0))],
        out_specs=pl.BlockSpec((1, tq, grp * hd), lambda b, i, g: (b, i, g)),
        out_shape=jax.ShapeDtypeStruct((bsz, t, d_b), BF16),
        scratch_shapes=[pltpu.VMEM((kv_h, n_cmp, hd), BF16), pltpu.VMEM((kv_h, hd, n_cmp), BF16),
                        pltpu.VMEM((kv_h, t, LANES), BF16), pltpu.VMEM((kv_h, hd, t), BF16),
                        pltpu.VMEM((kv_h, t, LANES), BF16), pltpu.VMEM((kv_h, hd, t), BF16),
                        pltpu.VMEM((grp, 1, tq), F32), pltpu.VMEM((grp, 1, tq), F32),
                        pltpu.VMEM((grp, hd, tq), F32)],
        compiler_params=_cparams(3),
        name="nsa_prompt",
    )(q3, cmp3, sel3, win3, gate3, szb3, wpool)


def _nsa_decode_t_kernel(pt_ref, *refs, pg, n_pages, page, l_cmp, hd, grp, kv_h, w_buf):
    del pt_ref
    q_ref = refs[0]
    cmp_refs = refs[1:1 + pg]
    sel_refs = refs[1 + pg:1 + 2 * pg]
    (kwin_ref, seln_ref, winn_ref, gate_ref, szb_ref, wrow_ref, seg_ref, o_ref,
     kcv_s, selm_s, oc_s, m_s, l_s, acc_s) = refs[1 + 2 * pg:]
    p = pl.program_id(1)
    n_steps = n_pages // pg
    past = n_pages * page
    n_cmp = past // l_cmp
    n_sel = past // L_SEL
    per_page = page // l_cmp
    pages_per_chunk = LANES // per_page
    steps_per_chunk = pages_per_chunk // pg
    n_chunks = n_pages // pages_per_chunk
    gw = kv_h * hd
    nc_pad = selm_s.shape[-1]
    qf = q_ref[0]
    q = qf.astype(BF16)
    head = lax.broadcasted_iota(jnp.int32, (H_B, 1), 0)
    slope = jnp.zeros((H_B, 1), F32)
    for hh in range(H_B):
        slope = jnp.where(head == hh, 2.0 ** -(hh + 1), slope)

    @pl.when(p < n_steps)
    def _pool():
        sub = p % steps_per_chunk
        prod = jnp.concatenate([cmp_refs[j][0] * wrow_ref[...] for j in range(pg)], axis=1)
        seg = seg_ref[pl.ds(pl.multiple_of(sub * (pg * page), pg * page), pg * page), :]
        tot = _mm(prod, seg, NN, pa=2)
        c = p // steps_per_chunk

        @pl.when(sub == 0)
        def _():
            kcv_s[c] = tot

        @pl.when(sub != 0)
        def _():
            kcv_s[c] = kcv_s[c] + tot

    @pl.when(p == n_steps - 1)
    def _compressed():
        n_io = lax.broadcasted_iota(jnp.int32, (1, n_cmp), 1)
        dist = (past - (l_cmp * n_io + (l_cmp - 1))).astype(F32)
        pair = jnp.where(lax.broadcasted_iota(jnp.int32, (n_cmp, nc_pad), 0) // (L_SEL // l_cmp)
                         == lax.broadcasted_iota(jnp.int32, (n_cmp, nc_pad), 1), 1.0, 0.0).astype(BF16)
        c_io = lax.broadcasted_iota(jnp.int32, (1, nc_pad), 1)
        i_r = lax.broadcasted_iota(jnp.int32, (nc_pad, nc_pad), 1)
        i_c = lax.broadcasted_iota(jnp.int32, (nc_pad, nc_pad), 0)
        selm_s[...] = jnp.zeros(selm_s.shape, F32)
        for g in range(kv_h):
            kc_t = jnp.concatenate([kcv_s[c, g * hd:(g + 1) * hd, :] for c in range(n_chunks)], axis=1)
            vc_t = jnp.concatenate([kcv_s[c, gw + g * hd: gw + (g + 1) * hd, :] for c in range(n_chunks)], axis=1)
            s = jnp.dot(q, kc_t.astype(BF16), preferred_element_type=F32) - slope * dist
            e = jnp.exp(s - jnp.max(s, axis=-1, keepdims=True))
            pc = e / jnp.sum(e, axis=-1, keepdims=True)
            oc_s[g] = lax.dot_general(pc.astype(BF16), vc_t.astype(BF16), NT, preferred_element_type=F32)
            in_grp = (head >= g * grp) & (head < (g + 1) * grp)
            imp = _mm(jnp.sum(jnp.where(in_grp, pc, 0.0), axis=0, keepdims=True), pair, NN, pa=3)
            forced = (c_io == 0) | (c_io == n_sel - 1)
            imp = jnp.where(forced, FORCE, imp)
            imp = jnp.where(c_io < n_sel, imp, -jnp.inf)
            v_r = jnp.broadcast_to(imp, (nc_pad, nc_pad))
            v_c = v_r.T
            beats = jnp.where(v_c > v_r, 1.0, jnp.where((v_c == v_r) & (i_c < i_r), 1.0, 0.0))
            rank = jnp.sum(beats, axis=0, keepdims=True) + jnp.where(imp < FORCE, 1.0, 0.0)
            chosen = (rank < min(K_SEL, n_sel + 1)) & (c_io < n_sel)
            selm_s[g:g + 1, :] = jnp.where(chosen, 1.0, 0.0)
        m_s[...] = jnp.full(m_s.shape, NEG, F32)
        l_s[...] = jnp.zeros(l_s.shape, F32)
        acc_s[...] = jnp.zeros(acc_s.shape, F32)

    @pl.when(p >= n_steps)
    def _selected():
        step = p - n_steps
        nk = pg * page
        kpos = lax.broadcasted_iota(jnp.int32, (1, nk), 1) + step * nk
        dist = (past - kpos).astype(F32)
        expand = jnp.where(lax.broadcasted_iota(jnp.int32, (nc_pad, nk), 0)
                           == (lax.broadcasted_iota(jnp.int32, (nc_pad, nk), 1) + step * nk) // L_SEL,
                           1.0, 0.0).astype(BF16)
        chosen = jnp.dot(selm_s[...].astype(BF16), expand, preferred_element_type=F32)
        for g in range(kv_h):
            k_t = jnp.concatenate([sel_refs[j][0, g * hd:(g + 1) * hd, :] for j in range(pg)], axis=1)
            v_t = jnp.concatenate([sel_refs[j][0, gw + g * hd: gw + (g + 1) * hd, :] for j in range(pg)], axis=1)
            s = jnp.dot(q, k_t.astype(BF16), preferred_element_type=F32) - slope * dist
            s = jnp.where(chosen[g:g + 1, :] > 0.5, s, NEG)
            m_old = m_s[g]
            m_new = jnp.maximum(m_old, jnp.max(s, axis=-1, keepdims=True))
            al = jnp.exp(m_old - m_new)
            pe = jnp.exp(s - m_new)
            m_s[g] = m_new
            l_s[g] = al * l_s[g] + jnp.sum(pe, axis=-1, keepdims=True)
            acc_s[g] = al * acc_s[g] + lax.dot_general(pe.astype(BF16), v_t.astype(BF16), NT,
                                                       preferred_element_type=F32)

    @pl.when(p == 2 * n_steps - 1)
    def _finish():
        gate = gate_ref[0]
        i_io = lax.broadcasted_iota(jnp.int32, (1, w_buf), 1)
        dist_w = w_buf - i_io
        valid_w = dist_w < WINDOW
        o = jnp.zeros((H_B, hd), F32)
        for g in range(kv_h):
            kn = seln_ref[0][:, g * hd:(g + 1) * hd]
            vn = seln_ref[0][:, gw + g * hd: gw + (g + 1) * hd]
            s_n = jnp.sum(qf * kn, axis=-1, keepdims=True)
            m_old = m_s[g]
            m_new = jnp.maximum(m_old, s_n)
            al = jnp.exp(m_old - m_new)
            pn = jnp.exp(s_n - m_new)
            o_s = (al * acc_s[g] + pn * vn) / (al * l_s[g] + pn)
            kw_t = kwin_ref[0, g * hd:(g + 1) * hd, :].astype(BF16)
            vw_t = kwin_ref[0, gw + g * hd: gw + (g + 1) * hd, :].astype(BF16)
            s = jnp.dot(q, kw_t, preferred_element_type=F32) - slope * dist_w.astype(F32)
            s = jnp.where(valid_w, s, NEG)
            kwn = winn_ref[0][:, g * hd:(g + 1) * hd]
            vwn = winn_ref[0][:, gw + g * hd: gw + (g + 1) * hd]
            s_wn = jnp.sum(qf * kwn, axis=-1, keepdims=True)
            mw = jnp.maximum(jnp.max(s, axis=-1, keepdims=True), s_wn)
            pw = jnp.where(valid_w, jnp.exp(s - mw), 0.0)
            pwn = jnp.exp(s_wn - mw)
            o_w = ((lax.dot_general(pw.astype(BF16), vw_t, NT, preferred_element_type=F32) + pwn * vwn)
                   / (jnp.sum(pw, axis=-1, keepdims=True) + pwn))
            gcol = []
            for c in range(3):
                col = jnp.zeros((H_B, 1), F32)
                for r in range(grp):
                    lane = g * LANES + 3 * r + c
                    col = jnp.where(head == g * grp + r, gate[:, lane:lane + 1], col)
                gcol.append(col)
            og = gcol[0] * oc_s[g] + gcol[1] * o_s + gcol[2] * o_w
            o = jnp.where((head >= g * grp) & (head < (g + 1) * grp), og, o)
        o_ref[0] = o * szb_ref[0]


def _nsa_decode_t(q3, cache_cmp_t, cache_sel_t, cache_win_t, page_table, sel_new, win_new, gate, szb3, wk, wv):
    bs, _, hd = q3.shape
    l_cmp, kv_h = wk.shape
    grp = H_B // kv_h
    gw = kv_h * hd
    page = cache_cmp_t.shape[-1]
    n_pages = page_table.shape[1]
    w_buf = cache_win_t.shape[-1]
    per_page = page // l_cmp
    pages_per_chunk = LANES // per_page
    pg = min(16, pages_per_chunk, n_pages)
    assert page == LANES and page % L_SEL == 0 and page % l_cmp == 0
    assert n_pages % pages_per_chunk == 0 and pages_per_chunk % pg == 0
    n_steps = n_pages // pg
    n_sel = n_pages * page // L_SEL
    nc_pad = -(-n_sel // LANES) * LANES
    wrow = jnp.tile(_pool_weights(wk, wv, hd, halves=False), (per_page, 1)).T
    pos = lax.broadcasted_iota(jnp.int32, (pages_per_chunk, page, LANES), 1)
    pj = lax.broadcasted_iota(jnp.int32, (pages_per_chunk, page, LANES), 0)
    ln = lax.broadcasted_iota(jnp.int32, (pages_per_chunk, page, LANES), 2)
    seg = (ln == pj * per_page + pos // l_cmp).astype(BF16).reshape(pages_per_chunk * page, LANES)

    def cmp_map(j):
        return lambda b, p, pt: (pt[b, jnp.minimum(p, n_steps - 1) * pg + j], 0, 0)

    def sel_map(j):
        return lambda b, p, pt: (pt[b, jnp.maximum(p - n_steps, 0) * pg + j], 0, 0)

    per_b = lambda shp: pl.BlockSpec((1,) + shp, lambda b, p, pt: (b, 0, 0))
    grid_spec = pltpu.PrefetchScalarGridSpec(
        num_scalar_prefetch=1,
        grid=(bs, 2 * n_steps),
        in_specs=([per_b((H_B, hd))]
                  + [pl.BlockSpec((1, 2 * gw, page), cmp_map(j)) for j in range(pg)]
                  + [pl.BlockSpec((1, 2 * gw, page), sel_map(j)) for j in range(pg)]
                  + [per_b((2 * gw, w_buf)), per_b((1, 2 * gw)), per_b((1, 2 * gw)),
                     per_b((1, kv_h * LANES)), per_b((H_B, hd)),
                     pl.BlockSpec((2 * gw, page), lambda b, p, pt: (0, 0)),
                     pl.BlockSpec(seg.shape, lambda b, p, pt: (0, 0))]),
        out_specs=per_b((H_B, hd)),
        scratch_shapes=[pltpu.VMEM((n_pages // pages_per_chunk, 2 * gw, LANES), F32),
                        pltpu.VMEM((SUBLANES, nc_pad), F32),
                        pltpu.VMEM((kv_h, H_B, hd), F32), pltpu.VMEM((kv_h, H_B, 1), F32),
                        pltpu.VMEM((kv_h, H_B, 1), F32), pltpu.VMEM((kv_h, H_B, hd), F32)])
    return pl.pallas_call(
        functools.partial(_nsa_decode_t_kernel, pg=pg, n_pages=n_pages, page=page, l_cmp=l_cmp, hd=hd, grp=grp,
                          kv_h=kv_h, w_buf=w_buf),
        grid_spec=grid_spec,
        out_shape=jax.ShapeDtypeStruct((bs, H_B, hd), F32),
        compiler_params=_cparams(2),
        name="nsa_decode",
    )(page_table, q3, *([cache_cmp_t] * pg), *([cache_sel_t] * pg), cache_win_t, sel_new, win_new, gate, szb3,
      wrow, seg)


def _out_proj_ln_kernel(*refs, n_in, alpha):
    a_refs, w_refs = refs[:n_in], refs[n_in:2 * n_in]
    x_ref, g_ref, b_ref, o_ref = refs[2 * n_in:]
    y = None
    for a_ref, w_ref in zip(a_refs, w_refs):
        t = jnp.dot(a_ref[...].astype(BF16), w_ref[...], preferred_element_type=F32)
        y = t if y is None else y + t
    o_ref[...] = _layer_norm_rows(alpha * x_ref[...] + y, g_ref[...], b_ref[...], LN_EPS)


def _out_proj_ln(a_list, w_list, x2, ln_g, ln_b, alpha):
    m, d = x2.shape
    tm = min(512, m)
    assert m % tm == 0
    n_in = len(a_list)
    return pl.pallas_call(
        functools.partial(_out_proj_ln_kernel, n_in=n_in, alpha=alpha),
        grid=(m // tm,),
        in_specs=([pl.BlockSpec((tm, a.shape[1]), lambda i: (i, 0)) for a in a_list]
                  + [pl.BlockSpec(w.shape, lambda i: (0, 0)) for w in w_list]
                  + [pl.BlockSpec((tm, d), lambda i: (i, 0)),
                     pl.BlockSpec((1, d), lambda i: (0, 0)), pl.BlockSpec((1, d), lambda i: (0, 0))]),
        out_specs=pl.BlockSpec((tm, d), lambda i: (i, 0)),
        out_shape=jax.ShapeDtypeStruct((m, d), F32),
        compiler_params=_cparams(1),
        name="out_proj_ln",
    )(*a_list, *w_list, x2, ln_g.reshape(1, d), ln_b.reshape(1, d))


def _rwkv_proj_kernel(*refs, seq, tiles_per_seq):
    if seq:
        x_ref, tail_ref, shift_ref = refs[:3]
        rest = refs[3:]
    else:
        x_ref, shift_ref = refs[:2]
        rest = refs[2:]
    (mu_ref, w_ref, w1_ref, w2_ref, a1_ref, a2_ref, w0_ref, a0_ref,
     r_ref, k_ref, v_ref, lw_ref, a_ref, sz_ref) = rest
    x = x_ref[...]
    if seq:
        i = pl.program_id(0)
        first = jnp.where(i % tiles_per_seq == 0, shift_ref[0], tail_ref[SUBLANES - 1:SUBLANES, :])
        row = lax.broadcasted_iota(jnp.int32, x.shape, 0)
        x_prev = jnp.where(row == 0, first, pltpu.roll(x, 1, 0))
    else:
        x_prev = shift_ref[...]
    dx = x_prev - x
    mix = lambda n: (x + dx * mu_ref[n:n + 1, :]).astype(BF16)
    r_ref[...] = jnp.dot(mix(0), w_ref[0], preferred_element_type=F32)
    k_ref[...] = jnp.dot(mix(1), w_ref[1], preferred_element_type=F32)
    v_ref[...] = jnp.dot(mix(2), w_ref[2], preferred_element_type=F32)
    sz_ref[...] = _silu(jnp.dot(mix(3), w_ref[3], preferred_element_type=F32))
    hw = jnp.tanh(jnp.dot(mix(4), w1_ref[...], preferred_element_type=F32)).astype(BF16)
    y = -(w0_ref[...] + jnp.dot(hw, w2_ref[...], preferred_element_type=F32))
    softplus = jnp.maximum(y, 0.0) + jnp.log(1.0 + jnp.exp(-jnp.abs(y)))
    lw_ref[...] = -jnp.exp(-softplus - 0.5)
    ha = jnp.dot(mix(5), a1_ref[...], preferred_element_type=F32).astype(BF16)
    a_ref[...] = _sigmoid(a0_ref[...] + jnp.dot(ha, a2_ref[...], preferred_element_type=F32))


def _rwkv_proj(x2, shift, t_len, mu, w_rkvz, w0, w1, w2, a0, a1, a2):
    m, d = x2.shape
    seq = t_len > 1
    tm = min(256, t_len) if seq else m
    assert m % tm == 0 and (not seq or (t_len % tm == 0 and tm % SUBLANES == 0))
    tiles_per_seq = t_len // tm if seq else 1
    full = lambda a: pl.BlockSpec(a.shape, lambda i: (0,) * a.ndim)
    row = pl.BlockSpec((tm, d), lambda i: (i, 0))
    if seq:
        blk = tm // SUBLANES
        lead = [x2, x2, shift.reshape(-1, 1, d)]
        lead_specs = [row, pl.BlockSpec((SUBLANES, d), lambda i: (jnp.maximum(i * blk - 1, 0), 0)),
                      pl.BlockSpec((1, 1, d), lambda i: (i // tiles_per_seq, 0, 0))]
    else:
        lead = [x2, shift]
        lead_specs = [row, row]
    ws = [mu, w_rkvz.astype(BF16), w1.astype(BF16), w2.astype(BF16), a1.astype(BF16), a2.astype(BF16),
          w0.reshape(1, d), a0.reshape(1, d)]
    return pl.pallas_call(
        functools.partial(_rwkv_proj_kernel, seq=seq, tiles_per_seq=tiles_per_seq),
        grid=(m // tm,),
        in_specs=lead_specs + [full(a) for a in ws],
        out_specs=(row,) * 6,
        out_shape=(jax.ShapeDtypeStruct((m, d), F32),) * 6,
        compiler_params=_cparams(1),
        name="rwkv_proj",
    )(*lead, *ws)


WKV_CHUNK = 64
WKV_PASSES = 1
WKV_SUB = 2


def _wkv_kernel(r_ref, k_ref, v_ref, lw_ref, a_ref, sz_ref, s0_ref, kk_ref, ka_ref, rk_ref, gg_ref, gb_ref,
                yz_ref, sfin_ref, h_s, *, c_len, n_sub, n_real, n_heads, hd):
    c = pl.program_id(1)
    mm = functools.partial(_mm, pa=WKV_PASSES, pb=WKV_PASSES)
    n_pairs = n_heads // 2
    pw = 2 * hd
    d = n_heads * hd
    pairs = range(n_pairs)
    iota = lambda shape, axis: lax.broadcasted_iota(jnp.int32, shape, axis)
    bd_mask = (iota((pw, pw), 0) // hd) == (iota((pw, pw), 1) // hd)
    bd_ones = jnp.where(bd_mask, 1.0, 0.0).astype(BF16)
    eye_mask = iota((pw, pw), 0) == iota((pw, pw), 1)
    eye_pb = jnp.where(eye_mask, 1.0, 0.0).astype(BF16)
    lane_lo = iota((1, pw), 1) < hd

    @pl.when(c == 0)
    def _():
        z = jnp.zeros((hd, hd), F32)
        for p in pairs:
            h_s[p] = jnp.concatenate([jnp.concatenate([s0_ref[0, 2 * p].T, z], axis=1),
                                      jnp.concatenate([z, s0_ref[0, 2 * p + 1].T], axis=1)], axis=0)

    def segsum(x):
        xb = x.astype(BF16)
        return jnp.concatenate([jnp.dot(xb[:, t * pw:(t + 1) * pw], bd_ones, preferred_element_type=F32)
                                for t in range(d // pw)], axis=1)

    r = r_ref[0]
    k = k_ref[0]
    v = v_ref[0]
    lw = lw_ref[0]
    a = a_ref[0]
    row = iota((c_len, c_len), 0)
    col = iota((c_len, c_len), 1)
    low_inc = row >= col
    low_exc = row > col
    eye = jnp.where(row == col, 1.0, 0.0)
    tri = jnp.where(low_inc, 1.0, 0.0).astype(BF16)
    kk = k * kk_ref[...]
    kap = kk / jnp.maximum(jnp.sqrt(segsum(kk * kk)), 1e-12)
    ka = kap * a
    kmod = k * (1.0 + (a - 1.0) * ka_ref[...])
    rkr = r * kmod * rk_ref[...]
    levels = max((n_real - 1).bit_length() - 1, 0)

    lhs_b, lhs2_t, vb, l_b, l_k, m_r = [], [], [], [], [], []
    for s in range(n_sub):
        rs = slice(s * c_len, (s + 1) * c_len)
        lw_s = lw[rs]
        cum = _mm(tri, lw_s, NN, pb=3)
        cum_end = cum[c_len - 1:c_len, :]
        p_inv = jnp.exp(-cum)
        p_rest = jnp.exp(cum_end - cum)
        p_end = jnp.exp(cum_end)
        lhs_top = kap[rs] * jnp.exp(cum - lw_s)
        r_t = r[rs] * jnp.exp(cum)
        rhs_top = ka[rs] * p_inv
        k_t = kmod[rs] * p_inv
        ke_top = ka[rs] * p_rest
        k_e = kmod[rs] * p_rest
        for p in pairs:
            ls = slice(p * pw, (p + 1) * pw)
            lhs_p = jnp.concatenate([lhs_top[:, ls], r_t[:, ls]], axis=0)
            rhs_p = jnp.concatenate([rhs_top[:, ls], k_t[:, ls]], axis=0).astype(BF16)
            lhs_b.append(lhs_p.astype(BF16))
            vb.append(v[rs, ls].astype(BF16))
            lhs2 = jnp.concatenate([ke_top[:, ls], k_e[:, ls], jnp.where(eye_mask, p_end[:, ls], 0.0)],
                                   axis=0).astype(BF16)
            lhs2_t.append(lax.dot_general(eye_pb, lhs2, NT, preferred_element_type=F32).astype(BF16))
            for lo in (True, False):
                am = mm(jnp.where(lane_lo == lo, lhs_p, 0.0), rhs_p, NT)
                l_b.append(jnp.where(low_exc, am[:c_len, :c_len], 0.0))
                l_k.append(jnp.where(low_exc, am[:c_len, c_len:], 0.0).astype(BF16))
                m_r.append(jnp.concatenate([jnp.where(low_inc, am[c_len:, :c_len], 0.0),
                                            jnp.where(low_inc, am[c_len:, c_len:], 0.0)], axis=1).astype(BF16))
    n_sp = n_sub * n_pairs
    both = lambda x0, x1: jnp.where(lane_lo, x0, x1)
    lkv = [both(mm(l_k[2 * i], vb[i], NN), mm(l_k[2 * i + 1], vb[i], NN)) for i in range(n_sp)]
    t_inv = [eye - x for x in l_b]
    pwr = l_b
    for _ in range(levels):
        pwr = [mm(x, x, NN) for x in pwr]
        t_inv = [t + mm(t, x, NN) for t, x in zip(t_inv, pwr)]

    h_cur = [h_s[p] for p in pairs]
    y_rows = []
    for s in range(n_sub):
        idx = [s * n_pairs + p for p in pairs]
        hb = [h_cur[p].astype(BF16) for p in pairs]
        gh = [mm(lhs_b[i], hb[p], NN) for p, i in zip(pairs, idx)]
        rhs_u = [gh[p][:c_len] + lkv[i] for p, i in zip(pairs, idx)]
        u = [-both(mm(t_inv[2 * i], rhs_u[p], NN), mm(t_inv[2 * i + 1], rhs_u[p], NN))
             for p, i in zip(pairs, idx)]
        uv = [jnp.concatenate([u[p].astype(BF16), vb[i]], axis=0) for p, i in zip(pairs, idx)]
        y = [gh[p][c_len:] + both(mm(m_r[2 * i], uv[p], NN), mm(m_r[2 * i + 1], uv[p], NN))
             for p, i in zip(pairs, idx)]
        h_cur = [jnp.where(bd_mask, mm(lhs2_t[i], jnp.concatenate([uv[p], hb[p]], axis=0), NN), 0.0)
                 for p, i in zip(pairs, idx)]
        y_rows.append(jnp.concatenate(y, axis=1))
    for p in pairs:
        h_s[p] = h_cur[p]

    y_all = jnp.concatenate(y_rows, axis=0) if n_sub > 1 else y_rows[0]
    dy = y_all - segsum(y_all) * (1.0 / hd)
    var = segsum(dy * dy) * (1.0 / hd)
    yn = dy * lax.rsqrt(var + GN_EPS) * gg_ref[...] + gb_ref[...]
    yz_ref[0] = ((yn + segsum(rkr) * v) * sz_ref[0]).astype(yz_ref.dtype)

    @pl.when(c == pl.num_programs(1) - 1)
    def _():
        for p in pairs:
            sfin_ref[0, 2 * p] = h_s[p, :hd, :hd].T
            sfin_ref[0, 2 * p + 1] = h_s[p, hd:, hd:].T


def _wkv_scan(r3, k3, v3, lw3, a3, sz3, s0, k_k, k_a, r_k, gn_g, gn_b, c_len, t_real):
    bsz, t, d = r3.shape
    n_heads = d // HD_C
    n_sub = WKV_SUB if t % (WKV_SUB * c_len) == 0 else 1
    rows = n_sub * c_len
    assert t % rows == 0 and n_heads % 2 == 0 and 2 * HD_C == LANES
    vec = lambda x: x.reshape(1, d)
    seq = pl.BlockSpec((1, rows, d), lambda b, c: (b, c, 0))
    st = pl.BlockSpec((1, n_heads, HD_C, HD_C), lambda b, c: (b, 0, 0, 0))
    par = pl.BlockSpec((1, d), lambda b, c: (0, 0))
    return pl.pallas_call(
        functools.partial(_wkv_kernel, c_len=c_len, n_sub=n_sub, n_real=min(t_real, c_len), n_heads=n_heads,
                          hd=HD_C),
        grid=(bsz, t // rows),
        in_specs=[seq] * 6 + [st] + [par] * 5,
        out_specs=(seq, st),
        out_shape=(jax.ShapeDtypeStruct((bsz, t, d), BF16),
                   jax.ShapeDtypeStruct((bsz, n_heads, HD_C, HD_C), F32)),
        scratch_shapes=[pltpu.VMEM((n_heads // 2, LANES, LANES), F32)],
        compiler_params=_cparams(2),
        name="wkv_scan",
    )(r3, k3, v3, lw3, a3, sz3, s0, vec(k_k), vec(k_a), vec(r_k), vec(gn_g), vec(gn_b))


def _even_layer(xp, xs, cache_cmp, cache_sel, cache_win, state_conv, page_table,
                w_in, conv_w, conv_b, cln_g, cln_b, wk, wv, w_out, ln_g, ln_b, alpha):
    bp, tp, d = xp.shape
    bs, ts, _ = xs.shape
    assert ts == 1
    d_a = conv_w.shape[-1]
    n_taps = conv_w.shape[0]
    l_cmp, kv_h = wk.shape
    hd = cache_cmp.shape[-1]
    d_b = H_B * hd
    d_kv = 2 * kv_h * hd
    w_bf = _prep_w_in(w_in, d_a, d_b, kv_h, hd)
    cw = conv_w.reshape(n_taps, d_a)
    w_out_bf = w_out.astype(BF16)
    kv_shape = lambda b, t: (b, t, 2, kv_h, hd)

    u, sza, q, cmp_n, sel_n, win_n, szb, gate = _even_in_proj(xp.reshape(bp * tp, d), w_bf, d_a, d_b, d_kv, kv_h, hd)
    r3 = lambda a: a.reshape(bp, tp, a.shape[-1])
    u3 = r3(u)
    ya = _conv_branch(jnp.zeros((bp, CONV_HIST, d_a), F32), u3, r3(sza), cw, conv_b, cln_g, cln_b)
    yb = _nsa_prompt(r3(q), r3(cmp_n), r3(sel_n), r3(win_n), r3(gate), r3(szb), wk, wv)
    yp = _out_proj_ln([ya.reshape(bp * tp, d_a), yb.reshape(bp * tp, d_b)], [w_out_bf[:d_a], w_out_bf[d_a:]],
                      xp.reshape(bp * tp, d), ln_g, ln_b, alpha).reshape(bp, tp, d)
    w_keep = min(WINDOW, tp)
    outs_p = (cmp_n.reshape(kv_shape(bp, tp)), sel_n.reshape(kv_shape(bp, tp)),
              win_n.reshape(kv_shape(bp, tp))[:, tp - w_keep:], u3[:, tp - (n_taps - 1):])

    u, sza, q, cmp_s, sel_s, win_s, szb, gate = _even_in_proj(xs.reshape(bs, d), w_bf, d_a, d_b, d_kv, kv_h, hd)
    ext = jnp.concatenate([state_conv, u[:, None, :]], axis=1)
    hist = jnp.pad(state_conv, ((0, 0), (CONV_HIST - (n_taps - 1), 0), (0, 0)))
    pad_rows = lambda a: jnp.pad(a[:, None, :], ((0, 0), (0, SUBLANES - 1), (0, 0)))
    ya = _conv_branch(hist, pad_rows(u), pad_rows(sza), cw, conv_b, cln_g, cln_b)[:, 0]
    n_pool, page = cache_cmp.shape[:2]
    w_buf = cache_win.shape[1]
    fm = lambda c: jnp.transpose(c, (0, 2, 3, 4, 1)).reshape(c.shape[0], d_kv, c.shape[1])
    yb = _nsa_decode_t(q.astype(F32).reshape(bs, H_B, hd), fm(cache_cmp), fm(cache_sel), fm(cache_win), page_table,
                       sel_s[:, None, :], win_s[:, None, :], gate[:, None, :], szb.reshape(bs, H_B, hd), wk, wv)
    ys = _out_proj_ln([ya, yb.reshape(bs, d_b)], [w_out_bf[:d_a], w_out_bf[d_a:]], xs.reshape(bs, d),
                      ln_g, ln_b, alpha).reshape(bs, 1, d)
    ctx = jnp.concatenate([cache_win, win_s.reshape(kv_shape(bs, 1))], axis=1)
    outs_s = (cmp_s.reshape(kv_shape(bs, 1)), sel_s.reshape(kv_shape(bs, 1)),
              ctx[:, ctx.shape[1] - min(WINDOW, ctx.shape[1]):], ext[:, 1:])
    return yp, ys, outs_p, outs_s


def _odd_group(x3, shift, s0, mu, w_rkvz, w0, w1, w2, a0, a1, a2, k_k, k_a, r_k, gn_g, gn_b, w_out_bf,
               ln_g, ln_b, alpha):
    bsz, t, d = x3.shape
    x2 = x3.reshape(bsz * t, d)
    r, k, v, lw, a, sz = _rwkv_proj(x2, shift, t, mu, w_rkvz, w0, w1, w2, a0, a1, a2)
    c_len = WKV_CHUNK if t >= WKV_CHUNK else -(-t // SUBLANES) * SUBLANES
    t_pad = -(-t // c_len) * c_len
    r3 = lambda z: jnp.pad(z.reshape(bsz, t, d), ((0, 0), (0, t_pad - t), (0, 0)))
    yz, s_fin = _wkv_scan(r3(r), r3(k), r3(v), r3(lw), r3(a), r3(sz), s0, k_k, k_a, r_k.reshape(-1), gn_g, gn_b,
                          c_len, t)
    y = _out_proj_ln([yz[:, :t].reshape(bsz * t, d)], [w_out_bf], x2, ln_g, ln_b, alpha).reshape(bsz, t, d)
    return y, s_fin, x3[:, -1]


def kernel(x_prompt, x_sample, cache_cmp_kv, cache_sel_kv, cache_win_kv, state_conv, state_wkv, state_shift,
           page_table, w_in_even, conv_w, conv_b, conv_ln_g, conv_ln_b, wk_cmp, wv_cmp, w_out_even, mu_c, w_rkvz,
           w0, w1, w2, a0, a1, a2, k_k, k_a, r_k, gn_g, gn_b, w_out_odd, ln_g, ln_b):
    depth = ln_g.shape[0]
    alpha = (2 * depth) ** 0.25
    bp, _, d = x_prompt.shape
    n_heads = d // HD_C
    xp, xs = x_prompt, x_sample
    even_p, even_s, odd_p, odd_s = [], [], [], []
    for l in range(depth):
        if l % 2 == 0:
            e = l // 2
            xp, xs, o_p, o_s = _even_layer(
                xp, xs, cache_cmp_kv[e], cache_sel_kv[e], cache_win_kv[e], state_conv[e], page_table,
                w_in_even[e], conv_w[e], conv_b[e], conv_ln_g[e], conv_ln_b[e], wk_cmp[e], wv_cmp[e],
                w_out_even[e], ln_g[l], ln_b[l], alpha)
            even_p.append(o_p)
            even_s.append(o_s)
        else:
            o = l // 2
            po = (mu_c[o], w_rkvz[o], w0[o], w1[o], w2[o], a0[o], a1[o], a2[o], k_k[o], k_a[o], r_k[o],
                  gn_g[o], gn_b[o], w_out_odd[o].astype(BF16), ln_g[l], ln_b[l], alpha)
            xp, s_p, h_p = _odd_group(xp, jnp.zeros((bp, d), F32), jnp.zeros((bp, n_heads, HD_C, HD_C), F32), *po)
            xs, s_s, h_s = _odd_group(xs, state_shift[o], state_wkv[o], *po)
            odd_p.append((s_p, h_p))
            odd_s.append((s_s, h_s))
    stack = lambda items, j: jnp.stack([it[j] for it in items])
    return (xp, xs,
            stack(even_p, 0), stack(even_s, 0), stack(even_p, 1), stack(even_s, 1),
            stack(even_p, 2), stack(even_s, 2), stack(even_p, 3), stack(even_s, 3),
            stack(odd_p, 0), stack(odd_s, 0), stack(odd_p, 1), stack(odd_s, 1))
```

```python
import functools

import jax
import jax.numpy as jnp
from jax import lax
from jax.experimental import pallas as pl
from jax.experimental.pallas import tpu as pltpu

F32 = jnp.float32
BF16 = jnp.bfloat16

H_B = 8
L_SEL = 64
K_SEL = 16
WINDOW = 512
FORCE = 1e4
HD_C = 64
GN_EPS = 64e-5
LN_EPS = 1e-5
NEG = -1e30

LANES = 128
SUBLANES = 8
VMEM_LIMIT = 56 * 1024 * 1024

NT = (((1,), (1,)), ((), ()))
TN = (((0,), (0,)), ((), ()))
NN = (((1,), (0,)), ((), ()))


def _cparams(n_axes):
    return pltpu.CompilerParams(dimension_semantics=("arbitrary",) * n_axes,
                                vmem_limit_bytes=VMEM_LIMIT)


def _sigmoid(x):
    return 1.0 / (1.0 + jnp.exp(-x))


def _silu(x):
    return x * _sigmoid(x)


def _split_bf16(x, n):
    parts, rem = [], x
    for i in range(n):
        p = rem.astype(BF16)
        parts.append(p)
        if i + 1 < n:
            rem = rem - p.astype(F32)
    return parts


def _mm(a, b, dims=NN, pa=1, pb=1):
    aa = [a] if a.dtype == BF16 else _split_bf16(a, pa)
    bb = [b] if b.dtype == BF16 else _split_bf16(b, pb)
    keep = max(len(aa), len(bb))
    out = None
    for i, ai in enumerate(aa):
        for j, bj in enumerate(bb):
            if i + j < keep:
                t = lax.dot_general(ai, bj, dims, preferred_element_type=F32)
                out = t if out is None else out + t
    return out


def _layer_norm_rows(h, g, b, eps):
    mu = jnp.mean(h, axis=-1, keepdims=True)
    d = h - mu
    var = jnp.mean(d * d, axis=-1, keepdims=True)
    return d * lax.rsqrt(var + eps) * g + b


def _even_in_proj_kernel(x_ref, w_ref, u_ref, sza_ref, q_ref, cmp_ref, sel_ref, win_ref, szb_ref, gate_ref,
                         *t_refs, d_a, d_b, d_kv, q_scale):
    xb = x_ref[...].astype(BF16)

    def seg(lo, n):
        return jnp.dot(xb, w_ref[:, lo:lo + n], preferred_element_type=F32)

    o = 0
    a_val = seg(o, d_a); o += d_a
    a_glu = seg(o, d_a); o += d_a
    u_ref[...] = a_val * _sigmoid(a_glu)
    sza_ref[...] = _silu(seg(o, d_a)); o += d_a
    q_ref[...] = (seg(o, d_b) * q_scale).astype(BF16); o += d_b
    for j, ref in enumerate((cmp_ref, sel_ref, win_ref)):
        kv = seg(o, d_kv); o += d_kv
        ref[...] = kv
        if t_refs:
            t_refs[j][0] = kv.T
    szb_ref[...] = _silu(seg(o, d_b)); o += d_b
    gate_ref[...] = _sigmoid(seg(o, gate_ref.shape[-1]))


def _prep_w_in(w, d_a, d_b, kv_h, hd):
    grp = H_B // kv_h
    c_kv6 = 3 * d_a + d_b
    c_g3 = c_kv6 + 6 * kv_h * hd
    c_zb = c_g3 + 3 * H_B
    gate_blocks = []
    for g in range(kv_h):
        blk = w[:, c_g3 + g * grp * 3: c_g3 + (g + 1) * grp * 3]
        gate_blocks.append(jnp.pad(blk, ((0, 0), (0, LANES - grp * 3))))
    wn = jnp.concatenate([w[:, :c_g3], w[:, c_zb:c_zb + d_b]] + gate_blocks, axis=1)
    return wn.astype(BF16)


def _even_in_proj(x2, w_bf, d_a, d_b, d_kv, kv_h, hd, t_len):
    m, d = x2.shape
    tm = min(512, m)
    assert m % tm == 0
    n_gate = kv_h * LANES
    row = lambda n: pl.BlockSpec((tm, n), lambda i: (i, 0))
    out_shape = (jax.ShapeDtypeStruct((m, d_a), F32), jax.ShapeDtypeStruct((m, d_a), F32),
                 jax.ShapeDtypeStruct((m, d_b), BF16),
                 jax.ShapeDtypeStruct((m, d_kv), F32), jax.ShapeDtypeStruct((m, d_kv), F32),
                 jax.ShapeDtypeStruct((m, d_kv), F32),
                 jax.ShapeDtypeStruct((m, d_b), F32), jax.ShapeDtypeStruct((m, n_gate), F32))
    out_specs = (row(d_a), row(d_a), row(d_b), row(d_kv), row(d_kv), row(d_kv), row(d_b), row(n_gate))
    if t_len % tm == 0 and tm % LANES == 0:
        per_seq = t_len // tm
        t_spec = pl.BlockSpec((1, d_kv, tm), lambda i: (i // per_seq, 0, i % per_seq))
        out_shape += (jax.ShapeDtypeStruct((m // t_len, d_kv, t_len), F32),) * 3
        out_specs += (t_spec,) * 3
    return pl.pallas_call(
        functools.partial(_even_in_proj_kernel, d_a=d_a, d_b=d_b, d_kv=d_kv, q_scale=hd ** -0.5),
        grid=(m // tm,),
        in_specs=[row(d), pl.BlockSpec(w_bf.shape, lambda i: (0, 0))],
        out_specs=out_specs,
        out_shape=out_shape,
        compiler_params=_cparams(1),
        name="even_in_proj",
    )(x2, w_bf)


CONV_HIST = 32


def _conv_kernel(hist_ref, u_ref, sza_ref, w_ref, cb_ref, g_ref, b_ref, ya_ref, win_ref, sh_ref,
                 *, tq, rb, n_taps):
    i = pl.program_id(1)
    base = pl.multiple_of(i * tq, SUBLANES)
    win_ref[CONV_HIST:, :] = u_ref[0, pl.ds(base, tq), :]

    @pl.when(i == 0)
    def _():
        win_ref[:CONV_HIST, :] = hist_ref[0]

    @pl.when(i > 0)
    def _():
        win_ref[:CONV_HIST, :] = u_ref[0, pl.ds(base - CONV_HIST, CONV_HIST), :]

    w = win_ref[...]
    n = tq + CONV_HIST
    sh_ref[0] = w
    for s in range(1, SUBLANES):
        sh_ref[s] = pltpu.roll(w, n - s, 0)

    first = CONV_HIST - (n_taps - 1)
    cb = cb_ref[...]
    g = g_ref[...]
    b = b_ref[...]

    def block(k, carry):
        r0 = pl.multiple_of(k * rb, SUBLANES)
        acc = jnp.zeros((rb, u_ref.shape[-1]), F32) + cb
        for j in range(n_taps):
            a, s = divmod(first + j, SUBLANES)
            acc = acc + sh_ref[s, pl.ds(r0 + SUBLANES * a, rb), :] * w_ref[j:j + 1, :]
        y = _silu(_layer_norm_rows(acc, g, b, LN_EPS)) * sza_ref[0, pl.ds(r0, rb), :]
        ya_ref[0, pl.ds(r0, rb), :] = y.astype(ya_ref.dtype)
        return carry

    lax.fori_loop(0, tq // rb, block, 0)


def _conv_branch(hist, u3, sza3, conv_w, conv_b, ln_g, ln_b):
    bsz, t, d = u3.shape
    n_taps = conv_w.shape[0]
    assert n_taps - 1 <= CONV_HIST and t % SUBLANES == 0
    tq = min(256, t)
    rb = min(32, tq)
    assert t % tq == 0 and tq % rb == 0
    vec = lambda a: a.reshape(1, d)
    full = lambda shp: pl.BlockSpec(shp, lambda b, i: (0,) * len(shp))
    return pl.pallas_call(
        functools.partial(_conv_kernel, tq=tq, rb=rb, n_taps=n_taps),
        grid=(bsz, t // tq),
        in_specs=[pl.BlockSpec((1, CONV_HIST, d), lambda b, i: (b, 0, 0)),
                  pl.BlockSpec((1, t, d), lambda b, i: (b, 0, 0)),
                  pl.BlockSpec((1, tq, d), lambda b, i: (b, i, 0)),
                  full((n_taps, d)), full((1, d)), full((1, d)), full((1, d))],
        out_specs=pl.BlockSpec((1, tq, d), lambda b, i: (b, i, 0)),
        out_shape=jax.ShapeDtypeStruct((bsz, t, d), BF16),
        scratch_shapes=[pltpu.VMEM((tq + CONV_HIST, d), F32),
                        pltpu.VMEM((SUBLANES, tq + CONV_HIST, d), F32)],
        compiler_params=_cparams(2),
        name="conv_branch",
    )(hist, u3, sza3, conv_w, vec(conv_b), vec(ln_g), vec(ln_b))


SEL_OFF = -(2.0 ** 100)


def _nsa_prompt_kernel(q_ref, kcmp_ref, ksel_ref, kwin_ref, gate_ref, szb_ref, wpool_ref, o_ref,
                       kc_s, vc_s, ks_s, vs_s, kw_s, vw_s, m_s, l_s, acc_s, s_scr,
                       *, tq, t_len, l_cmp, hd, grp, kv_h):
    i = pl.program_id(1)
    g = pl.program_id(2)
    n_cmp = t_len // l_cmp
    n_sel = t_len // L_SEL
    half = n_cmp // 2
    gw = kv_h * hd

    @pl.when((i == 0) & (g == 0))
    def _prep():
        x3 = kcmp_ref[0].reshape(half, 2 * l_cmp, 2 * gw)
        pooled = jnp.concatenate([jnp.sum(x3 * wpool_ref[0][None], axis=1),
                                  jnp.sum(x3 * wpool_ref[1][None], axis=1)], axis=0)
        pos = lax.broadcasted_iota(jnp.int32, (t_len, LANES), 0)
        lane = lax.broadcasted_iota(jnp.int32, (t_len, LANES), 1)
        blk = pos // L_SEL
        feat = jnp.where(lane == blk, 1.0, 0.0)
        feat = jnp.where(lane == n_sel, blk.astype(F32), feat)
        feat = jnp.where(lane == n_sel + 1, (pos - blk * L_SEL).astype(F32), feat)
        for src, dst in ((ksel_ref, ks_s), (kwin_ref, kw_s)):
            k01 = src[0, :, :gw]
            dst[0] = jnp.where(lane < hd, feat, pltpu.roll(k01, hd, 1)).astype(BF16)
            dst[1] = jnp.where(lane < hd, feat, k01).astype(BF16)
        vc_pad = jnp.concatenate([pooled[:, gw:], jnp.zeros((LANES - n_cmp, gw), F32)], axis=0) \
            if n_cmp < LANES else pooled[:, gw:]
        vc_t = vc_pad.T
        vs_t = ksel_ref[0, :, gw:].T
        vw_t = kwin_ref[0, :, gw:].T
        for gg in range(kv_h):
            kc_s[gg] = pooled[:, gg * hd:(gg + 1) * hd].astype(BF16)
            vc_s[gg] = vc_t[gg * hd:(gg + 1) * hd, :n_cmp].astype(BF16)
            vs_s[gg] = vs_t[gg * hd:(gg + 1) * hd].astype(BF16)
            vw_s[gg] = vw_t[gg * hd:(gg + 1) * hd].astype(BF16)

    t0 = i * tq
    gs = jnp.float32(1.0)
    for gg in range(1, kv_h):
        gs = jnp.where(g == gg, 2.0 ** -(gg * grp), gs)
    slopes = [gs * (2.0 ** -(r + 1)) for r in range(grp)]
    qt = q_ref[0]
    qs = jnp.concatenate([qt[:, r * hd:(r + 1) * hd] for r in range(grp)], axis=0)

    s_t = lax.dot_general(kc_s[g], qs, NT, preferred_element_type=F32)
    n_io = lax.broadcasted_iota(jnp.int32, (n_cmp, tq), 0)
    t_io = lax.broadcasted_iota(jnp.int32, (n_cmp, tq), 1) + t0
    c_end = jnp.where(n_io < half, 2 * l_cmp * n_io + (l_cmp - 1), 2 * l_cmp * (n_io - half) + (2 * l_cmp - 1))
    dist_c = t_io - c_end
    valid_c = dist_c >= 0
    dist_cf = dist_c.astype(F32)
    imp = jnp.zeros((n_sel, tq), F32)
    o_c = []
    for r in range(grp):
        s = jnp.where(valid_c, s_t[:, r * tq:(r + 1) * tq] - slopes[r] * dist_cf, NEG)
        e = jnp.exp(s - jnp.max(s, axis=0, keepdims=True))
        p = jnp.where(valid_c, e / jnp.sum(e, axis=0, keepdims=True), 0.0)
        imp = imp + (p[:half] + p[half:])
        o_c.append(jnp.dot(vc_s[g], p.astype(BF16), preferred_element_type=F32))

    sb = lax.broadcasted_iota(jnp.int32, (n_sel, tq), 0)
    tb = (lax.broadcasted_iota(jnp.int32, (n_sel, tq), 1) + t0) // L_SEL
    forced = (sb == 0) | (sb == tb) | (sb == tb - 1)
    imp = jnp.where(forced, FORCE, imp)
    imp = jnp.where(sb <= tb, imp, -jnp.inf)
    rank = jnp.zeros((n_sel, tq), F32)
    for j in range(n_sel):
        vj = imp[j:j + 1, :]
        beats = jnp.where(imp > vj, 1.0, jnp.where((imp == vj) & (sb < j), 1.0, 0.0))
        rank = jnp.where(sb == j, jnp.sum(beats, axis=0, keepdims=True), rank)
    off_t = jnp.where(rank < min(K_SEL, n_sel), 0.0, SEL_OFF)
    feat_sel = jnp.concatenate([off_t, jnp.zeros((LANES - n_sel, tq), F32)], axis=0).T

    qf = q_ref[0].astype(F32)
    lane_q = lax.broadcasted_iota(jnp.int32, (tq, LANES), 1)

    def q_aug(r, feat):
        tile = qf[:, (r // 2) * LANES:(r // 2 + 1) * LANES]
        if r % 2 == 0:
            tile = pltpu.roll(tile, hd, 1)
        x = jnp.where(lane_q < hd, feat, tile)
        x = jnp.where(lane_q == n_sel, slopes[r] * L_SEL, x)
        return jnp.where(lane_q == n_sel + 1, slopes[r], x).astype(BF16)

    d0 = (lax.broadcasted_iota(jnp.int32, (tq, tq), 1) - lax.broadcasted_iota(jnp.int32, (tq, tq), 0))

    def reset():
        m_s[...] = jnp.full(m_s.shape, NEG, F32)
        l_s[...] = jnp.zeros(l_s.shape, F32)
        acc_s[...] = jnp.zeros(acc_s.shape, F32)

    heads = range(grp)

    def scores(q_list, k_s, kt):
        k0 = pl.multiple_of(kt * tq, tq)
        kk = k_s[g, pl.ds(k0, tq), :]
        return [lax.dot_general(kk, q_list[r], NT, preferred_element_type=F32) for r in heads]

    def update(s, v_s, kt, mask):
        vt = v_s[g, :, pl.ds(pl.multiple_of(kt * tq, tq), tq)]
        if mask is not None:
            s = [jnp.where(mask, s[r], NEG) for r in heads]
        m_old = [m_s[r] for r in heads]
        m_new = [jnp.maximum(m_old[r], jnp.max(s[r], axis=0, keepdims=True)) for r in heads]
        al = [jnp.exp(m_old[r] - m_new[r]) for r in heads]
        p = [jnp.exp(s[r] - m_new[r]) for r in heads]
        for r in heads:
            m_s[r] = m_new[r]
            l_s[r] = al[r] * l_s[r] + jnp.sum(p[r], axis=0, keepdims=True)
            acc_s[r] = al[r] * acc_s[r] + jnp.dot(vt, p[r].astype(BF16), preferred_element_type=F32)

    def attend(q_list, k_s, v_s, lo, interior_mask, diag_mask):
        reset()

        def put(slot, kt):
            s = scores(q_list, k_s, kt)
            for r in heads:
                s_scr[slot, r] = s[r]

        def take(slot):
            return [s_scr[slot, r] for r in heads]

        odd = (i - lo) % 2

        @pl.when(odd == 1)
        def _():
            update(scores(q_list, k_s, lo), v_s, lo, interior_mask(lo))

        lo2 = lo + odd
        put(0, lo2)

        def body(j, carry):
            kt = lo2 + 2 * j
            put(1, kt + 1)
            update(take(0), v_s, kt, interior_mask(kt))
            put(0, kt + 2)
            update(take(1), v_s, kt + 1, interior_mask(kt + 1))
            return carry

        lax.fori_loop(0, (i - lo2) // 2, body, 0)
        update(take(0), v_s, i, diag_mask)
        return [acc_s[r] / l_s[r] for r in heads]

    o_s = attend([q_aug(r, feat_sel) for r in heads], ks_s, vs_s, 0, lambda kt: None, d0 >= 0)

    w_tiles = -(-(WINDOW - 1) // tq)
    o_w = attend([q_aug(r, 0.0) for r in heads], kw_s, vw_s, jnp.maximum(i - w_tiles, 0),
                 lambda kt: d0 + (i - kt) * tq < WINDOW, (d0 >= 0) & (d0 < WINDOW))

    gate_t = gate_ref[0].T
    outs = []
    for r in range(grp):
        outs.append(gate_t[3 * r:3 * r + 1] * o_c[r]
                    + gate_t[3 * r + 1:3 * r + 2] * o_s[r]
                    + gate_t[3 * r + 2:3 * r + 3] * o_w[r])
    o_ref[0] = (jnp.concatenate(outs, axis=0).T * szb_ref[0]).astype(o_ref.dtype)


def _pool_weights(wk, wv, hd, halves):
    l_cmp, kv_h = wk.shape
    row = jnp.concatenate([jnp.repeat(wk, hd, axis=1), jnp.repeat(wv, hd, axis=1)], axis=1)
    if not halves:
        return row
    z = jnp.zeros_like(row)
    return jnp.stack([jnp.concatenate([row, z], axis=0), jnp.concatenate([z, row], axis=0)])


def _nsa_prompt(q3, cmp3, sel3, win3, gate3, szb3, wk, wv):
    bsz, t, d_b = q3.shape
    l_cmp, kv_h = wk.shape
    grp = H_B // kv_h
    hd = d_b // H_B
    gw = kv_h * hd
    tq = min(256, t)
    assert t % tq == 0 and t % (2 * l_cmp) == 0 and L_SEL == 2 * l_cmp and tq % L_SEL == 0
    n_cmp = t // l_cmp
    n_sel = t // L_SEL
    assert gw == LANES and 2 * hd == LANES and n_sel + 2 <= hd and n_sel % SUBLANES == 0 and n_cmp <= LANES
    wpool = _pool_weights(wk, wv, hd, halves=True)
    kv_spec = pl.BlockSpec((1, t, 2 * gw), lambda b, i, g: (b, 0, 0))
    return pl.pallas_call(
        functools.partial(_nsa_prompt_kernel, tq=tq, t_len=t, l_cmp=l_cmp, hd=hd, grp=grp, kv_h=kv_h),
        grid=(bsz, t // tq, kv_h),
        in_specs=[pl.BlockSpec((1, tq, grp * hd), lambda b, i, g: (b, i, g)),
                  kv_spec, kv_spec, kv_spec,
                  pl.BlockSpec((1, tq, LANES), lambda b, i, g: (b, i, g)),
                  pl.BlockSpec((1, tq, grp * hd), lambda b, i, g: (b, i, g)),
                  pl.BlockSpec(wpool.shape, lambda b, i, g: (0, 0, 0))],
        out_specs=pl.BlockSpec((1, tq, grp * hd), lambda b, i, g: (b, i, g)),
        out_shape=jax.ShapeDtypeStruct((bsz, t, d_b), BF16),
        scratch_shapes=[pltpu.VMEM((kv_h, n_cmp, hd), BF16), pltpu.VMEM((kv_h, hd, n_cmp), BF16),
                        pltpu.VMEM((kv_h, t, LANES), BF16), pltpu.VMEM((kv_h, hd, t), BF16),
                        pltpu.VMEM((kv_h, t, LANES), BF16), pltpu.VMEM((kv_h, hd, t), BF16),
                        pltpu.VMEM((grp, 1, tq), F32), pltpu.VMEM((grp, 1, tq), F32),
                        pltpu.VMEM((grp, hd, tq), F32), pltpu.VMEM((2, grp, tq, tq), F32)],
        compiler_params=_cparams(3),
        name="nsa_prompt",
    )(q3, cmp3, sel3, win3, gate3, szb3, wpool)


def _nsa_decode_t_kernel(pt_ref, *refs, pg, n_pages, page, l_cmp, hd, grp, kv_h, w_buf):
    del pt_ref
    q_ref = refs[0]
    cmp_refs = refs[1:1 + pg]
    sel_refs = refs[1 + pg:1 + 2 * pg]
    (kwin_ref, seln_ref, winn_ref, gate_ref, szb_ref, wrow_ref, seg_ref, o_ref,
     kcv_s, selm_s, oc_s, m_s, l_s, acc_s) = refs[1 + 2 * pg:]
    p = pl.program_id(1)
    n_steps = n_pages // pg
    past = n_pages * page
    n_cmp = past // l_cmp
    n_sel = past // L_SEL
    per_page = page // l_cmp
    pages_per_chunk = LANES // per_page
    steps_per_chunk = pages_per_chunk // pg
    n_chunks = n_pages // pages_per_chunk
    gw = kv_h * hd
    nc_pad = selm_s.shape[-1]
    qf = q_ref[0]
    q = qf.astype(BF16)
    head = lax.broadcasted_iota(jnp.int32, (H_B, 1), 0)
    slope = jnp.zeros((H_B, 1), F32)
    for hh in range(H_B):
        slope = jnp.where(head == hh, 2.0 ** -(hh + 1), slope)

    @pl.when(p < n_steps)
    def _pool():
        sub = p % steps_per_chunk
        prod = jnp.concatenate([cmp_refs[j][0] * wrow_ref[...] for j in range(pg)], axis=1)
        seg = seg_ref[pl.ds(pl.multiple_of(sub * (pg * page), pg * page), pg * page), :]
        tot = _mm(prod, seg, NN, pa=2)
        c = p // steps_per_chunk

        @pl.when(sub == 0)
        def _():
            kcv_s[c] = tot

        @pl.when(sub != 0)
        def _():
            kcv_s[c] = kcv_s[c] + tot

    @pl.when(p == n_steps - 1)
    def _compressed():
        n_io = lax.broadcasted_iota(jnp.int32, (1, n_cmp), 1)
        dist = (past - (l_cmp * n_io + (l_cmp - 1))).astype(F32)
        pair = jnp.where(lax.broadcasted_iota(jnp.int32, (n_cmp, nc_pad), 0) // (L_SEL // l_cmp)
                         == lax.broadcasted_iota(jnp.int32, (n_cmp, nc_pad), 1), 1.0, 0.0).astype(BF16)
        c_io = lax.broadcasted_iota(jnp.int32, (1, nc_pad), 1)
        i_r = lax.broadcasted_iota(jnp.int32, (nc_pad, nc_pad), 1)
        i_c = lax.broadcasted_iota(jnp.int32, (nc_pad, nc_pad), 0)
        selm_s[...] = jnp.zeros(selm_s.shape, F32)
        for g in range(kv_h):
            kc_t = jnp.concatenate([kcv_s[c, g * hd:(g + 1) * hd, :] for c in range(n_chunks)], axis=1)
            vc_t = jnp.concatenate([kcv_s[c, gw + g * hd: gw + (g + 1) * hd, :] for c in range(n_chunks)], axis=1)
            s = jnp.dot(q, kc_t.astype(BF16), preferred_element_type=F32) - slope * dist
            e = jnp.exp(s - jnp.max(s, axis=-1, keepdims=True))
            pc = e / jnp.sum(e, axis=-1, keepdims=True)
            oc_s[g] = lax.dot_general(pc.astype(BF16), vc_t.astype(BF16), NT, preferred_element_type=F32)
            in_grp = (head >= g * grp) & (head < (g + 1) * grp)
            imp = _mm(jnp.sum(jnp.where(in_grp, pc, 0.0), axis=0, keepdims=True), pair, NN, pa=3)
            forced = (c_io == 0) | (c_io == n_sel - 1)
            imp = jnp.where(forced, FORCE, imp)
            imp = jnp.where(c_io < n_sel, imp, -jnp.inf)
            v_r = jnp.broadcast_to(imp, (nc_pad, nc_pad))
            v_c = v_r.T
            beats = jnp.where(v_c > v_r, 1.0, jnp.where((v_c == v_r) & (i_c < i_r), 1.0, 0.0))
            rank = jnp.sum(beats, axis=0, keepdims=True) + jnp.where(imp < FORCE, 1.0, 0.0)
            chosen = (rank < min(K_SEL, n_sel + 1)) & (c_io < n_sel)
            selm_s[g:g + 1, :] = jnp.where(chosen, 1.0, 0.0)
        m_s[...] = jnp.full(m_s.shape, NEG, F32)
        l_s[...] = jnp.zeros(l_s.shape, F32)
        acc_s[...] = jnp.zeros(acc_s.shape, F32)

    @pl.when(p >= n_steps)
    def _selected():
        step = p - n_steps
        nk = pg * page
        kpos = lax.broadcasted_iota(jnp.int32, (1, nk), 1) + step * nk
        dist = (past - kpos).astype(F32)
        expand = jnp.where(lax.broadcasted_iota(jnp.int32, (nc_pad, nk), 0)
                           == (lax.broadcasted_iota(jnp.int32, (nc_pad, nk), 1) + step * nk) // L_SEL,
                           1.0, 0.0).astype(BF16)
        chosen = jnp.dot(selm_s[...].astype(BF16), expand, preferred_element_type=F32)
        for g in range(kv_h):
            k_t = jnp.concatenate([sel_refs[j][0, g * hd:(g + 1) * hd, :] for j in range(pg)], axis=1)
            v_t = jnp.concatenate([sel_refs[j][0, gw + g * hd: gw + (g + 1) * hd, :] for j in range(pg)], axis=1)
            s = jnp.dot(q, k_t.astype(BF16), preferred_element_type=F32) - slope * dist
            s = jnp.where(chosen[g:g + 1, :] > 0.5, s, NEG)
            m_old = m_s[g]
            m_new = jnp.maximum(m_old, jnp.max(s, axis=-1, keepdims=True))
            al = jnp.exp(m_old - m_new)
            pe = jnp.exp(s - m_new)
            m_s[g] = m_new
            l_s[g] = al * l_s[g] + jnp.sum(pe, axis=-1, keepdims=True)
            acc_s[g] = al * acc_s[g] + lax.dot_general(pe.astype(BF16), v_t.astype(BF16), NT,
                                                       preferred_element_type=F32)

    @pl.when(p == 2 * n_steps - 1)
    def _finish():
        gate = gate_ref[0]
        i_io = lax.broadcasted_iota(jnp.int32, (1, w_buf), 1)
        dist_w = w_buf - i_io
        valid_w = dist_w < WINDOW
        o = jnp.zeros((H_B, hd), F32)
        for g in range(kv_h):
            kn = seln_ref[0][:, g * hd:(g + 1) * hd]
            vn = seln_ref[0][:, gw + g * hd: gw + (g + 1) * hd]
            s_n = jnp.sum(qf * kn, axis=-1, keepdims=True)
            m_old = m_s[g]
            m_new = jnp.maximum(m_old, s_n)
            al = jnp.exp(m_old - m_new)
            pn = jnp.exp(s_n - m_new)
            o_s = (al * acc_s[g] + pn * vn) / (al * l_s[g] + pn)
            kw_t = kwin_ref[0, g * hd:(g + 1) * hd, :].astype(BF16)
            vw_t = kwin_ref[0, gw + g * hd: gw + (g + 1) * hd, :].astype(BF16)
            s = jnp.dot(q, kw_t, preferred_element_type=F32) - slope * dist_w.astype(F32)
            s = jnp.where(valid_w, s, NEG)
            kwn = winn_ref[0][:, g * hd:(g + 1) * hd]
            vwn = winn_ref[0][:, gw + g * hd: gw + (g + 1) * hd]
            s_wn = jnp.sum(qf * kwn, axis=-1, keepdims=True)
            mw = jnp.maximum(jnp.max(s, axis=-1, keepdims=True), s_wn)
            pw = jnp.where(valid_w, jnp.exp(s - mw), 0.0)
            pwn = jnp.exp(s_wn - mw)
            o_w = ((lax.dot_general(pw.astype(BF16), vw_t, NT, preferred_element_type=F32) + pwn * vwn)
                   / (jnp.sum(pw, axis=-1, keepdims=True) + pwn))
            gcol = []
            for c in range(3):
                col = jnp.zeros((H_B, 1), F32)
                for r in range(grp):
                    lane = g * LANES + 3 * r + c
                    col = jnp.where(head == g * grp + r, gate[:, lane:lane + 1], col)
                gcol.append(col)
            og = gcol[0] * oc_s[g] + gcol[1] * o_s + gcol[2] * o_w
            o = jnp.where((head >= g * grp) & (head < (g + 1) * grp), og, o)
        o_ref[0] = o * szb_ref[0]


def _nsa_decode_t(q3, cache_cmp_t, cache_sel_t, cache_win_t, page_table, sel_new, win_new, gate, szb3, wk, wv):
    bs, _, hd = q3.shape
    l_cmp, kv_h = wk.shape
    grp = H_B // kv_h
    gw = kv_h * hd
    page = cache_cmp_t.shape[-1]
    n_pages = page_table.shape[1]
    w_buf = cache_win_t.shape[-1]
    per_page = page // l_cmp
    pages_per_chunk = LANES // per_page
    pg = min(16, pages_per_chunk, n_pages)
    assert page == LANES and page % L_SEL == 0 and page % l_cmp == 0
    assert n_pages % pages_per_chunk == 0 and pages_per_chunk % pg == 0
    n_steps = n_pages // pg
    n_sel = n_pages * page // L_SEL
    nc_pad = -(-n_sel // LANES) * LANES
    wrow = jnp.tile(_pool_weights(wk, wv, hd, halves=False), (per_page, 1)).T
    pos = lax.broadcasted_iota(jnp.int32, (pages_per_chunk, page, LANES), 1)
    pj = lax.broadcasted_iota(jnp.int32, (pages_per_chunk, page, LANES), 0)
    ln = lax.broadcasted_iota(jnp.int32, (pages_per_chunk, page, LANES), 2)
    seg = (ln == pj * per_page + pos // l_cmp).astype(BF16).reshape(pages_per_chunk * page, LANES)

    def cmp_map(j):
        return lambda b, p, pt: (pt[b, jnp.minimum(p, n_steps - 1) * pg + j], 0, 0)

    def sel_map(j):
        return lambda b, p, pt: (pt[b, jnp.maximum(p - n_steps, 0) * pg + j], 0, 0)

    per_b = lambda shp: pl.BlockSpec((1,) + shp, lambda b, p, pt: (b, 0, 0))
    grid_spec = pltpu.PrefetchScalarGridSpec(
        num_scalar_prefetch=1,
        grid=(bs, 2 * n_steps),
        in_specs=([per_b((H_B, hd))]
                  + [pl.BlockSpec((1, 2 * gw, page), cmp_map(j)) for j in range(pg)]
                  + [pl.BlockSpec((1, 2 * gw, page), sel_map(j)) for j in range(pg)]
                  + [per_b((2 * gw, w_buf)), per_b((1, 2 * gw)), per_b((1, 2 * gw)),
                     per_b((1, kv_h * LANES)), per_b((H_B, hd)),
                     pl.BlockSpec((2 * gw, page), lambda b, p, pt: (0, 0)),
                     pl.BlockSpec(seg.shape, lambda b, p, pt: (0, 0))]),
        out_specs=per_b((H_B, hd)),
        scratch_shapes=[pltpu.VMEM((n_pages // pages_per_chunk, 2 * gw, LANES), F32),
                        pltpu.VMEM((SUBLANES, nc_pad), F32),
                        pltpu.VMEM((kv_h, H_B, hd), F32), pltpu.VMEM((kv_h, H_B, 1), F32),
                        pltpu.VMEM((kv_h, H_B, 1), F32), pltpu.VMEM((kv_h, H_B, hd), F32)])
    return pl.pallas_call(
        functools.partial(_nsa_decode_t_kernel, pg=pg, n_pages=n_pages, page=page, l_cmp=l_cmp, hd=hd, grp=grp,
                          kv_h=kv_h, w_buf=w_buf),
        grid_spec=grid_spec,
        out_shape=jax.ShapeDtypeStruct((bs, H_B, hd), F32),
        compiler_params=_cparams(2),
        name="nsa_decode",
    )(page_table, q3, *([cache_cmp_t] * pg), *([cache_sel_t] * pg), cache_win_t, sel_new, win_new, gate, szb3,
      wrow, seg)


def _out_proj_ln_kernel(*refs, n_in, alpha):
    a_refs, w_refs = refs[:n_in], refs[n_in:2 * n_in]
    x_ref, g_ref, b_ref, o_ref = refs[2 * n_in:]
    y = None
    for a_ref, w_ref in zip(a_refs, w_refs):
        t = jnp.dot(a_ref[...].astype(BF16), w_ref[...], preferred_element_type=F32)
        y = t if y is None else y + t
    o_ref[...] = _layer_norm_rows(alpha * x_ref[...] + y, g_ref[...], b_ref[...], LN_EPS)


def _out_proj_ln(a_list, w_list, x2, ln_g, ln_b, alpha):
    m, d = x2.shape
    tm = min(512, m)
    assert m % tm == 0
    n_in = len(a_list)
    return pl.pallas_call(
        functools.partial(_out_proj_ln_kernel, n_in=n_in, alpha=alpha),
        grid=(m // tm,),
        in_specs=([pl.BlockSpec((tm, a.shape[1]), lambda i: (i, 0)) for a in a_list]
                  + [pl.BlockSpec(w.shape, lambda i: (0, 0)) for w in w_list]
                  + [pl.BlockSpec((tm, d), lambda i: (i, 0)),
                     pl.BlockSpec((1, d), lambda i: (0, 0)), pl.BlockSpec((1, d), lambda i: (0, 0))]),
        out_specs=pl.BlockSpec((tm, d), lambda i: (i, 0)),
        out_shape=jax.ShapeDtypeStruct((m, d), F32),
        compiler_params=_cparams(1),
        name="out_proj_ln",
    )(*a_list, *w_list, x2, ln_g.reshape(1, d), ln_b.reshape(1, d))


def _rwkv_proj_kernel(*refs, seq, tiles_per_seq):
    if seq:
        x_ref, tail_ref, shift_ref = refs[:3]
        rest = refs[3:]
    else:
        x_ref, shift_ref = refs[:2]
        rest = refs[2:]
    (mu_ref, w_ref, w1_ref, w2_ref, a1_ref, a2_ref, w0_ref, a0_ref,
     r_ref, k_ref, v_ref, lw_ref, a_ref, sz_ref) = rest
    x = x_ref[...]
    if seq:
        i = pl.program_id(0)
        first = jnp.where(i % tiles_per_seq == 0, shift_ref[0], tail_ref[SUBLANES - 1:SUBLANES, :])
        row = lax.broadcasted_iota(jnp.int32, x.shape, 0)
        x_prev = jnp.where(row == 0, first, pltpu.roll(x, 1, 0))
    else:
        x_prev = shift_ref[...]
    dx = x_prev - x
    mix = lambda n: (x + dx * mu_ref[n:n + 1, :]).astype(BF16)
    r_ref[...] = jnp.dot(mix(0), w_ref[0], preferred_element_type=F32)
    k_ref[...] = jnp.dot(mix(1), w_ref[1], preferred_element_type=F32)
    v_ref[...] = jnp.dot(mix(2), w_ref[2], preferred_element_type=F32)
    sz_ref[...] = _silu(jnp.dot(mix(3), w_ref[3], preferred_element_type=F32))
    hw = jnp.tanh(jnp.dot(mix(4), w1_ref[...], preferred_element_type=F32)).astype(BF16)
    y = -(w0_ref[...] + jnp.dot(hw, w2_ref[...], preferred_element_type=F32))
    softplus = jnp.maximum(y, 0.0) + jnp.log(1.0 + jnp.exp(-jnp.abs(y)))
    lw_ref[...] = -jnp.exp(-softplus - 0.5)
    ha = jnp.dot(mix(5), a1_ref[...], preferred_element_type=F32).astype(BF16)
    a_ref[...] = _sigmoid(a0_ref[...] + jnp.dot(ha, a2_ref[...], preferred_element_type=F32))


def _rwkv_proj(x2, shift, t_len, mu, w_rkvz, w0, w1, w2, a0, a1, a2):
    m, d = x2.shape
    seq = t_len > 1
    tm = min(512, t_len) if seq else m
    assert m % tm == 0 and (not seq or (t_len % tm == 0 and tm % SUBLANES == 0))
    tiles_per_seq = t_len // tm if seq else 1
    full = lambda a: pl.BlockSpec(a.shape, lambda i: (0,) * a.ndim, pipeline_mode=pl.Buffered(1))
    row = pl.BlockSpec((tm, d), lambda i: (i, 0))
    if seq:
        blk = tm // SUBLANES
        lead = [x2, x2, shift.reshape(-1, 1, d)]
        lead_specs = [row, pl.BlockSpec((SUBLANES, d), lambda i: (jnp.maximum(i * blk - 1, 0), 0)),
                      pl.BlockSpec((1, 1, d), lambda i: (i // tiles_per_seq, 0, 0))]
    else:
        lead = [x2, shift]
        lead_specs = [row, row]
    ws = [mu, w_rkvz.astype(BF16), w1.astype(BF16), w2.astype(BF16), a1.astype(BF16), a2.astype(BF16),
          w0.reshape(1, d), a0.reshape(1, d)]
    return pl.pallas_call(
        functools.partial(_rwkv_proj_kernel, seq=seq, tiles_per_seq=tiles_per_seq),
        grid=(m // tm,),
        in_specs=lead_specs + [full(a) for a in ws],
        out_specs=(row,) * 6,
        out_shape=(jax.ShapeDtypeStruct((m, d), F32),) * 6,
        compiler_params=_cparams(1),
        name="rwkv_proj",
    )(*lead, *ws)


WKV_CHUNK = 64
WKV_PASSES = 1
WKV_SUB = 2


def _wkv_kernel(r_ref, k_ref, v_ref, lw_ref, a_ref, sz_ref, s0_ref, kk_ref, ka_ref, rk_ref, gg_ref, gb_ref,
                yz_ref, sfin_ref, h_s, *, c_len, n_sub, n_real, n_heads, hd):
    c = pl.program_id(1)
    mm = functools.partial(_mm, pa=WKV_PASSES, pb=WKV_PASSES)
    n_pairs = n_heads // 2
    pw = 2 * hd
    d = n_heads * hd
    pairs = range(n_pairs)
    iota = lambda shape, axis: lax.broadcasted_iota(jnp.int32, shape, axis)
    bd_mask = (iota((pw, pw), 0) // hd) == (iota((pw, pw), 1) // hd)
    bd_ones = jnp.where(bd_mask, 1.0, 0.0).astype(BF16)
    eye_mask = iota((pw, pw), 0) == iota((pw, pw), 1)
    eye_pb = jnp.where(eye_mask, 1.0, 0.0).astype(BF16)
    lane_lo = iota((1, pw), 1) < hd

    @pl.when(c == 0)
    def _():
        z = jnp.zeros((hd, hd), F32)
        for p in pairs:
            h_s[p] = jnp.concatenate([jnp.concatenate([s0_ref[0, 2 * p].T, z], axis=1),
                                      jnp.concatenate([z, s0_ref[0, 2 * p + 1].T], axis=1)], axis=0)

    def segsum(x):
        xb = x.astype(BF16)
        return jnp.concatenate([jnp.dot(xb[:, t * pw:(t + 1) * pw], bd_ones, preferred_element_type=F32)
                                for t in range(d // pw)], axis=1)

    r = r_ref[0]
    k = k_ref[0]
    v = v_ref[0]
    lw = lw_ref[0]
    a = a_ref[0]
    row = iota((c_len, c_len), 0)
    col = iota((c_len, c_len), 1)
    low_inc = row >= col
    low_exc = row > col
    eye = jnp.where(row == col, 1.0, 0.0)
    tri = jnp.where(low_inc, 1.0, 0.0).astype(BF16)
    kk = k * kk_ref[...]
    kap = kk / jnp.maximum(jnp.sqrt(segsum(kk * kk)), 1e-12)
    ka = kap * a
    kmod = k * (1.0 + (a - 1.0) * ka_ref[...])
    rkr = r * kmod * rk_ref[...]
    levels = max((n_real - 1).bit_length() - 1, 0)

    lhs_b, lhs2_t, vb, l_b, l_k, m_r = [], [], [], [], [], []
    for s in range(n_sub):
        rs = slice(s * c_len, (s + 1) * c_len)
        lw_s = lw[rs]
        cum = _mm(tri, lw_s, NN, pb=3)
        cum_end = cum[c_len - 1:c_len, :]
        p_inv = jnp.exp(-cum)
        p_rest = jnp.exp(cum_end - cum)
        p_end = jnp.exp(cum_end)
        lhs_top = kap[rs] * jnp.exp(cum - lw_s)
        r_t = r[rs] * jnp.exp(cum)
        rhs_top = ka[rs] * p_inv
        k_t = kmod[rs] * p_inv
        ke_top = ka[rs] * p_rest
        k_e = kmod[rs] * p_rest
        for p in pairs:
            ls = slice(p * pw, (p + 1) * pw)
            lhs_p = jnp.concatenate([lhs_top[:, ls], r_t[:, ls]], axis=0)
            rhs_p = jnp.concatenate([rhs_top[:, ls], k_t[:, ls]], axis=0).astype(BF16)
            lhs_b.append(lhs_p.astype(BF16))
            vb.append(v[rs, ls].astype(BF16))
            lhs2 = jnp.concatenate([ke_top[:, ls], k_e[:, ls], jnp.where(eye_mask, p_end[:, ls], 0.0)],
                                   axis=0).astype(BF16)
            lhs2_t.append(lax.dot_general(eye_pb, lhs2, NT, preferred_element_type=F32).astype(BF16))
            for lo in (True, False):
                am = mm(jnp.where(lane_lo == lo, lhs_p, 0.0), rhs_p, NT)
                l_b.append(jnp.where(low_exc, am[:c_len, :c_len], 0.0))
                l_k.append(jnp.where(low_exc, am[:c_len, c_len:], 0.0).astype(BF16))
                m_r.append(jnp.concatenate([jnp.where(low_inc, am[c_len:, :c_len], 0.0),
                                            jnp.where(low_inc, am[c_len:, c_len:], 0.0)], axis=1).astype(BF16))
    n_sp = n_sub * n_pairs
    both = lambda x0, x1: jnp.where(lane_lo, x0, x1)
    lkv = [both(mm(l_k[2 * i], vb[i], NN), mm(l_k[2 * i + 1], vb[i], NN)) for i in range(n_sp)]
    t_inv = [eye - x for x in l_b]
    pwr = l_b
    for _ in range(levels):
        pwr = [mm(x, x, NN) for x in pwr]
        t_inv = [t + mm(t, x, NN) for t, x in zip(t_inv, pwr)]

    h_cur = [h_s[p] for p in pairs]
    y_rows = []
    for s in range(n_sub):
        idx = [s * n_pairs + p for p in pairs]
        hb = [h_cur[p].astype(BF16) for p in pairs]
        gh = [mm(lhs_b[i], hb[p], NN) for p, i in zip(pairs, idx)]
        rhs_u = [gh[p][:c_len] + lkv[i] for p, i in zip(pairs, idx)]
        u = [-both(mm(t_inv[2 * i], rhs_u[p], NN), mm(t_inv[2 * i + 1], rhs_u[p], NN))
             for p, i in zip(pairs, idx)]
        uv = [jnp.concatenate([u[p].astype(BF16), vb[i]], axis=0) for p, i in zip(pairs, idx)]
        y = [gh[p][c_len:] + both(mm(m_r[2 * i], uv[p], NN), mm(m_r[2 * i + 1], uv[p], NN))
             for p, i in zip(pairs, idx)]
        h_cur = [jnp.where(bd_mask, mm(lhs2_t[i], jnp.concatenate([uv[p], hb[p]], axis=0), NN), 0.0)
                 for p, i in zip(pairs, idx)]
        y_rows.append(jnp.concatenate(y, axis=1))
    for p in pairs:
        h_s[p] = h_cur[p]

    y_all = jnp.concatenate(y_rows, axis=0) if n_sub > 1 else y_rows[0]
    dy = y_all - segsum(y_all) * (1.0 / hd)
    var = segsum(dy * dy) * (1.0 / hd)
    yn = dy * lax.rsqrt(var + GN_EPS) * gg_ref[...] + gb_ref[...]
    yz_ref[0] = ((yn + segsum(rkr) * v) * sz_ref[0]).astype(yz_ref.dtype)

    @pl.when(c == pl.num_programs(1) - 1)
    def _():
        for p in pairs:
            sfin_ref[0, 2 * p] = h_s[p, :hd, :hd].T
            sfin_ref[0, 2 * p + 1] = h_s[p, hd:, hd:].T


def _wkv_scan(r3, k3, v3, lw3, a3, sz3, s0, k_k, k_a, r_k, gn_g, gn_b, c_len, t_real):
    bsz, t, d = r3.shape
    n_heads = d // HD_C
    n_sub = WKV_SUB if t % (WKV_SUB * c_len) == 0 else 1
    rows = n_sub * c_len
    assert t % rows == 0 and n_heads % 2 == 0 and 2 * HD_C == LANES
    vec = lambda x: x.reshape(1, d)
    seq = pl.BlockSpec((1, rows, d), lambda b, c: (b, c, 0))
    st = pl.BlockSpec((1, n_heads, HD_C, HD_C), lambda b, c: (b, 0, 0, 0))
    par = pl.BlockSpec((1, d), lambda b, c: (0, 0))
    return pl.pallas_call(
        functools.partial(_wkv_kernel, c_len=c_len, n_sub=n_sub, n_real=min(t_real, c_len), n_heads=n_heads,
                          hd=HD_C),
        grid=(bsz, t // rows),
        in_specs=[seq] * 6 + [st] + [par] * 5,
        out_specs=(seq, st),
        out_shape=(jax.ShapeDtypeStruct((bsz, t, d), BF16),
                   jax.ShapeDtypeStruct((bsz, n_heads, HD_C, HD_C), F32)),
        scratch_shapes=[pltpu.VMEM((n_heads // 2, LANES, LANES), F32)],
        compiler_params=_cparams(2),
        name="wkv_scan",
    )(r3, k3, v3, lw3, a3, sz3, s0, vec(k_k), vec(k_a), vec(r_k), vec(gn_g), vec(gn_b))


def _even_layer(xp, xs, cache_cmp, cache_sel, cache_win, state_conv, page_table,
                w_in, conv_w, conv_b, cln_g, cln_b, wk, wv, w_out, ln_g, ln_b, alpha):
    bp, tp, d = xp.shape
    bs, ts, _ = xs.shape
    assert ts == 1
    d_a = conv_w.shape[-1]
    n_taps = conv_w.shape[0]
    l_cmp, kv_h = wk.shape
    hd = cache_cmp.shape[-1]
    d_b = H_B * hd
    d_kv = 2 * kv_h * hd
    w_bf = _prep_w_in(w_in, d_a, d_b, kv_h, hd)
    cw = conv_w.reshape(n_taps, d_a)
    w_out_bf = w_out.astype(BF16)
    kv_shape = lambda b, t: (b, t, 2, kv_h, hd)

    proj = _even_in_proj(xp.reshape(bp * tp, d), w_bf, d_a, d_b, d_kv, kv_h, hd, tp)
    u, sza, q, cmp_n, sel_n, win_n, szb, gate = proj[:8]
    if len(proj) > 8:
        new_kv = [a.reshape(bp, 2, kv_h, hd, tp).transpose(0, 4, 1, 2, 3) for a in proj[8:]]
    else:
        new_kv = [a.reshape(bp, tp, 2, kv_h, hd) for a in (cmp_n, sel_n, win_n)]
    r3 = lambda a: a.reshape(bp, tp, a.shape[-1])
    u3 = r3(u)
    ya = _conv_branch(jnp.zeros((bp, CONV_HIST, d_a), F32), u3, r3(sza), cw, conv_b, cln_g, cln_b)
    yb = _nsa_prompt(r3(q), r3(cmp_n), r3(sel_n), r3(win_n), r3(gate), r3(szb), wk, wv)
    yp = _out_proj_ln([ya.reshape(bp * tp, d_a), yb.reshape(bp * tp, d_b)], [w_out_bf[:d_a], w_out_bf[d_a:]],
                      xp.reshape(bp * tp, d), ln_g, ln_b, alpha).reshape(bp, tp, d)
    w_keep = min(WINDOW, tp)
    outs_p = (new_kv[0], new_kv[1], new_kv[2][:, tp - w_keep:], u3[:, tp - (n_taps - 1):])

    u, sza, q, cmp_s, sel_s, win_s, szb, gate = _even_in_proj(xs.reshape(bs, d), w_bf, d_a, d_b, d_kv, kv_h, hd, 1)
    ext = jnp.concatenate([state_conv, u[:, None, :]], axis=1)
    hist = jnp.pad(state_conv, ((0, 0), (CONV_HIST - (n_taps - 1), 0), (0, 0)))
    pad_rows = lambda a: jnp.pad(a[:, None, :], ((0, 0), (0, SUBLANES - 1), (0, 0)))
    ya = _conv_branch(hist, pad_rows(u), pad_rows(sza), cw, conv_b, cln_g, cln_b)[:, 0]
    n_pool, page = cache_cmp.shape[:2]
    w_buf = cache_win.shape[1]
    fm = lambda c: jnp.transpose(c, (0, 2, 3, 4, 1)).reshape(c.shape[0], d_kv, c.shape[1])
    yb = _nsa_decode_t(q.astype(F32).reshape(bs, H_B, hd), fm(cache_cmp), fm(cache_sel), fm(cache_win), page_table,
                       sel_s[:, None, :], win_s[:, None, :], gate[:, None, :], szb.reshape(bs, H_B, hd), wk, wv)
    ys = _out_proj_ln([ya, yb.reshape(bs, d_b)], [w_out_bf[:d_a], w_out_bf[d_a:]], xs.reshape(bs, d),
                      ln_g, ln_b, alpha).reshape(bs, 1, d)
    ctx = jnp.concatenate([cache_win, win_s.reshape(kv_shape(bs, 1))], axis=1)
    outs_s = (cmp_s.reshape(kv_shape(bs, 1)), sel_s.reshape(kv_shape(bs, 1)),
              ctx[:, ctx.shape[1] - min(WINDOW, ctx.shape[1]):], ext[:, 1:])
    return yp, ys, outs_p, outs_s


def _odd_group(x3, shift, s0, mu, w_rkvz, w0, w1, w2, a0, a1, a2, k_k, k_a, r_k, gn_g, gn_b, w_out_bf,
               ln_g, ln_b, alpha):
    bsz, t, d = x3.shape
    x2 = x3.reshape(bsz * t, d)
    r, k, v, lw, a, sz = _rwkv_proj(x2, shift, t, mu, w_rkvz, w0, w1, w2, a0, a1, a2)
    c_len = WKV_CHUNK if t >= WKV_CHUNK else -(-t // SUBLANES) * SUBLANES
    t_pad = -(-t // c_len) * c_len
    r3 = lambda z: jnp.pad(z.reshape(bsz, t, d), ((0, 0), (0, t_pad - t), (0, 0)))
    yz, s_fin = _wkv_scan(r3(r), r3(k), r3(v), r3(lw), r3(a), r3(sz), s0, k_k, k_a, r_k.reshape(-1), gn_g, gn_b,
                          c_len, t)
    y = _out_proj_ln([yz[:, :t].reshape(bsz * t, d)], [w_out_bf], x2, ln_g, ln_b, alpha).reshape(bsz, t, d)
    return y, s_fin, x3[:, -1]


def kernel(x_prompt, x_sample, cache_cmp_kv, cache_sel_kv, cache_win_kv, state_conv, state_wkv, state_shift,
           page_table, w_in_even, conv_w, conv_b, conv_ln_g, conv_ln_b, wk_cmp, wv_cmp, w_out_even, mu_c, w_rkvz,
           w0, w1, w2, a0, a1, a2, k_k, k_a, r_k, gn_g, gn_b, w_out_odd, ln_g, ln_b):
    depth = ln_g.shape[0]
    alpha = (2 * depth) ** 0.25
    bp, _, d = x_prompt.shape
    n_heads = d // HD_C
    xp, xs = x_prompt, x_sample
    even_p, even_s, odd_p, odd_s = [], [], [], []
    for l in range(depth):
        if l % 2 == 0:
            e = l // 2
            xp, xs, o_p, o_s = _even_layer(
                xp, xs, cache_cmp_kv[e], cache_sel_kv[e], cache_win_kv[e], state_conv[e], page_table,
                w_in_even[e], conv_w[e], conv_b[e], conv_ln_g[e], conv_ln_b[e], wk_cmp[e], wv_cmp[e],
                w_out_even[e], ln_g[l], ln_b[l], alpha)
            even_p.append(o_p)
            even_s.append(o_s)
        else:
            o = l // 2
            po = (mu_c[o], w_rkvz[o], w0[o], w1[o], w2[o], a0[o], a1[o], a2[o], k_k[o], k_a[o], r_k[o],
                  gn_g[o], gn_b[o], w_out_odd[o].astype(BF16), ln_g[l], ln_b[l], alpha)
            xp, s_p, h_p = _odd_group(xp, jnp.zeros((bp, d), F32), jnp.zeros((bp, n_heads, HD_C, HD_C), F32), *po)
            xs, s_s, h_s = _odd_group(xs, state_shift[o], state_wkv[o], *po)
            odd_p.append((s_p, h_p))
            odd_s.append((s_s, h_s))
    stack = lambda items, j: jnp.stack([it[j] for it in items])
    return (xp, xs,
            stack(even_p, 0), stack(even_s, 0), stack(even_p, 1), stack(even_s, 1),
            stack(even_p, 2), stack(even_s, 2), stack(even_p, 3), stack(even_s, 3),
            stack(odd_p, 0), stack(odd_s, 0), stack(odd_p, 1), stack(odd_s, 1))
```

```python
import functools

import jax
import jax.numpy as jnp
from jax import lax
from jax.experimental import pallas as pl
from jax.experimental.pallas import tpu as pltpu

F32 = jnp.float32
BF16 = jnp.bfloat16

H_B = 8
L_SEL = 64
K_SEL = 16
WINDOW = 512
FORCE = 1e4
HD_C = 64
GN_EPS = 64e-5
LN_EPS = 1e-5
NEG = -1e30

LANES = 128
SUBLANES = 8
VMEM_LIMIT = 56 * 1024 * 1024

NT = (((1,), (1,)), ((), ()))
TN = (((0,), (0,)), ((), ()))
NN = (((1,), (0,)), ((), ()))


def _cparams(n_axes):
    return pltpu.CompilerParams(dimension_semantics=("arbitrary",) * n_axes,
                                vmem_limit_bytes=VMEM_LIMIT)


def _sigmoid(x):
    return 1.0 / (1.0 + jnp.exp(-x))


def _silu(x):
    return x * _sigmoid(x)


def _split_bf16(x, n):
    parts, rem = [], x
    for i in range(n):
        p = rem.astype(BF16)
        parts.append(p)
        if i + 1 < n:
            rem = rem - p.astype(F32)
    return parts


def _mm(a, b, dims=NN, pa=1, pb=1):
    aa = [a] if a.dtype == BF16 else _split_bf16(a, pa)
    bb = [b] if b.dtype == BF16 else _split_bf16(b, pb)
    keep = max(len(aa), len(bb))
    out = None
    for i, ai in enumerate(aa):
        for j, bj in enumerate(bb):
            if i + j < keep:
                t = lax.dot_general(ai, bj, dims, preferred_element_type=F32)
                out = t if out is None else out + t
    return out


def _layer_norm_rows(h, g, b, eps):
    mu = jnp.mean(h, axis=-1, keepdims=True)
    d = h - mu
    var = jnp.mean(d * d, axis=-1, keepdims=True)
    return d * lax.rsqrt(var + eps) * g + b


def _even_in_proj_kernel(x_ref, w_ref, u_ref, sza_ref, q_ref, cmp_ref, sel_ref, win_ref, szb_ref, gate_ref,
                         *t_refs, d_a, d_b, d_kv, q_scale):
    xb = x_ref[...].astype(BF16)

    def seg(lo, n):
        return jnp.dot(xb, w_ref[:, lo:lo + n], preferred_element_type=F32)

    o = 0
    a_val = seg(o, d_a); o += d_a
    a_glu = seg(o, d_a); o += d_a
    u_ref[...] = a_val * _sigmoid(a_glu)
    sza_ref[...] = _silu(seg(o, d_a)); o += d_a
    q_ref[...] = (seg(o, d_b) * q_scale).astype(BF16); o += d_b
    for j, ref in enumerate((cmp_ref, sel_ref, win_ref)):
        kv = seg(o, d_kv); o += d_kv
        ref[...] = kv
        if t_refs:
            t_refs[j][0] = kv.T
    szb_ref[...] = _silu(seg(o, d_b)); o += d_b
    gate_ref[...] = _sigmoid(seg(o, gate_ref.shape[-1]))


def _prep_w_in(w, d_a, d_b, kv_h, hd):
    grp = H_B // kv_h
    c_kv6 = 3 * d_a + d_b
    c_g3 = c_kv6 + 6 * kv_h * hd
    c_zb = c_g3 + 3 * H_B
    gate_blocks = []
    for g in range(kv_h):
        blk = w[:, c_g3 + g * grp * 3: c_g3 + (g + 1) * grp * 3]
        gate_blocks.append(jnp.pad(blk, ((0, 0), (0, LANES - grp * 3))))
    wn = jnp.concatenate([w[:, :c_g3], w[:, c_zb:c_zb + d_b]] + gate_blocks, axis=1)
    return wn.astype(BF16)


def _even_in_proj(x2, w_bf, d_a, d_b, d_kv, kv_h, hd, t_len):
    m, d = x2.shape
    tm = min(512, m)
    assert m % tm == 0
    n_gate = kv_h * LANES
    row = lambda n: pl.BlockSpec((tm, n), lambda i: (i, 0))
    out_shape = (jax.ShapeDtypeStruct((m, d_a), F32), jax.ShapeDtypeStruct((m, d_a), F32),
                 jax.ShapeDtypeStruct((m, d_b), BF16),
                 jax.ShapeDtypeStruct((m, d_kv), F32), jax.ShapeDtypeStruct((m, d_kv), F32),
                 jax.ShapeDtypeStruct((m, d_kv), F32),
                 jax.ShapeDtypeStruct((m, d_b), F32), jax.ShapeDtypeStruct((m, n_gate), F32))
    out_specs = (row(d_a), row(d_a), row(d_b), row(d_kv), row(d_kv), row(d_kv), row(d_b), row(n_gate))
    if t_len % tm == 0 and tm % LANES == 0:
        per_seq = t_len // tm
        t_spec = pl.BlockSpec((1, d_kv, tm), lambda i: (i // per_seq, 0, i % per_seq))
        out_shape += (jax.ShapeDtypeStruct((m // t_len, d_kv, t_len), F32),) * 3
        out_specs += (t_spec,) * 3
    return pl.pallas_call(
        functools.partial(_even_in_proj_kernel, d_a=d_a, d_b=d_b, d_kv=d_kv, q_scale=hd ** -0.5),
        grid=(m // tm,),
        in_specs=[row(d), pl.BlockSpec(w_bf.shape, lambda i: (0, 0))],
        out_specs=out_specs,
        out_shape=out_shape,
        compiler_params=_cparams(1),
        name="even_in_proj",
    )(x2, w_bf)


CONV_HIST = 32


def _conv_kernel(hist_ref, u_ref, sza_ref, w_ref, cb_ref, g_ref, b_ref, ya_ref, win_ref, sh_ref, conv_ref,
                 *, tq, rb, n_taps):
    i = pl.program_id(1)
    base = pl.multiple_of(i * tq, SUBLANES)
    win_ref[CONV_HIST:, :] = u_ref[0, pl.ds(base, tq), :]

    @pl.when(i == 0)
    def _():
        win_ref[:CONV_HIST, :] = hist_ref[0]

    @pl.when(i > 0)
    def _():
        win_ref[:CONV_HIST, :] = u_ref[0, pl.ds(base - CONV_HIST, CONV_HIST), :]

    w = win_ref[...]
    n = tq + CONV_HIST
    sh_ref[0] = w
    for s in range(1, SUBLANES):
        sh_ref[s] = pltpu.roll(w, n - s, 0)

    first = CONV_HIST - (n_taps - 1)
    cb = cb_ref[...]
    g = g_ref[...]
    b = b_ref[...]

    def block(k, carry):
        r0 = pl.multiple_of(k * rb, SUBLANES)
        acc = jnp.zeros((rb, u_ref.shape[-1]), F32) + cb
        for j in range(n_taps):
            a, s = divmod(first + j, SUBLANES)
            acc = acc + sh_ref[s, pl.ds(r0 + SUBLANES * a, rb), :] * w_ref[j:j + 1, :]
        conv_ref[pl.ds(r0, rb), :] = acc
        return carry

    lax.fori_loop(0, tq // rb, block, 0)
    y = _silu(_layer_norm_rows(conv_ref[...], g, b, LN_EPS)) * sza_ref[0]
    ya_ref[0] = y.astype(ya_ref.dtype)


def _conv_branch(hist, u3, sza3, conv_w, conv_b, ln_g, ln_b):
    bsz, t, d = u3.shape
    n_taps = conv_w.shape[0]
    assert n_taps - 1 <= CONV_HIST and t % SUBLANES == 0
    tq = min(256, t)
    rb = min(32, tq)
    assert t % tq == 0 and tq % rb == 0
    vec = lambda a: a.reshape(1, d)
    full = lambda shp: pl.BlockSpec(shp, lambda b, i: (0,) * len(shp))
    return pl.pallas_call(
        functools.partial(_conv_kernel, tq=tq, rb=rb, n_taps=n_taps),
        grid=(bsz, t // tq),
        in_specs=[pl.BlockSpec((1, CONV_HIST, d), lambda b, i: (b, 0, 0)),
                  pl.BlockSpec((1, t, d), lambda b, i: (b, 0, 0)),
                  pl.BlockSpec((1, tq, d), lambda b, i: (b, i, 0)),
                  full((n_taps, d)), full((1, d)), full((1, d)), full((1, d))],
        out_specs=pl.BlockSpec((1, tq, d), lambda b, i: (b, i, 0)),
        out_shape=jax.ShapeDtypeStruct((bsz, t, d), BF16),
        scratch_shapes=[pltpu.VMEM((tq + CONV_HIST, d), F32),
                        pltpu.VMEM((SUBLANES, tq + CONV_HIST, d), F32), pltpu.VMEM((tq, d), F32)],
        compiler_params=_cparams(2),
        name="conv_branch",
    )(hist, u3, sza3, conv_w, vec(conv_b), vec(ln_g), vec(ln_b))


SEL_OFF = -(2.0 ** 100)


def _nsa_prompt_kernel(q_ref, kcmp_ref, ksel_ref, kwin_ref, gate_ref, szb_ref, wpool_ref, o_ref,
                       kc_s, vc_s, ks_s, vs_s, kw_s, vw_s, m_s, l_s, acc_s, s_scr,
                       *, tq, t_len, l_cmp, hd, grp, kv_h):
    i = pl.program_id(1)
    g = pl.program_id(2)
    n_cmp = t_len // l_cmp
    n_sel = t_len // L_SEL
    half = n_cmp // 2
    gw = kv_h * hd

    @pl.when((i == 0) & (g == 0))
    def _prep():
        x3 = kcmp_ref[0].reshape(half, 2 * l_cmp, 2 * gw)
        pooled = jnp.concatenate([jnp.sum(x3 * wpool_ref[0][None], axis=1),
                                  jnp.sum(x3 * wpool_ref[1][None], axis=1)], axis=0)
        pos = lax.broadcasted_iota(jnp.int32, (t_len, LANES), 0)
        lane = lax.broadcasted_iota(jnp.int32, (t_len, LANES), 1)
        blk = pos // L_SEL
        feat = jnp.where(lane == blk, 1.0, 0.0)
        feat = jnp.where(lane == n_sel, blk.astype(F32), feat)
        feat = jnp.where(lane == n_sel + 1, (pos - blk * L_SEL).astype(F32), feat)
        for src, dst in ((ksel_ref, ks_s), (kwin_ref, kw_s)):
            k01 = src[0, :, :gw]
            dst[0] = jnp.where(lane < hd, feat, pltpu.roll(k01, hd, 1)).astype(BF16)
            dst[1] = jnp.where(lane < hd, feat, k01).astype(BF16)
        vc_pad = jnp.concatenate([pooled[:, gw:], jnp.zeros((LANES - n_cmp, gw), F32)], axis=0) \
            if n_cmp < LANES else pooled[:, gw:]
        vc_t = vc_pad.T
        vs_t = ksel_ref[0, :, gw:].T
        vw_t = kwin_ref[0, :, gw:].T
        for gg in range(kv_h):
            kc_s[gg] = pooled[:, gg * hd:(gg + 1) * hd].astype(BF16)
            vc_s[gg] = vc_t[gg * hd:(gg + 1) * hd, :n_cmp].astype(BF16)
            vs_s[gg] = vs_t[gg * hd:(gg + 1) * hd].astype(BF16)
            vw_s[gg] = vw_t[gg * hd:(gg + 1) * hd].astype(BF16)

    t0 = i * tq
    gs = jnp.float32(1.0)
    for gg in range(1, kv_h):
        gs = jnp.where(g == gg, 2.0 ** -(gg * grp), gs)
    slopes = [gs * (2.0 ** -(r + 1)) for r in range(grp)]
    qt = q_ref[0]
    qs = jnp.concatenate([qt[:, r * hd:(r + 1) * hd] for r in range(grp)], axis=0)

    s_t = lax.dot_general(kc_s[g], qs, NT, preferred_element_type=F32)
    n_io = lax.broadcasted_iota(jnp.int32, (n_cmp, tq), 0)
    t_io = lax.broadcasted_iota(jnp.int32, (n_cmp, tq), 1) + t0
    c_end = jnp.where(n_io < half, 2 * l_cmp * n_io + (l_cmp - 1), 2 * l_cmp * (n_io - half) + (2 * l_cmp - 1))
    dist_c = t_io - c_end
    valid_c = dist_c >= 0
    dist_cf = dist_c.astype(F32)
    imp = jnp.zeros((n_sel, tq), F32)
    o_c = []
    for r in range(grp):
        s = jnp.where(valid_c, s_t[:, r * tq:(r + 1) * tq] - slopes[r] * dist_cf, NEG)
        e = jnp.exp(s - jnp.max(s, axis=0, keepdims=True))
        p = jnp.where(valid_c, e / jnp.sum(e, axis=0, keepdims=True), 0.0)
        imp = imp + (p[:half] + p[half:])
        o_c.append(jnp.dot(vc_s[g], p.astype(BF16), preferred_element_type=F32))

    sb = lax.broadcasted_iota(jnp.int32, (n_sel, tq), 0)
    tb = (lax.broadcasted_iota(jnp.int32, (n_sel, tq), 1) + t0) // L_SEL
    forced = (sb == 0) | (sb == tb) | (sb == tb - 1)
    imp = jnp.where(forced, FORCE, imp)
    imp = jnp.where(sb <= tb, imp, -jnp.inf)
    rank = jnp.zeros((n_sel, tq), F32)
    for j in range(n_sel):
        vj = imp[j:j + 1, :]
        beats = jnp.where(imp > vj, 1.0, jnp.where((imp == vj) & (sb < j), 1.0, 0.0))
        rank = jnp.where(sb == j, jnp.sum(beats, axis=0, keepdims=True), rank)
    off_t = jnp.where(rank < min(K_SEL, n_sel), 0.0, SEL_OFF)
    feat_sel = jnp.concatenate([off_t, jnp.zeros((LANES - n_sel, tq), F32)], axis=0).T

    qf = q_ref[0].astype(F32)
    lane_q = lax.broadcasted_iota(jnp.int32, (tq, LANES), 1)

    def q_aug(r, feat):
        tile = qf[:, (r // 2) * LANES:(r // 2 + 1) * LANES]
        if r % 2 == 0:
            tile = pltpu.roll(tile, hd, 1)
        x = jnp.where(lane_q < hd, feat, tile)
        x = jnp.where(lane_q == n_sel, slopes[r] * L_SEL, x)
        return jnp.where(lane_q == n_sel + 1, slopes[r], x).astype(BF16)

    d0 = (lax.broadcasted_iota(jnp.int32, (tq, tq), 1) - lax.broadcasted_iota(jnp.int32, (tq, tq), 0))

    def reset():
        m_s[...] = jnp.full(m_s.shape, NEG, F32)
        l_s[...] = jnp.zeros(l_s.shape, F32)
        acc_s[...] = jnp.zeros(acc_s.shape, F32)

    heads = range(grp)

    def scores(q_list, k_s, kt):
        k0 = pl.multiple_of(kt * tq, tq)
        kk = k_s[g, pl.ds(k0, tq), :]
        return [lax.dot_general(kk, q_list[r], NT, preferred_element_type=F32) for r in heads]

    def update(s, v_s, kt, mask):
        vt = v_s[g, :, pl.ds(pl.multiple_of(kt * tq, tq), tq)]
        if mask is not None:
            s = [jnp.where(mask, s[r], NEG) for r in heads]
        m_old = [m_s[r] for r in heads]
        m_new = [jnp.maximum(m_old[r], jnp.max(s[r], axis=0, keepdims=True)) for r in heads]
        al = [jnp.exp(m_old[r] - m_new[r]) for r in heads]
        p = [jnp.exp(s[r] - m_new[r]) for r in heads]
        for r in heads:
            m_s[r] = m_new[r]
            l_s[r] = al[r] * l_s[r] + jnp.sum(p[r], axis=0, keepdims=True)
            acc_s[r] = al[r] * acc_s[r] + jnp.dot(vt, p[r].astype(BF16), preferred_element_type=F32)

    def attend(q_list, k_s, v_s, lo, interior_mask, diag_mask):
        reset()

        def put(slot, kt):
            s = scores(q_list, k_s, kt)
            for r in heads:
                s_scr[slot, r] = s[r]

        def take(slot):
            return [s_scr[slot, r] for r in heads]

        odd = (i - lo) % 2

        @pl.when(odd == 1)
        def _():
            update(scores(q_list, k_s, lo), v_s, lo, interior_mask(lo))

        lo2 = lo + odd
        put(0, lo2)

        def body(j, carry):
            kt = lo2 + 2 * j
            put(1, kt + 1)
            update(take(0), v_s, kt, interior_mask(kt))
            put(0, kt + 2)
            update(take(1), v_s, kt + 1, interior_mask(kt + 1))
            return carry

        lax.fori_loop(0, (i - lo2) // 2, body, 0)
        update(take(0), v_s, i, diag_mask)
        return [acc_s[r] / l_s[r] for r in heads]

    o_s = attend([q_aug(r, feat_sel) for r in heads], ks_s, vs_s, 0, lambda kt: None, d0 >= 0)

    w_tiles = -(-(WINDOW - 1) // tq)
    o_w = attend([q_aug(r, 0.0) for r in heads], kw_s, vw_s, jnp.maximum(i - w_tiles, 0),
                 lambda kt: d0 + (i - kt) * tq < WINDOW, (d0 >= 0) & (d0 < WINDOW))

    gate_t = gate_ref[0].T
    outs = []
    for r in range(grp):
        outs.append(gate_t[3 * r:3 * r + 1] * o_c[r]
                    + gate_t[3 * r + 1:3 * r + 2] * o_s[r]
                    + gate_t[3 * r + 2:3 * r + 3] * o_w[r])
    o_ref[0] = (jnp.concatenate(outs, axis=0).T * szb_ref[0]).astype(o_ref.dtype)


def _pool_weights(wk, wv, hd, halves):
    l_cmp, kv_h = wk.shape
    row = jnp.concatenate([jnp.repeat(wk, hd, axis=1), jnp.repeat(wv, hd, axis=1)], axis=1)
    if not halves:
        return row
    z = jnp.zeros_like(row)
    return jnp.stack([jnp.concatenate([row, z], axis=0), jnp.concatenate([z, row], axis=0)])


def _nsa_prompt(q3, cmp3, sel3, win3, gate3, szb3, wk, wv):
    bsz, t, d_b = q3.shape
    l_cmp, kv_h = wk.shape
    grp = H_B // kv_h
    hd = d_b // H_B
    gw = kv_h * hd
    tq = min(256, t)
    assert t % tq == 0 and t % (2 * l_cmp) == 0 and L_SEL == 2 * l_cmp and tq % L_SEL == 0
    n_cmp = t // l_cmp
    n_sel = t // L_SEL
    assert gw == LANES and 2 * hd == LANES and n_sel + 2 <= hd and n_sel % SUBLANES == 0 and n_cmp <= LANES
    wpool = _pool_weights(wk, wv, hd, halves=True)
    kv_spec = pl.BlockSpec((1, t, 2 * gw), lambda b, i, g: (b, 0, 0))
    return pl.pallas_call(
        functools.partial(_nsa_prompt_kernel, tq=tq, t_len=t, l_cmp=l_cmp, hd=hd, grp=grp, kv_h=kv_h),
        grid=(bsz, t // tq, kv_h),
        in_specs=[pl.BlockSpec((1, tq, grp * hd), lambda b, i, g: (b, i, g)),
                  kv_spec, kv_spec, kv_spec,
                  pl.BlockSpec((1, tq, LANES), lambda b, i, g: (b, i, g)),
                  pl.BlockSpec((1, tq, grp * hd), lambda b, i, g: (b, i, g)),
                  pl.BlockSpec(wpool.shape, lambda b, i, g: (0, 0, 0))],
        out_specs=pl.BlockSpec((1, tq, grp * hd), lambda b, i, g: (b, i, g)),
        out_shape=jax.ShapeDtypeStruct((bsz, t, d_b), BF16),
        scratch_shapes=[pltpu.VMEM((kv_h, n_cmp, hd), BF16), pltpu.VMEM((kv_h, hd, n_cmp), BF16),
                        pltpu.VMEM((kv_h, t, LANES), BF16), pltpu.VMEM((kv_h, hd, t), BF16),
                        pltpu.VMEM((kv_h, t, LANES), BF16), pltpu.VMEM((kv_h, hd, t), BF16),
                        pltpu.VMEM((grp, 1, tq), F32), pltpu.VMEM((grp, 1, tq), F32),
                        pltpu.VMEM((grp, hd, tq), F32), pltpu.VMEM((2, grp, tq, tq), F32)],
        compiler_params=_cparams(3),
        name="nsa_prompt",
    )(q3, cmp3, sel3, win3, gate3, szb3, wpool)


def _nsa_decode_t_kernel(pt_ref, *refs, pg, n_pages, page, l_cmp, hd, grp, kv_h, w_buf):
    del pt_ref
    q_ref = refs[0]
    cmp_refs = refs[1:1 + pg]
    sel_refs = refs[1 + pg:1 + 2 * pg]
    (kwin_ref, seln_ref, winn_ref, gate_ref, szb_ref, wrow_ref, seg_ref, o_ref,
     kcv_s, selm_s, oc_s, m_s, l_s, acc_s) = refs[1 + 2 * pg:]
    p = pl.program_id(1)
    n_steps = n_pages // pg
    past = n_pages * page
    n_cmp = past // l_cmp
    n_sel = past // L_SEL
    per_page = page // l_cmp
    pages_per_chunk = LANES // per_page
    steps_per_chunk = pages_per_chunk // pg
    n_chunks = n_pages // pages_per_chunk
    gw = kv_h * hd
    nc_pad = selm_s.shape[-1]
    qf = q_ref[0]
    q = qf.astype(BF16)
    head = lax.broadcasted_iota(jnp.int32, (H_B, 1), 0)
    slope = jnp.zeros((H_B, 1), F32)
    for hh in range(H_B):
        slope = jnp.where(head == hh, 2.0 ** -(hh + 1), slope)

    @pl.when(p < n_steps)
    def _pool():
        sub = p % steps_per_chunk
        prod = jnp.concatenate([cmp_refs[j][0] * wrow_ref[...] for j in range(pg)], axis=1)
        seg = seg_ref[pl.ds(pl.multiple_of(sub * (pg * page), pg * page), pg * page), :]
        tot = _mm(prod, seg, NN, pa=2)
        c = p // steps_per_chunk

        @pl.when(sub == 0)
        def _():
            kcv_s[c] = tot

        @pl.when(sub != 0)
        def _():
            kcv_s[c] = kcv_s[c] + tot

    @pl.when(p == n_steps - 1)
    def _compressed():
        n_io = lax.broadcasted_iota(jnp.int32, (1, n_cmp), 1)
        dist = (past - (l_cmp * n_io + (l_cmp - 1))).astype(F32)
        pair = jnp.where(lax.broadcasted_iota(jnp.int32, (n_cmp, nc_pad), 0) // (L_SEL // l_cmp)
                         == lax.broadcasted_iota(jnp.int32, (n_cmp, nc_pad), 1), 1.0, 0.0).astype(BF16)
        c_io = lax.broadcasted_iota(jnp.int32, (1, nc_pad), 1)
        i_r = lax.broadcasted_iota(jnp.int32, (nc_pad, nc_pad), 1)
        i_c = lax.broadcasted_iota(jnp.int32, (nc_pad, nc_pad), 0)
        selm_s[...] = jnp.zeros(selm_s.shape, F32)
        for g in range(kv_h):
            kc_t = jnp.concatenate([kcv_s[c, g * hd:(g + 1) * hd, :] for c in range(n_chunks)], axis=1)
            vc_t = jnp.concatenate([kcv_s[c, gw + g * hd: gw + (g + 1) * hd, :] for c in range(n_chunks)], axis=1)
            s = jnp.dot(q, kc_t.astype(BF16), preferred_element_type=F32) - slope * dist
            e = jnp.exp(s - jnp.max(s, axis=-1, keepdims=True))
            pc = e / jnp.sum(e, axis=-1, keepdims=True)
            oc_s[g] = lax.dot_general(pc.astype(BF16), vc_t.astype(BF16), NT, preferred_element_type=F32)
            in_grp = (head >= g * grp) & (head < (g + 1) * grp)
            imp = _mm(jnp.sum(jnp.where(in_grp, pc, 0.0), axis=0, keepdims=True), pair, NN, pa=3)
            forced = (c_io == 0) | (c_io == n_sel - 1)
            imp = jnp.where(forced, FORCE, imp)
            imp = jnp.where(c_io < n_sel, imp, -jnp.inf)
            v_r = jnp.broadcast_to(imp, (nc_pad, nc_pad))
            v_c = v_r.T
            beats = jnp.where(v_c > v_r, 1.0, jnp.where((v_c == v_r) & (i_c < i_r), 1.0, 0.0))
            rank = jnp.sum(beats, axis=0, keepdims=True) + jnp.where(imp < FORCE, 1.0, 0.0)
            chosen = (rank < min(K_SEL, n_sel + 1)) & (c_io < n_sel)
            selm_s[g:g + 1, :] = jnp.where(chosen, 1.0, 0.0)
        m_s[...] = jnp.full(m_s.shape, NEG, F32)
        l_s[...] = jnp.zeros(l_s.shape, F32)
        acc_s[...] = jnp.zeros(acc_s.shape, F32)

    @pl.when(p >= n_steps)
    def _selected():
        step = p - n_steps
        nk = pg * page
        kpos = lax.broadcasted_iota(jnp.int32, (1, nk), 1) + step * nk
        dist = (past - kpos).astype(F32)
        expand = jnp.where(lax.broadcasted_iota(jnp.int32, (nc_pad, nk), 0)
                           == (lax.broadcasted_iota(jnp.int32, (nc_pad, nk), 1) + step * nk) // L_SEL,
                           1.0, 0.0).astype(BF16)
        chosen = jnp.dot(selm_s[...].astype(BF16), expand, preferred_element_type=F32)
        for g in range(kv_h):
            k_t = jnp.concatenate([sel_refs[j][0, g * hd:(g + 1) * hd, :] for j in range(pg)], axis=1)
            v_t = jnp.concatenate([sel_refs[j][0, gw + g * hd: gw + (g + 1) * hd, :] for j in range(pg)], axis=1)
            s = jnp.dot(q, k_t.astype(BF16), preferred_element_type=F32) - slope * dist
            s = jnp.where(chosen[g:g + 1, :] > 0.5, s, NEG)
            m_old = m_s[g]
            m_new = jnp.maximum(m_old, jnp.max(s, axis=-1, keepdims=True))
            al = jnp.exp(m_old - m_new)
            pe = jnp.exp(s - m_new)
            m_s[g] = m_new
            l_s[g] = al * l_s[g] + jnp.sum(pe, axis=-1, keepdims=True)
            acc_s[g] = al * acc_s[g] + lax.dot_general(pe.astype(BF16), v_t.astype(BF16), NT,
                                                       preferred_element_type=F32)

    @pl.when(p == 2 * n_steps - 1)
    def _finish():
        gate = gate_ref[0]
        i_io = lax.broadcasted_iota(jnp.int32, (1, w_buf), 1)
        dist_w = w_buf - i_io
        valid_w = dist_w < WINDOW
        o = jnp.zeros((H_B, hd), F32)
        for g in range(kv_h):
            kn = seln_ref[0][:, g * hd:(g + 1) * hd]
            vn = seln_ref[0][:, gw + g * hd: gw + (g + 1) * hd]
            s_n = jnp.sum(qf * kn, axis=-1, keepdims=True)
            m_old = m_s[g]
            m_new = jnp.maximum(m_old, s_n)
            al = jnp.exp(m_old - m_new)
            pn = jnp.exp(s_n - m_new)
            o_s = (al * acc_s[g] + pn * vn) / (al * l_s[g] + pn)
            kw_t = kwin_ref[0, g * hd:(g + 1) * hd, :].astype(BF16)
            vw_t = kwin_ref[0, gw + g * hd: gw + (g + 1) * hd, :].astype(BF16)
            s = jnp.dot(q, kw_t, preferred_element_type=F32) - slope * dist_w.astype(F32)
            s = jnp.where(valid_w, s, NEG)
            kwn = winn_ref[0][:, g * hd:(g + 1) * hd]
            vwn = winn_ref[0][:, gw + g * hd: gw + (g + 1) * hd]
            s_wn = jnp.sum(qf * kwn, axis=-1, keepdims=True)
            mw = jnp.maximum(jnp.max(s, axis=-1, keepdims=True), s_wn)
            pw = jnp.where(valid_w, jnp.exp(s - mw), 0.0)
            pwn = jnp.exp(s_wn - mw)
            o_w = ((lax.dot_general(pw.astype(BF16), vw_t, NT, preferred_element_type=F32) + pwn * vwn)
                   / (jnp.sum(pw, axis=-1, keepdims=True) + pwn))
            gcol = []
            for c in range(3):
                col = jnp.zeros((H_B, 1), F32)
                for r in range(grp):
                    lane = g * LANES + 3 * r + c
                    col = jnp.where(head == g * grp + r, gate[:, lane:lane + 1], col)
                gcol.append(col)
            og = gcol[0] * oc_s[g] + gcol[1] * o_s + gcol[2] * o_w
            o = jnp.where((head >= g * grp) & (head < (g + 1) * grp), og, o)
        o_ref[0] = o * szb_ref[0]


def _nsa_decode_t(q3, cache_cmp_t, cache_sel_t, cache_win_t, page_table, sel_new, win_new, gate, szb3, wk, wv):
    bs, _, hd = q3.shape
    l_cmp, kv_h = wk.shape
    grp = H_B // kv_h
    gw = kv_h * hd
    page = cache_cmp_t.shape[-1]
    n_pages = page_table.shape[1]
    w_buf = cache_win_t.shape[-1]
    per_page = page // l_cmp
    pages_per_chunk = LANES // per_page
    pg = min(16, pages_per_chunk, n_pages)
    assert page == LANES and page % L_SEL == 0 and page % l_cmp == 0
    assert n_pages % pages_per_chunk == 0 and pages_per_chunk % pg == 0
    n_steps = n_pages // pg
    n_sel = n_pages * page // L_SEL
    nc_pad = -(-n_sel // LANES) * LANES
    wrow = jnp.tile(_pool_weights(wk, wv, hd, halves=False), (per_page, 1)).T
    pos = lax.broadcasted_iota(jnp.int32, (pages_per_chunk, page, LANES), 1)
    pj = lax.broadcasted_iota(jnp.int32, (pages_per_chunk, page, LANES), 0)
    ln = lax.broadcasted_iota(jnp.int32, (pages_per_chunk, page, LANES), 2)
    seg = (ln == pj * per_page + pos // l_cmp).astype(BF16).reshape(pages_per_chunk * page, LANES)

    def cmp_map(j):
        return lambda b, p, pt: (pt[b, jnp.minimum(p, n_steps - 1) * pg + j], 0, 0)

    def sel_map(j):
        return lambda b, p, pt: (pt[b, jnp.maximum(p - n_steps, 0) * pg + j], 0, 0)

    per_b = lambda shp: pl.BlockSpec((1,) + shp, lambda b, p, pt: (b, 0, 0))
    grid_spec = pltpu.PrefetchScalarGridSpec(
        num_scalar_prefetch=1,
        grid=(bs, 2 * n_steps),
        in_specs=([per_b((H_B, hd))]
                  + [pl.BlockSpec((1, 2 * gw, page), cmp_map(j)) for j in range(pg)]
                  + [pl.BlockSpec((1, 2 * gw, page), sel_map(j)) for j in range(pg)]
                  + [per_b((2 * gw, w_buf)), per_b((1, 2 * gw)), per_b((1, 2 * gw)),
                     per_b((1, kv_h * LANES)), per_b((H_B, hd)),
                     pl.BlockSpec((2 * gw, page), lambda b, p, pt: (0, 0)),
                     pl.BlockSpec(seg.shape, lambda b, p, pt: (0, 0))]),
        out_specs=per_b((H_B, hd)),
        scratch_shapes=[pltpu.VMEM((n_pages // pages_per_chunk, 2 * gw, LANES), F32),
                        pltpu.VMEM((SUBLANES, nc_pad), F32),
                        pltpu.VMEM((kv_h, H_B, hd), F32), pltpu.VMEM((kv_h, H_B, 1), F32),
                        pltpu.VMEM((kv_h, H_B, 1), F32), pltpu.VMEM((kv_h, H_B, hd), F32)])
    return pl.pallas_call(
        functools.partial(_nsa_decode_t_kernel, pg=pg, n_pages=n_pages, page=page, l_cmp=l_cmp, hd=hd, grp=grp,
                          kv_h=kv_h, w_buf=w_buf),
        grid_spec=grid_spec,
        out_shape=jax.ShapeDtypeStruct((bs, H_B, hd), F32),
        compiler_params=_cparams(2),
        name="nsa_decode",
    )(page_table, q3, *([cache_cmp_t] * pg), *([cache_sel_t] * pg), cache_win_t, sel_new, win_new, gate, szb3,
      wrow, seg)


def _out_proj_ln_kernel(*refs, n_in, alpha):
    a_refs, w_refs = refs[:n_in], refs[n_in:2 * n_in]
    x_ref, g_ref, b_ref, o_ref = refs[2 * n_in:]
    y = None
    for a_ref, w_ref in zip(a_refs, w_refs):
        t = jnp.dot(a_ref[...].astype(BF16), w_ref[...], preferred_element_type=F32)
        y = t if y is None else y + t
    o_ref[...] = _layer_norm_rows(alpha * x_ref[...] + y, g_ref[...], b_ref[...], LN_EPS)


def _out_proj_ln(a_list, w_list, x2, ln_g, ln_b, alpha):
    m, d = x2.shape
    tm = min(512, m)
    assert m % tm == 0
    n_in = len(a_list)
    return pl.pallas_call(
        functools.partial(_out_proj_ln_kernel, n_in=n_in, alpha=alpha),
        grid=(m // tm,),
        in_specs=([pl.BlockSpec((tm, a.shape[1]), lambda i: (i, 0)) for a in a_list]
                  + [pl.BlockSpec(w.shape, lambda i: (0, 0)) for w in w_list]
                  + [pl.BlockSpec((tm, d), lambda i: (i, 0)),
                     pl.BlockSpec((1, d), lambda i: (0, 0)), pl.BlockSpec((1, d), lambda i: (0, 0))]),
        out_specs=pl.BlockSpec((tm, d), lambda i: (i, 0)),
        out_shape=jax.ShapeDtypeStruct((m, d), F32),
        compiler_params=_cparams(1),
        name="out_proj_ln",
    )(*a_list, *w_list, x2, ln_g.reshape(1, d), ln_b.reshape(1, d))


def _rwkv_proj_kernel(*refs, seq, tiles_per_seq):
    if seq:
        x_ref, tail_ref, shift_ref = refs[:3]
        rest = refs[3:]
    else:
        x_ref, shift_ref = refs[:2]
        rest = refs[2:]
    (mu_ref, w_ref, w1_ref, w2_ref, a1_ref, a2_ref, w0_ref, a0_ref,
     r_ref, k_ref, v_ref, lw_ref, a_ref, sz_ref) = rest
    x = x_ref[...]
    if seq:
        i = pl.program_id(0)
        first = jnp.where(i % tiles_per_seq == 0, shift_ref[0], tail_ref[SUBLANES - 1:SUBLANES, :])
        row = lax.broadcasted_iota(jnp.int32, x.shape, 0)
        x_prev = jnp.where(row == 0, first, pltpu.roll(x, 1, 0))
    else:
        x_prev = shift_ref[...]
    dx = x_prev - x
    mix = lambda n: (x + dx * mu_ref[n:n + 1, :]).astype(BF16)
    r_ref[...] = jnp.dot(mix(0), w_ref[0], preferred_element_type=F32)
    k_ref[...] = jnp.dot(mix(1), w_ref[1], preferred_element_type=F32)
    v_ref[...] = jnp.dot(mix(2), w_ref[2], preferred_element_type=F32)
    sz_ref[...] = _silu(jnp.dot(mix(3), w_ref[3], preferred_element_type=F32))
    hw = jnp.tanh(jnp.dot(mix(4), w1_ref[...], preferred_element_type=F32)).astype(BF16)
    y = -(w0_ref[...] + jnp.dot(hw, w2_ref[...], preferred_element_type=F32))
    softplus = jnp.maximum(y, 0.0) + jnp.log(1.0 + jnp.exp(-jnp.abs(y)))
    lw_ref[...] = -jnp.exp(-softplus - 0.5)
    ha = jnp.dot(mix(5), a1_ref[...], preferred_element_type=F32).astype(BF16)
    a_ref[...] = _sigmoid(a0_ref[...] + jnp.dot(ha, a2_ref[...], preferred_element_type=F32))


def _rwkv_proj(x2, shift, t_len, mu, w_rkvz, w0, w1, w2, a0, a1, a2):
    m, d = x2.shape
    seq = t_len > 1
    tm = min(512, t_len) if seq else m
    assert m % tm == 0 and (not seq or (t_len % tm == 0 and tm % SUBLANES == 0))
    tiles_per_seq = t_len // tm if seq else 1
    full = lambda a: pl.BlockSpec(a.shape, lambda i: (0,) * a.ndim, pipeline_mode=pl.Buffered(1))
    row = pl.BlockSpec((tm, d), lambda i: (i, 0))
    if seq:
        blk = tm // SUBLANES
        lead = [x2, x2, shift.reshape(-1, 1, d)]
        lead_specs = [row, pl.BlockSpec((SUBLANES, d), lambda i: (jnp.maximum(i * blk - 1, 0), 0)),
                      pl.BlockSpec((1, 1, d), lambda i: (i // tiles_per_seq, 0, 0))]
    else:
        lead = [x2, shift]
        lead_specs = [row, row]
    ws = [mu, w_rkvz.astype(BF16), w1.astype(BF16), w2.astype(BF16), a1.astype(BF16), a2.astype(BF16),
          w0.reshape(1, d), a0.reshape(1, d)]
    return pl.pallas_call(
        functools.partial(_rwkv_proj_kernel, seq=seq, tiles_per_seq=tiles_per_seq),
        grid=(m // tm,),
        in_specs=lead_specs + [full(a) for a in ws],
        out_specs=(row,) * 6,
        out_shape=(jax.ShapeDtypeStruct((m, d), F32),) * 6,
        compiler_params=_cparams(1),
        name="rwkv_proj",
    )(*lead, *ws)


WKV_CHUNK = 64
WKV_PASSES = 1
WKV_SUB = 2


def _wkv_kernel(r_ref, k_ref, v_ref, lw_ref, a_ref, sz_ref, s0_ref, kk_ref, ka_ref, rk_ref, gg_ref, gb_ref,
                yz_ref, sfin_ref, h_s, *, c_len, n_sub, n_real, n_heads, hd):
    c = pl.program_id(1)
    mm = functools.partial(_mm, pa=WKV_PASSES, pb=WKV_PASSES)
    n_pairs = n_heads // 2
    pw = 2 * hd
    d = n_heads * hd
    pairs = range(n_pairs)
    iota = lambda shape, axis: lax.broadcasted_iota(jnp.int32, shape, axis)
    bd_mask = (iota((pw, pw), 0) // hd) == (iota((pw, pw), 1) // hd)
    bd_ones = jnp.where(bd_mask, 1.0, 0.0).astype(BF16)
    eye_mask = iota((pw, pw), 0) == iota((pw, pw), 1)
    eye_pb = jnp.where(eye_mask, 1.0, 0.0).astype(BF16)
    lane_lo = iota((1, pw), 1) < hd

    @pl.when(c == 0)
    def _():
        z = jnp.zeros((hd, hd), F32)
        for p in pairs:
            h_s[p] = jnp.concatenate([jnp.concatenate([s0_ref[0, 2 * p].T, z], axis=1),
                                      jnp.concatenate([z, s0_ref[0, 2 * p + 1].T], axis=1)], axis=0)

    def segsum(x):
        xb = x.astype(BF16)
        return jnp.concatenate([jnp.dot(xb[:, t * pw:(t + 1) * pw], bd_ones, preferred_element_type=F32)
                                for t in range(d // pw)], axis=1)

    r = r_ref[0]
    k = k_ref[0]
    v = v_ref[0]
    lw = lw_ref[0]
    a = a_ref[0]
    tri = jnp.where(iota((c_len, c_len), 0) >= iota((c_len, c_len), 1), 1.0, 0.0).astype(BF16)
    col2 = iota((c_len, 2 * c_len), 1) % c_len
    low_exc2 = iota((c_len, 2 * c_len), 0) > col2
    eye2 = jnp.where(iota((c_len, 2 * c_len), 0) == col2, 1.0, 0.0)
    low_inc4 = iota((c_len, 4 * c_len), 0) >= iota((c_len, 4 * c_len), 1) % c_len
    cc_mask = (iota((2 * c_len, 2 * c_len), 0) // c_len) == (iota((2 * c_len, 2 * c_len), 1) // c_len)

    def by_head(x):
        return jnp.concatenate([jnp.where(lane_lo, x, 0.0), jnp.where(lane_lo, 0.0, x)], axis=0).astype(BF16)
    kk = k * kk_ref[...]
    kap = kk / jnp.maximum(jnp.sqrt(segsum(kk * kk)), 1e-12)
    ka = kap * a
    kmod = k * (1.0 + (a - 1.0) * ka_ref[...])
    rkr = r * kmod * rk_ref[...]
    levels = max((n_real - 1).bit_length() - 1, 0)

    lhs_b, lhs2_t, vb, v_st, l_b, l_k, m_r = [], [], [], [], [], [], []
    for s in range(n_sub):
        rs = slice(s * c_len, (s + 1) * c_len)
        lw_s = lw[rs]
        cum = _mm(tri, lw_s, NN, pb=3)
        cum_end = cum[c_len - 1:c_len, :]
        p_inv = jnp.exp(-cum)
        p_rest = jnp.exp(cum_end - cum)
        p_end = jnp.exp(cum_end)
        lhs_top = kap[rs] * jnp.exp(cum - lw_s)
        r_t = r[rs] * jnp.exp(cum)
        rhs_top = ka[rs] * p_inv
        k_t = kmod[rs] * p_inv
        ke_top = ka[rs] * p_rest
        k_e = kmod[rs] * p_rest
        for p in pairs:
            ls = slice(p * pw, (p + 1) * pw)
            lhs_b.append(jnp.concatenate([lhs_top[:, ls], r_t[:, ls]], axis=0).astype(BF16))
            v_p = v[rs, ls]
            vb.append(v_p.astype(BF16))
            v_st.append(by_head(v_p))
            lhs2 = jnp.concatenate([ke_top[:, ls], k_e[:, ls], jnp.where(eye_mask, p_end[:, ls], 0.0)],
                                   axis=0).astype(BF16)
            lhs2_t.append(lax.dot_general(eye_pb, lhs2, NT, preferred_element_type=F32).astype(BF16))
            rhs_st = jnp.concatenate([by_head(rhs_top[:, ls]), by_head(k_t[:, ls])], axis=0)
            am = mm(lhs_b[-1], rhs_st, NT)
            l_b.append(jnp.where(low_exc2, am[:c_len, :2 * c_len], 0.0))
            l_k.append(jnp.where(low_exc2, am[:c_len, 2 * c_len:], 0.0).astype(BF16))
            m_r.append(jnp.where(low_inc4, am[c_len:], 0.0).astype(BF16))
    n_sp = n_sub * n_pairs
    lkv = [mm(l_k[i], v_st[i], NN) for i in range(n_sp)]
    blockdiag = lambda x: jnp.where(cc_mask, jnp.concatenate([x, x], axis=0), 0.0).astype(BF16)
    t_inv = [eye2 - x for x in l_b]
    pwr = l_b
    for _ in range(levels):
        pwr = [mm(x, blockdiag(x), NN) for x in pwr]
        t_inv = [t + mm(t, blockdiag(x), NN) for t, x in zip(t_inv, pwr)]

    h_cur = [h_s[p] for p in pairs]
    y_rows = []
    for s in range(n_sub):
        idx = [s * n_pairs + p for p in pairs]
        hb = [h_cur[p].astype(BF16) for p in pairs]
        gh = [mm(lhs_b[i], hb[p], NN) for p, i in zip(pairs, idx)]
        u = [-mm(t_inv[i], by_head(gh[p][:c_len] + lkv[i]), NN) for p, i in zip(pairs, idx)]
        uv = [jnp.concatenate([u[p].astype(BF16), vb[i]], axis=0) for p, i in zip(pairs, idx)]
        y = [gh[p][c_len:] + mm(m_r[i], jnp.concatenate([by_head(u[p]), v_st[i]], axis=0), NN)
             for p, i in zip(pairs, idx)]
        h_cur = [jnp.where(bd_mask, mm(lhs2_t[i], jnp.concatenate([uv[p], hb[p]], axis=0), NN), 0.0)
                 for p, i in zip(pairs, idx)]
        y_rows.append(jnp.concatenate(y, axis=1))
    for p in pairs:
        h_s[p] = h_cur[p]

    y_all = jnp.concatenate(y_rows, axis=0) if n_sub > 1 else y_rows[0]
    dy = y_all - segsum(y_all) * (1.0 / hd)
    var = segsum(dy * dy) * (1.0 / hd)
    yn = dy * lax.rsqrt(var + GN_EPS) * gg_ref[...] + gb_ref[...]
    yz_ref[0] = ((yn + segsum(rkr) * v) * sz_ref[0]).astype(yz_ref.dtype)

    @pl.when(c == pl.num_programs(1) - 1)
    def _():
        for p in pairs:
            sfin_ref[0, 2 * p] = h_s[p, :hd, :hd].T
            sfin_ref[0, 2 * p + 1] = h_s[p, hd:, hd:].T


def _wkv_scan(r3, k3, v3, lw3, a3, sz3, s0, k_k, k_a, r_k, gn_g, gn_b, c_len, t_real):
    bsz, t, d = r3.shape
    n_heads = d // HD_C
    n_sub = WKV_SUB if t % (WKV_SUB * c_len) == 0 else 1
    rows = n_sub * c_len
    assert t % rows == 0 and n_heads % 2 == 0 and 2 * HD_C == LANES
    vec = lambda x: x.reshape(1, d)
    seq = pl.BlockSpec((1, rows, d), lambda b, c: (b, c, 0))
    st = pl.BlockSpec((1, n_heads, HD_C, HD_C), lambda b, c: (b, 0, 0, 0))
    par = pl.BlockSpec((1, d), lambda b, c: (0, 0))
    return pl.pallas_call(
        functools.partial(_wkv_kernel, c_len=c_len, n_sub=n_sub, n_real=min(t_real, c_len), n_heads=n_heads,
                          hd=HD_C),
        grid=(bsz, t // rows),
        in_specs=[seq] * 6 + [st] + [par] * 5,
        out_specs=(seq, st),
        out_shape=(jax.ShapeDtypeStruct((bsz, t, d), BF16),
                   jax.ShapeDtypeStruct((bsz, n_heads, HD_C, HD_C), F32)),
        scratch_shapes=[pltpu.VMEM((n_heads // 2, LANES, LANES), F32)],
        compiler_params=_cparams(2),
        name="wkv_scan",
    )(r3, k3, v3, lw3, a3, sz3, s0, vec(k_k), vec(k_a), vec(r_k), vec(gn_g), vec(gn_b))


def _even_layer(xp, xs, cache_cmp, cache_sel, cache_win, state_conv, page_table,
                w_in, conv_w, conv_b, cln_g, cln_b, wk, wv, w_out, ln_g, ln_b, alpha):
    bp, tp, d = xp.shape
    bs, ts, _ = xs.shape
    assert ts == 1
    d_a = conv_w.shape[-1]
    n_taps = conv_w.shape[0]
    l_cmp, kv_h = wk.shape
    hd = cache_cmp.shape[-1]
    d_b = H_B * hd
    d_kv = 2 * kv_h * hd
    w_bf = _prep_w_in(w_in, d_a, d_b, kv_h, hd)
    cw = conv_w.reshape(n_taps, d_a)
    w_out_bf = w_out.astype(BF16)
    kv_shape = lambda b, t: (b, t, 2, kv_h, hd)

    proj = _even_in_proj(xp.reshape(bp * tp, d), w_bf, d_a, d_b, d_kv, kv_h, hd, tp)
    u, sza, q, cmp_n, sel_n, win_n, szb, gate = proj[:8]
    if len(proj) > 8:
        new_kv = [a.reshape(bp, 2, kv_h, hd, tp).transpose(0, 4, 1, 2, 3) for a in proj[8:]]
    else:
        new_kv = [a.reshape(bp, tp, 2, kv_h, hd) for a in (cmp_n, sel_n, win_n)]
    r3 = lambda a: a.reshape(bp, tp, a.shape[-1])
    u3 = r3(u)
    ya = _conv_branch(jnp.zeros((bp, CONV_HIST, d_a), F32), u3, r3(sza), cw, conv_b, cln_g, cln_b)
    yb = _nsa_prompt(r3(q), r3(cmp_n), r3(sel_n), r3(win_n), r3(gate), r3(szb), wk, wv)
    yp = _out_proj_ln([ya.reshape(bp * tp, d_a), yb.reshape(bp * tp, d_b)], [w_out_bf[:d_a], w_out_bf[d_a:]],
                      xp.reshape(bp * tp, d), ln_g, ln_b, alpha).reshape(bp, tp, d)
    w_keep = min(WINDOW, tp)
    outs_p = (new_kv[0], new_kv[1], new_kv[2][:, tp - w_keep:], u3[:, tp - (n_taps - 1):])

    u, sza, q, cmp_s, sel_s, win_s, szb, gate = _even_in_proj(xs.reshape(bs, d), w_bf, d_a, d_b, d_kv, kv_h, hd, 1)
    ext = jnp.concatenate([state_conv, u[:, None, :]], axis=1)
    hist = jnp.pad(state_conv, ((0, 0), (CONV_HIST - (n_taps - 1), 0), (0, 0)))
    pad_rows = lambda a: jnp.pad(a[:, None, :], ((0, 0), (0, SUBLANES - 1), (0, 0)))
    ya = _conv_branch(hist, pad_rows(u), pad_rows(sza), cw, conv_b, cln_g, cln_b)[:, 0]
    n_pool, page = cache_cmp.shape[:2]
    w_buf = cache_win.shape[1]
    fm = lambda c: jnp.transpose(c, (0, 2, 3, 4, 1)).reshape(c.shape[0], d_kv, c.shape[1])
    yb = _nsa_decode_t(q.astype(F32).reshape(bs, H_B, hd), fm(cache_cmp), fm(cache_sel), fm(cache_win), page_table,
                       sel_s[:, None, :], win_s[:, None, :], gate[:, None, :], szb.reshape(bs, H_B, hd), wk, wv)
    ys = _out_proj_ln([ya, yb.reshape(bs, d_b)], [w_out_bf[:d_a], w_out_bf[d_a:]], xs.reshape(bs, d),
                      ln_g, ln_b, alpha).reshape(bs, 1, d)
    ctx = jnp.concatenate([cache_win, win_s.reshape(kv_shape(bs, 1))], axis=1)
    outs_s = (cmp_s.reshape(kv_shape(bs, 1)), sel_s.reshape(kv_shape(bs, 1)),
              ctx[:, ctx.shape[1] - min(WINDOW, ctx.shape[1]):], ext[:, 1:])
    return yp, ys, outs_p, outs_s


def _odd_group(x3, shift, s0, mu, w_rkvz, w0, w1, w2, a0, a1, a2, k_k, k_a, r_k, gn_g, gn_b, w_out_bf,
               ln_g, ln_b, alpha):
    bsz, t, d = x3.shape
    x2 = x3.reshape(bsz * t, d)
    r, k, v, lw, a, sz = _rwkv_proj(x2, shift, t, mu, w_rkvz, w0, w1, w2, a0, a1, a2)
    c_len = WKV_CHUNK if t >= WKV_CHUNK else -(-t // SUBLANES) * SUBLANES
    t_pad = -(-t // c_len) * c_len
    r3 = lambda z: jnp.pad(z.reshape(bsz, t, d), ((0, 0), (0, t_pad - t), (0, 0)))
    yz, s_fin = _wkv_scan(r3(r), r3(k), r3(v), r3(lw), r3(a), r3(sz), s0, k_k, k_a, r_k.reshape(-1), gn_g, gn_b,
                          c_len, t)
    y = _out_proj_ln([yz[:, :t].reshape(bsz * t, d)], [w_out_bf], x2, ln_g, ln_b, alpha).reshape(bsz, t, d)
    return y, s_fin, x3[:, -1]


def kernel(x_prompt, x_sample, cache_cmp_kv, cache_sel_kv, cache_win_kv, state_conv, state_wkv, state_shift,
           page_table, w_in_even, conv_w, conv_b, conv_ln_g, conv_ln_b, wk_cmp, wv_cmp, w_out_even, mu_c, w_rkvz,
           w0, w1, w2, a0, a1, a2, k_k, k_a, r_k, gn_g, gn_b, w_out_odd, ln_g, ln_b):
    depth = ln_g.shape[0]
    alpha = (2 * depth) ** 0.25
    bp, _, d = x_prompt.shape
    n_heads = d // HD_C
    xp, xs = x_prompt, x_sample
    even_p, even_s, odd_p, odd_s = [], [], [], []
    for l in range(depth):
        if l % 2 == 0:
            e = l // 2
            xp, xs, o_p, o_s = _even_layer(
                xp, xs, cache_cmp_kv[e], cache_sel_kv[e], cache_win_kv[e], state_conv[e], page_table,
                w_in_even[e], conv_w[e], conv_b[e], conv_ln_g[e], conv_ln_b[e], wk_cmp[e], wv_cmp[e],
                w_out_even[e], ln_g[l], ln_b[l], alpha)
            even_p.append(o_p)
            even_s.append(o_s)
        else:
            o = l // 2
            po = (mu_c[o], w_rkvz[o], w0[o], w1[o], w2[o], a0[o], a1[o], a2[o], k_k[o], k_a[o], r_k[o],
                  gn_g[o], gn_b[o], w_out_odd[o].astype(BF16), ln_g[l], ln_b[l], alpha)
            xp, s_p, h_p = _odd_group(xp, jnp.zeros((bp, d), F32), jnp.zeros((bp, n_heads, HD_C, HD_C), F32), *po)
            xs, s_s, h_s = _odd_group(xs, state_shift[o], state_wkv[o], *po)
            odd_p.append((s_p, h_p))
            odd_s.append((s_s, h_s))
    stack = lambda items, j: jnp.stack([it[j] for it in items])
    return (xp, xs,
            stack(even_p, 0), stack(even_s, 0), stack(even_p, 1), stack(even_s, 1),
            stack(even_p, 2), stack(even_s, 2), stack(even_p, 3), stack(even_s, 3),
            stack(odd_p, 0), stack(odd_s, 0), stack(odd_p, 1), stack(odd_s, 1))
```

```python
import functools

import jax
import jax.numpy as jnp
from jax import lax
from jax.experimental import pallas as pl
from jax.experimental.pallas import tpu as pltpu

F32 = jnp.float32
BF16 = jnp.bfloat16

H_B = 8
L_SEL = 64
K_SEL = 16
WINDOW = 512
FORCE = 1e4
HD_C = 64
GN_EPS = 64e-5
LN_EPS = 1e-5
NEG = -1e30

LANES = 128
SUBLANES = 8
VMEM_LIMIT = 56 * 1024 * 1024

NT = (((1,), (1,)), ((), ()))
TN = (((0,), (0,)), ((), ()))
NN = (((1,), (0,)), ((), ()))


def _cparams(n_axes):
    return pltpu.CompilerParams(dimension_semantics=("arbitrary",) * n_axes,
                                vmem_limit_bytes=VMEM_LIMIT)


def _sigmoid(x):
    return 1.0 / (1.0 + jnp.exp(-x))


def _silu(x):
    return x * _sigmoid(x)


def _split_bf16(x, n):
    parts, rem = [], x
    for i in range(n):
        p = rem.astype(BF16)
        parts.append(p)
        if i + 1 < n:
            rem = rem - p.astype(F32)
    return parts


def _mm(a, b, dims=NN, pa=1, pb=1):
    aa = [a] if a.dtype == BF16 else _split_bf16(a, pa)
    bb = [b] if b.dtype == BF16 else _split_bf16(b, pb)
    keep = max(len(aa), len(bb))
    out = None
    for i, ai in enumerate(aa):
        for j, bj in enumerate(bb):
            if i + j < keep:
                t = lax.dot_general(ai, bj, dims, preferred_element_type=F32)
                out = t if out is None else out + t
    return out


def _layer_norm_rows(h, g, b, eps):
    mu = jnp.mean(h, axis=-1, keepdims=True)
    d = h - mu
    var = jnp.mean(d * d, axis=-1, keepdims=True)
    return d * lax.rsqrt(var + eps) * g + b


def _even_in_proj_kernel(x_ref, w_ref, u_ref, sza_ref, q_ref, cmp_ref, sel_ref, win_ref, szb_ref, gate_ref,
                         *t_refs, d_a, d_b, d_kv, q_scale):
    xb = x_ref[...].astype(BF16)

    def seg(lo, n):
        return jnp.dot(xb, w_ref[:, lo:lo + n], preferred_element_type=F32)

    o = 0
    a_val = seg(o, d_a); o += d_a
    a_glu = seg(o, d_a); o += d_a
    u_ref[...] = a_val * _sigmoid(a_glu)
    sza_ref[...] = _silu(seg(o, d_a)); o += d_a
    q_ref[...] = (seg(o, d_b) * q_scale).astype(BF16); o += d_b
    for j, ref in enumerate((cmp_ref, sel_ref, win_ref)):
        kv = seg(o, d_kv); o += d_kv
        ref[...] = kv
        if t_refs:
            t_refs[j][0] = kv.T
    szb_ref[...] = _silu(seg(o, d_b)); o += d_b
    gate_ref[...] = _sigmoid(seg(o, gate_ref.shape[-1]))


def _prep_w_in(w, d_a, d_b, kv_h, hd):
    grp = H_B // kv_h
    c_kv6 = 3 * d_a + d_b
    c_g3 = c_kv6 + 6 * kv_h * hd
    c_zb = c_g3 + 3 * H_B
    gate_blocks = []
    for g in range(kv_h):
        blk = w[:, c_g3 + g * grp * 3: c_g3 + (g + 1) * grp * 3]
        gate_blocks.append(jnp.pad(blk, ((0, 0), (0, LANES - grp * 3))))
    wn = jnp.concatenate([w[:, :c_g3], w[:, c_zb:c_zb + d_b]] + gate_blocks, axis=1)
    return wn.astype(BF16)


def _even_in_proj(x2, w_bf, d_a, d_b, d_kv, kv_h, hd, t_len):
    m, d = x2.shape
    tm = min(512, m)
    assert m % tm == 0
    n_gate = kv_h * LANES
    row = lambda n: pl.BlockSpec((tm, n), lambda i: (i, 0))
    out_shape = (jax.ShapeDtypeStruct((m, d_a), F32), jax.ShapeDtypeStruct((m, d_a), F32),
                 jax.ShapeDtypeStruct((m, d_b), BF16),
                 jax.ShapeDtypeStruct((m, d_kv), F32), jax.ShapeDtypeStruct((m, d_kv), F32),
                 jax.ShapeDtypeStruct((m, d_kv), F32),
                 jax.ShapeDtypeStruct((m, d_b), F32), jax.ShapeDtypeStruct((m, n_gate), F32))
    out_specs = (row(d_a), row(d_a), row(d_b), row(d_kv), row(d_kv), row(d_kv), row(d_b), row(n_gate))
    if t_len % tm == 0 and tm % LANES == 0:
        per_seq = t_len // tm
        t_spec = pl.BlockSpec((1, d_kv, tm), lambda i: (i // per_seq, 0, i % per_seq))
        out_shape += (jax.ShapeDtypeStruct((m // t_len, d_kv, t_len), F32),) * 3
        out_specs += (t_spec,) * 3
    return pl.pallas_call(
        functools.partial(_even_in_proj_kernel, d_a=d_a, d_b=d_b, d_kv=d_kv, q_scale=hd ** -0.5),
        grid=(m // tm,),
        in_specs=[row(d), pl.BlockSpec(w_bf.shape, lambda i: (0, 0))],
        out_specs=out_specs,
        out_shape=out_shape,
        compiler_params=_cparams(1),
        name="even_in_proj",
    )(x2, w_bf)


CONV_HIST = 32


def _conv_kernel(hist_ref, u_ref, sza_ref, w_ref, cb_ref, g_ref, b_ref, ya_ref, win_ref, sh_ref, conv_ref,
                 *, tq, rb, n_taps):
    i = pl.program_id(1)
    base = pl.multiple_of(i * tq, SUBLANES)
    win_ref[CONV_HIST:, :] = u_ref[0, pl.ds(base, tq), :]

    @pl.when(i == 0)
    def _():
        win_ref[:CONV_HIST, :] = hist_ref[0]

    @pl.when(i > 0)
    def _():
        win_ref[:CONV_HIST, :] = u_ref[0, pl.ds(base - CONV_HIST, CONV_HIST), :]

    w = win_ref[...]
    n = tq + CONV_HIST
    sh_ref[0] = w
    for s in range(1, SUBLANES):
        sh_ref[s] = pltpu.roll(w, n - s, 0)

    first = CONV_HIST - (n_taps - 1)
    cb = cb_ref[...]
    g = g_ref[...]
    b = b_ref[...]

    def block(k, carry):
        r0 = pl.multiple_of(k * rb, SUBLANES)
        acc = jnp.zeros((rb, u_ref.shape[-1]), F32) + cb
        for j in range(n_taps):
            a, s = divmod(first + j, SUBLANES)
            acc = acc + sh_ref[s, pl.ds(r0 + SUBLANES * a, rb), :] * w_ref[j:j + 1, :]
        conv_ref[pl.ds(r0, rb), :] = acc
        return carry

    lax.fori_loop(0, tq // rb, block, 0)
    y = _silu(_layer_norm_rows(conv_ref[...], g, b, LN_EPS)) * sza_ref[0]
    ya_ref[0] = y.astype(ya_ref.dtype)


def _conv_branch(hist, u3, sza3, conv_w, conv_b, ln_g, ln_b):
    bsz, t, d = u3.shape
    n_taps = conv_w.shape[0]
    assert n_taps - 1 <= CONV_HIST and t % SUBLANES == 0
    tq = min(256, t)
    rb = min(32, tq)
    assert t % tq == 0 and tq % rb == 0
    vec = lambda a: a.reshape(1, d)
    full = lambda shp: pl.BlockSpec(shp, lambda b, i: (0,) * len(shp))
    return pl.pallas_call(
        functools.partial(_conv_kernel, tq=tq, rb=rb, n_taps=n_taps),
        grid=(bsz, t // tq),
        in_specs=[pl.BlockSpec((1, CONV_HIST, d), lambda b, i: (b, 0, 0)),
                  pl.BlockSpec((1, t, d), lambda b, i: (b, 0, 0)),
                  pl.BlockSpec((1, tq, d), lambda b, i: (b, i, 0)),
                  full((n_taps, d)), full((1, d)), full((1, d)), full((1, d))],
        out_specs=pl.BlockSpec((1, tq, d), lambda b, i: (b, i, 0)),
        out_shape=jax.ShapeDtypeStruct((bsz, t, d), BF16),
        scratch_shapes=[pltpu.VMEM((tq + CONV_HIST, d), F32),
                        pltpu.VMEM((SUBLANES, tq + CONV_HIST, d), F32), pltpu.VMEM((tq, d), F32)],
        compiler_params=_cparams(2),
        name="conv_branch",
    )(hist, u3, sza3, conv_w, vec(conv_b), vec(ln_g), vec(ln_b))


SEL_OFF = -(2.0 ** 100)


def _nsa_prompt_kernel(q_ref, kcmp_ref, ksel_ref, kwin_ref, gate_ref, szb_ref, wpool_ref, o_ref,
                       kc_s, vc_s, ks_s, vs_s, kw_s, vw_s, m_s, l_s, acc_s, s_scr,
                       *, tq, t_len, l_cmp, hd, grp, kv_h):
    i = pl.program_id(1)
    g = pl.program_id(2)
    n_cmp = t_len // l_cmp
    n_sel = t_len // L_SEL
    half = n_cmp // 2
    gw = kv_h * hd

    @pl.when((i == 0) & (g == 0))
    def _prep():
        x3 = kcmp_ref[0].reshape(half, 2 * l_cmp, 2 * gw)
        pooled = jnp.concatenate([jnp.sum(x3 * wpool_ref[0][None], axis=1),
                                  jnp.sum(x3 * wpool_ref[1][None], axis=1)], axis=0)
        pos = lax.broadcasted_iota(jnp.int32, (t_len, LANES), 0)
        lane = lax.broadcasted_iota(jnp.int32, (t_len, LANES), 1)
        blk = pos // L_SEL
        feat = jnp.where(lane == blk, 1.0, 0.0)
        feat = jnp.where(lane == n_sel, blk.astype(F32), feat)
        feat = jnp.where(lane == n_sel + 1, (pos - blk * L_SEL).astype(F32), feat)
        for src, dst in ((ksel_ref, ks_s), (kwin_ref, kw_s)):
            k01 = src[0, :, :gw]
            dst[0] = jnp.where(lane < hd, feat, pltpu.roll(k01, hd, 1)).astype(BF16)
            dst[1] = jnp.where(lane < hd, feat, k01).astype(BF16)
        vc_pad = jnp.concatenate([pooled[:, gw:], jnp.zeros((LANES - n_cmp, gw), F32)], axis=0) \
            if n_cmp < LANES else pooled[:, gw:]
        vc_t = vc_pad.T
        vs_t = ksel_ref[0, :, gw:].T
        vw_t = kwin_ref[0, :, gw:].T
        for gg in range(kv_h):
            kc_s[gg] = pooled[:, gg * hd:(gg + 1) * hd].astype(BF16)
            vc_s[gg] = vc_t[gg * hd:(gg + 1) * hd, :n_cmp].astype(BF16)
            vs_s[gg] = vs_t[gg * hd:(gg + 1) * hd].astype(BF16)
            vw_s[gg] = vw_t[gg * hd:(gg + 1) * hd].astype(BF16)

    t0 = i * tq
    gs = jnp.float32(1.0)
    for gg in range(1, kv_h):
        gs = jnp.where(g == gg, 2.0 ** -(gg * grp), gs)
    slopes = [gs * (2.0 ** -(r + 1)) for r in range(grp)]
    qt = q_ref[0]
    qs = jnp.concatenate([qt[:, r * hd:(r + 1) * hd] for r in range(grp)], axis=0)

    s_t = lax.dot_general(kc_s[g], qs, NT, preferred_element_type=F32)
    n_io = lax.broadcasted_iota(jnp.int32, (n_cmp, tq), 0)
    t_io = lax.broadcasted_iota(jnp.int32, (n_cmp, tq), 1) + t0
    c_end = jnp.where(n_io < half, 2 * l_cmp * n_io + (l_cmp - 1), 2 * l_cmp * (n_io - half) + (2 * l_cmp - 1))
    dist_c = t_io - c_end
    valid_c = dist_c >= 0
    dist_cf = dist_c.astype(F32)
    imp = jnp.zeros((n_sel, tq), F32)
    o_c = []
    for r in range(grp):
        s = jnp.where(valid_c, s_t[:, r * tq:(r + 1) * tq] - slopes[r] * dist_cf, NEG)
        e = jnp.exp(s - jnp.max(s, axis=0, keepdims=True))
        p = jnp.where(valid_c, e / jnp.sum(e, axis=0, keepdims=True), 0.0)
        imp = imp + (p[:half] + p[half:])
        o_c.append(jnp.dot(vc_s[g], p.astype(BF16), preferred_element_type=F32))

    sb = lax.broadcasted_iota(jnp.int32, (n_sel, tq), 0)
    tb = (lax.broadcasted_iota(jnp.int32, (n_sel, tq), 1) + t0) // L_SEL
    forced = (sb == 0) | (sb == tb) | (sb == tb - 1)
    imp = jnp.where(forced, FORCE, imp)
    imp = jnp.where(sb <= tb, imp, -jnp.inf)
    rank = jnp.zeros((n_sel, tq), F32)
    for j in range(n_sel):
        vj = imp[j:j + 1, :]
        beats = jnp.where(imp > vj, 1.0, jnp.where((imp == vj) & (sb < j), 1.0, 0.0))
        rank = jnp.where(sb == j, jnp.sum(beats, axis=0, keepdims=True), rank)
    off_t = jnp.where(rank < min(K_SEL, n_sel), 0.0, SEL_OFF)
    feat_sel = jnp.concatenate([off_t, jnp.zeros((LANES - n_sel, tq), F32)], axis=0).T

    qf = q_ref[0].astype(F32)
    lane_q = lax.broadcasted_iota(jnp.int32, (tq, LANES), 1)

    def q_aug(r, feat):
        tile = qf[:, (r // 2) * LANES:(r // 2 + 1) * LANES]
        if r % 2 == 0:
            tile = pltpu.roll(tile, hd, 1)
        x = jnp.where(lane_q < hd, feat, tile)
        x = jnp.where(lane_q == n_sel, slopes[r] * L_SEL, x)
        return jnp.where(lane_q == n_sel + 1, slopes[r], x).astype(BF16)

    d0 = (lax.broadcasted_iota(jnp.int32, (tq, tq), 1) - lax.broadcasted_iota(jnp.int32, (tq, tq), 0))

    def reset():
        m_s[...] = jnp.full(m_s.shape, NEG, F32)
        l_s[...] = jnp.zeros(l_s.shape, F32)
        acc_s[...] = jnp.zeros(acc_s.shape, F32)

    heads = range(grp)

    def scores(q_list, k_s, kt):
        k0 = pl.multiple_of(kt * tq, tq)
        kk = k_s[g, pl.ds(k0, tq), :]
        return [lax.dot_general(kk, q_list[r], NT, preferred_element_type=F32) for r in heads]

    def update(s, v_s, kt, mask):
        vt = v_s[g, :, pl.ds(pl.multiple_of(kt * tq, tq), tq)]
        if mask is not None:
            s = [jnp.where(mask, s[r], NEG) for r in heads]
        m_old = [m_s[r] for r in heads]
        m_new = [jnp.maximum(m_old[r], jnp.max(s[r], axis=0, keepdims=True)) for r in heads]
        al = [jnp.exp(m_old[r] - m_new[r]) for r in heads]
        p = [jnp.exp(s[r] - m_new[r]) for r in heads]
        for r in heads:
            m_s[r] = m_new[r]
            l_s[r] = al[r] * l_s[r] + jnp.sum(p[r], axis=0, keepdims=True)
            acc_s[r] = al[r] * acc_s[r] + jnp.dot(vt, p[r].astype(BF16), preferred_element_type=F32)

    def attend(q_list, k_s, v_s, lo, interior_mask, diag_mask):
        reset()

        def put(slot, kt):
            s = scores(q_list, k_s, kt)
            for r in heads:
                s_scr[slot, r] = s[r]

        def take(slot):
            return [s_scr[slot, r] for r in heads]

        odd = (i - lo) % 2

        @pl.when(odd == 1)
        def _():
            update(scores(q_list, k_s, lo), v_s, lo, interior_mask(lo))

        lo2 = lo + odd
        put(0, lo2)

        def body(j, carry):
            kt = lo2 + 2 * j
            put(1, kt + 1)
            update(take(0), v_s, kt, interior_mask(kt))
            put(0, kt + 2)
            update(take(1), v_s, kt + 1, interior_mask(kt + 1))
            return carry

        lax.fori_loop(0, (i - lo2) // 2, body, 0)
        update(take(0), v_s, i, diag_mask)
        return [acc_s[r] / l_s[r] for r in heads]

    o_s = attend([q_aug(r, feat_sel) for r in heads], ks_s, vs_s, 0, lambda kt: None, d0 >= 0)

    w_tiles = -(-(WINDOW - 1) // tq)
    o_w = attend([q_aug(r, 0.0) for r in heads], kw_s, vw_s, jnp.maximum(i - w_tiles, 0),
                 lambda kt: d0 + (i - kt) * tq < WINDOW, (d0 >= 0) & (d0 < WINDOW))

    gate_t = gate_ref[0].T
    outs = []
    for r in range(grp):
        outs.append(gate_t[3 * r:3 * r + 1] * o_c[r]
                    + gate_t[3 * r + 1:3 * r + 2] * o_s[r]
                    + gate_t[3 * r + 2:3 * r + 3] * o_w[r])
    o_ref[0] = (jnp.concatenate(outs, axis=0).T * szb_ref[0]).astype(o_ref.dtype)


def _pool_weights(wk, wv, hd, halves):
    l_cmp, kv_h = wk.shape
    row = jnp.concatenate([jnp.repeat(wk, hd, axis=1), jnp.repeat(wv, hd, axis=1)], axis=1)
    if not halves:
        return row
    z = jnp.zeros_like(row)
    return jnp.stack([jnp.concatenate([row, z], axis=0), jnp.concatenate([z, row], axis=0)])


def _nsa_prompt(q3, cmp3, sel3, win3, gate3, szb3, wk, wv):
    bsz, t, d_b = q3.shape
    l_cmp, kv_h = wk.shape
    grp = H_B // kv_h
    hd = d_b // H_B
    gw = kv_h * hd
    tq = min(256, t)
    assert t % tq == 0 and t % (2 * l_cmp) == 0 and L_SEL == 2 * l_cmp and tq % L_SEL == 0
    n_cmp = t // l_cmp
    n_sel = t // L_SEL
    assert gw == LANES and 2 * hd == LANES and n_sel + 2 <= hd and n_sel % SUBLANES == 0 and n_cmp <= LANES
    wpool = _pool_weights(wk, wv, hd, halves=True)
    kv_spec = pl.BlockSpec((1, t, 2 * gw), lambda b, i, g: (b, 0, 0))
    return pl.pallas_call(
        functools.partial(_nsa_prompt_kernel, tq=tq, t_len=t, l_cmp=l_cmp, hd=hd, grp=grp, kv_h=kv_h),
        grid=(bsz, t // tq, kv_h),
        in_specs=[pl.BlockSpec((1, tq, grp * hd), lambda b, i, g: (b, i, g)),
                  kv_spec, kv_spec, kv_spec,
                  pl.BlockSpec((1, tq, LANES), lambda b, i, g: (b, i, g)),
                  pl.BlockSpec((1, tq, grp * hd), lambda b, i, g: (b, i, g)),
                  pl.BlockSpec(wpool.shape, lambda b, i, g: (0, 0, 0))],
        out_specs=pl.BlockSpec((1, tq, grp * hd), lambda b, i, g: (b, i, g)),
        out_shape=jax.ShapeDtypeStruct((bsz, t, d_b), BF16),
        scratch_shapes=[pltpu.VMEM((kv_h, n_cmp, hd), BF16), pltpu.VMEM((kv_h, hd, n_cmp), BF16),
                        pltpu.VMEM((kv_h, t, LANES), BF16), pltpu.VMEM((kv_h, hd, t), BF16),
                        pltpu.VMEM((kv_h, t, LANES), BF16), pltpu.VMEM((kv_h, hd, t), BF16),
                        pltpu.VMEM((grp, 1, tq), F32), pltpu.VMEM((grp, 1, tq), F32),
                        pltpu.VMEM((grp, hd, tq), F32), pltpu.VMEM((2, grp, tq, tq), F32)],
        compiler_params=_cparams(3),
        name="nsa_prompt",
    )(q3, cmp3, sel3, win3, gate3, szb3, wpool)


def _decode_slopes():
    head = lax.broadcasted_iota(jnp.int32, (H_B, 1), 0)
    slope = jnp.zeros((H_B, 1), F32)
    for hh in range(H_B):
        slope = jnp.where(head == hh, 2.0 ** -(hh + 1), slope)
    return head, slope


def _nsa_decode_cmp_kernel(pt_ref, q_ref, *refs, pg, n_pages, page, l_cmp, hd, grp, kv_h, n_slot):
    del pt_ref
    cmp_refs = refs[:pg]
    wrow_ref, seg_ref, oc_ref, idx_ref, kcv_s = refs[pg:]
    p = pl.program_id(1)
    n_chunks = n_pages // pg
    past = n_pages * page
    n_cmp = past // l_cmp
    n_sel = past // L_SEL
    gw = kv_h * hd
    nc_pad = idx_ref.shape[-1]
    prod = jnp.concatenate([cmp_refs[j][0] * wrow_ref[...] for j in range(pg)], axis=1)
    kcv_s[p] = _mm(prod, seg_ref[...], NN, pa=2)

    @pl.when(p == n_chunks - 1)
    def _compressed():
        head, slope = _decode_slopes()
        q = q_ref[0].astype(BF16)
        n_io = lax.broadcasted_iota(jnp.int32, (1, n_cmp), 1)
        dist = (past - (l_cmp * n_io + (l_cmp - 1))).astype(F32)
        pair = jnp.where(lax.broadcasted_iota(jnp.int32, (n_cmp, nc_pad), 0) // (L_SEL // l_cmp)
                         == lax.broadcasted_iota(jnp.int32, (n_cmp, nc_pad), 1), 1.0, 0.0).astype(BF16)
        c_io = lax.broadcasted_iota(jnp.int32, (1, nc_pad), 1)
        i_r = lax.broadcasted_iota(jnp.int32, (nc_pad, nc_pad), 1)
        i_c = lax.broadcasted_iota(jnp.int32, (nc_pad, nc_pad), 0)
        before = jnp.where(i_c < i_r, 1.0, 0.0).astype(BF16)
        slot = lax.broadcasted_iota(jnp.int32, (n_slot, nc_pad), 0).astype(F32)
        for g in range(kv_h):
            kc_t = jnp.concatenate([kcv_s[c, g * hd:(g + 1) * hd, :] for c in range(n_chunks)], axis=1)
            vc_t = jnp.concatenate([kcv_s[c, gw + g * hd: gw + (g + 1) * hd, :] for c in range(n_chunks)], axis=1)
            s = jnp.dot(q, kc_t.astype(BF16), preferred_element_type=F32) - slope * dist
            e = jnp.exp(s - jnp.max(s, axis=-1, keepdims=True))
            pc = e / jnp.sum(e, axis=-1, keepdims=True)
            oc_ref[0, g] = lax.dot_general(pc.astype(BF16), vc_t.astype(BF16), NT, preferred_element_type=F32)
            in_grp = (head >= g * grp) & (head < (g + 1) * grp)
            imp = _mm(jnp.sum(jnp.where(in_grp, pc, 0.0), axis=0, keepdims=True), pair, NN, pa=3)
            forced = (c_io == 0) | (c_io == n_sel - 1)
            imp = jnp.where(forced, FORCE, imp)
            imp = jnp.where(c_io < n_sel, imp, -jnp.inf)
            v_r = jnp.broadcast_to(imp, (nc_pad, nc_pad))
            v_c = v_r.T
            beats = jnp.where(v_c > v_r, 1.0, jnp.where((v_c == v_r) & (i_c < i_r), 1.0, 0.0))
            rank = jnp.sum(beats, axis=0, keepdims=True) + jnp.where(imp < FORCE, 1.0, 0.0)
            chosen = jnp.where((rank < min(K_SEL, n_sel + 1)) & (c_io < n_sel), 1.0, 0.0)
            n_before = jnp.dot(jnp.broadcast_to(chosen, (SUBLANES, nc_pad)).astype(BF16), before,
                               preferred_element_type=F32)[:1]
            hit = jnp.where((n_before == slot) & (chosen > 0.5), 1.0, 0.0)
            ids = jnp.sum(hit * c_io.astype(F32), axis=-1, keepdims=True)
            ids = jnp.where(jnp.sum(hit, axis=-1, keepdims=True) > 0.5, ids, -1.0)
            idx_ref[0, g] = jnp.broadcast_to(ids, (n_slot, nc_pad)).astype(jnp.int32)


def _nsa_decode_sel_kernel(pt_ref, idx_ref, q_ref, *refs, n_slot, n_pages, page, hd, grp, kv_h, w_buf):
    del pt_ref
    n_blk = kv_h * n_slot
    k_refs = refs[:n_blk]
    v_refs = refs[n_blk:2 * n_blk]
    kwin_ref, seln_ref, winn_ref, gate_ref, szb_ref, oc_ref, o_ref = refs[2 * n_blk:]
    b = pl.program_id(0)
    past = n_pages * page
    gw = kv_h * hd
    per_page = page // L_SEL
    head, slope = _decode_slopes()
    qf = q_ref[0]
    q = qf.astype(BF16)
    gate = gate_ref[0]
    lane = lax.broadcasted_iota(jnp.int32, (1, page), 1)
    i_io = lax.broadcasted_iota(jnp.int32, (1, w_buf), 1)
    dist_w = w_buf - i_io
    valid_w = dist_w < WINDOW
    o = jnp.zeros((H_B, hd), F32)
    for g in range(kv_h):
        dists, valids = [], []
        for j in range(n_slot):
            blk = idx_ref[b, g, j]
            safe = jnp.maximum(blk, 0)
            dists.append(past - ((safe // per_page) * page + lane))
            valids.append(lane // L_SEL == jnp.where(blk >= 0, safe % per_page, -1))
        dist = jnp.concatenate(dists, axis=1).astype(F32)
        valid = jnp.concatenate(valids, axis=1)
        k_t = jnp.concatenate([k_refs[g * n_slot + j][0] for j in range(n_slot)], axis=1).astype(BF16)
        v_t = jnp.concatenate([v_refs[g * n_slot + j][0] for j in range(n_slot)], axis=1).astype(BF16)
        s = jnp.where(valid, jnp.dot(q, k_t, preferred_element_type=F32) - slope * dist, NEG)
        kn = seln_ref[0][:, g * hd:(g + 1) * hd]
        vn = seln_ref[0][:, gw + g * hd: gw + (g + 1) * hd]
        s_n = jnp.sum(qf * kn, axis=-1, keepdims=True)
        m = jnp.maximum(jnp.max(s, axis=-1, keepdims=True), s_n)
        pe = jnp.where(valid, jnp.exp(s - m), 0.0)
        pn = jnp.exp(s_n - m)
        o_s = ((lax.dot_general(pe.astype(BF16), v_t, NT, preferred_element_type=F32) + pn * vn)
               / (jnp.sum(pe, axis=-1, keepdims=True) + pn))
        kw_t = kwin_ref[0, g * hd:(g + 1) * hd, :].astype(BF16)
        vw_t = kwin_ref[0, gw + g * hd: gw + (g + 1) * hd, :].astype(BF16)
        s = jnp.dot(q, kw_t, preferred_element_type=F32) - slope * dist_w.astype(F32)
        s = jnp.where(valid_w, s, NEG)
        kwn = winn_ref[0][:, g * hd:(g + 1) * hd]
        vwn = winn_ref[0][:, gw + g * hd: gw + (g + 1) * hd]
        s_wn = jnp.sum(qf * kwn, axis=-1, keepdims=True)
        mw = jnp.maximum(jnp.max(s, axis=-1, keepdims=True), s_wn)
        pw = jnp.where(valid_w, jnp.exp(s - mw), 0.0)
        pwn = jnp.exp(s_wn - mw)
        o_w = ((lax.dot_general(pw.astype(BF16), vw_t, NT, preferred_element_type=F32) + pwn * vwn)
               / (jnp.sum(pw, axis=-1, keepdims=True) + pwn))
        gcol = []
        for c in range(3):
            col = jnp.zeros((H_B, 1), F32)
            for r in range(grp):
                ln = g * LANES + 3 * r + c
                col = jnp.where(head == g * grp + r, gate[:, ln:ln + 1], col)
            gcol.append(col)
        og = gcol[0] * oc_ref[0, g] + gcol[1] * o_s + gcol[2] * o_w
        o = jnp.where((head >= g * grp) & (head < (g + 1) * grp), og, o)
    o_ref[0] = o * szb_ref[0]


def _nsa_decode2(q3, cache_cmp_t, cache_sel_t, cache_win_t, page_table, sel_new, win_new, gate, szb3, wk, wv):
    bs, _, hd = q3.shape
    l_cmp, kv_h = wk.shape
    grp = H_B // kv_h
    gw = kv_h * hd
    page = cache_cmp_t.shape[-1]
    n_pages = page_table.shape[1]
    w_buf = cache_win_t.shape[-1]
    per_page = page // l_cmp
    pg = LANES // per_page
    assert page == LANES and page % L_SEL == 0 and page % l_cmp == 0 and n_pages % pg == 0
    n_chunks = n_pages // pg
    n_sel = n_pages * page // L_SEL
    n_slot = min(K_SEL, n_sel + 1) - 1
    assert n_slot >= 1
    nc_pad = -(-n_sel // LANES) * LANES
    wrow = jnp.tile(_pool_weights(wk, wv, hd, halves=False), (per_page, 1)).T
    pos = lax.broadcasted_iota(jnp.int32, (pg, page, LANES), 1)
    pj = lax.broadcasted_iota(jnp.int32, (pg, page, LANES), 0)
    ln = lax.broadcasted_iota(jnp.int32, (pg, page, LANES), 2)
    seg = (ln == pj * per_page + pos // l_cmp).astype(BF16).reshape(pg * page, LANES)

    per_b = lambda shp: pl.BlockSpec((1,) + shp, lambda b, p, pt: (b,) + (0,) * len(shp))
    oc, idx = pl.pallas_call(
        functools.partial(_nsa_decode_cmp_kernel, pg=pg, n_pages=n_pages, page=page, l_cmp=l_cmp, hd=hd, grp=grp,
                          kv_h=kv_h, n_slot=n_slot),
        grid_spec=pltpu.PrefetchScalarGridSpec(
            num_scalar_prefetch=1,
            grid=(bs, n_chunks),
            in_specs=([per_b((H_B, hd))]
                      + [pl.BlockSpec((1, 2 * gw, page), (lambda j: lambda b, p, pt: (pt[b, p * pg + j], 0, 0))(j))
                         for j in range(pg)]
                      + [pl.BlockSpec((2 * gw, page), lambda b, p, pt: (0, 0)),
                         pl.BlockSpec(seg.shape, lambda b, p, pt: (0, 0))]),
            out_specs=(per_b((kv_h, H_B, hd)), per_b((kv_h, n_slot, nc_pad))),
            scratch_shapes=[pltpu.VMEM((n_chunks, 2 * gw, LANES), F32)]),
        out_shape=(jax.ShapeDtypeStruct((bs, kv_h, H_B, hd), F32),
                   jax.ShapeDtypeStruct((bs, kv_h, n_slot, nc_pad), jnp.int32)),
        compiler_params=_cparams(2),
        name="nsa_decode_cmp",
    )(page_table, q3, *([cache_cmp_t] * pg), wrow, seg)
    blk_ids = idx[:, :, :, 0]

    sel_per_page = page // L_SEL

    def blk_map(g, j, row_blk):
        return lambda b, pt, ids: (pt[b, jnp.maximum(ids[b, g, j], 0) // sel_per_page], row_blk, 0)

    per_b2 = lambda shp: pl.BlockSpec((1,) + shp, lambda b, pt, ids: (b,) + (0,) * len(shp))
    slots = [(g, j) for g in range(kv_h) for j in range(n_slot)]
    return pl.pallas_call(
        functools.partial(_nsa_decode_sel_kernel, n_slot=n_slot, n_pages=n_pages, page=page, hd=hd, grp=grp,
                          kv_h=kv_h, w_buf=w_buf),
        grid_spec=pltpu.PrefetchScalarGridSpec(
            num_scalar_prefetch=2,
            grid=(bs,),
            in_specs=([per_b2((H_B, hd))]
                      + [pl.BlockSpec((1, hd, page), blk_map(g, j, g)) for g, j in slots]
                      + [pl.BlockSpec((1, hd, page), blk_map(g, j, kv_h + g)) for g, j in slots]
                      + [per_b2((2 * gw, w_buf)), per_b2((1, 2 * gw)), per_b2((1, 2 * gw)),
                         per_b2((1, kv_h * LANES)), per_b2((H_B, hd)), per_b2((kv_h, H_B, hd))]),
            out_specs=per_b2((H_B, hd))),
        out_shape=jax.ShapeDtypeStruct((bs, H_B, hd), F32),
        compiler_params=_cparams(1),
        name="nsa_decode_sel",
    )(page_table, blk_ids, q3, *([cache_sel_t] * (2 * len(slots))), cache_win_t, sel_new, win_new, gate, szb3, oc)


def _out_proj_ln_kernel(*refs, n_in, alpha):
    a_refs, w_refs = refs[:n_in], refs[n_in:2 * n_in]
    x_ref, g_ref, b_ref, o_ref = refs[2 * n_in:]
    y = None
    for a_ref, w_ref in zip(a_refs, w_refs):
        t = jnp.dot(a_ref[...].astype(BF16), w_ref[...], preferred_element_type=F32)
        y = t if y is None else y + t
    o_ref[...] = _layer_norm_rows(alpha * x_ref[...] + y, g_ref[...], b_ref[...], LN_EPS)


def _out_proj_ln(a_list, w_list, x2, ln_g, ln_b, alpha):
    m, d = x2.shape
    tm = min(512, m)
    assert m % tm == 0
    n_in = len(a_list)
    return pl.pallas_call(
        functools.partial(_out_proj_ln_kernel, n_in=n_in, alpha=alpha),
        grid=(m // tm,),
        in_specs=([pl.BlockSpec((tm, a.shape[1]), lambda i: (i, 0)) for a in a_list]
                  + [pl.BlockSpec(w.shape, lambda i: (0, 0)) for w in w_list]
                  + [pl.BlockSpec((tm, d), lambda i: (i, 0)),
                     pl.BlockSpec((1, d), lambda i: (0, 0)), pl.BlockSpec((1, d), lambda i: (0, 0))]),
        out_specs=pl.BlockSpec((tm, d), lambda i: (i, 0)),
        out_shape=jax.ShapeDtypeStruct((m, d), F32),
        compiler_params=_cparams(1),
        name="out_proj_ln",
    )(*a_list, *w_list, x2, ln_g.reshape(1, d), ln_b.reshape(1, d))


def _rwkv_proj_kernel(*refs, seq, tiles_per_seq):
    if seq:
        x_ref, tail_ref, shift_ref = refs[:3]
        rest = refs[3:]
    else:
        x_ref, shift_ref = refs[:2]
        rest = refs[2:]
    (mu_ref, w_ref, w1_ref, w2_ref, a1_ref, a2_ref, w0_ref, a0_ref,
     r_ref, k_ref, v_ref, lw_ref, a_ref, sz_ref) = rest
    x = x_ref[...]
    if seq:
        i = pl.program_id(0)
        first = jnp.where(i % tiles_per_seq == 0, shift_ref[0], tail_ref[SUBLANES - 1:SUBLANES, :])
        row = lax.broadcasted_iota(jnp.int32, x.shape, 0)
        x_prev = jnp.where(row == 0, first, pltpu.roll(x, 1, 0))
    else:
        x_prev = shift_ref[...]
    dx = x_prev - x
    mix = lambda n: (x + dx * mu_ref[n:n + 1, :]).astype(BF16)
    r_ref[...] = jnp.dot(mix(0), w_ref[0], preferred_element_type=F32)
    k_ref[...] = jnp.dot(mix(1), w_ref[1], preferred_element_type=F32)
    v_ref[...] = jnp.dot(mix(2), w_ref[2], preferred_element_type=F32)
    sz_ref[...] = _silu(jnp.dot(mix(3), w_ref[3], preferred_element_type=F32))
    hw = jnp.tanh(jnp.dot(mix(4), w1_ref[...], preferred_element_type=F32)).astype(BF16)
    y = -(w0_ref[...] + jnp.dot(hw, w2_ref[...], preferred_element_type=F32))
    softplus = jnp.maximum(y, 0.0) + jnp.log(1.0 + jnp.exp(-jnp.abs(y)))
    lw_ref[...] = -jnp.exp(-softplus - 0.5)
    ha = jnp.dot(mix(5), a1_ref[...], preferred_element_type=F32).astype(BF16)
    a_ref[...] = _sigmoid(a0_ref[...] + jnp.dot(ha, a2_ref[...], preferred_element_type=F32))


def _rwkv_proj(x2, shift, t_len, mu, w_rkvz, w0, w1, w2, a0, a1, a2):
    m, d = x2.shape
    seq = t_len > 1
    tm = min(512, t_len) if seq else m
    assert m % tm == 0 and (not seq or (t_len % tm == 0 and tm % SUBLANES == 0))
    tiles_per_seq = t_len // tm if seq else 1
    full = lambda a: pl.BlockSpec(a.shape, lambda i: (0,) * a.ndim, pipeline_mode=pl.Buffered(1))
    row = pl.BlockSpec((tm, d), lambda i: (i, 0))
    if seq:
        blk = tm // SUBLANES
        lead = [x2, x2, shift.reshape(-1, 1, d)]
        lead_specs = [row, pl.BlockSpec((SUBLANES, d), lambda i: (jnp.maximum(i * blk - 1, 0), 0)),
                      pl.BlockSpec((1, 1, d), lambda i: (i // tiles_per_seq, 0, 0))]
    else:
        lead = [x2, shift]
        lead_specs = [row, row]
    ws = [mu, w_rkvz.astype(BF16), w1.astype(BF16), w2.astype(BF16), a1.astype(BF16), a2.astype(BF16),
          w0.reshape(1, d), a0.reshape(1, d)]
    return pl.pallas_call(
        functools.partial(_rwkv_proj_kernel, seq=seq, tiles_per_seq=tiles_per_seq),
        grid=(m // tm,),
        in_specs=lead_specs + [full(a) for a in ws],
        out_specs=(row,) * 6,
        out_shape=(jax.ShapeDtypeStruct((m, d), F32),) * 6,
        compiler_params=_cparams(1),
        name="rwkv_proj",
    )(*lead, *ws)


WKV_CHUNK = 64
WKV_PASSES = 1
WKV_SUB = 2


def _wkv_kernel(r_ref, k_ref, v_ref, lw_ref, a_ref, sz_ref, s0_ref, kk_ref, ka_ref, rk_ref, gg_ref, gb_ref,
                yz_ref, sfin_ref, h_s, *, c_len, n_sub, n_real, n_heads, hd):
    c = pl.program_id(1)
    mm = functools.partial(_mm, pa=WKV_PASSES, pb=WKV_PASSES)
    n_pairs = n_heads // 2
    pw = 2 * hd
    d = n_heads * hd
    pairs = range(n_pairs)
    iota = lambda shape, axis: lax.broadcasted_iota(jnp.int32, shape, axis)
    bd_mask = (iota((pw, pw), 0) // hd) == (iota((pw, pw), 1) // hd)
    bd_ones = jnp.where(bd_mask, 1.0, 0.0).astype(BF16)
    eye_mask = iota((pw, pw), 0) == iota((pw, pw), 1)
    eye_pb = jnp.where(eye_mask, 1.0, 0.0).astype(BF16)
    lane_lo = iota((1, pw), 1) < hd

    @pl.when(c == 0)
    def _():
        z = jnp.zeros((hd, hd), F32)
        for p in pairs:
            h_s[p] = jnp.concatenate([jnp.concatenate([s0_ref[0, 2 * p].T, z], axis=1),
                                      jnp.concatenate([z, s0_ref[0, 2 * p + 1].T], axis=1)], axis=0)

    def segsum(x):
        xb = x.astype(BF16)
        return jnp.concatenate([jnp.dot(xb[:, t * pw:(t + 1) * pw], bd_ones, preferred_element_type=F32)
                                for t in range(d // pw)], axis=1)

    r = r_ref[0]
    k = k_ref[0]
    v = v_ref[0]
    lw = lw_ref[0]
    a = a_ref[0]
    tri = jnp.where(iota((c_len, c_len), 0) >= iota((c_len, c_len), 1), 1.0, 0.0).astype(BF16)
    col2 = iota((c_len, 2 * c_len), 1) % c_len
    low_exc2 = iota((c_len, 2 * c_len), 0) > col2
    eye2 = jnp.where(iota((c_len, 2 * c_len), 0) == col2, 1.0, 0.0)
    low_inc4 = iota((c_len, 4 * c_len), 0) >= iota((c_len, 4 * c_len), 1) % c_len
    cc_mask = (iota((2 * c_len, 2 * c_len), 0) // c_len) == (iota((2 * c_len, 2 * c_len), 1) // c_len)

    def by_head(x):
        return jnp.concatenate([jnp.where(lane_lo, x, 0.0), jnp.where(lane_lo, 0.0, x)], axis=0).astype(BF16)
    kk = k * kk_ref[...]
    kap = kk / jnp.maximum(jnp.sqrt(segsum(kk * kk)), 1e-12)
    ka = kap * a
    kmod = k * (1.0 + (a - 1.0) * ka_ref[...])
    rkr = r * kmod * rk_ref[...]
    levels = max((n_real - 1).bit_length() - 1, 0)

    lhs_b, lhs2_t, vb, v_st, l_b, l_k, m_r = [], [], [], [], [], [], []
    for s in range(n_sub):
        rs = slice(s * c_len, (s + 1) * c_len)
        lw_s = lw[rs]
        cum = _mm(tri, lw_s, NN, pb=3)
        cum_end = cum[c_len - 1:c_len, :]
        p_inv = jnp.exp(-cum)
        p_rest = jnp.exp(cum_end - cum)
        p_end = jnp.exp(cum_end)
        lhs_top = kap[rs] * jnp.exp(cum - lw_s)
        r_t = r[rs] * jnp.exp(cum)
        rhs_top = ka[rs] * p_inv
        k_t = kmod[rs] * p_inv
        ke_top = ka[rs] * p_rest
        k_e = kmod[rs] * p_rest
        for p in pairs:
            ls = slice(p * pw, (p + 1) * pw)
            lhs_b.append(jnp.concatenate([lhs_top[:, ls], r_t[:, ls]], axis=0).astype(BF16))
            v_p = v[rs, ls]
            vb.append(v_p.astype(BF16))
            v_st.append(by_head(v_p))
            lhs2 = jnp.concatenate([ke_top[:, ls], k_e[:, ls], jnp.where(eye_mask, p_end[:, ls], 0.0)],
                                   axis=0).astype(BF16)
            lhs2_t.append(lax.dot_general(eye_pb, lhs2, NT, preferred_element_type=F32).astype(BF16))
            rhs_st = jnp.concatenate([by_head(rhs_top[:, ls]), by_head(k_t[:, ls])], axis=0)
            am = mm(lhs_b[-1], rhs_st, NT)
            l_b.append(jnp.where(low_exc2, am[:c_len, :2 * c_len], 0.0))
            l_k.append(jnp.where(low_exc2, am[:c_len, 2 * c_len:], 0.0).astype(BF16))
            m_r.append(jnp.where(low_inc4, am[c_len:], 0.0).astype(BF16))
    n_sp = n_sub * n_pairs
    lkv = [mm(l_k[i], v_st[i], NN) for i in range(n_sp)]
    blockdiag = lambda x: jnp.where(cc_mask, jnp.concatenate([x, x], axis=0), 0.0).astype(BF16)
    t_inv = [eye2 - x for x in l_b]
    pwr = l_b
    for _ in range(levels):
        pwr = [mm(x, blockdiag(x), NN) for x in pwr]
        t_inv = [t + mm(t, blockdiag(x), NN) for t, x in zip(t_inv, pwr)]

    h_cur = [h_s[p] for p in pairs]
    y_rows = []
    for s in range(n_sub):
        idx = [s * n_pairs + p for p in pairs]
        hb = [h_cur[p].astype(BF16) for p in pairs]
        gh = [mm(lhs_b[i], hb[p], NN) for p, i in zip(pairs, idx)]
        u = [-mm(t_inv[i], by_head(gh[p][:c_len] + lkv[i]), NN) for p, i in zip(pairs, idx)]
        uv = [jnp.concatenate([u[p].astype(BF16), vb[i]], axis=0) for p, i in zip(pairs, idx)]
        y = [gh[p][c_len:] + mm(m_r[i], jnp.concatenate([by_head(u[p]), v_st[i]], axis=0), NN)
             for p, i in zip(pairs, idx)]
        h_cur = [jnp.where(bd_mask, mm(lhs2_t[i], jnp.concatenate([uv[p], hb[p]], axis=0), NN), 0.0)
                 for p, i in zip(pairs, idx)]
        y_rows.append(jnp.concatenate(y, axis=1))
    for p in pairs:
        h_s[p] = h_cur[p]

    y_all = jnp.concatenate(y_rows, axis=0) if n_sub > 1 else y_rows[0]
    dy = y_all - segsum(y_all) * (1.0 / hd)
    var = segsum(dy * dy) * (1.0 / hd)
    yn = dy * lax.rsqrt(var + GN_EPS) * gg_ref[...] + gb_ref[...]
    yz_ref[0] = ((yn + segsum(rkr) * v) * sz_ref[0]).astype(yz_ref.dtype)

    @pl.when(c == pl.num_programs(1) - 1)
    def _():
        for p in pairs:
            sfin_ref[0, 2 * p] = h_s[p, :hd, :hd].T
            sfin_ref[0, 2 * p + 1] = h_s[p, hd:, hd:].T


def _wkv_scan(r3, k3, v3, lw3, a3, sz3, s0, k_k, k_a, r_k, gn_g, gn_b, c_len, t_real):
    bsz, t, d = r3.shape
    n_heads = d // HD_C
    n_sub = WKV_SUB if t % (WKV_SUB * c_len) == 0 else 1
    rows = n_sub * c_len
    assert t % rows == 0 and n_heads % 2 == 0 and 2 * HD_C == LANES
    vec = lambda x: x.reshape(1, d)
    seq = pl.BlockSpec((1, rows, d), lambda b, c: (b, c, 0))
    st = pl.BlockSpec((1, n_heads, HD_C, HD_C), lambda b, c: (b, 0, 0, 0))
    par = pl.BlockSpec((1, d), lambda b, c: (0, 0))
    return pl.pallas_call(
        functools.partial(_wkv_kernel, c_len=c_len, n_sub=n_sub, n_real=min(t_real, c_len), n_heads=n_heads,
                          hd=HD_C),
        grid=(bsz, t // rows),
        in_specs=[seq] * 6 + [st] + [par] * 5,
        out_specs=(seq, st),
        out_shape=(jax.ShapeDtypeStruct((bsz, t, d), BF16),
                   jax.ShapeDtypeStruct((bsz, n_heads, HD_C, HD_C), F32)),
        scratch_shapes=[pltpu.VMEM((n_heads // 2, LANES, LANES), F32)],
        compiler_params=_cparams(2),
        name="wkv_scan",
    )(r3, k3, v3, lw3, a3, sz3, s0, vec(k_k), vec(k_a), vec(r_k), vec(gn_g), vec(gn_b))


def _even_layer(xp, xs, cache_cmp, cache_sel, cache_win, state_conv, page_table,
                w_in, conv_w, conv_b, cln_g, cln_b, wk, wv, w_out, ln_g, ln_b, alpha):
    bp, tp, d = xp.shape
    bs, ts, _ = xs.shape
    assert ts == 1
    d_a = conv_w.shape[-1]
    n_taps = conv_w.shape[0]
    l_cmp, kv_h = wk.shape
    hd = cache_cmp.shape[-1]
    d_b = H_B * hd
    d_kv = 2 * kv_h * hd
    w_bf = _prep_w_in(w_in, d_a, d_b, kv_h, hd)
    cw = conv_w.reshape(n_taps, d_a)
    w_out_bf = w_out.astype(BF16)
    kv_shape = lambda b, t: (b, t, 2, kv_h, hd)

    proj = _even_in_proj(xp.reshape(bp * tp, d), w_bf, d_a, d_b, d_kv, kv_h, hd, tp)
    u, sza, q, cmp_n, sel_n, win_n, szb, gate = proj[:8]
    if len(proj) > 8:
        new_kv = [a.reshape(bp, 2, kv_h, hd, tp).transpose(0, 4, 1, 2, 3) for a in proj[8:]]
    else:
        new_kv = [a.reshape(bp, tp, 2, kv_h, hd) for a in (cmp_n, sel_n, win_n)]
    r3 = lambda a: a.reshape(bp, tp, a.shape[-1])
    u3 = r3(u)
    ya = _conv_branch(jnp.zeros((bp, CONV_HIST, d_a), F32), u3, r3(sza), cw, conv_b, cln_g, cln_b)
    yb = _nsa_prompt(r3(q), r3(cmp_n), r3(sel_n), r3(win_n), r3(gate), r3(szb), wk, wv)
    yp = _out_proj_ln([ya.reshape(bp * tp, d_a), yb.reshape(bp * tp, d_b)], [w_out_bf[:d_a], w_out_bf[d_a:]],
                      xp.reshape(bp * tp, d), ln_g, ln_b, alpha).reshape(bp, tp, d)
    w_keep = min(WINDOW, tp)
    outs_p = (new_kv[0], new_kv[1], new_kv[2][:, tp - w_keep:], u3[:, tp - (n_taps - 1):])

    u, sza, q, cmp_s, sel_s, win_s, szb, gate = _even_in_proj(xs.reshape(bs, d), w_bf, d_a, d_b, d_kv, kv_h, hd, 1)
    ext = jnp.concatenate([state_conv, u[:, None, :]], axis=1)
    hist = jnp.pad(state_conv, ((0, 0), (CONV_HIST - (n_taps - 1), 0), (0, 0)))
    pad_rows = lambda a: jnp.pad(a[:, None, :], ((0, 0), (0, SUBLANES - 1), (0, 0)))
    ya = _conv_branch(hist, pad_rows(u), pad_rows(sza), cw, conv_b, cln_g, cln_b)[:, 0]
    n_pool, page = cache_cmp.shape[:2]
    w_buf = cache_win.shape[1]
    fm = lambda c: jnp.transpose(c, (0, 2, 3, 4, 1)).reshape(c.shape[0], d_kv, c.shape[1])
    yb = _nsa_decode2(q.astype(F32).reshape(bs, H_B, hd), fm(cache_cmp), fm(cache_sel), fm(cache_win), page_table,
                      sel_s[:, None, :], win_s[:, None, :], gate[:, None, :], szb.reshape(bs, H_B, hd), wk, wv)
    ys = _out_proj_ln([ya, yb.reshape(bs, d_b)], [w_out_bf[:d_a], w_out_bf[d_a:]], xs.reshape(bs, d),
                      ln_g, ln_b, alpha).reshape(bs, 1, d)
    ctx = jnp.concatenate([cache_win, win_s.reshape(kv_shape(bs, 1))], axis=1)
    outs_s = (cmp_s.reshape(kv_shape(bs, 1)), sel_s.reshape(kv_shape(bs, 1)),
              ctx[:, ctx.shape[1] - min(WINDOW, ctx.shape[1]):], ext[:, 1:])
    return yp, ys, outs_p, outs_s


def _odd_group(x3, shift, s0, mu, w_rkvz, w0, w1, w2, a0, a1, a2, k_k, k_a, r_k, gn_g, gn_b, w_out_bf,
               ln_g, ln_b, alpha):
    bsz, t, d = x3.shape
    x2 = x3.reshape(bsz * t, d)
    r, k, v, lw, a, sz = _rwkv_proj(x2, shift, t, mu, w_rkvz, w0, w1, w2, a0, a1, a2)
    c_len = WKV_CHUNK if t >= WKV_CHUNK else -(-t // SUBLANES) * SUBLANES
    t_pad = -(-t // c_len) * c_len
    r3 = lambda z: jnp.pad(z.reshape(bsz, t, d), ((0, 0), (0, t_pad - t), (0, 0)))
    yz, s_fin = _wkv_scan(r3(r), r3(k), r3(v), r3(lw), r3(a), r3(sz), s0, k_k, k_a, r_k.reshape(-1), gn_g, gn_b,
                          c_len, t)
    y = _out_proj_ln([yz[:, :t].reshape(bsz * t, d)], [w_out_bf], x2, ln_g, ln_b, alpha).reshape(bsz, t, d)
    return y, s_fin, x3[:, -1]


def kernel(x_prompt, x_sample, cache_cmp_kv, cache_sel_kv, cache_win_kv, state_conv, state_wkv, state_shift,
           page_table, w_in_even, conv_w, conv_b, conv_ln_g, conv_ln_b, wk_cmp, wv_cmp, w_out_even, mu_c, w_rkvz,
           w0, w1, w2, a0, a1, a2, k_k, k_a, r_k, gn_g, gn_b, w_out_odd, ln_g, ln_b):
    depth = ln_g.shape[0]
    alpha = (2 * depth) ** 0.25
    bp, _, d = x_prompt.shape
    n_heads = d // HD_C
    xp, xs = x_prompt, x_sample
    even_p, even_s, odd_p, odd_s = [], [], [], []
    for l in range(depth):
        if l % 2 == 0:
            e = l // 2
            xp, xs, o_p, o_s = _even_layer(
                xp, xs, cache_cmp_kv[e], cache_sel_kv[e], cache_win_kv[e], state_conv[e], page_table,
                w_in_even[e], conv_w[e], conv_b[e], conv_ln_g[e], conv_ln_b[e], wk_cmp[e], wv_cmp[e],
                w_out_even[e], ln_g[l], ln_b[l], alpha)
            even_p.append(o_p)
            even_s.append(o_s)
        else:
            o = l // 2
            po = (mu_c[o], w_rkvz[o], w0[o], w1[o], w2[o], a0[o], a1[o], a2[o], k_k[o], k_a[o], r_k[o],
                  gn_g[o], gn_b[o], w_out_odd[o].astype(BF16), ln_g[l], ln_b[l], alpha)
            xp, s_p, h_p = _odd_group(xp, jnp.zeros((bp, d), F32), jnp.zeros((bp, n_heads, HD_C, HD_C), F32), *po)
            xs, s_s, h_s = _odd_group(xs, state_shift[o], state_wkv[o], *po)
            odd_p.append((s_p, h_p))
            odd_s.append((s_s, h_s))
    stack = lambda items, j: jnp.stack([it[j] for it in items])
    return (xp, xs,
            stack(even_p, 0), stack(even_s, 0), stack(even_p, 1), stack(even_s, 1),
            stack(even_p, 2), stack(even_s, 2), stack(even_p, 3), stack(even_s, 3),
            stack(odd_p, 0), stack(odd_s, 0), stack(odd_p, 1), stack(odd_s, 1))
```

```python
import functools

import jax
import jax.numpy as jnp
from jax import lax
from jax.experimental import pallas as pl
from jax.experimental.pallas import tpu as pltpu

F32 = jnp.float32
BF16 = jnp.bfloat16

H_B = 8
L_SEL = 64
K_SEL = 16
WINDOW = 512
FORCE = 1e4
HD_C = 64
GN_EPS = 64e-5
LN_EPS = 1e-5
NEG = -1e30

LANES = 128
SUBLANES = 8
VMEM_LIMIT = 56 * 1024 * 1024

NT = (((1,), (1,)), ((), ()))
TN = (((0,), (0,)), ((), ()))
NN = (((1,), (0,)), ((), ()))


def _cparams(n_axes):
    return pltpu.CompilerParams(dimension_semantics=("arbitrary",) * n_axes,
                                vmem_limit_bytes=VMEM_LIMIT)


def _sigmoid(x):
    return 1.0 / (1.0 + jnp.exp(-x))


def _silu(x):
    return x * _sigmoid(x)


def _split_bf16(x, n):
    parts, rem = [], x
    for i in range(n):
        p = rem.astype(BF16)
        parts.append(p)
        if i + 1 < n:
            rem = rem - p.astype(F32)
    return parts


def _mm(a, b, dims=NN, pa=1, pb=1):
    aa = [a] if a.dtype == BF16 else _split_bf16(a, pa)
    bb = [b] if b.dtype == BF16 else _split_bf16(b, pb)
    keep = max(len(aa), len(bb))
    out = None
    for i, ai in enumerate(aa):
        for j, bj in enumerate(bb):
            if i + j < keep:
                t = lax.dot_general(ai, bj, dims, preferred_element_type=F32)
                out = t if out is None else out + t
    return out


def _layer_norm_rows(h, g, b, eps):
    mu = jnp.mean(h, axis=-1, keepdims=True)
    d = h - mu
    var = jnp.mean(d * d, axis=-1, keepdims=True)
    return d * lax.rsqrt(var + eps) * g + b


def _even_in_proj_kernel(x_ref, w_ref, u_ref, sza_ref, q_ref, cmp_ref, sel_ref, win_ref, szb_ref, gate_ref,
                         *t_refs, d_a, d_b, d_kv, q_scale):
    xb = x_ref[...].astype(BF16)

    def seg(lo, n):
        return jnp.dot(xb, w_ref[:, lo:lo + n], preferred_element_type=F32)

    o = 0
    a_val = seg(o, d_a); o += d_a
    a_glu = seg(o, d_a); o += d_a
    u_ref[...] = a_val * _sigmoid(a_glu)
    sza_ref[...] = _silu(seg(o, d_a)); o += d_a
    q_ref[...] = (seg(o, d_b) * q_scale).astype(BF16); o += d_b
    for j, ref in enumerate((cmp_ref, sel_ref, win_ref)):
        kv = seg(o, d_kv); o += d_kv
        ref[...] = kv
        if t_refs:
            t_refs[j][0] = kv.T
    szb_ref[...] = _silu(seg(o, d_b)); o += d_b
    gate_ref[...] = _sigmoid(seg(o, gate_ref.shape[-1]))


def _prep_w_in(w, d_a, d_b, kv_h, hd):
    grp = H_B // kv_h
    c_kv6 = 3 * d_a + d_b
    c_g3 = c_kv6 + 6 * kv_h * hd
    c_zb = c_g3 + 3 * H_B
    gate_blocks = []
    for g in range(kv_h):
        blk = w[:, c_g3 + g * grp * 3: c_g3 + (g + 1) * grp * 3]
        gate_blocks.append(jnp.pad(blk, ((0, 0), (0, LANES - grp * 3))))
    wn = jnp.concatenate([w[:, :c_g3], w[:, c_zb:c_zb + d_b]] + gate_blocks, axis=1)
    return wn.astype(BF16)


def _even_in_proj(x2, w_bf, d_a, d_b, d_kv, kv_h, hd, t_len):
    m, d = x2.shape
    tm = min(512, m)
    assert m % tm == 0
    n_gate = kv_h * LANES
    row = lambda n: pl.BlockSpec((tm, n), lambda i: (i, 0))
    out_shape = (jax.ShapeDtypeStruct((m, d_a), F32), jax.ShapeDtypeStruct((m, d_a), F32),
                 jax.ShapeDtypeStruct((m, d_b), BF16),
                 jax.ShapeDtypeStruct((m, d_kv), F32), jax.ShapeDtypeStruct((m, d_kv), F32),
                 jax.ShapeDtypeStruct((m, d_kv), F32),
                 jax.ShapeDtypeStruct((m, d_b), F32), jax.ShapeDtypeStruct((m, n_gate), F32))
    out_specs = (row(d_a), row(d_a), row(d_b), row(d_kv), row(d_kv), row(d_kv), row(d_b), row(n_gate))
    if t_len % tm == 0 and tm % LANES == 0:
        per_seq = t_len // tm
        t_spec = pl.BlockSpec((1, d_kv, tm), lambda i: (i // per_seq, 0, i % per_seq))
        out_shape += (jax.ShapeDtypeStruct((m // t_len, d_kv, t_len), F32),) * 3
        out_specs += (t_spec,) * 3
    return pl.pallas_call(
        functools.partial(_even_in_proj_kernel, d_a=d_a, d_b=d_b, d_kv=d_kv, q_scale=hd ** -0.5),
        grid=(m // tm,),
        in_specs=[row(d), pl.BlockSpec(w_bf.shape, lambda i: (0, 0), pipeline_mode=pl.Buffered(1))],
        out_specs=out_specs,
        out_shape=out_shape,
        compiler_params=_cparams(1),
        name="even_in_proj",
    )(x2, w_bf)


CONV_HIST = 32


def _conv_kernel(hist_ref, u_ref, sza_ref, w_ref, cb_ref, g_ref, b_ref, ya_ref, win_ref, sh_ref, conv_ref,
                 *, tq, rb, n_taps):
    i = pl.program_id(1)
    base = pl.multiple_of(i * tq, SUBLANES)
    win_ref[CONV_HIST:, :] = u_ref[0, pl.ds(base, tq), :]

    @pl.when(i == 0)
    def _():
        win_ref[:CONV_HIST, :] = hist_ref[0]

    @pl.when(i > 0)
    def _():
        win_ref[:CONV_HIST, :] = u_ref[0, pl.ds(base - CONV_HIST, CONV_HIST), :]

    w = win_ref[...]
    n = tq + CONV_HIST
    sh_ref[0] = w
    for s in range(1, SUBLANES):
        sh_ref[s] = pltpu.roll(w, n - s, 0)

    first = CONV_HIST - (n_taps - 1)
    cb = cb_ref[...]
    g = g_ref[...]
    b = b_ref[...]

    def block(k, carry):
        r0 = pl.multiple_of(k * rb, SUBLANES)
        acc = jnp.zeros((rb, u_ref.shape[-1]), F32) + cb
        for j in range(n_taps):
            a, s = divmod(first + j, SUBLANES)
            acc = acc + sh_ref[s, pl.ds(r0 + SUBLANES * a, rb), :] * w_ref[j:j + 1, :]
        conv_ref[pl.ds(r0, rb), :] = acc
        return carry

    lax.fori_loop(0, tq // rb, block, 0)
    y = _silu(_layer_norm_rows(conv_ref[...], g, b, LN_EPS)) * sza_ref[0]
    ya_ref[0] = y.astype(ya_ref.dtype)


def _conv_branch(hist, u3, sza3, conv_w, conv_b, ln_g, ln_b):
    bsz, t, d = u3.shape
    n_taps = conv_w.shape[0]
    assert n_taps - 1 <= CONV_HIST and t % SUBLANES == 0
    tq = min(256, t)
    rb = min(32, tq)
    assert t % tq == 0 and tq % rb == 0
    vec = lambda a: a.reshape(1, d)
    full = lambda shp: pl.BlockSpec(shp, lambda b, i: (0,) * len(shp))
    return pl.pallas_call(
        functools.partial(_conv_kernel, tq=tq, rb=rb, n_taps=n_taps),
        grid=(bsz, t // tq),
        in_specs=[pl.BlockSpec((1, CONV_HIST, d), lambda b, i: (b, 0, 0)),
                  pl.BlockSpec((1, t, d), lambda b, i: (b, 0, 0)),
                  pl.BlockSpec((1, tq, d), lambda b, i: (b, i, 0)),
                  full((n_taps, d)), full((1, d)), full((1, d)), full((1, d))],
        out_specs=pl.BlockSpec((1, tq, d), lambda b, i: (b, i, 0)),
        out_shape=jax.ShapeDtypeStruct((bsz, t, d), BF16),
        scratch_shapes=[pltpu.VMEM((tq + CONV_HIST, d), F32),
                        pltpu.VMEM((SUBLANES, tq + CONV_HIST, d), F32), pltpu.VMEM((tq, d), F32)],
        compiler_params=_cparams(2),
        name="conv_branch",
    )(hist, u3, sza3, conv_w, vec(conv_b), vec(ln_g), vec(ln_b))


SEL_OFF = -(2.0 ** 100)


def _nsa_prompt_kernel(q_ref, kcmp_ref, ksel_ref, kwin_ref, gate_ref, szb_ref, wpool_ref, o_ref,
                       kc_s, vc_s, ks_s, vs_s, kw_s, vw_s, m_s, l_s, acc_s, s_scr,
                       *, tq, t_len, l_cmp, hd, grp, kv_h):
    i = pl.program_id(1)
    g = pl.program_id(2)
    n_cmp = t_len // l_cmp
    n_sel = t_len // L_SEL
    half = n_cmp // 2
    gw = kv_h * hd

    @pl.when((i == 0) & (g == 0))
    def _prep():
        x3 = kcmp_ref[0].reshape(half, 2 * l_cmp, 2 * gw)
        pooled = jnp.concatenate([jnp.sum(x3 * wpool_ref[0][None], axis=1),
                                  jnp.sum(x3 * wpool_ref[1][None], axis=1)], axis=0)
        pos = lax.broadcasted_iota(jnp.int32, (t_len, LANES), 0)
        lane = lax.broadcasted_iota(jnp.int32, (t_len, LANES), 1)
        blk = pos // L_SEL
        feat = jnp.where(lane == blk, 1.0, 0.0)
        feat = jnp.where(lane == n_sel, blk.astype(F32), feat)
        feat = jnp.where(lane == n_sel + 1, (pos - blk * L_SEL).astype(F32), feat)
        for src, dst in ((ksel_ref, ks_s), (kwin_ref, kw_s)):
            k01 = src[0, :, :gw]
            dst[0] = jnp.where(lane < hd, feat, pltpu.roll(k01, hd, 1)).astype(BF16)
            dst[1] = jnp.where(lane < hd, feat, k01).astype(BF16)
        vc_pad = jnp.concatenate([pooled[:, gw:], jnp.zeros((LANES - n_cmp, gw), F32)], axis=0) \
            if n_cmp < LANES else pooled[:, gw:]
        vc_t = vc_pad.T
        vs_t = ksel_ref[0, :, gw:].T
        vw_t = kwin_ref[0, :, gw:].T
        for gg in range(kv_h):
            kc_s[gg] = pooled[:, gg * hd:(gg + 1) * hd].astype(BF16)
            vc_s[gg] = vc_t[gg * hd:(gg + 1) * hd, :n_cmp].astype(BF16)
            vs_s[gg] = vs_t[gg * hd:(gg + 1) * hd].astype(BF16)
            vw_s[gg] = vw_t[gg * hd:(gg + 1) * hd].astype(BF16)

    t0 = i * tq
    gs = jnp.float32(1.0)
    for gg in range(1, kv_h):
        gs = jnp.where(g == gg, 2.0 ** -(gg * grp), gs)
    slopes = [gs * (2.0 ** -(r + 1)) for r in range(grp)]
    qt = q_ref[0]
    qs = jnp.concatenate([qt[:, r * hd:(r + 1) * hd] for r in range(grp)], axis=0)

    s_t = lax.dot_general(kc_s[g], qs, NT, preferred_element_type=F32)
    n_io = lax.broadcasted_iota(jnp.int32, (n_cmp, tq), 0)
    t_io = lax.broadcasted_iota(jnp.int32, (n_cmp, tq), 1) + t0
    c_end = jnp.where(n_io < half, 2 * l_cmp * n_io + (l_cmp - 1), 2 * l_cmp * (n_io - half) + (2 * l_cmp - 1))
    dist_c = t_io - c_end
    valid_c = dist_c >= 0
    dist_cf = dist_c.astype(F32)
    imp = jnp.zeros((n_sel, tq), F32)
    o_c = []
    for r in range(grp):
        s = jnp.where(valid_c, s_t[:, r * tq:(r + 1) * tq] - slopes[r] * dist_cf, NEG)
        e = jnp.exp(s - jnp.max(s, axis=0, keepdims=True))
        p = jnp.where(valid_c, e / jnp.sum(e, axis=0, keepdims=True), 0.0)
        imp = imp + (p[:half] + p[half:])
        o_c.append(jnp.dot(vc_s[g], p.astype(BF16), preferred_element_type=F32))

    sb = lax.broadcasted_iota(jnp.int32, (n_sel, tq), 0)
    tb = (lax.broadcasted_iota(jnp.int32, (n_sel, tq), 1) + t0) // L_SEL
    forced = (sb == 0) | (sb == tb) | (sb == tb - 1)
    imp = jnp.where(forced, FORCE, imp)
    imp = jnp.where(sb <= tb, imp, -jnp.inf)
    rank = jnp.zeros((n_sel, tq), F32)
    for j in range(n_sel):
        vj = imp[j:j + 1, :]
        beats = jnp.where(imp > vj, 1.0, jnp.where((imp == vj) & (sb < j), 1.0, 0.0))
        rank = jnp.where(sb == j, jnp.sum(beats, axis=0, keepdims=True), rank)
    off_t = jnp.where(rank < min(K_SEL, n_sel), 0.0, SEL_OFF)
    feat_sel = jnp.concatenate([off_t, jnp.zeros((LANES - n_sel, tq), F32)], axis=0).T

    qf = q_ref[0].astype(F32)
    lane_q = lax.broadcasted_iota(jnp.int32, (tq, LANES), 1)

    def q_aug(r, feat):
        tile = qf[:, (r // 2) * LANES:(r // 2 + 1) * LANES]
        if r % 2 == 0:
            tile = pltpu.roll(tile, hd, 1)
        x = jnp.where(lane_q < hd, feat, tile)
        x = jnp.where(lane_q == n_sel, slopes[r] * L_SEL, x)
        return jnp.where(lane_q == n_sel + 1, slopes[r], x).astype(BF16)

    d0 = (lax.broadcasted_iota(jnp.int32, (tq, tq), 1) - lax.broadcasted_iota(jnp.int32, (tq, tq), 0))

    def reset():
        m_s[...] = jnp.full(m_s.shape, NEG, F32)
        l_s[...] = jnp.zeros(l_s.shape, F32)
        acc_s[...] = jnp.zeros(acc_s.shape, F32)

    heads = range(grp)

    def scores(q_list, k_s, kt):
        k0 = pl.multiple_of(kt * tq, tq)
        kk = k_s[g, pl.ds(k0, tq), :]
        return [lax.dot_general(kk, q_list[r], NT, preferred_element_type=F32) for r in heads]

    def update(s, v_s, kt, mask):
        vt = v_s[g, :, pl.ds(pl.multiple_of(kt * tq, tq), tq)]
        if mask is not None:
            s = [jnp.where(mask, s[r], NEG) for r in heads]
        m_old = [m_s[r] for r in heads]
        m_new = [jnp.maximum(m_old[r], jnp.max(s[r], axis=0, keepdims=True)) for r in heads]
        al = [jnp.exp(m_old[r] - m_new[r]) for r in heads]
        p = [jnp.exp(s[r] - m_new[r]) for r in heads]
        for r in heads:
            m_s[r] = m_new[r]
            l_s[r] = al[r] * l_s[r] + jnp.sum(p[r], axis=0, keepdims=True)
            acc_s[r] = al[r] * acc_s[r] + jnp.dot(vt, p[r].astype(BF16), preferred_element_type=F32)

    def attend(q_list, k_s, v_s, lo, interior_mask, diag_mask):
        reset()

        def put(slot, kt):
            s = scores(q_list, k_s, kt)
            for r in heads:
                s_scr[slot, r] = s[r]

        def take(slot):
            return [s_scr[slot, r] for r in heads]

        odd = (i - lo) % 2

        @pl.when(odd == 1)
        def _():
            update(scores(q_list, k_s, lo), v_s, lo, interior_mask(lo))

        lo2 = lo + odd
        put(0, lo2)

        def body(j, carry):
            kt = lo2 + 2 * j
            put(1, kt + 1)
            update(take(0), v_s, kt, interior_mask(kt))
            put(0, kt + 2)
            update(take(1), v_s, kt + 1, interior_mask(kt + 1))
            return carry

        lax.fori_loop(0, (i - lo2) // 2, body, 0)
        update(take(0), v_s, i, diag_mask)
        return [acc_s[r] / l_s[r] for r in heads]

    o_s = attend([q_aug(r, feat_sel) for r in heads], ks_s, vs_s, 0, lambda kt: None, d0 >= 0)

    w_tiles = -(-(WINDOW - 1) // tq)
    o_w = attend([q_aug(r, 0.0) for r in heads], kw_s, vw_s, jnp.maximum(i - w_tiles, 0),
                 lambda kt: d0 + (i - kt) * tq < WINDOW, (d0 >= 0) & (d0 < WINDOW))

    gate_t = gate_ref[0].T
    outs = []
    for r in range(grp):
        outs.append(gate_t[3 * r:3 * r + 1] * o_c[r]
                    + gate_t[3 * r + 1:3 * r + 2] * o_s[r]
                    + gate_t[3 * r + 2:3 * r + 3] * o_w[r])
    o_ref[0] = (jnp.concatenate(outs, axis=0).T * szb_ref[0]).astype(o_ref.dtype)


def _pool_weights(wk, wv, hd, halves):
    l_cmp, kv_h = wk.shape
    row = jnp.concatenate([jnp.repeat(wk, hd, axis=1), jnp.repeat(wv, hd, axis=1)], axis=1)
    if not halves:
        return row
    z = jnp.zeros_like(row)
    return jnp.stack([jnp.concatenate([row, z], axis=0), jnp.concatenate([z, row], axis=0)])


def _nsa_prompt(q3, cmp3, sel3, win3, gate3, szb3, wk, wv):
    bsz, t, d_b = q3.shape
    l_cmp, kv_h = wk.shape
    grp = H_B // kv_h
    hd = d_b // H_B
    gw = kv_h * hd
    tq = min(256, t)
    assert t % tq == 0 and t % (2 * l_cmp) == 0 and L_SEL == 2 * l_cmp and tq % L_SEL == 0
    n_cmp = t // l_cmp
    n_sel = t // L_SEL
    assert gw == LANES and 2 * hd == LANES and n_sel + 2 <= hd and n_sel % SUBLANES == 0 and n_cmp <= LANES
    wpool = _pool_weights(wk, wv, hd, halves=True)
    kv_spec = pl.BlockSpec((1, t, 2 * gw), lambda b, i, g: (b, 0, 0))
    return pl.pallas_call(
        functools.partial(_nsa_prompt_kernel, tq=tq, t_len=t, l_cmp=l_cmp, hd=hd, grp=grp, kv_h=kv_h),
        grid=(bsz, t // tq, kv_h),
        in_specs=[pl.BlockSpec((1, tq, grp * hd), lambda b, i, g: (b, i, g)),
                  kv_spec, kv_spec, kv_spec,
                  pl.BlockSpec((1, tq, LANES), lambda b, i, g: (b, i, g)),
                  pl.BlockSpec((1, tq, grp * hd), lambda b, i, g: (b, i, g)),
                  pl.BlockSpec(wpool.shape, lambda b, i, g: (0, 0, 0))],
        out_specs=pl.BlockSpec((1, tq, grp * hd), lambda b, i, g: (b, i, g)),
        out_shape=jax.ShapeDtypeStruct((bsz, t, d_b), BF16),
        scratch_shapes=[pltpu.VMEM((kv_h, n_cmp, hd), BF16), pltpu.VMEM((kv_h, hd, n_cmp), BF16),
                        pltpu.VMEM((kv_h, t, LANES), BF16), pltpu.VMEM((kv_h, hd, t), BF16),
                        pltpu.VMEM((kv_h, t, LANES), BF16), pltpu.VMEM((kv_h, hd, t), BF16),
                        pltpu.VMEM((grp, 1, tq), F32), pltpu.VMEM((grp, 1, tq), F32),
                        pltpu.VMEM((grp, hd, tq), F32), pltpu.VMEM((2, grp, tq, tq), F32)],
        compiler_params=_cparams(3),
        name="nsa_prompt",
    )(q3, cmp3, sel3, win3, gate3, szb3, wpool)


def _decode_slopes():
    head = lax.broadcasted_iota(jnp.int32, (H_B, 1), 0)
    slope = jnp.zeros((H_B, 1), F32)
    for hh in range(H_B):
        slope = jnp.where(head == hh, 2.0 ** -(hh + 1), slope)
    return head, slope


def _nsa_decode_cmp_kernel(pt_ref, q_ref, *refs, pg, n_pages, page, l_cmp, hd, grp, kv_h, n_slot):
    del pt_ref
    cmp_refs = refs[:pg]
    wrow_ref, seg_ref, oc_ref, idx_ref, kcv_s = refs[pg:]
    p = pl.program_id(1)
    n_chunks = n_pages // pg
    past = n_pages * page
    n_cmp = past // l_cmp
    n_sel = past // L_SEL
    gw = kv_h * hd
    nc_pad = idx_ref.shape[-1]
    prod = jnp.concatenate([cmp_refs[j][0] * wrow_ref[...] for j in range(pg)], axis=1)
    kcv_s[p] = _mm(prod, seg_ref[...], NN)

    @pl.when(p == n_chunks - 1)
    def _compressed():
        head, slope = _decode_slopes()
        q = q_ref[0].astype(BF16)
        n_io = lax.broadcasted_iota(jnp.int32, (1, n_cmp), 1)
        dist = (past - (l_cmp * n_io + (l_cmp - 1))).astype(F32)
        pair = jnp.where(lax.broadcasted_iota(jnp.int32, (n_cmp, nc_pad), 0) // (L_SEL // l_cmp)
                         == lax.broadcasted_iota(jnp.int32, (n_cmp, nc_pad), 1), 1.0, 0.0).astype(BF16)
        c_io = lax.broadcasted_iota(jnp.int32, (1, nc_pad), 1)
        i_r = lax.broadcasted_iota(jnp.int32, (nc_pad, nc_pad), 1)
        i_c = lax.broadcasted_iota(jnp.int32, (nc_pad, nc_pad), 0)
        before = jnp.where(i_c < i_r, 1.0, 0.0).astype(BF16)
        slot = lax.broadcasted_iota(jnp.int32, (n_slot, nc_pad), 0).astype(F32)
        for g in range(kv_h):
            kc_t = jnp.concatenate([kcv_s[c, g * hd:(g + 1) * hd, :] for c in range(n_chunks)], axis=1)
            vc_t = jnp.concatenate([kcv_s[c, gw + g * hd: gw + (g + 1) * hd, :] for c in range(n_chunks)], axis=1)
            s = jnp.dot(q, kc_t.astype(BF16), preferred_element_type=F32) - slope * dist
            e = jnp.exp(s - jnp.max(s, axis=-1, keepdims=True))
            pc = e / jnp.sum(e, axis=-1, keepdims=True)
            oc_ref[0, g] = lax.dot_general(pc.astype(BF16), vc_t.astype(BF16), NT, preferred_element_type=F32)
            in_grp = (head >= g * grp) & (head < (g + 1) * grp)
            imp = _mm(jnp.sum(jnp.where(in_grp, pc, 0.0), axis=0, keepdims=True), pair, NN, pa=3)
            forced = (c_io == 0) | (c_io == n_sel - 1)
            imp = jnp.where(forced, FORCE, imp)
            imp = jnp.where(c_io < n_sel, imp, -jnp.inf)
            v_r = jnp.broadcast_to(imp, (nc_pad, nc_pad))
            v_c = v_r.T
            beats = jnp.where(v_c > v_r, 1.0, jnp.where((v_c == v_r) & (i_c < i_r), 1.0, 0.0))
            rank = jnp.sum(beats, axis=0, keepdims=True) + jnp.where(imp < FORCE, 1.0, 0.0)
            chosen = jnp.where((rank < min(K_SEL, n_sel + 1)) & (c_io < n_sel), 1.0, 0.0)
            n_before = jnp.dot(jnp.broadcast_to(chosen, (SUBLANES, nc_pad)).astype(BF16), before,
                               preferred_element_type=F32)[:1]
            hit = jnp.where((n_before == slot) & (chosen > 0.5), 1.0, 0.0)
            ids = jnp.sum(hit * c_io.astype(F32), axis=-1, keepdims=True)
            ids = jnp.where(jnp.sum(hit, axis=-1, keepdims=True) > 0.5, ids, -1.0)
            idx_ref[0, g] = jnp.broadcast_to(ids, (n_slot, nc_pad)).astype(jnp.int32)


def _nsa_decode_sel_kernel(pt_ref, idx_ref, q_ref, *refs, n_slot, n_pages, page, hd, grp, kv_h, w_buf):
    del pt_ref
    n_blk = kv_h * n_slot
    k_refs = refs[:n_blk]
    v_refs = refs[n_blk:2 * n_blk]
    kwin_ref, seln_ref, winn_ref, gate_ref, szb_ref, oc_ref, o_ref = refs[2 * n_blk:]
    b = pl.program_id(0)
    past = n_pages * page
    gw = kv_h * hd
    per_page = page // L_SEL
    head, slope = _decode_slopes()
    qf = q_ref[0]
    q = qf.astype(BF16)
    gate = gate_ref[0]
    lane = lax.broadcasted_iota(jnp.int32, (1, page), 1)
    i_io = lax.broadcasted_iota(jnp.int32, (1, w_buf), 1)
    dist_w = w_buf - i_io
    valid_w = dist_w < WINDOW
    o = jnp.zeros((H_B, hd), F32)
    for g in range(kv_h):
        dists, valids = [], []
        for j in range(n_slot):
            blk = idx_ref[b, g, j]
            safe = jnp.maximum(blk, 0)
            dists.append(past - ((safe // per_page) * page + lane))
            valids.append(lane // L_SEL == jnp.where(blk >= 0, safe % per_page, -1))
        dist = jnp.concatenate(dists, axis=1).astype(F32)
        valid = jnp.concatenate(valids, axis=1)
        k_t = jnp.concatenate([k_refs[g * n_slot + j][0] for j in range(n_slot)], axis=1).astype(BF16)
        v_t = jnp.concatenate([v_refs[g * n_slot + j][0] for j in range(n_slot)], axis=1).astype(BF16)
        s = jnp.where(valid, jnp.dot(q, k_t, preferred_element_type=F32) - slope * dist, NEG)
        kn = seln_ref[0][:, g * hd:(g + 1) * hd]
        vn = seln_ref[0][:, gw + g * hd: gw + (g + 1) * hd]
        s_n = jnp.sum(qf * kn, axis=-1, keepdims=True)
        m = jnp.maximum(jnp.max(s, axis=-1, keepdims=True), s_n)
        pe = jnp.where(valid, jnp.exp(s - m), 0.0)
        pn = jnp.exp(s_n - m)
        o_s = ((lax.dot_general(pe.astype(BF16), v_t, NT, preferred_element_type=F32) + pn * vn)
               / (jnp.sum(pe, axis=-1, keepdims=True) + pn))
        kw_t = kwin_ref[0, g * hd:(g + 1) * hd, :].astype(BF16)
        vw_t = kwin_ref[0, gw + g * hd: gw + (g + 1) * hd, :].astype(BF16)
        s = jnp.dot(q, kw_t, preferred_element_type=F32) - slope * dist_w.astype(F32)
        s = jnp.where(valid_w, s, NEG)
        kwn = winn_ref[0][:, g * hd:(g + 1) * hd]
        vwn = winn_ref[0][:, gw + g * hd: gw + (g + 1) * hd]
        s_wn = jnp.sum(qf * kwn, axis=-1, keepdims=True)
        mw = jnp.maximum(jnp.max(s, axis=-1, keepdims=True), s_wn)
        pw = jnp.where(valid_w, jnp.exp(s - mw), 0.0)
        pwn = jnp.exp(s_wn - mw)
        o_w = ((lax.dot_general(pw.astype(BF16), vw_t, NT, preferred_element_type=F32) + pwn * vwn)
               / (jnp.sum(pw, axis=-1, keepdims=True) + pwn))
        gcol = []
        for c in range(3):
            col = jnp.zeros((H_B, 1), F32)
            for r in range(grp):
                ln = g * LANES + 3 * r + c
                col = jnp.where(head == g * grp + r, gate[:, ln:ln + 1], col)
            gcol.append(col)
        og = gcol[0] * oc_ref[0, g] + gcol[1] * o_s + gcol[2] * o_w
        o = jnp.where((head >= g * grp) & (head < (g + 1) * grp), og, o)
    o_ref[0] = o * szb_ref[0]


def _nsa_decode2(q3, cache_cmp_t, cache_sel_t, cache_win_t, page_table, sel_new, win_new, gate, szb3, wk, wv):
    bs, _, hd = q3.shape
    l_cmp, kv_h = wk.shape
    grp = H_B // kv_h
    gw = kv_h * hd
    page = cache_cmp_t.shape[-1]
    n_pages = page_table.shape[1]
    w_buf = cache_win_t.shape[-1]
    per_page = page // l_cmp
    pg = LANES // per_page
    assert page == LANES and page % L_SEL == 0 and page % l_cmp == 0 and n_pages % pg == 0
    n_chunks = n_pages // pg
    n_sel = n_pages * page // L_SEL
    n_slot = min(K_SEL, n_sel + 1) - 1
    assert n_slot >= 1
    nc_pad = -(-n_sel // LANES) * LANES
    wrow = jnp.tile(_pool_weights(wk, wv, hd, halves=False), (per_page, 1)).T
    pos = lax.broadcasted_iota(jnp.int32, (pg, page, LANES), 1)
    pj = lax.broadcasted_iota(jnp.int32, (pg, page, LANES), 0)
    ln = lax.broadcasted_iota(jnp.int32, (pg, page, LANES), 2)
    seg = (ln == pj * per_page + pos // l_cmp).astype(BF16).reshape(pg * page, LANES)

    per_b = lambda shp: pl.BlockSpec((1,) + shp, lambda b, p, pt: (b,) + (0,) * len(shp))
    oc, idx = pl.pallas_call(
        functools.partial(_nsa_decode_cmp_kernel, pg=pg, n_pages=n_pages, page=page, l_cmp=l_cmp, hd=hd, grp=grp,
                          kv_h=kv_h, n_slot=n_slot),
        grid_spec=pltpu.PrefetchScalarGridSpec(
            num_scalar_prefetch=1,
            grid=(bs, n_chunks),
            in_specs=([per_b((H_B, hd))]
                      + [pl.BlockSpec((1, 2 * gw, page), (lambda j: lambda b, p, pt: (pt[b, p * pg + j], 0, 0))(j))
                         for j in range(pg)]
                      + [pl.BlockSpec((2 * gw, page), lambda b, p, pt: (0, 0)),
                         pl.BlockSpec(seg.shape, lambda b, p, pt: (0, 0))]),
            out_specs=(per_b((kv_h, H_B, hd)), per_b((kv_h, n_slot, nc_pad))),
            scratch_shapes=[pltpu.VMEM((n_chunks, 2 * gw, LANES), F32)]),
        out_shape=(jax.ShapeDtypeStruct((bs, kv_h, H_B, hd), F32),
                   jax.ShapeDtypeStruct((bs, kv_h, n_slot, nc_pad), jnp.int32)),
        compiler_params=_cparams(2),
        name="nsa_decode_cmp",
    )(page_table, q3, *([cache_cmp_t] * pg), wrow, seg)
    blk_ids = idx[:, :, :, 0]

    sel_per_page = page // L_SEL

    def blk_map(g, j, row_blk):
        return lambda b, pt, ids: (pt[b, jnp.maximum(ids[b, g, j], 0) // sel_per_page], row_blk, 0)

    per_b2 = lambda shp: pl.BlockSpec((1,) + shp, lambda b, pt, ids: (b,) + (0,) * len(shp))
    slots = [(g, j) for g in range(kv_h) for j in range(n_slot)]
    return pl.pallas_call(
        functools.partial(_nsa_decode_sel_kernel, n_slot=n_slot, n_pages=n_pages, page=page, hd=hd, grp=grp,
                          kv_h=kv_h, w_buf=w_buf),
        grid_spec=pltpu.PrefetchScalarGridSpec(
            num_scalar_prefetch=2,
            grid=(bs,),
            in_specs=([per_b2((H_B, hd))]
                      + [pl.BlockSpec((1, hd, page), blk_map(g, j, g)) for g, j in slots]
                      + [pl.BlockSpec((1, hd, page), blk_map(g, j, kv_h + g)) for g, j in slots]
                      + [per_b2((2 * gw, w_buf)), per_b2((1, 2 * gw)), per_b2((1, 2 * gw)),
                         per_b2((1, kv_h * LANES)), per_b2((H_B, hd)), per_b2((kv_h, H_B, hd))]),
            out_specs=per_b2((H_B, hd))),
        out_shape=jax.ShapeDtypeStruct((bs, H_B, hd), F32),
        compiler_params=_cparams(1),
        name="nsa_decode_sel",
    )(page_table, blk_ids, q3, *([cache_sel_t] * (2 * len(slots))), cache_win_t, sel_new, win_new, gate, szb3, oc)


def _out_proj_ln_kernel(*refs, n_in, alpha):
    a_refs, w_refs = refs[:n_in], refs[n_in:2 * n_in]
    x_ref, g_ref, b_ref, o_ref = refs[2 * n_in:]
    y = None
    for a_ref, w_ref in zip(a_refs, w_refs):
        t = jnp.dot(a_ref[...].astype(BF16), w_ref[...], preferred_element_type=F32)
        y = t if y is None else y + t
    o_ref[...] = _layer_norm_rows(alpha * x_ref[...] + y, g_ref[...], b_ref[...], LN_EPS)


def _out_proj_ln(a_list, w_list, x2, ln_g, ln_b, alpha):
    m, d = x2.shape
    tm = min(1024, m)
    assert m % tm == 0
    n_in = len(a_list)
    return pl.pallas_call(
        functools.partial(_out_proj_ln_kernel, n_in=n_in, alpha=alpha),
        grid=(m // tm,),
        in_specs=([pl.BlockSpec((tm, a.shape[1]), lambda i: (i, 0)) for a in a_list]
                  + [pl.BlockSpec(w.shape, lambda i: (0, 0), pipeline_mode=pl.Buffered(1)) for w in w_list]
                  + [pl.BlockSpec((tm, d), lambda i: (i, 0)),
                     pl.BlockSpec((1, d), lambda i: (0, 0)), pl.BlockSpec((1, d), lambda i: (0, 0))]),
        out_specs=pl.BlockSpec((tm, d), lambda i: (i, 0)),
        out_shape=jax.ShapeDtypeStruct((m, d), F32),
        compiler_params=_cparams(1),
        name="out_proj_ln",
    )(*a_list, *w_list, x2, ln_g.reshape(1, d), ln_b.reshape(1, d))


def _rwkv_proj_kernel(*refs, seq, tiles_per_seq):
    if seq:
        x_ref, tail_ref, shift_ref = refs[:3]
        rest = refs[3:]
    else:
        x_ref, shift_ref = refs[:2]
        rest = refs[2:]
    (mu_ref, w_ref, w1_ref, w2_ref, a1_ref, a2_ref, w0_ref, a0_ref,
     r_ref, k_ref, v_ref, lw_ref, a_ref, sz_ref) = rest
    x = x_ref[...]
    if seq:
        i = pl.program_id(0)
        first = jnp.where(i % tiles_per_seq == 0, shift_ref[0], tail_ref[SUBLANES - 1:SUBLANES, :])
        row = lax.broadcasted_iota(jnp.int32, x.shape, 0)
        x_prev = jnp.where(row == 0, first, pltpu.roll(x, 1, 0))
    else:
        x_prev = shift_ref[...]
    dx = x_prev - x
    mix = lambda n: (x + dx * mu_ref[n:n + 1, :]).astype(BF16)
    r_ref[...] = jnp.dot(mix(0), w_ref[0], preferred_element_type=F32)
    k_ref[...] = jnp.dot(mix(1), w_ref[1], preferred_element_type=F32)
    v_ref[...] = jnp.dot(mix(2), w_ref[2], preferred_element_type=F32)
    sz_ref[...] = _silu(jnp.dot(mix(3), w_ref[3], preferred_element_type=F32))
    hw = jnp.tanh(jnp.dot(mix(4), w1_ref[...], preferred_element_type=F32)).astype(BF16)
    y = -(w0_ref[...] + jnp.dot(hw, w2_ref[...], preferred_element_type=F32))
    softplus = jnp.maximum(y, 0.0) + jnp.log(1.0 + jnp.exp(-jnp.abs(y)))
    lw_ref[...] = -jnp.exp(-softplus - 0.5)
    ha = jnp.dot(mix(5), a1_ref[...], preferred_element_type=F32).astype(BF16)
    a_ref[...] = _sigmoid(a0_ref[...] + jnp.dot(ha, a2_ref[...], preferred_element_type=F32))


def _rwkv_proj(x2, shift, t_len, mu, w_rkvz, w0, w1, w2, a0, a1, a2):
    m, d = x2.shape
    seq = t_len > 1
    tm = min(512, t_len) if seq else m
    assert m % tm == 0 and (not seq or (t_len % tm == 0 and tm % SUBLANES == 0))
    tiles_per_seq = t_len // tm if seq else 1
    full = lambda a: pl.BlockSpec(a.shape, lambda i: (0,) * a.ndim, pipeline_mode=pl.Buffered(1))
    row = pl.BlockSpec((tm, d), lambda i: (i, 0))
    if seq:
        blk = tm // SUBLANES
        lead = [x2, x2, shift.reshape(-1, 1, d)]
        lead_specs = [row, pl.BlockSpec((SUBLANES, d), lambda i: (jnp.maximum(i * blk - 1, 0), 0)),
                      pl.BlockSpec((1, 1, d), lambda i: (i // tiles_per_seq, 0, 0))]
    else:
        lead = [x2, shift]
        lead_specs = [row, row]
    ws = [mu, w_rkvz.astype(BF16), w1.astype(BF16), w2.astype(BF16), a1.astype(BF16), a2.astype(BF16),
          w0.reshape(1, d), a0.reshape(1, d)]
    return pl.pallas_call(
        functools.partial(_rwkv_proj_kernel, seq=seq, tiles_per_seq=tiles_per_seq),
        grid=(m // tm,),
        in_specs=lead_specs + [full(a) for a in ws],
        out_specs=(row,) * 6,
        out_shape=(jax.ShapeDtypeStruct((m, d), F32),) * 6,
        compiler_params=_cparams(1),
        name="rwkv_proj",
    )(*lead, *ws)


WKV_CHUNK = 64
WKV_PASSES = 1
WKV_SUB = 2


def _wkv_kernel(r_ref, k_ref, v_ref, lw_ref, a_ref, sz_ref, s0_ref, kk_ref, ka_ref, rk_ref, gg_ref, gb_ref,
                yz_ref, sfin_ref, h_s, *, c_len, n_sub, n_real, n_heads, hd):
    c = pl.program_id(1)
    mm = functools.partial(_mm, pa=WKV_PASSES, pb=WKV_PASSES)
    n_pairs = n_heads // 2
    pw = 2 * hd
    d = n_heads * hd
    pairs = range(n_pairs)
    iota = lambda shape, axis: lax.broadcasted_iota(jnp.int32, shape, axis)
    bd_mask = (iota((pw, pw), 0) // hd) == (iota((pw, pw), 1) // hd)
    bd_ones = jnp.where(bd_mask, 1.0, 0.0).astype(BF16)
    eye_mask = iota((pw, pw), 0) == iota((pw, pw), 1)
    eye_pb = jnp.where(eye_mask, 1.0, 0.0).astype(BF16)
    lane_lo = iota((1, pw), 1) < hd

    @pl.when(c == 0)
    def _():
        z = jnp.zeros((hd, hd), F32)
        for p in pairs:
            h_s[p] = jnp.concatenate([jnp.concatenate([s0_ref[0, 2 * p].T, z], axis=1),
                                      jnp.concatenate([z, s0_ref[0, 2 * p + 1].T], axis=1)], axis=0)

    def segsum(x):
        xb = x.astype(BF16)
        return jnp.concatenate([jnp.dot(xb[:, t * pw:(t + 1) * pw], bd_ones, preferred_element_type=F32)
                                for t in range(d // pw)], axis=1)

    r = r_ref[0]
    k = k_ref[0]
    v = v_ref[0]
    lw = lw_ref[0]
    a = a_ref[0]
    tri = jnp.where(iota((c_len, c_len), 0) >= iota((c_len, c_len), 1), 1.0, 0.0).astype(BF16)
    col2 = iota((c_len, 2 * c_len), 1) % c_len
    low_exc2 = iota((c_len, 2 * c_len), 0) > col2
    eye2 = jnp.where(iota((c_len, 2 * c_len), 0) == col2, 1.0, 0.0)
    low_inc4 = iota((c_len, 4 * c_len), 0) >= iota((c_len, 4 * c_len), 1) % c_len
    cc_mask = (iota((2 * c_len, 2 * c_len), 0) // c_len) == (iota((2 * c_len, 2 * c_len), 1) // c_len)

    def by_head(x):
        return jnp.concatenate([jnp.where(lane_lo, x, 0.0), jnp.where(lane_lo, 0.0, x)], axis=0).astype(BF16)
    kk = k * kk_ref[...]
    kap = kk / jnp.maximum(jnp.sqrt(segsum(kk * kk)), 1e-12)
    ka = kap * a
    kmod = k * (1.0 + (a - 1.0) * ka_ref[...])
    rkr = r * kmod * rk_ref[...]
    levels = max((n_real - 1).bit_length() - 1, 0)

    lhs_b, lhs2_t, vb, v_st, l_b, l_k, m_r = [], [], [], [], [], [], []
    for s in range(n_sub):
        rs = slice(s * c_len, (s + 1) * c_len)
        lw_s = lw[rs]
        cum = _mm(tri, lw_s, NN, pb=3)
        cum_end = cum[c_len - 1:c_len, :]
        p_inv = jnp.exp(-cum)
        p_rest = jnp.exp(cum_end - cum)
        p_end = jnp.exp(cum_end)
        lhs_top = kap[rs] * jnp.exp(cum - lw_s)
        r_t = r[rs] * jnp.exp(cum)
        rhs_top = ka[rs] * p_inv
        k_t = kmod[rs] * p_inv
        ke_top = ka[rs] * p_rest
        k_e = kmod[rs] * p_rest
        for p in pairs:
            ls = slice(p * pw, (p + 1) * pw)
            lhs_b.append(jnp.concatenate([lhs_top[:, ls], r_t[:, ls]], axis=0).astype(BF16))
            v_p = v[rs, ls]
            vb.append(v_p.astype(BF16))
            v_st.append(by_head(v_p))
            lhs2 = jnp.concatenate([ke_top[:, ls], k_e[:, ls], jnp.where(eye_mask, p_end[:, ls], 0.0)],
                                   axis=0).astype(BF16)
            lhs2_t.append(lax.dot_general(eye_pb, lhs2, NT, preferred_element_type=F32).astype(BF16))
            rhs_st = jnp.concatenate([by_head(rhs_top[:, ls]), by_head(k_t[:, ls])], axis=0)
            am = mm(lhs_b[-1], rhs_st, NT)
            l_b.append(jnp.where(low_exc2, am[:c_len, :2 * c_len], 0.0))
            l_k.append(jnp.where(low_exc2, am[:c_len, 2 * c_len:], 0.0).astype(BF16))
            m_r.append(jnp.where(low_inc4, am[c_len:], 0.0).astype(BF16))
    n_sp = n_sub * n_pairs
    lkv = [mm(l_k[i], v_st[i], NN) for i in range(n_sp)]
    blockdiag = lambda x: jnp.where(cc_mask, jnp.concatenate([x, x], axis=0), 0.0).astype(BF16)
    t_inv = [eye2 - x for x in l_b]
    pwr = l_b
    for _ in range(levels):
        pwr = [mm(x, blockdiag(x), NN) for x in pwr]
        t_inv = [t + mm(t, blockdiag(x), NN) for t, x in zip(t_inv, pwr)]

    h_cur = [h_s[p] for p in pairs]
    y_rows = []
    for s in range(n_sub):
        idx = [s * n_pairs + p for p in pairs]
        hb = [h_cur[p].astype(BF16) for p in pairs]
        gh = [mm(lhs_b[i], hb[p], NN) for p, i in zip(pairs, idx)]
        u = [-mm(t_inv[i], by_head(gh[p][:c_len] + lkv[i]), NN) for p, i in zip(pairs, idx)]
        uv = [jnp.concatenate([u[p].astype(BF16), vb[i]], axis=0) for p, i in zip(pairs, idx)]
        y = [gh[p][c_len:] + mm(m_r[i], jnp.concatenate([by_head(u[p]), v_st[i]], axis=0), NN)
             for p, i in zip(pairs, idx)]
        h_cur = [jnp.where(bd_mask, mm(lhs2_t[i], jnp.concatenate([uv[p], hb[p]], axis=0), NN), 0.0)
                 for p, i in zip(pairs, idx)]
        y_rows.append(jnp.concatenate(y, axis=1))
    for p in pairs:
        h_s[p] = h_cur[p]

    y_all = jnp.concatenate(y_rows, axis=0) if n_sub > 1 else y_rows[0]
    dy = y_all - segsum(y_all) * (1.0 / hd)
    var = segsum(dy * dy) * (1.0 / hd)
    yn = dy * lax.rsqrt(var + GN_EPS) * gg_ref[...] + gb_ref[...]
    yz_ref[0] = ((yn + segsum(rkr) * v) * sz_ref[0]).astype(yz_ref.dtype)

    @pl.when(c == pl.num_programs(1) - 1)
    def _():
        for p in pairs:
            sfin_ref[0, 2 * p] = h_s[p, :hd, :hd].T
            sfin_ref[0, 2 * p + 1] = h_s[p, hd:, hd:].T


def _wkv_scan(r3, k3, v3, lw3, a3, sz3, s0, k_k, k_a, r_k, gn_g, gn_b, c_len, t_real):
    bsz, t, d = r3.shape
    n_heads = d // HD_C
    n_sub = WKV_SUB if t % (WKV_SUB * c_len) == 0 else 1
    rows = n_sub * c_len
    assert t % rows == 0 and n_heads % 2 == 0 and 2 * HD_C == LANES
    vec = lambda x: x.reshape(1, d)
    seq = pl.BlockSpec((1, rows, d), lambda b, c: (b, c, 0))
    st = pl.BlockSpec((1, n_heads, HD_C, HD_C), lambda b, c: (b, 0, 0, 0))
    par = pl.BlockSpec((1, d), lambda b, c: (0, 0))
    return pl.pallas_call(
        functools.partial(_wkv_kernel, c_len=c_len, n_sub=n_sub, n_real=min(t_real, c_len), n_heads=n_heads,
                          hd=HD_C),
        grid=(bsz, t // rows),
        in_specs=[seq] * 6 + [st] + [par] * 5,
        out_specs=(seq, st),
        out_shape=(jax.ShapeDtypeStruct((bsz, t, d), BF16),
                   jax.ShapeDtypeStruct((bsz, n_heads, HD_C, HD_C), F32)),
        scratch_shapes=[pltpu.VMEM((n_heads // 2, LANES, LANES), F32)],
        compiler_params=_cparams(2),
        name="wkv_scan",
    )(r3, k3, v3, lw3, a3, sz3, s0, vec(k_k), vec(k_a), vec(r_k), vec(gn_g), vec(gn_b))


def _even_layer(xp, xs, cache_cmp, cache_sel, cache_win, state_conv, page_table,
                w_in, conv_w, conv_b, cln_g, cln_b, wk, wv, w_out, ln_g, ln_b, alpha):
    bp, tp, d = xp.shape
    bs, ts, _ = xs.shape
    assert ts == 1
    d_a = conv_w.shape[-1]
    n_taps = conv_w.shape[0]
    l_cmp, kv_h = wk.shape
    hd = cache_cmp.shape[-1]
    d_b = H_B * hd
    d_kv = 2 * kv_h * hd
    w_bf = _prep_w_in(w_in, d_a, d_b, kv_h, hd)
    cw = conv_w.reshape(n_taps, d_a)
    w_out_bf = w_out.astype(BF16)
    kv_shape = lambda b, t: (b, t, 2, kv_h, hd)

    proj = _even_in_proj(xp.reshape(bp * tp, d), w_bf, d_a, d_b, d_kv, kv_h, hd, tp)
    u, sza, q, cmp_n, sel_n, win_n, szb, gate = proj[:8]
    if len(proj) > 8:
        new_kv = [a.reshape(bp, 2, kv_h, hd, tp).transpose(0, 4, 1, 2, 3) for a in proj[8:]]
    else:
        new_kv = [a.reshape(bp, tp, 2, kv_h, hd) for a in (cmp_n, sel_n, win_n)]
    r3 = lambda a: a.reshape(bp, tp, a.shape[-1])
    u3 = r3(u)
    ya = _conv_branch(jnp.zeros((bp, CONV_HIST, d_a), F32), u3, r3(sza), cw, conv_b, cln_g, cln_b)
    yb = _nsa_prompt(r3(q), r3(cmp_n), r3(sel_n), r3(win_n), r3(gate), r3(szb), wk, wv)
    yp = _out_proj_ln([ya.reshape(bp * tp, d_a), yb.reshape(bp * tp, d_b)], [w_out_bf[:d_a], w_out_bf[d_a:]],
                      xp.reshape(bp * tp, d), ln_g, ln_b, alpha).reshape(bp, tp, d)
    w_keep = min(WINDOW, tp)
    outs_p = (new_kv[0], new_kv[1], new_kv[2][:, tp - w_keep:], u3[:, tp - (n_taps - 1):])

    u, sza, q, cmp_s, sel_s, win_s, szb, gate = _even_in_proj(xs.reshape(bs, d), w_bf, d_a, d_b, d_kv, kv_h, hd, 1)
    ext = jnp.concatenate([state_conv, u[:, None, :]], axis=1)
    hist = jnp.pad(state_conv, ((0, 0), (CONV_HIST - (n_taps - 1), 0), (0, 0)))
    pad_rows = lambda a: jnp.pad(a[:, None, :], ((0, 0), (0, SUBLANES - 1), (0, 0)))
    ya = _conv_branch(hist, pad_rows(u), pad_rows(sza), cw, conv_b, cln_g, cln_b)[:, 0]
    n_pool, page = cache_cmp.shape[:2]
    w_buf = cache_win.shape[1]
    fm = lambda c: jnp.transpose(c, (0, 2, 3, 4, 1)).reshape(c.shape[0], d_kv, c.shape[1])
    yb = _nsa_decode2(q.astype(F32).reshape(bs, H_B, hd), fm(cache_cmp), fm(cache_sel), fm(cache_win), page_table,
                      sel_s[:, None, :], win_s[:, None, :], gate[:, None, :], szb.reshape(bs, H_B, hd), wk, wv)
    ys = _out_proj_ln([ya, yb.reshape(bs, d_b)], [w_out_bf[:d_a], w_out_bf[d_a:]], xs.reshape(bs, d),
                      ln_g, ln_b, alpha).reshape(bs, 1, d)
    ctx = jnp.concatenate([cache_win, win_s.reshape(kv_shape(bs, 1))], axis=1)
    outs_s = (cmp_s.reshape(kv_shape(bs, 1)), sel_s.reshape(kv_shape(bs, 1)),
              ctx[:, ctx.shape[1] - min(WINDOW, ctx.shape[1]):], ext[:, 1:])
    return yp, ys, outs_p, outs_s


def _odd_group(x3, shift, s0, mu, w_rkvz, w0, w1, w2, a0, a1, a2, k_k, k_a, r_k, gn_g, gn_b, w_out_bf,
               ln_g, ln_b, alpha):
    bsz, t, d = x3.shape
    x2 = x3.reshape(bsz * t, d)
    r, k, v, lw, a, sz = _rwkv_proj(x2, shift, t, mu, w_rkvz, w0, w1, w2, a0, a1, a2)
    c_len = WKV_CHUNK if t >= WKV_CHUNK else -(-t // SUBLANES) * SUBLANES
    t_pad = -(-t // c_len) * c_len
    r3 = lambda z: jnp.pad(z.reshape(bsz, t, d), ((0, 0), (0, t_pad - t), (0, 0)))
    yz, s_fin = _wkv_scan(r3(r), r3(k), r3(v), r3(lw), r3(a), r3(sz), s0, k_k, k_a, r_k.reshape(-1), gn_g, gn_b,
                          c_len, t)
    y = _out_proj_ln([yz[:, :t].reshape(bsz * t, d)], [w_out_bf], x2, ln_g, ln_b, alpha).reshape(bsz, t, d)
    return y, s_fin, x3[:, -1]


def kernel(x_prompt, x_sample, cache_cmp_kv, cache_sel_kv, cache_win_kv, state_conv, state_wkv, state_shift,
           page_table, w_in_even, conv_w, conv_b, conv_ln_g, conv_ln_b, wk_cmp, wv_cmp, w_out_even, mu_c, w_rkvz,
           w0, w1, w2, a0, a1, a2, k_k, k_a, r_k, gn_g, gn_b, w_out_odd, ln_g, ln_b):
    depth = ln_g.shape[0]
    alpha = (2 * depth) ** 0.25
    bp, _, d = x_prompt.shape
    n_heads = d // HD_C
    xp, xs = x_prompt, x_sample
    even_p, even_s, odd_p, odd_s = [], [], [], []
    for l in range(depth):
        if l % 2 == 0:
            e = l // 2
            xp, xs, o_p, o_s = _even_layer(
                xp, xs, cache_cmp_kv[e], cache_sel_kv[e], cache_win_kv[e], state_conv[e], page_table,
                w_in_even[e], conv_w[e], conv_b[e], conv_ln_g[e], conv_ln_b[e], wk_cmp[e], wv_cmp[e],
                w_out_even[e], ln_g[l], ln_b[l], alpha)
            even_p.append(o_p)
            even_s.append(o_s)
        else:
            o = l // 2
            po = (mu_c[o], w_rkvz[o], w0[o], w1[o], w2[o], a0[o], a1[o], a2[o], k_k[o], k_a[o], r_k[o],
                  gn_g[o], gn_b[o], w_out_odd[o].astype(BF16), ln_g[l], ln_b[l], alpha)
            xp, s_p, h_p = _odd_group(xp, jnp.zeros((bp, d), F32), jnp.zeros((bp, n_heads, HD_C, HD_C), F32), *po)
            xs, s_s, h_s = _odd_group(xs, state_shift[o], state_wkv[o], *po)
            odd_p.append((s_p, h_p))
            odd_s.append((s_s, h_s))
    stack = lambda items, j: jnp.stack([it[j] for it in items])
    return (xp, xs,
            stack(even_p, 0), stack(even_s, 0), stack(even_p, 1), stack(even_s, 1),
            stack(even_p, 2), stack(even_s, 2), stack(even_p, 3), stack(even_s, 3),
            stack(odd_p, 0), stack(odd_s, 0), stack(odd_p, 1), stack(odd_s, 1))
```

```python
import functools

import jax
import jax.numpy as jnp
from jax import lax
from jax.experimental import pallas as pl
from jax.experimental.pallas import tpu as pltpu

F32 = jnp.float32
BF16 = jnp.bfloat16

H_B = 8
L_SEL = 64
K_SEL = 16
WINDOW = 512
FORCE = 1e4
HD_C = 64
GN_EPS = 64e-5
LN_EPS = 1e-5
NEG = -1e30

LANES = 128
SUBLANES = 8
VMEM_LIMIT = 56 * 1024 * 1024

NT = (((1,), (1,)), ((), ()))
TN = (((0,), (0,)), ((), ()))
NN = (((1,), (0,)), ((), ()))


def _cparams(n_axes):
    return pltpu.CompilerParams(dimension_semantics=("arbitrary",) * n_axes,
                                vmem_limit_bytes=VMEM_LIMIT)


def _sigmoid(x):
    return 1.0 / (1.0 + jnp.exp(-x))


def _silu(x):
    return x * _sigmoid(x)


def _split_bf16(x, n):
    parts, rem = [], x
    for i in range(n):
        p = rem.astype(BF16)
        parts.append(p)
        if i + 1 < n:
            rem = rem - p.astype(F32)
    return parts


def _mm(a, b, dims=NN, pa=1, pb=1):
    aa = [a] if a.dtype == BF16 else _split_bf16(a, pa)
    bb = [b] if b.dtype == BF16 else _split_bf16(b, pb)
    keep = max(len(aa), len(bb))
    out = None
    for i, ai in enumerate(aa):
        for j, bj in enumerate(bb):
            if i + j < keep:
                t = lax.dot_general(ai, bj, dims, preferred_element_type=F32)
                out = t if out is None else out + t
    return out


def _layer_norm_rows(h, g, b, eps):
    mu = jnp.mean(h, axis=-1, keepdims=True)
    d = h - mu
    var = jnp.mean(d * d, axis=-1, keepdims=True)
    return d * lax.rsqrt(var + eps) * g + b


def _even_in_proj_kernel(x_ref, w_ref, u_ref, sza_ref, q_ref, cmp_ref, sel_ref, win_ref, szb_ref, gate_ref,
                         *t_refs, d_a, d_b, d_kv, q_scale):
    xb = x_ref[...].astype(BF16)

    def seg(lo, n):
        return jnp.dot(xb, w_ref[:, lo:lo + n], preferred_element_type=F32)

    o = 0
    a_val = seg(o, d_a); o += d_a
    a_glu = seg(o, d_a); o += d_a
    u_ref[...] = a_val * _sigmoid(a_glu)
    sza_ref[...] = _silu(seg(o, d_a)); o += d_a
    q_ref[...] = (seg(o, d_b) * q_scale).astype(BF16); o += d_b
    for j, ref in enumerate((cmp_ref, sel_ref, win_ref)):
        kv = seg(o, d_kv); o += d_kv
        ref[...] = kv
        if t_refs:
            t_refs[j][0] = kv.T
    szb_ref[...] = _silu(seg(o, d_b)); o += d_b
    gate_ref[...] = _sigmoid(seg(o, gate_ref.shape[-1]))


def _prep_w_in(w, d_a, d_b, kv_h, hd):
    grp = H_B // kv_h
    c_kv6 = 3 * d_a + d_b
    c_g3 = c_kv6 + 6 * kv_h * hd
    c_zb = c_g3 + 3 * H_B
    gate_blocks = []
    for g in range(kv_h):
        blk = w[:, c_g3 + g * grp * 3: c_g3 + (g + 1) * grp * 3]
        gate_blocks.append(jnp.pad(blk, ((0, 0), (0, LANES - grp * 3))))
    wn = jnp.concatenate([w[:, :c_g3], w[:, c_zb:c_zb + d_b]] + gate_blocks, axis=1)
    return wn.astype(BF16)


def _even_in_proj(x2, w_bf, d_a, d_b, d_kv, kv_h, hd, t_len):
    m, d = x2.shape
    tm = min(512, m)
    assert m % tm == 0
    n_gate = kv_h * LANES
    row = lambda n: pl.BlockSpec((tm, n), lambda i: (i, 0))
    out_shape = (jax.ShapeDtypeStruct((m, d_a), F32), jax.ShapeDtypeStruct((m, d_a), F32),
                 jax.ShapeDtypeStruct((m, d_b), BF16),
                 jax.ShapeDtypeStruct((m, d_kv), F32), jax.ShapeDtypeStruct((m, d_kv), F32),
                 jax.ShapeDtypeStruct((m, d_kv), F32),
                 jax.ShapeDtypeStruct((m, d_b), F32), jax.ShapeDtypeStruct((m, n_gate), F32))
    out_specs = (row(d_a), row(d_a), row(d_b), row(d_kv), row(d_kv), row(d_kv), row(d_b), row(n_gate))
    if t_len % tm == 0 and tm % LANES == 0:
        per_seq = t_len // tm
        t_spec = pl.BlockSpec((1, d_kv, tm), lambda i: (i // per_seq, 0, i % per_seq))
        out_shape += (jax.ShapeDtypeStruct((m // t_len, d_kv, t_len), F32),) * 3
        out_specs += (t_spec,) * 3
    return pl.pallas_call(
        functools.partial(_even_in_proj_kernel, d_a=d_a, d_b=d_b, d_kv=d_kv, q_scale=hd ** -0.5),
        grid=(m // tm,),
        in_specs=[row(d), pl.BlockSpec(w_bf.shape, lambda i: (0, 0), pipeline_mode=pl.Buffered(1))],
        out_specs=out_specs,
        out_shape=out_shape,
        compiler_params=_cparams(1),
        name="even_in_proj",
    )(x2, w_bf)


CONV_HIST = 32


def _conv_kernel(hist_ref, u_ref, sza_ref, w_ref, cb_ref, g_ref, b_ref, ya_ref, win_ref, sh_ref, conv_ref,
                 *, tq, rb, n_taps):
    i = pl.program_id(1)
    base = pl.multiple_of(i * tq, SUBLANES)
    win_ref[CONV_HIST:, :] = u_ref[0, pl.ds(base, tq), :]

    @pl.when(i == 0)
    def _():
        win_ref[:CONV_HIST, :] = hist_ref[0]

    @pl.when(i > 0)
    def _():
        win_ref[:CONV_HIST, :] = u_ref[0, pl.ds(base - CONV_HIST, CONV_HIST), :]

    w = win_ref[...]
    n = tq + CONV_HIST
    sh_ref[0] = w
    for s in range(1, SUBLANES):
        sh_ref[s] = pltpu.roll(w, n - s, 0)

    first = CONV_HIST - (n_taps - 1)
    cb = cb_ref[...]
    g = g_ref[...]
    b = b_ref[...]

    def block(k, carry):
        r0 = pl.multiple_of(k * rb, SUBLANES)
        acc = jnp.zeros((rb, u_ref.shape[-1]), F32) + cb
        for j in range(n_taps):
            a, s = divmod(first + j, SUBLANES)
            acc = acc + sh_ref[s, pl.ds(r0 + SUBLANES * a, rb), :] * w_ref[j:j + 1, :]
        conv_ref[pl.ds(r0, rb), :] = acc
        return carry

    lax.fori_loop(0, tq // rb, block, 0)
    y = _silu(_layer_norm_rows(conv_ref[...], g, b, LN_EPS)) * sza_ref[0]
    ya_ref[0] = y.astype(ya_ref.dtype)


def _conv_branch(hist, u3, sza3, conv_w, conv_b, ln_g, ln_b):
    bsz, t, d = u3.shape
    n_taps = conv_w.shape[0]
    assert n_taps - 1 <= CONV_HIST and t % SUBLANES == 0
    tq = min(256, t)
    rb = min(32, tq)
    assert t % tq == 0 and tq % rb == 0
    vec = lambda a: a.reshape(1, d)
    full = lambda shp: pl.BlockSpec(shp, lambda b, i: (0,) * len(shp))
    return pl.pallas_call(
        functools.partial(_conv_kernel, tq=tq, rb=rb, n_taps=n_taps),
        grid=(bsz, t // tq),
        in_specs=[pl.BlockSpec((1, CONV_HIST, d), lambda b, i: (b, 0, 0)),
                  pl.BlockSpec((1, t, d), lambda b, i: (b, 0, 0)),
                  pl.BlockSpec((1, tq, d), lambda b, i: (b, i, 0)),
                  full((n_taps, d)), full((1, d)), full((1, d)), full((1, d))],
        out_specs=pl.BlockSpec((1, tq, d), lambda b, i: (b, i, 0)),
        out_shape=jax.ShapeDtypeStruct((bsz, t, d), BF16),
        scratch_shapes=[pltpu.VMEM((tq + CONV_HIST, d), F32),
                        pltpu.VMEM((SUBLANES, tq + CONV_HIST, d), F32), pltpu.VMEM((tq, d), F32)],
        compiler_params=_cparams(2),
        name="conv_branch",
    )(hist, u3, sza3, conv_w, vec(conv_b), vec(ln_g), vec(ln_b))


SEL_OFF = -(2.0 ** 100)


def _nsa_prompt_kernel(q_ref, kcmp_ref, ksel_ref, kwin_ref, gate_ref, szb_ref, wpool_ref, o_ref,
                       kc_s, vc_s, ks_s, vs_s, kw_s, vw_s, m_s, l_s, acc_s, s_scr,
                       *, tq, t_len, l_cmp, hd, grp, kv_h):
    i = pl.program_id(1)
    g = pl.program_id(2)
    n_cmp = t_len // l_cmp
    n_sel = t_len // L_SEL
    half = n_cmp // 2
    gw = kv_h * hd

    @pl.when((i == 0) & (g == 0))
    def _prep():
        x3 = kcmp_ref[0].reshape(half, 2 * l_cmp, 2 * gw)
        pooled = jnp.concatenate([jnp.sum(x3 * wpool_ref[0][None], axis=1),
                                  jnp.sum(x3 * wpool_ref[1][None], axis=1)], axis=0)
        pos = lax.broadcasted_iota(jnp.int32, (t_len, LANES), 0)
        lane = lax.broadcasted_iota(jnp.int32, (t_len, LANES), 1)
        blk = pos // L_SEL
        feat = jnp.where(lane == blk, 1.0, 0.0)
        feat = jnp.where(lane == n_sel, blk.astype(F32), feat)
        feat = jnp.where(lane == n_sel + 1, (pos - blk * L_SEL).astype(F32), feat)
        for src, dst in ((ksel_ref, ks_s), (kwin_ref, kw_s)):
            k01 = src[0, :, :gw]
            dst[0] = jnp.where(lane < hd, feat, pltpu.roll(k01, hd, 1)).astype(BF16)
            dst[1] = jnp.where(lane < hd, feat, k01).astype(BF16)
        vc_pad = jnp.concatenate([pooled[:, gw:], jnp.zeros((LANES - n_cmp, gw), F32)], axis=0) \
            if n_cmp < LANES else pooled[:, gw:]
        vc_t = vc_pad.T
        vs_t = ksel_ref[0, :, gw:].T
        vw_t = kwin_ref[0, :, gw:].T
        for gg in range(kv_h):
            kc_s[gg] = pooled[:, gg * hd:(gg + 1) * hd].astype(BF16)
            vc_s[gg] = vc_t[gg * hd:(gg + 1) * hd, :n_cmp].astype(BF16)
            vs_s[gg] = vs_t[gg * hd:(gg + 1) * hd].astype(BF16)
            vw_s[gg] = vw_t[gg * hd:(gg + 1) * hd].astype(BF16)

    t0 = i * tq
    gs = jnp.float32(1.0)
    for gg in range(1, kv_h):
        gs = jnp.where(g == gg, 2.0 ** -(gg * grp), gs)
    slopes = [gs * (2.0 ** -(r + 1)) for r in range(grp)]
    qt = q_ref[0]
    qs = jnp.concatenate([qt[:, r * hd:(r + 1) * hd] for r in range(grp)], axis=0)

    s_t = lax.dot_general(kc_s[g], qs, NT, preferred_element_type=F32)
    n_io = lax.broadcasted_iota(jnp.int32, (n_cmp, tq), 0)
    t_io = lax.broadcasted_iota(jnp.int32, (n_cmp, tq), 1) + t0
    c_end = jnp.where(n_io < half, 2 * l_cmp * n_io + (l_cmp - 1), 2 * l_cmp * (n_io - half) + (2 * l_cmp - 1))
    dist_c = t_io - c_end
    valid_c = dist_c >= 0
    dist_cf = dist_c.astype(F32)
    imp = jnp.zeros((n_sel, tq), F32)
    o_c = []
    for r in range(grp):
        s = jnp.where(valid_c, s_t[:, r * tq:(r + 1) * tq] - slopes[r] * dist_cf, NEG)
        e = jnp.exp(s - jnp.max(s, axis=0, keepdims=True))
        p = jnp.where(valid_c, e / jnp.sum(e, axis=0, keepdims=True), 0.0)
        imp = imp + (p[:half] + p[half:])
        o_c.append(jnp.dot(vc_s[g], p.astype(BF16), preferred_element_type=F32))

    sb = lax.broadcasted_iota(jnp.int32, (n_sel, tq), 0)
    tb = (lax.broadcasted_iota(jnp.int32, (n_sel, tq), 1) + t0) // L_SEL
    forced = (sb == 0) | (sb == tb) | (sb == tb - 1)
    imp = jnp.where(forced, FORCE, imp)
    imp = jnp.where(sb <= tb, imp, -jnp.inf)
    rank = jnp.zeros((n_sel, tq), F32)
    for j in range(n_sel):
        vj = imp[j:j + 1, :]
        beats = jnp.where(imp > vj, 1.0, jnp.where((imp == vj) & (sb < j), 1.0, 0.0))
        rank = jnp.where(sb == j, jnp.sum(beats, axis=0, keepdims=True), rank)
    off_t = jnp.where(rank < min(K_SEL, n_sel), 0.0, SEL_OFF)
    feat_sel = jnp.concatenate([off_t, jnp.zeros((LANES - n_sel, tq), F32)], axis=0).T

    qf = q_ref[0].astype(F32)
    lane_q = lax.broadcasted_iota(jnp.int32, (tq, LANES), 1)

    def q_aug(r, feat):
        tile = qf[:, (r // 2) * LANES:(r // 2 + 1) * LANES]
        if r % 2 == 0:
            tile = pltpu.roll(tile, hd, 1)
        x = jnp.where(lane_q < hd, feat, tile)
        x = jnp.where(lane_q == n_sel, slopes[r] * L_SEL, x)
        return jnp.where(lane_q == n_sel + 1, slopes[r], x).astype(BF16)

    d0 = (lax.broadcasted_iota(jnp.int32, (tq, tq), 1) - lax.broadcasted_iota(jnp.int32, (tq, tq), 0))

    def reset():
        m_s[...] = jnp.full(m_s.shape, NEG, F32)
        l_s[...] = jnp.zeros(l_s.shape, F32)
        acc_s[...] = jnp.zeros(acc_s.shape, F32)

    heads = range(grp)

    def scores(q_list, k_s, kt):
        k0 = pl.multiple_of(kt * tq, tq)
        kk = k_s[g, pl.ds(k0, tq), :]
        return [lax.dot_general(kk, q_list[r], NT, preferred_element_type=F32) for r in heads]

    def update(s, v_s, kt, mask):
        vt = v_s[g, :, pl.ds(pl.multiple_of(kt * tq, tq), tq)]
        if mask is not None:
            s = [jnp.where(mask, s[r], NEG) for r in heads]
        m_old = [m_s[r] for r in heads]
        m_new = [jnp.maximum(m_old[r], jnp.max(s[r], axis=0, keepdims=True)) for r in heads]
        al = [jnp.exp(m_old[r] - m_new[r]) for r in heads]
        p = [jnp.exp(s[r] - m_new[r]) for r in heads]
        for r in heads:
            m_s[r] = m_new[r]
            l_s[r] = al[r] * l_s[r] + jnp.sum(p[r], axis=0, keepdims=True)
            acc_s[r] = al[r] * acc_s[r] + jnp.dot(vt, p[r].astype(BF16), preferred_element_type=F32)

    def attend(q_list, k_s, v_s, lo, interior_mask, diag_mask):
        reset()

        def put(slot, kt):
            s = scores(q_list, k_s, kt)
            for r in heads:
                s_scr[slot, r] = s[r]

        def take(slot):
            return [s_scr[slot, r] for r in heads]

        odd = (i - lo) % 2

        @pl.when(odd == 1)
        def _():
            update(scores(q_list, k_s, lo), v_s, lo, interior_mask(lo))

        lo2 = lo + odd
        put(0, lo2)

        def body(j, carry):
            kt = lo2 + 2 * j
            put(1, kt + 1)
            update(take(0), v_s, kt, interior_mask(kt))
            put(0, kt + 2)
            update(take(1), v_s, kt + 1, interior_mask(kt + 1))
            return carry

        lax.fori_loop(0, (i - lo2) // 2, body, 0)
        update(take(0), v_s, i, diag_mask)
        return [acc_s[r] / l_s[r] for r in heads]

    o_s = attend([q_aug(r, feat_sel) for r in heads], ks_s, vs_s, 0, lambda kt: None, d0 >= 0)

    w_tiles = -(-(WINDOW - 1) // tq)
    o_w = attend([q_aug(r, 0.0) for r in heads], kw_s, vw_s, jnp.maximum(i - w_tiles, 0),
                 lambda kt: d0 + (i - kt) * tq < WINDOW, (d0 >= 0) & (d0 < WINDOW))

    gate_t = gate_ref[0].T
    outs = []
    for r in range(grp):
        outs.append(gate_t[3 * r:3 * r + 1] * o_c[r]
                    + gate_t[3 * r + 1:3 * r + 2] * o_s[r]
                    + gate_t[3 * r + 2:3 * r + 3] * o_w[r])
    o_ref[0] = (jnp.concatenate(outs, axis=0).T * szb_ref[0]).astype(o_ref.dtype)


def _pool_weights(wk, wv, hd, halves):
    l_cmp, kv_h = wk.shape
    row = jnp.concatenate([jnp.repeat(wk, hd, axis=1), jnp.repeat(wv, hd, axis=1)], axis=1)
    if not halves:
        return row
    z = jnp.zeros_like(row)
    return jnp.stack([jnp.concatenate([row, z], axis=0), jnp.concatenate([z, row], axis=0)])


def _nsa_prompt(q3, cmp3, sel3, win3, gate3, szb3, wk, wv):
    bsz, t, d_b = q3.shape
    l_cmp, kv_h = wk.shape
    grp = H_B // kv_h
    hd = d_b // H_B
    gw = kv_h * hd
    tq = min(256, t)
    assert t % tq == 0 and t % (2 * l_cmp) == 0 and L_SEL == 2 * l_cmp and tq % L_SEL == 0
    n_cmp = t // l_cmp
    n_sel = t // L_SEL
    assert gw == LANES and 2 * hd == LANES and n_sel + 2 <= hd and n_sel % SUBLANES == 0 and n_cmp <= LANES
    wpool = _pool_weights(wk, wv, hd, halves=True)
    kv_spec = pl.BlockSpec((1, t, 2 * gw), lambda b, i, g: (b, 0, 0))
    return pl.pallas_call(
        functools.partial(_nsa_prompt_kernel, tq=tq, t_len=t, l_cmp=l_cmp, hd=hd, grp=grp, kv_h=kv_h),
        grid=(bsz, t // tq, kv_h),
        in_specs=[pl.BlockSpec((1, tq, grp * hd), lambda b, i, g: (b, i, g)),
                  kv_spec, kv_spec, kv_spec,
                  pl.BlockSpec((1, tq, LANES), lambda b, i, g: (b, i, g)),
                  pl.BlockSpec((1, tq, grp * hd), lambda b, i, g: (b, i, g)),
                  pl.BlockSpec(wpool.shape, lambda b, i, g: (0, 0, 0))],
        out_specs=pl.BlockSpec((1, tq, grp * hd), lambda b, i, g: (b, i, g)),
        out_shape=jax.ShapeDtypeStruct((bsz, t, d_b), BF16),
        scratch_shapes=[pltpu.VMEM((kv_h, n_cmp, hd), BF16), pltpu.VMEM((kv_h, hd, n_cmp), BF16),
                        pltpu.VMEM((kv_h, t, LANES), BF16), pltpu.VMEM((kv_h, hd, t), BF16),
                        pltpu.VMEM((kv_h, t, LANES), BF16), pltpu.VMEM((kv_h, hd, t), BF16),
                        pltpu.VMEM((grp, 1, tq), F32), pltpu.VMEM((grp, 1, tq), F32),
                        pltpu.VMEM((grp, hd, tq), F32), pltpu.VMEM((2, grp, tq, tq), F32)],
        compiler_params=_cparams(3),
        name="nsa_prompt",
    )(q3, cmp3, sel3, win3, gate3, szb3, wpool)


def _decode_slopes():
    head = lax.broadcasted_iota(jnp.int32, (H_B, 1), 0)
    slope = jnp.zeros((H_B, 1), F32)
    for hh in range(H_B):
        slope = jnp.where(head == hh, 2.0 ** -(hh + 1), slope)
    return head, slope


def _nsa_decode_cmp_kernel(pt_ref, q_ref, *refs, pg, n_pages, page, l_cmp, hd, grp, kv_h, n_slot):
    del pt_ref
    cmp_refs = refs[:pg]
    wrow_ref, seg_ref, oc_ref, idx_ref, kcv_s = refs[pg:]
    p = pl.program_id(1)
    n_chunks = n_pages // pg
    past = n_pages * page
    n_cmp = past // l_cmp
    n_sel = past // L_SEL
    gw = kv_h * hd
    nc_pad = idx_ref.shape[-1]
    prod = jnp.concatenate([cmp_refs[j][0] * wrow_ref[...] for j in range(pg)], axis=1)
    kcv_s[p] = _mm(prod, seg_ref[...], NN)

    @pl.when(p == n_chunks - 1)
    def _compressed():
        head, slope = _decode_slopes()
        q = q_ref[0].astype(BF16)
        n_io = lax.broadcasted_iota(jnp.int32, (1, n_cmp), 1)
        dist = (past - (l_cmp * n_io + (l_cmp - 1))).astype(F32)
        pair = jnp.where(lax.broadcasted_iota(jnp.int32, (n_cmp, nc_pad), 0) // (L_SEL // l_cmp)
                         == lax.broadcasted_iota(jnp.int32, (n_cmp, nc_pad), 1), 1.0, 0.0).astype(BF16)
        c_io = lax.broadcasted_iota(jnp.int32, (1, nc_pad), 1)
        i_r = lax.broadcasted_iota(jnp.int32, (nc_pad, nc_pad), 1)
        i_c = lax.broadcasted_iota(jnp.int32, (nc_pad, nc_pad), 0)
        before = jnp.where(i_c < i_r, 1.0, 0.0).astype(BF16)
        slot = lax.broadcasted_iota(jnp.int32, (n_slot, nc_pad), 0).astype(F32)
        for g in range(kv_h):
            kc_t = jnp.concatenate([kcv_s[c, g * hd:(g + 1) * hd, :] for c in range(n_chunks)], axis=1)
            vc_t = jnp.concatenate([kcv_s[c, gw + g * hd: gw + (g + 1) * hd, :] for c in range(n_chunks)], axis=1)
            s = jnp.dot(q, kc_t.astype(BF16), preferred_element_type=F32) - slope * dist
            e = jnp.exp(s - jnp.max(s, axis=-1, keepdims=True))
            pc = e / jnp.sum(e, axis=-1, keepdims=True)
            oc_ref[0, g] = lax.dot_general(pc.astype(BF16), vc_t.astype(BF16), NT, preferred_element_type=F32)
            in_grp = (head >= g * grp) & (head < (g + 1) * grp)
            imp = _mm(jnp.sum(jnp.where(in_grp, pc, 0.0), axis=0, keepdims=True), pair, NN, pa=3)
            forced = (c_io == 0) | (c_io == n_sel - 1)
            imp = jnp.where(forced, FORCE, imp)
            imp = jnp.where(c_io < n_sel, imp, -jnp.inf)
            v_r = jnp.broadcast_to(imp, (nc_pad, nc_pad))
            v_c = v_r.T
            beats = jnp.where(v_c > v_r, 1.0, jnp.where((v_c == v_r) & (i_c < i_r), 1.0, 0.0))
            rank = jnp.sum(beats, axis=0, keepdims=True) + jnp.where(imp < FORCE, 1.0, 0.0)
            chosen = jnp.where((rank < min(K_SEL, n_sel + 1)) & (c_io < n_sel), 1.0, 0.0)
            n_before = jnp.dot(jnp.broadcast_to(chosen, (SUBLANES, nc_pad)).astype(BF16), before,
                               preferred_element_type=F32)[:1]
            hit = jnp.where((n_before == slot) & (chosen > 0.5), 1.0, 0.0)
            ids = jnp.sum(hit * c_io.astype(F32), axis=-1, keepdims=True)
            ids = jnp.where(jnp.sum(hit, axis=-1, keepdims=True) > 0.5, ids, -1.0)
            idx_ref[0, g] = jnp.broadcast_to(ids, (n_slot, nc_pad)).astype(jnp.int32)


def _nsa_decode_sel_kernel(pt_ref, idx_ref, q_ref, *refs, n_slot, n_pages, page, hd, grp, kv_h, w_buf):
    del pt_ref
    n_blk = kv_h * n_slot
    k_refs = refs[:n_blk]
    v_refs = refs[n_blk:2 * n_blk]
    kwin_ref, seln_ref, winn_ref, gate_ref, szb_ref, oc_ref, o_ref = refs[2 * n_blk:]
    b = pl.program_id(0)
    past = n_pages * page
    gw = kv_h * hd
    per_page = page // L_SEL
    head, slope = _decode_slopes()
    qf = q_ref[0]
    q = qf.astype(BF16)
    gate = gate_ref[0]
    lane = lax.broadcasted_iota(jnp.int32, (1, page), 1)
    i_io = lax.broadcasted_iota(jnp.int32, (1, w_buf), 1)
    dist_w = w_buf - i_io
    valid_w = dist_w < WINDOW
    o = jnp.zeros((H_B, hd), F32)
    for g in range(kv_h):
        dists, valids = [], []
        for j in range(n_slot):
            blk = idx_ref[b, g, j]
            safe = jnp.maximum(blk, 0)
            dists.append(past - ((safe // per_page) * page + lane))
            valids.append(lane // L_SEL == jnp.where(blk >= 0, safe % per_page, -1))
        dist = jnp.concatenate(dists, axis=1).astype(F32)
        valid = jnp.concatenate(valids, axis=1)
        k_t = jnp.concatenate([k_refs[g * n_slot + j][0] for j in range(n_slot)], axis=1).astype(BF16)
        v_t = jnp.concatenate([v_refs[g * n_slot + j][0] for j in range(n_slot)], axis=1).astype(BF16)
        s = jnp.where(valid, jnp.dot(q, k_t, preferred_element_type=F32) - slope * dist, NEG)
        kn = seln_ref[0][:, g * hd:(g + 1) * hd]
        vn = seln_ref[0][:, gw + g * hd: gw + (g + 1) * hd]
        s_n = jnp.sum(qf * kn, axis=-1, keepdims=True)
        m = jnp.maximum(jnp.max(s, axis=-1, keepdims=True), s_n)
        pe = jnp.where(valid, jnp.exp(s - m), 0.0)
        pn = jnp.exp(s_n - m)
        o_s = ((lax.dot_general(pe.astype(BF16), v_t, NT, preferred_element_type=F32) + pn * vn)
               / (jnp.sum(pe, axis=-1, keepdims=True) + pn))
        kw_t = kwin_ref[0, g * hd:(g + 1) * hd, :].astype(BF16)
        vw_t = kwin_ref[0, gw + g * hd: gw + (g + 1) * hd, :].astype(BF16)
        s = jnp.dot(q, kw_t, preferred_element_type=F32) - slope * dist_w.astype(F32)
        s = jnp.where(valid_w, s, NEG)
        kwn = winn_ref[0][:, g * hd:(g + 1) * hd]
        vwn = winn_ref[0][:, gw + g * hd: gw + (g + 1) * hd]
        s_wn = jnp.sum(qf * kwn, axis=-1, keepdims=True)
        mw = jnp.maximum(jnp.max(s, axis=-1, keepdims=True), s_wn)
        pw = jnp.where(valid_w, jnp.exp(s - mw), 0.0)
        pwn = jnp.exp(s_wn - mw)
        o_w = ((lax.dot_general(pw.astype(BF16), vw_t, NT, preferred_element_type=F32) + pwn * vwn)
               / (jnp.sum(pw, axis=-1, keepdims=True) + pwn))
        gcol = []
        for c in range(3):
            col = jnp.zeros((H_B, 1), F32)
            for r in range(grp):
                ln = g * LANES + 3 * r + c
                col = jnp.where(head == g * grp + r, gate[:, ln:ln + 1], col)
            gcol.append(col)
        og = gcol[0] * oc_ref[0, g] + gcol[1] * o_s + gcol[2] * o_w
        o = jnp.where((head >= g * grp) & (head < (g + 1) * grp), og, o)
    o_ref[0] = o * szb_ref[0]


def _nsa_decode2(q3, cache_cmp_t, cache_sel_t, cache_win_t, page_table, sel_new, win_new, gate, szb3, wk, wv):
    bs, _, hd = q3.shape
    l_cmp, kv_h = wk.shape
    grp = H_B // kv_h
    gw = kv_h * hd
    page = cache_cmp_t.shape[-1]
    n_pages = page_table.shape[1]
    w_buf = cache_win_t.shape[-1]
    per_page = page // l_cmp
    pg = LANES // per_page
    assert page == LANES and page % L_SEL == 0 and page % l_cmp == 0 and n_pages % pg == 0
    n_chunks = n_pages // pg
    n_sel = n_pages * page // L_SEL
    n_slot = min(K_SEL, n_sel + 1) - 1
    assert n_slot >= 1
    nc_pad = -(-n_sel // LANES) * LANES
    wrow = jnp.tile(_pool_weights(wk, wv, hd, halves=False), (per_page, 1)).T
    pos = lax.broadcasted_iota(jnp.int32, (pg, page, LANES), 1)
    pj = lax.broadcasted_iota(jnp.int32, (pg, page, LANES), 0)
    ln = lax.broadcasted_iota(jnp.int32, (pg, page, LANES), 2)
    seg = (ln == pj * per_page + pos // l_cmp).astype(BF16).reshape(pg * page, LANES)

    per_b = lambda shp: pl.BlockSpec((1,) + shp, lambda b, p, pt: (b,) + (0,) * len(shp))
    oc, idx = pl.pallas_call(
        functools.partial(_nsa_decode_cmp_kernel, pg=pg, n_pages=n_pages, page=page, l_cmp=l_cmp, hd=hd, grp=grp,
                          kv_h=kv_h, n_slot=n_slot),
        grid_spec=pltpu.PrefetchScalarGridSpec(
            num_scalar_prefetch=1,
            grid=(bs, n_chunks),
            in_specs=([per_b((H_B, hd))]
                      + [pl.BlockSpec((1, 2 * gw, page), (lambda j: lambda b, p, pt: (pt[b, p * pg + j], 0, 0))(j))
                         for j in range(pg)]
                      + [pl.BlockSpec((2 * gw, page), lambda b, p, pt: (0, 0)),
                         pl.BlockSpec(seg.shape, lambda b, p, pt: (0, 0))]),
            out_specs=(per_b((kv_h, H_B, hd)), per_b((kv_h, n_slot, nc_pad))),
            scratch_shapes=[pltpu.VMEM((n_chunks, 2 * gw, LANES), F32)]),
        out_shape=(jax.ShapeDtypeStruct((bs, kv_h, H_B, hd), F32),
                   jax.ShapeDtypeStruct((bs, kv_h, n_slot, nc_pad), jnp.int32)),
        compiler_params=_cparams(2),
        name="nsa_decode_cmp",
    )(page_table, q3, *([cache_cmp_t] * pg), wrow, seg)
    blk_ids = idx[:, :, :, 0]

    sel_per_page = page // L_SEL

    def blk_map(g, j, row_blk):
        return lambda b, pt, ids: (pt[b, jnp.maximum(ids[b, g, j], 0) // sel_per_page], row_blk, 0)

    per_b2 = lambda shp: pl.BlockSpec((1,) + shp, lambda b, pt, ids: (b,) + (0,) * len(shp))
    slots = [(g, j) for g in range(kv_h) for j in range(n_slot)]
    return pl.pallas_call(
        functools.partial(_nsa_decode_sel_kernel, n_slot=n_slot, n_pages=n_pages, page=page, hd=hd, grp=grp,
                          kv_h=kv_h, w_buf=w_buf),
        grid_spec=pltpu.PrefetchScalarGridSpec(
            num_scalar_prefetch=2,
            grid=(bs,),
            in_specs=([per_b2((H_B, hd))]
                      + [pl.BlockSpec((1, hd, page), blk_map(g, j, g)) for g, j in slots]
                      + [pl.BlockSpec((1, hd, page), blk_map(g, j, kv_h + g)) for g, j in slots]
                      + [per_b2((2 * gw, w_buf)), per_b2((1, 2 * gw)), per_b2((1, 2 * gw)),
                         per_b2((1, kv_h * LANES)), per_b2((H_B, hd)), per_b2((kv_h, H_B, hd))]),
            out_specs=per_b2((H_B, hd))),
        out_shape=jax.ShapeDtypeStruct((bs, H_B, hd), F32),
        compiler_params=_cparams(1),
        name="nsa_decode_sel",
    )(page_table, blk_ids, q3, *([cache_sel_t] * (2 * len(slots))), cache_win_t, sel_new, win_new, gate, szb3, oc)


def _out_proj_ln_kernel(*refs, n_in, alpha):
    a_refs, w_refs = refs[:n_in], refs[n_in:2 * n_in]
    x_ref, g_ref, b_ref, o_ref = refs[2 * n_in:]
    y = None
    for a_ref, w_ref in zip(a_refs, w_refs):
        t = jnp.dot(a_ref[...].astype(BF16), w_ref[...], preferred_element_type=F32)
        y = t if y is None else y + t
    o_ref[...] = _layer_norm_rows(alpha * x_ref[...] + y, g_ref[...], b_ref[...], LN_EPS)


def _out_proj_ln(a_list, w_list, x2, ln_g, ln_b, alpha):
    m, d = x2.shape
    tm = min(1024, m)
    assert m % tm == 0
    n_in = len(a_list)
    return pl.pallas_call(
        functools.partial(_out_proj_ln_kernel, n_in=n_in, alpha=alpha),
        grid=(m // tm,),
        in_specs=([pl.BlockSpec((tm, a.shape[1]), lambda i: (i, 0)) for a in a_list]
                  + [pl.BlockSpec(w.shape, lambda i: (0, 0), pipeline_mode=pl.Buffered(1)) for w in w_list]
                  + [pl.BlockSpec((tm, d), lambda i: (i, 0)),
                     pl.BlockSpec((1, d), lambda i: (0, 0)), pl.BlockSpec((1, d), lambda i: (0, 0))]),
        out_specs=pl.BlockSpec((tm, d), lambda i: (i, 0)),
        out_shape=jax.ShapeDtypeStruct((m, d), F32),
        compiler_params=_cparams(1),
        name="out_proj_ln",
    )(*a_list, *w_list, x2, ln_g.reshape(1, d), ln_b.reshape(1, d))


def _rwkv_proj_kernel(*refs, seq, tiles_per_seq):
    if seq:
        x_ref, tail_ref, shift_ref = refs[:3]
        rest = refs[3:]
    else:
        x_ref, shift_ref = refs[:2]
        rest = refs[2:]
    (mu_ref, w_ref, w1_ref, w2_ref, a1_ref, a2_ref, w0_ref, a0_ref,
     r_ref, k_ref, v_ref, lw_ref, a_ref, sz_ref) = rest
    x = x_ref[...]
    if seq:
        i = pl.program_id(0)
        first = jnp.where(i % tiles_per_seq == 0, shift_ref[0], tail_ref[SUBLANES - 1:SUBLANES, :])
        row = lax.broadcasted_iota(jnp.int32, x.shape, 0)
        x_prev = jnp.where(row == 0, first, pltpu.roll(x, 1, 0))
    else:
        x_prev = shift_ref[...]
    dx = x_prev - x
    mix = lambda n: (x + dx * mu_ref[n:n + 1, :]).astype(BF16)
    r_ref[...] = jnp.dot(mix(0), w_ref[0], preferred_element_type=F32)
    k_ref[...] = jnp.dot(mix(1), w_ref[1], preferred_element_type=F32)
    v_ref[...] = jnp.dot(mix(2), w_ref[2], preferred_element_type=F32)
    sz_ref[...] = _silu(jnp.dot(mix(3), w_ref[3], preferred_element_type=F32))
    hw = jnp.tanh(jnp.dot(mix(4), w1_ref[...], preferred_element_type=F32)).astype(BF16)
    y = -(w0_ref[...] + jnp.dot(hw, w2_ref[...], preferred_element_type=F32))
    softplus = jnp.maximum(y, 0.0) + jnp.log(1.0 + jnp.exp(-jnp.abs(y)))
    lw_ref[...] = -jnp.exp(-softplus - 0.5)
    ha = jnp.dot(mix(5), a1_ref[...], preferred_element_type=F32).astype(BF16)
    a_ref[...] = _sigmoid(a0_ref[...] + jnp.dot(ha, a2_ref[...], preferred_element_type=F32))


def _rwkv_proj(x2, shift, t_len, mu, w_rkvz, w0, w1, w2, a0, a1, a2):
    m, d = x2.shape
    seq = t_len > 1
    tm = min(512, t_len) if seq else m
    assert m % tm == 0 and (not seq or (t_len % tm == 0 and tm % SUBLANES == 0))
    tiles_per_seq = t_len // tm if seq else 1
    full = lambda a: pl.BlockSpec(a.shape, lambda i: (0,) * a.ndim, pipeline_mode=pl.Buffered(1))
    row = pl.BlockSpec((tm, d), lambda i: (i, 0))
    if seq:
        blk = tm // SUBLANES
        lead = [x2, x2, shift.reshape(-1, 1, d)]
        lead_specs = [row, pl.BlockSpec((SUBLANES, d), lambda i: (jnp.maximum(i * blk - 1, 0), 0)),
                      pl.BlockSpec((1, 1, d), lambda i: (i // tiles_per_seq, 0, 0))]
    else:
        lead = [x2, shift]
        lead_specs = [row, row]
    ws = [mu, w_rkvz.astype(BF16), w1.astype(BF16), w2.astype(BF16), a1.astype(BF16), a2.astype(BF16),
          w0.reshape(1, d), a0.reshape(1, d)]
    return pl.pallas_call(
        functools.partial(_rwkv_proj_kernel, seq=seq, tiles_per_seq=tiles_per_seq),
        grid=(m // tm,),
        in_specs=lead_specs + [full(a) for a in ws],
        out_specs=(row,) * 6,
        out_shape=(jax.ShapeDtypeStruct((m, d), F32),) * 6,
        compiler_params=_cparams(1),
        name="rwkv_proj",
    )(*lead, *ws)


WKV_CHUNK = 64
WKV_PASSES = 1
WKV_SUB = 8


def _wkv_kernel(r_ref, k_ref, v_ref, lw_ref, a_ref, sz_ref, s0_ref, kk_ref, ka_ref, rk_ref, gg_ref, gb_ref,
                yz_ref, sfin_ref, h_s, *, c_len, n_sub, n_real, n_heads, hd):
    c = pl.program_id(1)
    mm = functools.partial(_mm, pa=WKV_PASSES, pb=WKV_PASSES)
    n_pairs = n_heads // 2
    pw = 2 * hd
    d = n_heads * hd
    pairs = range(n_pairs)
    iota = lambda shape, axis: lax.broadcasted_iota(jnp.int32, shape, axis)
    bd_mask = (iota((pw, pw), 0) // hd) == (iota((pw, pw), 1) // hd)
    bd_ones = jnp.where(bd_mask, 1.0, 0.0).astype(BF16)
    eye_mask = iota((pw, pw), 0) == iota((pw, pw), 1)
    eye_pb = jnp.where(eye_mask, 1.0, 0.0).astype(BF16)
    lane_lo = iota((1, pw), 1) < hd

    @pl.when(c == 0)
    def _():
        z = jnp.zeros((hd, hd), F32)
        for p in pairs:
            h_s[p] = jnp.concatenate([jnp.concatenate([s0_ref[0, 2 * p].T, z], axis=1),
                                      jnp.concatenate([z, s0_ref[0, 2 * p + 1].T], axis=1)], axis=0)

    def segsum(x):
        xb = x.astype(BF16)
        return jnp.concatenate([jnp.dot(xb[:, t * pw:(t + 1) * pw], bd_ones, preferred_element_type=F32)
                                for t in range(d // pw)], axis=1)

    r = r_ref[0]
    k = k_ref[0]
    v = v_ref[0]
    lw = lw_ref[0]
    a = a_ref[0]
    tri = jnp.where(iota((c_len, c_len), 0) >= iota((c_len, c_len), 1), 1.0, 0.0).astype(BF16)
    col2 = iota((c_len, 2 * c_len), 1) % c_len
    low_exc2 = iota((c_len, 2 * c_len), 0) > col2
    eye2 = jnp.where(iota((c_len, 2 * c_len), 0) == col2, 1.0, 0.0)
    low_inc4 = iota((c_len, 4 * c_len), 0) >= iota((c_len, 4 * c_len), 1) % c_len
    cc_mask = (iota((2 * c_len, 2 * c_len), 0) // c_len) == (iota((2 * c_len, 2 * c_len), 1) // c_len)

    def by_head(x):
        return jnp.concatenate([jnp.where(lane_lo, x, 0.0), jnp.where(lane_lo, 0.0, x)], axis=0).astype(BF16)
    kk = k * kk_ref[...]
    kap = kk / jnp.maximum(jnp.sqrt(segsum(kk * kk)), 1e-12)
    ka = kap * a
    kmod = k * (1.0 + (a - 1.0) * ka_ref[...])
    rkr = r * kmod * rk_ref[...]
    levels = max((n_real - 1).bit_length() - 1, 0)

    lhs_b, lhs2_t, vb, v_st, l_b, l_k, m_r = [], [], [], [], [], [], []
    for s in range(n_sub):
        rs = slice(s * c_len, (s + 1) * c_len)
        lw_s = lw[rs]
        cum = _mm(tri, lw_s, NN, pb=3)
        cum_end = cum[c_len - 1:c_len, :]
        p_inv = jnp.exp(-cum)
        p_rest = jnp.exp(cum_end - cum)
        p_end = jnp.exp(cum_end)
        lhs_top = kap[rs] * jnp.exp(cum - lw_s)
        r_t = r[rs] * jnp.exp(cum)
        rhs_top = ka[rs] * p_inv
        k_t = kmod[rs] * p_inv
        ke_top = ka[rs] * p_rest
        k_e = kmod[rs] * p_rest
        for p in pairs:
            ls = slice(p * pw, (p + 1) * pw)
            lhs_b.append(jnp.concatenate([lhs_top[:, ls], r_t[:, ls]], axis=0).astype(BF16))
            v_p = v[rs, ls]
            vb.append(v_p.astype(BF16))
            v_st.append(by_head(v_p))
            lhs2 = jnp.concatenate([ke_top[:, ls], k_e[:, ls], jnp.where(eye_mask, p_end[:, ls], 0.0)],
                                   axis=0).astype(BF16)
            lhs2_t.append(lax.dot_general(eye_pb, lhs2, NT, preferred_element_type=F32).astype(BF16))
            rhs_st = jnp.concatenate([by_head(rhs_top[:, ls]), by_head(k_t[:, ls])], axis=0)
            am = mm(lhs_b[-1], rhs_st, NT)
            l_b.append(jnp.where(low_exc2, am[:c_len, :2 * c_len], 0.0))
            l_k.append(jnp.where(low_exc2, am[:c_len, 2 * c_len:], 0.0).astype(BF16))
            m_r.append(jnp.where(low_inc4, am[c_len:], 0.0).astype(BF16))
    n_sp = n_sub * n_pairs
    lkv = [mm(l_k[i], v_st[i], NN) for i in range(n_sp)]
    blockdiag = lambda x: jnp.where(cc_mask, jnp.concatenate([x, x], axis=0), 0.0).astype(BF16)
    t_inv = [eye2 - x for x in l_b]
    pwr = l_b
    for _ in range(levels):
        pwr = [mm(x, blockdiag(x), NN) for x in pwr]
        t_inv = [t + mm(t, blockdiag(x), NN) for t, x in zip(t_inv, pwr)]

    h_cur = [h_s[p] for p in pairs]
    y_rows = []
    for s in range(n_sub):
        idx = [s * n_pairs + p for p in pairs]
        hb = [h_cur[p].astype(BF16) for p in pairs]
        gh = [mm(lhs_b[i], hb[p], NN) for p, i in zip(pairs, idx)]
        u = [-mm(t_inv[i], by_head(gh[p][:c_len] + lkv[i]), NN) for p, i in zip(pairs, idx)]
        uv = [jnp.concatenate([u[p].astype(BF16), vb[i]], axis=0) for p, i in zip(pairs, idx)]
        y = [gh[p][c_len:] + mm(m_r[i], jnp.concatenate([by_head(u[p]), v_st[i]], axis=0), NN)
             for p, i in zip(pairs, idx)]
        h_cur = [jnp.where(bd_mask, mm(lhs2_t[i], jnp.concatenate([uv[p], hb[p]], axis=0), NN), 0.0)
                 for p, i in zip(pairs, idx)]
        y_rows.append(jnp.concatenate(y, axis=1))
    for p in pairs:
        h_s[p] = h_cur[p]

    y_all = jnp.concatenate(y_rows, axis=0) if n_sub > 1 else y_rows[0]
    dy = y_all - segsum(y_all) * (1.0 / hd)
    var = segsum(dy * dy) * (1.0 / hd)
    yn = dy * lax.rsqrt(var + GN_EPS) * gg_ref[...] + gb_ref[...]
    yz_ref[0] = ((yn + segsum(rkr) * v) * sz_ref[0]).astype(yz_ref.dtype)

    @pl.when(c == pl.num_programs(1) - 1)
    def _():
        for p in pairs:
            sfin_ref[0, 2 * p] = h_s[p, :hd, :hd].T
            sfin_ref[0, 2 * p + 1] = h_s[p, hd:, hd:].T


def _wkv_scan(r3, k3, v3, lw3, a3, sz3, s0, k_k, k_a, r_k, gn_g, gn_b, c_len, t_real):
    bsz, t, d = r3.shape
    n_heads = d // HD_C
    n_sub = WKV_SUB if t % (WKV_SUB * c_len) == 0 else 1
    rows = n_sub * c_len
    assert t % rows == 0 and n_heads % 2 == 0 and 2 * HD_C == LANES
    vec = lambda x: x.reshape(1, d)
    seq = pl.BlockSpec((1, rows, d), lambda b, c: (b, c, 0))
    st = pl.BlockSpec((1, n_heads, HD_C, HD_C), lambda b, c: (b, 0, 0, 0))
    par = pl.BlockSpec((1, d), lambda b, c: (0, 0))
    return pl.pallas_call(
        functools.partial(_wkv_kernel, c_len=c_len, n_sub=n_sub, n_real=min(t_real, c_len), n_heads=n_heads,
                          hd=HD_C),
        grid=(bsz, t // rows),
        in_specs=[seq] * 6 + [st] + [par] * 5,
        out_specs=(seq, st),
        out_shape=(jax.ShapeDtypeStruct((bsz, t, d), BF16),
                   jax.ShapeDtypeStruct((bsz, n_heads, HD_C, HD_C), F32)),
        scratch_shapes=[pltpu.VMEM((n_heads // 2, LANES, LANES), F32)],
        compiler_params=_cparams(2),
        name="wkv_scan",
    )(r3, k3, v3, lw3, a3, sz3, s0, vec(k_k), vec(k_a), vec(r_k), vec(gn_g), vec(gn_b))


def _even_layer(xp, xs, cache_cmp, cache_sel, cache_win, state_conv, page_table,
                w_in, conv_w, conv_b, cln_g, cln_b, wk, wv, w_out, ln_g, ln_b, alpha):
    bp, tp, d = xp.shape
    bs, ts, _ = xs.shape
    assert ts == 1
    d_a = conv_w.shape[-1]
    n_taps = conv_w.shape[0]
    l_cmp, kv_h = wk.shape
    hd = cache_cmp.shape[-1]
    d_b = H_B * hd
    d_kv = 2 * kv_h * hd
    w_bf = _prep_w_in(w_in, d_a, d_b, kv_h, hd)
    cw = conv_w.reshape(n_taps, d_a)
    w_out_bf = w_out.astype(BF16)
    kv_shape = lambda b, t: (b, t, 2, kv_h, hd)

    proj = _even_in_proj(xp.reshape(bp * tp, d), w_bf, d_a, d_b, d_kv, kv_h, hd, tp)
    u, sza, q, cmp_n, sel_n, win_n, szb, gate = proj[:8]
    if len(proj) > 8:
        new_kv = [a.reshape(bp, 2, kv_h, hd, tp).transpose(0, 4, 1, 2, 3) for a in proj[8:]]
    else:
        new_kv = [a.reshape(bp, tp, 2, kv_h, hd) for a in (cmp_n, sel_n, win_n)]
    r3 = lambda a: a.reshape(bp, tp, a.shape[-1])
    u3 = r3(u)
    ya = _conv_branch(jnp.zeros((bp, CONV_HIST, d_a), F32), u3, r3(sza), cw, conv_b, cln_g, cln_b)
    yb = _nsa_prompt(r3(q), r3(cmp_n), r3(sel_n), r3(win_n), r3(gate), r3(szb), wk, wv)
    yp = _out_proj_ln([ya.reshape(bp * tp, d_a), yb.reshape(bp * tp, d_b)], [w_out_bf[:d_a], w_out_bf[d_a:]],
                      xp.reshape(bp * tp, d), ln_g, ln_b, alpha).reshape(bp, tp, d)
    w_keep = min(WINDOW, tp)
    outs_p = (new_kv[0], new_kv[1], new_kv[2][:, tp - w_keep:], u3[:, tp - (n_taps - 1):])

    u, sza, q, cmp_s, sel_s, win_s, szb, gate = _even_in_proj(xs.reshape(bs, d), w_bf, d_a, d_b, d_kv, kv_h, hd, 1)
    ext = jnp.concatenate([state_conv, u[:, None, :]], axis=1)
    hist = jnp.pad(state_conv, ((0, 0), (CONV_HIST - (n_taps - 1), 0), (0, 0)))
    pad_rows = lambda a: jnp.pad(a[:, None, :], ((0, 0), (0, SUBLANES - 1), (0, 0)))
    ya = _conv_branch(hist, pad_rows(u), pad_rows(sza), cw, conv_b, cln_g, cln_b)[:, 0]
    n_pool, page = cache_cmp.shape[:2]
    w_buf = cache_win.shape[1]
    fm = lambda c: jnp.transpose(c, (0, 2, 3, 4, 1)).reshape(c.shape[0], d_kv, c.shape[1])
    yb = _nsa_decode2(q.astype(F32).reshape(bs, H_B, hd), fm(cache_cmp), fm(cache_sel), fm(cache_win), page_table,
                      sel_s[:, None, :], win_s[:, None, :], gate[:, None, :], szb.reshape(bs, H_B, hd), wk, wv)
    ys = _out_proj_ln([ya, yb.reshape(bs, d_b)], [w_out_bf[:d_a], w_out_bf[d_a:]], xs.reshape(bs, d),
                      ln_g, ln_b, alpha).reshape(bs, 1, d)
    ctx = jnp.concatenate([cache_win, win_s.reshape(kv_shape(bs, 1))], axis=1)
    outs_s = (cmp_s.reshape(kv_shape(bs, 1)), sel_s.reshape(kv_shape(bs, 1)),
              ctx[:, ctx.shape[1] - min(WINDOW, ctx.shape[1]):], ext[:, 1:])
    return yp, ys, outs_p, outs_s


def _odd_group(x3, shift, s0, mu, w_rkvz, w0, w1, w2, a0, a1, a2, k_k, k_a, r_k, gn_g, gn_b, w_out_bf,
               ln_g, ln_b, alpha):
    bsz, t, d = x3.shape
    x2 = x3.reshape(bsz * t, d)
    r, k, v, lw, a, sz = _rwkv_proj(x2, shift, t, mu, w_rkvz, w0, w1, w2, a0, a1, a2)
    c_len = WKV_CHUNK if t >= WKV_CHUNK else -(-t // SUBLANES) * SUBLANES
    t_pad = -(-t // c_len) * c_len
    r3 = lambda z: jnp.pad(z.reshape(bsz, t, d), ((0, 0), (0, t_pad - t), (0, 0)))
    yz, s_fin = _wkv_scan(r3(r), r3(k), r3(v), r3(lw), r3(a), r3(sz), s0, k_k, k_a, r_k.reshape(-1), gn_g, gn_b,
                          c_len, t)
    y = _out_proj_ln([yz[:, :t].reshape(bsz * t, d)], [w_out_bf], x2, ln_g, ln_b, alpha).reshape(bsz, t, d)
    return y, s_fin, x3[:, -1]


def kernel(x_prompt, x_sample, cache_cmp_kv, cache_sel_kv, cache_win_kv, state_conv, state_wkv, state_shift,
           page_table, w_in_even, conv_w, conv_b, conv_ln_g, conv_ln_b, wk_cmp, wv_cmp, w_out_even, mu_c, w_rkvz,
           w0, w1, w2, a0, a1, a2, k_k, k_a, r_k, gn_g, gn_b, w_out_odd, ln_g, ln_b):
    depth = ln_g.shape[0]
    alpha = (2 * depth) ** 0.25
    bp, _, d = x_prompt.shape
    n_heads = d // HD_C
    xp, xs = x_prompt, x_sample
    even_p, even_s, odd_p, odd_s = [], [], [], []
    for l in range(depth):
        if l % 2 == 0:
            e = l // 2
            xp, xs, o_p, o_s = _even_layer(
                xp, xs, cache_cmp_kv[e], cache_sel_kv[e], cache_win_kv[e], state_conv[e], page_table,
                w_in_even[e], conv_w[e], conv_b[e], conv_ln_g[e], conv_ln_b[e], wk_cmp[e], wv_cmp[e],
                w_out_even[e], ln_g[l], ln_b[l], alpha)
            even_p.append(o_p)
            even_s.append(o_s)
        else:
            o = l // 2
            po = (mu_c[o], w_rkvz[o], w0[o], w1[o], w2[o], a0[o], a1[o], a2[o], k_k[o], k_a[o], r_k[o],
                  gn_g[o], gn_b[o], w_out_odd[o].astype(BF16), ln_g[l], ln_b[l], alpha)
            xp, s_p, h_p = _odd_group(xp, jnp.zeros((bp, d), F32), jnp.zeros((bp, n_heads, HD_C, HD_C), F32), *po)
            xs, s_s, h_s = _odd_group(xs, state_shift[o], state_wkv[o], *po)
            odd_p.append((s_p, h_p))
            odd_s.append((s_s, h_s))
    stack = lambda items, j: jnp.stack([it[j] for it in items])
    return (xp, xs,
            stack(even_p, 0), stack(even_s, 0), stack(even_p, 1), stack(even_s, 1),
            stack(even_p, 2), stack(even_s, 2), stack(even_p, 3), stack(even_s, 3),
            stack(odd_p, 0), stack(odd_s, 0), stack(odd_p, 1), stack(odd_s, 1))
```

```python
import functools

import jax
import jax.numpy as jnp
from jax import lax
from jax.experimental import pallas as pl
from jax.experimental.pallas import tpu as pltpu

F32 = jnp.float32
BF16 = jnp.bfloat16

H_B = 8
L_SEL = 64
K_SEL = 16
WINDOW = 512
FORCE = 1e4
HD_C = 64
GN_EPS = 64e-5
LN_EPS = 1e-5
NEG = -1e30

LANES = 128
SUBLANES = 8
VMEM_LIMIT = 56 * 1024 * 1024

NT = (((1,), (1,)), ((), ()))
TN = (((0,), (0,)), ((), ()))
NN = (((1,), (0,)), ((), ()))


def _cparams(n_axes):
    return pltpu.CompilerParams(dimension_semantics=("arbitrary",) * n_axes,
                                vmem_limit_bytes=VMEM_LIMIT)


def _sigmoid(x):
    return 1.0 / (1.0 + jnp.exp(-x))


def _silu(x):
    return x * _sigmoid(x)


def _split_bf16(x, n):
    parts, rem = [], x
    for i in range(n):
        p = rem.astype(BF16)
        parts.append(p)
        if i + 1 < n:
            rem = rem - p.astype(F32)
    return parts


def _mm(a, b, dims=NN, pa=1, pb=1):
    aa = [a] if a.dtype == BF16 else _split_bf16(a, pa)
    bb = [b] if b.dtype == BF16 else _split_bf16(b, pb)
    keep = max(len(aa), len(bb))
    out = None
    for i, ai in enumerate(aa):
        for j, bj in enumerate(bb):
            if i + j < keep:
                t = lax.dot_general(ai, bj, dims, preferred_element_type=F32)
                out = t if out is None else out + t
    return out


def _layer_norm_rows(h, g, b, eps):
    mu = jnp.mean(h, axis=-1, keepdims=True)
    d = h - mu
    var = jnp.mean(d * d, axis=-1, keepdims=True)
    return d * lax.rsqrt(var + eps) * g + b


def _even_in_proj_kernel(x_ref, w_ref, u_ref, sza_ref, q_ref, cmp_ref, sel_ref, win_ref, szb_ref, gate_ref,
                         *t_refs, d_a, d_b, d_kv, q_scale):
    xb = x_ref[...].astype(BF16)

    def seg(lo, n):
        return jnp.dot(xb, w_ref[:, lo:lo + n], preferred_element_type=F32)

    o = 0
    a_val = seg(o, d_a); o += d_a
    a_glu = seg(o, d_a); o += d_a
    u_ref[...] = a_val * _sigmoid(a_glu)
    sza_ref[...] = _silu(seg(o, d_a)); o += d_a
    q_ref[...] = (seg(o, d_b) * q_scale).astype(BF16); o += d_b
    for j, ref in enumerate((cmp_ref, sel_ref, win_ref)):
        kv = seg(o, d_kv); o += d_kv
        ref[...] = kv
        if t_refs:
            t_refs[j][0] = kv.T
    szb_ref[...] = _silu(seg(o, d_b)); o += d_b
    gate_ref[...] = _sigmoid(seg(o, gate_ref.shape[-1]))


def _prep_w_in(w, d_a, d_b, kv_h, hd):
    grp = H_B // kv_h
    c_kv6 = 3 * d_a + d_b
    c_g3 = c_kv6 + 6 * kv_h * hd
    c_zb = c_g3 + 3 * H_B
    gate_blocks = []
    for g in range(kv_h):
        blk = w[:, c_g3 + g * grp * 3: c_g3 + (g + 1) * grp * 3]
        gate_blocks.append(jnp.pad(blk, ((0, 0), (0, LANES - grp * 3))))
    wn = jnp.concatenate([w[:, :c_g3], w[:, c_zb:c_zb + d_b]] + gate_blocks, axis=1)
    return wn.astype(BF16)


def _even_in_proj(x2, w_bf, d_a, d_b, d_kv, kv_h, hd, t_len):
    m, d = x2.shape
    tm = min(512, m)
    assert m % tm == 0
    n_gate = kv_h * LANES
    row = lambda n: pl.BlockSpec((tm, n), lambda i: (i, 0))
    out_shape = (jax.ShapeDtypeStruct((m, d_a), F32), jax.ShapeDtypeStruct((m, d_a), F32),
                 jax.ShapeDtypeStruct((m, d_b), BF16),
                 jax.ShapeDtypeStruct((m, d_kv), F32), jax.ShapeDtypeStruct((m, d_kv), F32),
                 jax.ShapeDtypeStruct((m, d_kv), F32),
                 jax.ShapeDtypeStruct((m, d_b), F32), jax.ShapeDtypeStruct((m, n_gate), F32))
    out_specs = (row(d_a), row(d_a), row(d_b), row(d_kv), row(d_kv), row(d_kv), row(d_b), row(n_gate))
    if t_len % tm == 0 and tm % LANES == 0:
        per_seq = t_len // tm
        t_spec = pl.BlockSpec((1, d_kv, tm), lambda i: (i // per_seq, 0, i % per_seq))
        out_shape += (jax.ShapeDtypeStruct((m // t_len, d_kv, t_len), F32),) * 3
        out_specs += (t_spec,) * 3
    return pl.pallas_call(
        functools.partial(_even_in_proj_kernel, d_a=d_a, d_b=d_b, d_kv=d_kv, q_scale=hd ** -0.5),
        grid=(m // tm,),
        in_specs=[row(d), pl.BlockSpec(w_bf.shape, lambda i: (0, 0), pipeline_mode=pl.Buffered(1))],
        out_specs=out_specs,
        out_shape=out_shape,
        compiler_params=_cparams(1),
        name="even_in_proj",
    )(x2, w_bf)


CONV_HIST = 32


def _conv_kernel(hist_ref, u_ref, sza_ref, w_ref, cb_ref, g_ref, b_ref, ya_ref, win_ref, sh_ref, conv_ref,
                 *, tq, rb, n_taps):
    i = pl.program_id(1)
    base = pl.multiple_of(i * tq, SUBLANES)
    win_ref[CONV_HIST:, :] = u_ref[0, pl.ds(base, tq), :]

    @pl.when(i == 0)
    def _():
        win_ref[:CONV_HIST, :] = hist_ref[0]

    @pl.when(i > 0)
    def _():
        win_ref[:CONV_HIST, :] = u_ref[0, pl.ds(base - CONV_HIST, CONV_HIST), :]

    w = win_ref[...]
    n = tq + CONV_HIST
    sh_ref[0] = w
    for s in range(1, SUBLANES):
        sh_ref[s] = pltpu.roll(w, n - s, 0)

    first = CONV_HIST - (n_taps - 1)
    cb = cb_ref[...]
    g = g_ref[...]
    b = b_ref[...]

    def block(k, carry):
        r0 = pl.multiple_of(k * rb, SUBLANES)
        acc = jnp.zeros((rb, u_ref.shape[-1]), F32) + cb
        for j in range(n_taps):
            a, s = divmod(first + j, SUBLANES)
            acc = acc + sh_ref[s, pl.ds(r0 + SUBLANES * a, rb), :] * w_ref[j:j + 1, :]
        conv_ref[pl.ds(r0, rb), :] = acc
        return carry

    lax.fori_loop(0, tq // rb, block, 0)
    y = _silu(_layer_norm_rows(conv_ref[...], g, b, LN_EPS)) * sza_ref[0]
    ya_ref[0] = y.astype(ya_ref.dtype)


def _conv_branch(hist, u3, sza3, conv_w, conv_b, ln_g, ln_b):
    bsz, t, d = u3.shape
    n_taps = conv_w.shape[0]
    assert n_taps - 1 <= CONV_HIST and t % SUBLANES == 0
    tq = min(256, t)
    rb = min(32, tq)
    assert t % tq == 0 and tq % rb == 0
    vec = lambda a: a.reshape(1, d)
    full = lambda shp: pl.BlockSpec(shp, lambda b, i: (0,) * len(shp))
    return pl.pallas_call(
        functools.partial(_conv_kernel, tq=tq, rb=rb, n_taps=n_taps),
        grid=(bsz, t // tq),
        in_specs=[pl.BlockSpec((1, CONV_HIST, d), lambda b, i: (b, 0, 0)),
                  pl.BlockSpec((1, t, d), lambda b, i: (b, 0, 0)),
                  pl.BlockSpec((1, tq, d), lambda b, i: (b, i, 0)),
                  full((n_taps, d)), full((1, d)), full((1, d)), full((1, d))],
        out_specs=pl.BlockSpec((1, tq, d), lambda b, i: (b, i, 0)),
        out_shape=jax.ShapeDtypeStruct((bsz, t, d), BF16),
        scratch_shapes=[pltpu.VMEM((tq + CONV_HIST, d), F32),
                        pltpu.VMEM((SUBLANES, tq + CONV_HIST, d), F32), pltpu.VMEM((tq, d), F32)],
        compiler_params=_cparams(2),
        name="conv_branch",
    )(hist, u3, sza3, conv_w, vec(conv_b), vec(ln_g), vec(ln_b))


SEL_OFF = -(2.0 ** 100)


def _nsa_prompt_kernel(q_ref, kcmp_ref, ksel_ref, kwin_ref, gate_ref, szb_ref, wpool_ref, o_ref,
                       kc_s, vc_s, ks_s, vs_s, kw_s, vw_s, m_s, l_s, acc_s, s_scr,
                       *, tq, t_len, l_cmp, hd, grp, kv_h):
    i = pl.program_id(1)
    n_cmp = t_len // l_cmp
    n_sel = t_len // L_SEL
    half = n_cmp // 2
    gw = kv_h * hd
    heads = range(kv_h * grp)
    slopes = [2.0 ** -(h + 1) for h in heads]

    @pl.when(i == 0)
    def _prep():
        x3 = kcmp_ref[0].reshape(half, 2 * l_cmp, 2 * gw)
        pooled = jnp.concatenate([jnp.sum(x3 * wpool_ref[0][None], axis=1),
                                  jnp.sum(x3 * wpool_ref[1][None], axis=1)], axis=0)
        pos = lax.broadcasted_iota(jnp.int32, (t_len, LANES), 0)
        lane = lax.broadcasted_iota(jnp.int32, (t_len, LANES), 1)
        blk = pos // L_SEL
        feat = jnp.where(lane == blk, 1.0, 0.0)
        feat = jnp.where(lane == n_sel, blk.astype(F32), feat)
        feat = jnp.where(lane == n_sel + 1, (pos - blk * L_SEL).astype(F32), feat)
        for src, dst in ((ksel_ref, ks_s), (kwin_ref, kw_s)):
            k01 = src[0, :, :gw]
            dst[0] = jnp.where(lane < hd, feat, pltpu.roll(k01, hd, 1)).astype(BF16)
            dst[1] = jnp.where(lane < hd, feat, k01).astype(BF16)
        vc_pad = jnp.concatenate([pooled[:, gw:], jnp.zeros((LANES - n_cmp, gw), F32)], axis=0) \
            if n_cmp < LANES else pooled[:, gw:]
        vc_t = vc_pad.T
        vs_t = ksel_ref[0, :, gw:].T
        vw_t = kwin_ref[0, :, gw:].T
        for gg in range(kv_h):
            kc_s[gg] = pooled[:, gg * hd:(gg + 1) * hd].astype(BF16)
            vc_s[gg] = vc_t[gg * hd:(gg + 1) * hd, :n_cmp].astype(BF16)
            vs_s[gg] = vs_t[gg * hd:(gg + 1) * hd].astype(BF16)
            vw_s[gg] = vw_t[gg * hd:(gg + 1) * hd].astype(BF16)

    t0 = i * tq
    qt = q_ref[0]

    n_io = lax.broadcasted_iota(jnp.int32, (n_cmp, tq), 0)
    t_io = lax.broadcasted_iota(jnp.int32, (n_cmp, tq), 1) + t0
    c_end = jnp.where(n_io < half, 2 * l_cmp * n_io + (l_cmp - 1), 2 * l_cmp * (n_io - half) + (2 * l_cmp - 1))
    dist_c = t_io - c_end
    valid_c = dist_c >= 0
    dist_cf = dist_c.astype(F32)
    imp = [jnp.zeros((n_sel, tq), F32) for _ in range(kv_h)]
    o_c = []
    for g in range(kv_h):
        qs = jnp.concatenate([qt[:, h * hd:(h + 1) * hd] for h in range(g * grp, (g + 1) * grp)], axis=0)
        s_t = lax.dot_general(kc_s[g], qs, NT, preferred_element_type=F32)
        for r in range(grp):
            s = jnp.where(valid_c, s_t[:, r * tq:(r + 1) * tq] - slopes[g * grp + r] * dist_cf, NEG)
            e = jnp.exp(s - jnp.max(s, axis=0, keepdims=True))
            p = jnp.where(valid_c, e / jnp.sum(e, axis=0, keepdims=True), 0.0)
            imp[g] = imp[g] + (p[:half] + p[half:])
            o_c.append(jnp.dot(vc_s[g], p.astype(BF16), preferred_element_type=F32))

    sb = lax.broadcasted_iota(jnp.int32, (n_sel, tq), 0)
    tb = (lax.broadcasted_iota(jnp.int32, (n_sel, tq), 1) + t0) // L_SEL
    forced = (sb == 0) | (sb == tb) | (sb == tb - 1)
    feat_sel = []
    for g in range(kv_h):
        val = jnp.where(sb <= tb, jnp.where(forced, FORCE, imp[g]), -jnp.inf)
        rank = jnp.zeros((n_sel, tq), F32)
        for j in range(n_sel):
            vj = val[j:j + 1, :]
            beats = jnp.where(val > vj, 1.0, jnp.where((val == vj) & (sb < j), 1.0, 0.0))
            rank = jnp.where(sb == j, jnp.sum(beats, axis=0, keepdims=True), rank)
        off_t = jnp.where(rank < min(K_SEL, n_sel), 0.0, SEL_OFF)
        feat_sel.append(jnp.concatenate([off_t, jnp.zeros((LANES - n_sel, tq), F32)], axis=0).T)

    qf = qt.astype(F32)
    lane_q = lax.broadcasted_iota(jnp.int32, (tq, LANES), 1)

    def q_aug(h, feat):
        tile = qf[:, (h // 2) * LANES:(h // 2 + 1) * LANES]
        if h % 2 == 0:
            tile = pltpu.roll(tile, hd, 1)
        x = jnp.where(lane_q < hd, feat, tile)
        x = jnp.where(lane_q == n_sel, slopes[h] * L_SEL, x)
        return jnp.where(lane_q == n_sel + 1, slopes[h], x).astype(BF16)

    d0 = (lax.broadcasted_iota(jnp.int32, (tq, tq), 1) - lax.broadcasted_iota(jnp.int32, (tq, tq), 0))

    def reset():
        m_s[...] = jnp.full(m_s.shape, NEG, F32)
        l_s[...] = jnp.zeros(l_s.shape, F32)
        acc_s[...] = jnp.zeros(acc_s.shape, F32)

    def scores(q_list, k_s, kt):
        k0 = pl.multiple_of(kt * tq, tq)
        kk = [k_s[g, pl.ds(k0, tq), :] for g in range(kv_h)]
        return [lax.dot_general(kk[r // grp], q_list[r], NT, preferred_element_type=F32) for r in heads]

    def update(s, v_s, kt, mask):
        k0 = pl.multiple_of(kt * tq, tq)
        vts = [v_s[g, :, pl.ds(k0, tq)] for g in range(kv_h)]
        if mask is not None:
            s = [jnp.where(mask, s[r], NEG) for r in heads]
        m_old = [m_s[r] for r in heads]
        m_new = [jnp.maximum(m_old[r], jnp.max(s[r], axis=0, keepdims=True)) for r in heads]
        al = [jnp.exp(m_old[r] - m_new[r]) for r in heads]
        p = [jnp.exp(s[r] - m_new[r]) for r in heads]
        for r in heads:
            m_s[r] = m_new[r]
            l_s[r] = al[r] * l_s[r] + jnp.sum(p[r], axis=0, keepdims=True)
            acc_s[r] = al[r] * acc_s[r] + jnp.dot(vts[r // grp], p[r].astype(BF16),
                                                  preferred_element_type=F32)

    def attend(q_list, k_s, v_s, lo, interior_mask, diag_mask):
        reset()

        def put(slot, kt):
            s = scores(q_list, k_s, kt)
            for r in heads:
                s_scr[slot, r] = s[r]

        def take(slot):
            return [s_scr[slot, r] for r in heads]

        odd = (i - lo) % 2

        @pl.when(odd == 1)
        def _():
            update(scores(q_list, k_s, lo), v_s, lo, interior_mask(lo))

        lo2 = lo + odd
        put(0, lo2)

        def body(j, carry):
            kt = lo2 + 2 * j
            put(1, kt + 1)
            update(take(0), v_s, kt, interior_mask(kt))
            put(0, kt + 2)
            update(take(1), v_s, kt + 1, interior_mask(kt + 1))
            return carry

        lax.fori_loop(0, (i - lo2) // 2, body, 0)
        update(take(0), v_s, i, diag_mask)
        return [acc_s[r] / l_s[r] for r in heads]

    o_s = attend([q_aug(r, feat_sel[r // grp]) for r in heads], ks_s, vs_s, 0, lambda kt: None, d0 >= 0)

    w_tiles = -(-(WINDOW - 1) // tq)
    o_w = attend([q_aug(r, 0.0) for r in heads], kw_s, vw_s, jnp.maximum(i - w_tiles, 0),
                 lambda kt: d0 + (i - kt) * tq < WINDOW, (d0 >= 0) & (d0 < WINDOW))

    gate_t = gate_ref[0].T
    outs = []
    for h in heads:
        row = (h // grp) * LANES + 3 * (h % grp)
        outs.append(gate_t[row:row + 1] * o_c[h]
                    + gate_t[row + 1:row + 2] * o_s[h]
                    + gate_t[row + 2:row + 3] * o_w[h])
    o_ref[0] = (jnp.concatenate(outs, axis=0).T * szb_ref[0]).astype(o_ref.dtype)


def _pool_weights(wk, wv, hd, halves):
    l_cmp, kv_h = wk.shape
    row = jnp.concatenate([jnp.repeat(wk, hd, axis=1), jnp.repeat(wv, hd, axis=1)], axis=1)
    if not halves:
        return row
    z = jnp.zeros_like(row)
    return jnp.stack([jnp.concatenate([row, z], axis=0), jnp.concatenate([z, row], axis=0)])


def _nsa_prompt(q3, cmp3, sel3, win3, gate3, szb3, wk, wv):
    bsz, t, d_b = q3.shape
    l_cmp, kv_h = wk.shape
    grp = H_B // kv_h
    hd = d_b // H_B
    gw = kv_h * hd
    tq = min(256, t)
    assert t % tq == 0 and t % (2 * l_cmp) == 0 and L_SEL == 2 * l_cmp and tq % L_SEL == 0
    n_cmp = t // l_cmp
    n_sel = t // L_SEL
    assert gw == LANES and 2 * hd == LANES and n_sel + 2 <= hd and n_sel % SUBLANES == 0 and n_cmp <= LANES
    wpool = _pool_weights(wk, wv, hd, halves=True)
    kv_spec = pl.BlockSpec((1, t, 2 * gw), lambda b, i: (b, 0, 0))
    tile = lambda n: pl.BlockSpec((1, tq, n), lambda b, i: (b, i, 0))
    return pl.pallas_call(
        functools.partial(_nsa_prompt_kernel, tq=tq, t_len=t, l_cmp=l_cmp, hd=hd, grp=grp, kv_h=kv_h),
        grid=(bsz, t // tq),
        in_specs=[tile(d_b), kv_spec, kv_spec, kv_spec, tile(kv_h * LANES), tile(d_b),
                  pl.BlockSpec(wpool.shape, lambda b, i: (0, 0, 0))],
        out_specs=tile(d_b),
        out_shape=jax.ShapeDtypeStruct((bsz, t, d_b), BF16),
        scratch_shapes=[pltpu.VMEM((kv_h, n_cmp, hd), BF16), pltpu.VMEM((kv_h, hd, n_cmp), BF16),
                        pltpu.VMEM((kv_h, t, LANES), BF16), pltpu.VMEM((kv_h, hd, t), BF16),
                        pltpu.VMEM((kv_h, t, LANES), BF16), pltpu.VMEM((kv_h, hd, t), BF16),
                        pltpu.VMEM((H_B, 1, tq), F32), pltpu.VMEM((H_B, 1, tq), F32),
                        pltpu.VMEM((H_B, hd, tq), F32), pltpu.VMEM((2, H_B, tq, tq), F32)],
        compiler_params=_cparams(2),
        name="nsa_prompt",
    )(q3, cmp3, sel3, win3, gate3, szb3, wpool)


def _decode_slopes():
    head = lax.broadcasted_iota(jnp.int32, (H_B, 1), 0)
    slope = jnp.zeros((H_B, 1), F32)
    for hh in range(H_B):
        slope = jnp.where(head == hh, 2.0 ** -(hh + 1), slope)
    return head, slope


def _nsa_decode_cmp_kernel(pt_ref, q_ref, *refs, pg, n_pages, page, l_cmp, hd, grp, kv_h, n_slot):
    del pt_ref
    cmp_refs = refs[:pg]
    wrow_ref, seg_ref, oc_ref, idx_ref, kcv_s = refs[pg:]
    p = pl.program_id(1)
    n_chunks = n_pages // pg
    past = n_pages * page
    n_cmp = past // l_cmp
    n_sel = past // L_SEL
    gw = kv_h * hd
    nc_pad = idx_ref.shape[-1]
    prod = jnp.concatenate([cmp_refs[j][0] * wrow_ref[...] for j in range(pg)], axis=1)
    kcv_s[p] = _mm(prod, seg_ref[...], NN)

    @pl.when(p == n_chunks - 1)
    def _compressed():
        head, slope = _decode_slopes()
        q = q_ref[0].astype(BF16)
        n_io = lax.broadcasted_iota(jnp.int32, (1, n_cmp), 1)
        dist = (past - (l_cmp * n_io + (l_cmp - 1))).astype(F32)
        pair = jnp.where(lax.broadcasted_iota(jnp.int32, (n_cmp, nc_pad), 0) // (L_SEL // l_cmp)
                         == lax.broadcasted_iota(jnp.int32, (n_cmp, nc_pad), 1), 1.0, 0.0).astype(BF16)
        c_io = lax.broadcasted_iota(jnp.int32, (1, nc_pad), 1)
        i_r = lax.broadcasted_iota(jnp.int32, (nc_pad, nc_pad), 1)
        i_c = lax.broadcasted_iota(jnp.int32, (nc_pad, nc_pad), 0)
        before = jnp.where(i_c < i_r, 1.0, 0.0).astype(BF16)
        slot = lax.broadcasted_iota(jnp.int32, (n_slot, nc_pad), 0).astype(F32)
        for g in range(kv_h):
            kc_t = jnp.concatenate([kcv_s[c, g * hd:(g + 1) * hd, :] for c in range(n_chunks)], axis=1)
            vc_t = jnp.concatenate([kcv_s[c, gw + g * hd: gw + (g + 1) * hd, :] for c in range(n_chunks)], axis=1)
            s = jnp.dot(q, kc_t.astype(BF16), preferred_element_type=F32) - slope * dist
            e = jnp.exp(s - jnp.max(s, axis=-1, keepdims=True))
            pc = e / jnp.sum(e, axis=-1, keepdims=True)
            oc_ref[0, g] = lax.dot_general(pc.astype(BF16), vc_t.astype(BF16), NT, preferred_element_type=F32)
            in_grp = (head >= g * grp) & (head < (g + 1) * grp)
            imp = _mm(jnp.sum(jnp.where(in_grp, pc, 0.0), axis=0, keepdims=True), pair, NN, pa=3)
            forced = (c_io == 0) | (c_io == n_sel - 1)
            imp = jnp.where(forced, FORCE, imp)
            imp = jnp.where(c_io < n_sel, imp, -jnp.inf)
            v_r = jnp.broadcast_to(imp, (nc_pad, nc_pad))
            v_c = v_r.T
            beats = jnp.where(v_c > v_r, 1.0, jnp.where((v_c == v_r) & (i_c < i_r), 1.0, 0.0))
            rank = jnp.sum(beats, axis=0, keepdims=True) + jnp.where(imp < FORCE, 1.0, 0.0)
            chosen = jnp.where((rank < min(K_SEL, n_sel + 1)) & (c_io < n_sel), 1.0, 0.0)
            n_before = jnp.dot(jnp.broadcast_to(chosen, (SUBLANES, nc_pad)).astype(BF16), before,
                               preferred_element_type=F32)[:1]
            hit = jnp.where((n_before == slot) & (chosen > 0.5), 1.0, 0.0)
            ids = jnp.sum(hit * c_io.astype(F32), axis=-1, keepdims=True)
            ids = jnp.where(jnp.sum(hit, axis=-1, keepdims=True) > 0.5, ids, -1.0)
            idx_ref[0, g] = jnp.broadcast_to(ids, (n_slot, nc_pad)).astype(jnp.int32)


def _nsa_decode_sel_kernel(pt_ref, idx_ref, q_ref, *refs, n_slot, n_pages, page, hd, grp, kv_h, w_buf):
    del pt_ref
    n_blk = kv_h * n_slot
    k_refs = refs[:n_blk]
    v_refs = refs[n_blk:2 * n_blk]
    kwin_ref, seln_ref, winn_ref, gate_ref, szb_ref, oc_ref, o_ref = refs[2 * n_blk:]
    b = pl.program_id(0)
    past = n_pages * page
    gw = kv_h * hd
    per_page = page // L_SEL
    head, slope = _decode_slopes()
    qf = q_ref[0]
    q = qf.astype(BF16)
    gate = gate_ref[0]
    lane = lax.broadcasted_iota(jnp.int32, (1, page), 1)
    i_io = lax.broadcasted_iota(jnp.int32, (1, w_buf), 1)
    dist_w = w_buf - i_io
    valid_w = dist_w < WINDOW
    o = jnp.zeros((H_B, hd), F32)
    for g in range(kv_h):
        dists, valids = [], []
        for j in range(n_slot):
            blk = idx_ref[b, g, j]
            safe = jnp.maximum(blk, 0)
            dists.append(past - ((safe // per_page) * page + lane))
            valids.append(lane // L_SEL == jnp.where(blk >= 0, safe % per_page, -1))
        dist = jnp.concatenate(dists, axis=1).astype(F32)
        valid = jnp.concatenate(valids, axis=1)
        k_t = jnp.concatenate([k_refs[g * n_slot + j][0] for j in range(n_slot)], axis=1).astype(BF16)
        v_t = jnp.concatenate([v_refs[g * n_slot + j][0] for j in range(n_slot)], axis=1).astype(BF16)
        s = jnp.where(valid, jnp.dot(q, k_t, preferred_element_type=F32) - slope * dist, NEG)
        kn = seln_ref[0][:, g * hd:(g + 1) * hd]
        vn = seln_ref[0][:, gw + g * hd: gw + (g + 1) * hd]
        s_n = jnp.sum(qf * kn, axis=-1, keepdims=True)
        m = jnp.maximum(jnp.max(s, axis=-1, keepdims=True), s_n)
        pe = jnp.where(valid, jnp.exp(s - m), 0.0)
        pn = jnp.exp(s_n - m)
        o_s = ((lax.dot_general(pe.astype(BF16), v_t, NT, preferred_element_type=F32) + pn * vn)
               / (jnp.sum(pe, axis=-1, keepdims=True) + pn))
        kw_t = kwin_ref[0, g * hd:(g + 1) * hd, :].astype(BF16)
        vw_t = kwin_ref[0, gw + g * hd: gw + (g + 1) * hd, :].astype(BF16)
        s = jnp.dot(q, kw_t, preferred_element_type=F32) - slope * dist_w.astype(F32)
        s = jnp.where(valid_w, s, NEG)
        kwn = winn_ref[0][:, g * hd:(g + 1) * hd]
        vwn = winn_ref[0][:, gw + g * hd: gw + (g + 1) * hd]
        s_wn = jnp.sum(qf * kwn, axis=-1, keepdims=True)
        mw = jnp.maximum(jnp.max(s, axis=-1, keepdims=True), s_wn)
        pw = jnp.where(valid_w, jnp.exp(s - mw), 0.0)
        pwn = jnp.exp(s_wn - mw)
        o_w = ((lax.dot_general(pw.astype(BF16), vw_t, NT, preferred_element_type=F32) + pwn * vwn)
               / (jnp.sum(pw, axis=-1, keepdims=True) + pwn))
        gcol = []
        for c in range(3):
            col = jnp.zeros((H_B, 1), F32)
            for r in range(grp):
                ln = g * LANES + 3 * r + c
                col = jnp.where(head == g * grp + r, gate[:, ln:ln + 1], col)
            gcol.append(col)
        og = gcol[0] * oc_ref[0, g] + gcol[1] * o_s + gcol[2] * o_w
        o = jnp.where((head >= g * grp) & (head < (g + 1) * grp), og, o)
    o_ref[0] = o * szb_ref[0]


def _nsa_decode2(q3, cache_cmp_t, cache_sel_t, cache_win_t, page_table, sel_new, win_new, gate, szb3, wk, wv):
    bs, _, hd = q3.shape
    l_cmp, kv_h = wk.shape
    grp = H_B // kv_h
    gw = kv_h * hd
    page = cache_cmp_t.shape[-1]
    n_pages = page_table.shape[1]
    w_buf = cache_win_t.shape[-1]
    per_page = page // l_cmp
    pg = LANES // per_page
    assert page == LANES and page % L_SEL == 0 and page % l_cmp == 0 and n_pages % pg == 0
    n_chunks = n_pages // pg
    n_sel = n_pages * page // L_SEL
    n_slot = min(K_SEL, n_sel + 1) - 1
    assert n_slot >= 1
    nc_pad = -(-n_sel // LANES) * LANES
    wrow = jnp.tile(_pool_weights(wk, wv, hd, halves=False), (per_page, 1)).T
    pos = lax.broadcasted_iota(jnp.int32, (pg, page, LANES), 1)
    pj = lax.broadcasted_iota(jnp.int32, (pg, page, LANES), 0)
    ln = lax.broadcasted_iota(jnp.int32, (pg, page, LANES), 2)
    seg = (ln == pj * per_page + pos // l_cmp).astype(BF16).reshape(pg * page, LANES)

    per_b = lambda shp: pl.BlockSpec((1,) + shp, lambda b, p, pt: (b,) + (0,) * len(shp))
    oc, idx = pl.pallas_call(
        functools.partial(_nsa_decode_cmp_kernel, pg=pg, n_pages=n_pages, page=page, l_cmp=l_cmp, hd=hd, grp=grp,
                          kv_h=kv_h, n_slot=n_slot),
        grid_spec=pltpu.PrefetchScalarGridSpec(
            num_scalar_prefetch=1,
            grid=(bs, n_chunks),
            in_specs=([per_b((H_B, hd))]
                      + [pl.BlockSpec((1, 2 * gw, page), (lambda j: lambda b, p, pt: (pt[b, p * pg + j], 0, 0))(j))
                         for j in range(pg)]
                      + [pl.BlockSpec((2 * gw, page), lambda b, p, pt: (0, 0)),
                         pl.BlockSpec(seg.shape, lambda b, p, pt: (0, 0))]),
            out_specs=(per_b((kv_h, H_B, hd)), per_b((kv_h, n_slot, nc_pad))),
            scratch_shapes=[pltpu.VMEM((n_chunks, 2 * gw, LANES), F32)]),
        out_shape=(jax.ShapeDtypeStruct((bs, kv_h, H_B, hd), F32),
                   jax.ShapeDtypeStruct((bs, kv_h, n_slot, nc_pad), jnp.int32)),
        compiler_params=_cparams(2),
        name="nsa_decode_cmp",
    )(page_table, q3, *([cache_cmp_t] * pg), wrow, seg)
    blk_ids = idx[:, :, :, 0]

    sel_per_page = page // L_SEL

    def blk_map(g, j, row_blk):
        return lambda b, pt, ids: (pt[b, jnp.maximum(ids[b, g, j], 0) // sel_per_page], row_blk, 0)

    per_b2 = lambda shp: pl.BlockSpec((1,) + shp, lambda b, pt, ids: (b,) + (0,) * len(shp))
    slots = [(g, j) for g in range(kv_h) for j in range(n_slot)]
    return pl.pallas_call(
        functools.partial(_nsa_decode_sel_kernel, n_slot=n_slot, n_pages=n_pages, page=page, hd=hd, grp=grp,
                          kv_h=kv_h, w_buf=w_buf),
        grid_spec=pltpu.PrefetchScalarGridSpec(
            num_scalar_prefetch=2,
            grid=(bs,),
            in_specs=([per_b2((H_B, hd))]
                      + [pl.BlockSpec((1, hd, page), blk_map(g, j, g)) for g, j in slots]
                      + [pl.BlockSpec((1, hd, page), blk_map(g, j, kv_h + g)) for g, j in slots]
                      + [per_b2((2 * gw, w_buf)), per_b2((1, 2 * gw)), per_b2((1, 2 * gw)),
                         per_b2((1, kv_h * LANES)), per_b2((H_B, hd)), per_b2((kv_h, H_B, hd))]),
            out_specs=per_b2((H_B, hd))),
        out_shape=jax.ShapeDtypeStruct((bs, H_B, hd), F32),
        compiler_params=_cparams(1),
        name="nsa_decode_sel",
    )(page_table, blk_ids, q3, *([cache_sel_t] * (2 * len(slots))), cache_win_t, sel_new, win_new, gate, szb3, oc)


def _out_proj_ln_kernel(*refs, n_in, alpha):
    a_refs, w_refs = refs[:n_in], refs[n_in:2 * n_in]
    x_ref, g_ref, b_ref, o_ref = refs[2 * n_in:]
    y = None
    for a_ref, w_ref in zip(a_refs, w_refs):
        t = jnp.dot(a_ref[...].astype(BF16), w_ref[...], preferred_element_type=F32)
        y = t if y is None else y + t
    o_ref[...] = _layer_norm_rows(alpha * x_ref[...] + y, g_ref[...], b_ref[...], LN_EPS)


def _out_proj_ln(a_list, w_list, x2, ln_g, ln_b, alpha):
    m, d = x2.shape
    tm = min(1024, m)
    assert m % tm == 0
    n_in = len(a_list)
    return pl.pallas_call(
        functools.partial(_out_proj_ln_kernel, n_in=n_in, alpha=alpha),
        grid=(m // tm,),
        in_specs=([pl.BlockSpec((tm, a.shape[1]), lambda i: (i, 0)) for a in a_list]
                  + [pl.BlockSpec(w.shape, lambda i: (0, 0), pipeline_mode=pl.Buffered(1)) for w in w_list]
                  + [pl.BlockSpec((tm, d), lambda i: (i, 0)),
                     pl.BlockSpec((1, d), lambda i: (0, 0)), pl.BlockSpec((1, d), lambda i: (0, 0))]),
        out_specs=pl.BlockSpec((tm, d), lambda i: (i, 0)),
        out_shape=jax.ShapeDtypeStruct((m, d), F32),
        compiler_params=_cparams(1),
        name="out_proj_ln",
    )(*a_list, *w_list, x2, ln_g.reshape(1, d), ln_b.reshape(1, d))


def _rwkv_proj_kernel(*refs, seq, tiles_per_seq):
    if seq:
        x_ref, tail_ref, shift_ref = refs[:3]
        rest = refs[3:]
    else:
        x_ref, shift_ref = refs[:2]
        rest = refs[2:]
    (mu_ref, w_ref, w1_ref, w2_ref, a1_ref, a2_ref, w0_ref, a0_ref,
     r_ref, k_ref, v_ref, lw_ref, a_ref, sz_ref) = rest
    x = x_ref[...]
    if seq:
        i = pl.program_id(0)
        first = jnp.where(i % tiles_per_seq == 0, shift_ref[0], tail_ref[SUBLANES - 1:SUBLANES, :])
        row = lax.broadcasted_iota(jnp.int32, x.shape, 0)
        x_prev = jnp.where(row == 0, first, pltpu.roll(x, 1, 0))
    else:
        x_prev = shift_ref[...]
    dx = x_prev - x
    mix = lambda n: (x + dx * mu_ref[n:n + 1, :]).astype(BF16)
    r_ref[...] = jnp.dot(mix(0), w_ref[0], preferred_element_type=F32)
    k_ref[...] = jnp.dot(mix(1), w_ref[1], preferred_element_type=F32)
    v_ref[...] = jnp.dot(mix(2), w_ref[2], preferred_element_type=F32)
    sz_ref[...] = _silu(jnp.dot(mix(3), w_ref[3], preferred_element_type=F32))
    hw = jnp.tanh(jnp.dot(mix(4), w1_ref[...], preferred_element_type=F32)).astype(BF16)
    y = -(w0_ref[...] + jnp.dot(hw, w2_ref[...], preferred_element_type=F32))
    softplus = jnp.maximum(y, 0.0) + jnp.log(1.0 + jnp.exp(-jnp.abs(y)))
    lw_ref[...] = -jnp.exp(-softplus - 0.5)
    ha = jnp.dot(mix(5), a1_ref[...], preferred_element_type=F32).astype(BF16)
    a_ref[...] = _sigmoid(a0_ref[...] + jnp.dot(ha, a2_ref[...], preferred_element_type=F32))


def _rwkv_proj(x2, shift, t_len, mu, w_rkvz, w0, w1, w2, a0, a1, a2):
    m, d = x2.shape
    seq = t_len > 1
    tm = min(512, t_len) if seq else m
    assert m % tm == 0 and (not seq or (t_len % tm == 0 and tm % SUBLANES == 0))
    tiles_per_seq = t_len // tm if seq else 1
    full = lambda a: pl.BlockSpec(a.shape, lambda i: (0,) * a.ndim, pipeline_mode=pl.Buffered(1))
    row = pl.BlockSpec((tm, d), lambda i: (i, 0))
    if seq:
        blk = tm // SUBLANES
        lead = [x2, x2, shift.reshape(-1, 1, d)]
        lead_specs = [row, pl.BlockSpec((SUBLANES, d), lambda i: (jnp.maximum(i * blk - 1, 0), 0)),
                      pl.BlockSpec((1, 1, d), lambda i: (i // tiles_per_seq, 0, 0))]
    else:
        lead = [x2, shift]
        lead_specs = [row, row]
    ws = [mu, w_rkvz.astype(BF16), w1.astype(BF16), w2.astype(BF16), a1.astype(BF16), a2.astype(BF16),
          w0.reshape(1, d), a0.reshape(1, d)]
    return pl.pallas_call(
        functools.partial(_rwkv_proj_kernel, seq=seq, tiles_per_seq=tiles_per_seq),
        grid=(m // tm,),
        in_specs=lead_specs + [full(a) for a in ws],
        out_specs=(row,) * 6,
        out_shape=(jax.ShapeDtypeStruct((m, d), F32),) * 6,
        compiler_params=_cparams(1),
        name="rwkv_proj",
    )(*lead, *ws)


WKV_CHUNK = 64
WKV_PASSES = 1
WKV_SUB = 8


def _wkv_kernel(r_ref, k_ref, v_ref, lw_ref, a_ref, sz_ref, s0_ref, kk_ref, ka_ref, rk_ref, gg_ref, gb_ref,
                yz_ref, sfin_ref, h_s, *, c_len, n_sub, n_real, n_heads, hd):
    c = pl.program_id(1)
    mm = functools.partial(_mm, pa=WKV_PASSES, pb=WKV_PASSES)
    n_pairs = n_heads // 2
    pw = 2 * hd
    d = n_heads * hd
    pairs = range(n_pairs)
    iota = lambda shape, axis: lax.broadcasted_iota(jnp.int32, shape, axis)
    bd_mask = (iota((pw, pw), 0) // hd) == (iota((pw, pw), 1) // hd)
    bd_ones = jnp.where(bd_mask, 1.0, 0.0).astype(BF16)
    eye_mask = iota((pw, pw), 0) == iota((pw, pw), 1)
    eye_pb = jnp.where(eye_mask, 1.0, 0.0).astype(BF16)
    lane_lo = iota((1, pw), 1) < hd

    @pl.when(c == 0)
    def _():
        z = jnp.zeros((hd, hd), F32)
        for p in pairs:
            h_s[p] = jnp.concatenate([jnp.concatenate([s0_ref[0, 2 * p].T, z], axis=1),
                                      jnp.concatenate([z, s0_ref[0, 2 * p + 1].T], axis=1)], axis=0)

    def segsum(x):
        xb = x.astype(BF16)
        return jnp.concatenate([jnp.dot(xb[:, t * pw:(t + 1) * pw], bd_ones, preferred_element_type=F32)
                                for t in range(d // pw)], axis=1)

    r = r_ref[0]
    k = k_ref[0]
    v = v_ref[0]
    lw = lw_ref[0]
    a = a_ref[0]
    tri = jnp.where(iota((c_len, c_len), 0) >= iota((c_len, c_len), 1), 1.0, 0.0).astype(BF16)
    col2 = iota((c_len, 2 * c_len), 1) % c_len
    low_exc2 = iota((c_len, 2 * c_len), 0) > col2
    eye2 = jnp.where(iota((c_len, 2 * c_len), 0) == col2, 1.0, 0.0)
    low_inc4 = iota((c_len, 4 * c_len), 0) >= iota((c_len, 4 * c_len), 1) % c_len
    cc_mask = (iota((2 * c_len, 2 * c_len), 0) // c_len) == (iota((2 * c_len, 2 * c_len), 1) // c_len)

    def by_head(x):
        return jnp.concatenate([jnp.where(lane_lo, x, 0.0), jnp.where(lane_lo, 0.0, x)], axis=0).astype(BF16)
    kk = k * kk_ref[...]
    kap = kk / jnp.maximum(jnp.sqrt(segsum(kk * kk)), 1e-12)
    ka = kap * a
    kmod = k * (1.0 + (a - 1.0) * ka_ref[...])
    rkr = r * kmod * rk_ref[...]
    levels = max((n_real - 1).bit_length() - 1, 0)

    lhs_b, lhs2_t, vb, v_st, l_b, l_k, m_r = [], [], [], [], [], [], []
    for s in range(n_sub):
        rs = slice(s * c_len, (s + 1) * c_len)
        lw_s = lw[rs]
        cum = _mm(tri, lw_s, NN, pb=3)
        cum_end = cum[c_len - 1:c_len, :]
        p_inv = jnp.exp(-cum)
        p_rest = jnp.exp(cum_end - cum)
        p_end = jnp.exp(cum_end)
        lhs_top = kap[rs] * jnp.exp(cum - lw_s)
        r_t = r[rs] * jnp.exp(cum)
        rhs_top = ka[rs] * p_inv
        k_t = kmod[rs] * p_inv
        ke_top = ka[rs] * p_rest
        k_e = kmod[rs] * p_rest
        for p in pairs:
            ls = slice(p * pw, (p + 1) * pw)
            lhs_b.append(jnp.concatenate([lhs_top[:, ls], r_t[:, ls]], axis=0).astype(BF16))
            v_p = v[rs, ls]
            vb.append(v_p.astype(BF16))
            v_st.append(by_head(v_p))
            lhs2 = jnp.concatenate([ke_top[:, ls], k_e[:, ls], jnp.where(eye_mask, p_end[:, ls], 0.0)],
                                   axis=0).astype(BF16)
            lhs2_t.append(lax.dot_general(eye_pb, lhs2, NT, preferred_element_type=F32).astype(BF16))
            rhs_st = jnp.concatenate([by_head(rhs_top[:, ls]), by_head(k_t[:, ls])], axis=0)
            am = mm(lhs_b[-1], rhs_st, NT)
            l_b.append(jnp.where(low_exc2, am[:c_len, :2 * c_len], 0.0))
            l_k.append(jnp.where(low_exc2, am[:c_len, 2 * c_len:], 0.0).astype(BF16))
            m_r.append(jnp.where(low_inc4, am[c_len:], 0.0).astype(BF16))
    n_sp = n_sub * n_pairs
    lkv = [mm(l_k[i], v_st[i], NN) for i in range(n_sp)]
    blockdiag = lambda x: jnp.where(cc_mask, jnp.concatenate([x, x], axis=0), 0.0).astype(BF16)
    t_inv = [eye2 - x for x in l_b]
    pwr = l_b
    for _ in range(levels):
        pwr = [mm(x, blockdiag(x), NN) for x in pwr]
        t_inv = [t + mm(t, blockdiag(x), NN) for t, x in zip(t_inv, pwr)]

    h_cur = [h_s[p] for p in pairs]
    y_rows = []
    for s in range(n_sub):
        idx = [s * n_pairs + p for p in pairs]
        hb = [h_cur[p].astype(BF16) for p in pairs]
        gh = [mm(lhs_b[i], hb[p], NN) for p, i in zip(pairs, idx)]
        u = [-mm(t_inv[i], by_head(gh[p][:c_len] + lkv[i]), NN) for p, i in zip(pairs, idx)]
        uv = [jnp.concatenate([u[p].astype(BF16), vb[i]], axis=0) for p, i in zip(pairs, idx)]
        y = [gh[p][c_len:] + mm(m_r[i], jnp.concatenate([by_head(u[p]), v_st[i]], axis=0), NN)
             for p, i in zip(pairs, idx)]
        h_cur = [jnp.where(bd_mask, mm(lhs2_t[i], jnp.concatenate([uv[p], hb[p]], axis=0), NN), 0.0)
                 for p, i in zip(pairs, idx)]
        y_rows.append(jnp.concatenate(y, axis=1))
    for p in pairs:
        h_s[p] = h_cur[p]

    y_all = jnp.concatenate(y_rows, axis=0) if n_sub > 1 else y_rows[0]
    dy = y_all - segsum(y_all) * (1.0 / hd)
    var = segsum(dy * dy) * (1.0 / hd)
    yn = dy * lax.rsqrt(var + GN_EPS) * gg_ref[...] + gb_ref[...]
    yz_ref[0] = ((yn + segsum(rkr) * v) * sz_ref[0]).astype(yz_ref.dtype)

    @pl.when(c == pl.num_programs(1) - 1)
    def _():
        for p in pairs:
            sfin_ref[0, 2 * p] = h_s[p, :hd, :hd].T
            sfin_ref[0, 2 * p + 1] = h_s[p, hd:, hd:].T


def _wkv_scan(r3, k3, v3, lw3, a3, sz3, s0, k_k, k_a, r_k, gn_g, gn_b, c_len, t_real):
    bsz, t, d = r3.shape
    n_heads = d // HD_C
    n_sub = WKV_SUB if t % (WKV_SUB * c_len) == 0 else 1
    rows = n_sub * c_len
    assert t % rows == 0 and n_heads % 2 == 0 and 2 * HD_C == LANES
    vec = lambda x: x.reshape(1, d)
    seq = pl.BlockSpec((1, rows, d), lambda b, c: (b, c, 0))
    st = pl.BlockSpec((1, n_heads, HD_C, HD_C), lambda b, c: (b, 0, 0, 0))
    par = pl.BlockSpec((1, d), lambda b, c: (0, 0))
    return pl.pallas_call(
        functools.partial(_wkv_kernel, c_len=c_len, n_sub=n_sub, n_real=min(t_real, c_len), n_heads=n_heads,
                          hd=HD_C),
        grid=(bsz, t // rows),
        in_specs=[seq] * 6 + [st] + [par] * 5,
        out_specs=(seq, st),
        out_shape=(jax.ShapeDtypeStruct((bsz, t, d), BF16),
                   jax.ShapeDtypeStruct((bsz, n_heads, HD_C, HD_C), F32)),
        scratch_shapes=[pltpu.VMEM((n_heads // 2, LANES, LANES), F32)],
        compiler_params=_cparams(2),
        name="wkv_scan",
    )(r3, k3, v3, lw3, a3, sz3, s0, vec(k_k), vec(k_a), vec(r_k), vec(gn_g), vec(gn_b))


def _even_layer(xp, xs, cache_cmp, cache_sel, cache_win, state_conv, page_table,
                w_in, conv_w, conv_b, cln_g, cln_b, wk, wv, w_out, ln_g, ln_b, alpha):
    bp, tp, d = xp.shape
    bs, ts, _ = xs.shape
    assert ts == 1
    d_a = conv_w.shape[-1]
    n_taps = conv_w.shape[0]
    l_cmp, kv_h = wk.shape
    hd = cache_cmp.shape[-1]
    d_b = H_B * hd
    d_kv = 2 * kv_h * hd
    w_bf = _prep_w_in(w_in, d_a, d_b, kv_h, hd)
    cw = conv_w.reshape(n_taps, d_a)
    w_out_bf = w_out.astype(BF16)
    kv_shape = lambda b, t: (b, t, 2, kv_h, hd)

    proj = _even_in_proj(xp.reshape(bp * tp, d), w_bf, d_a, d_b, d_kv, kv_h, hd, tp)
    u, sza, q, cmp_n, sel_n, win_n, szb, gate = proj[:8]
    if len(proj) > 8:
        new_kv = [a.reshape(bp, 2, kv_h, hd, tp).transpose(0, 4, 1, 2, 3) for a in proj[8:]]
    else:
        new_kv = [a.reshape(bp, tp, 2, kv_h, hd) for a in (cmp_n, sel_n, win_n)]
    r3 = lambda a: a.reshape(bp, tp, a.shape[-1])
    u3 = r3(u)
    ya = _conv_branch(jnp.zeros((bp, CONV_HIST, d_a), F32), u3, r3(sza), cw, conv_b, cln_g, cln_b)
    yb = _nsa_prompt(r3(q), r3(cmp_n), r3(sel_n), r3(win_n), r3(gate), r3(szb), wk, wv)
    yp = _out_proj_ln([ya.reshape(bp * tp, d_a), yb.reshape(bp * tp, d_b)], [w_out_bf[:d_a], w_out_bf[d_a:]],
                      xp.reshape(bp * tp, d), ln_g, ln_b, alpha).reshape(bp, tp, d)
    w_keep = min(WINDOW, tp)
    outs_p = (new_kv[0], new_kv[1], new_kv[2][:, tp - w_keep:], u3[:, tp - (n_taps - 1):])

    u, sza, q, cmp_s, sel_s, win_s, szb, gate = _even_in_proj(xs.reshape(bs, d), w_bf, d_a, d_b, d_kv, kv_h, hd, 1)
    ext = jnp.concatenate([state_conv, u[:, None, :]], axis=1)
    hist = jnp.pad(state_conv, ((0, 0), (CONV_HIST - (n_taps - 1), 0), (0, 0)))
    pad_rows = lambda a: jnp.pad(a[:, None, :], ((0, 0), (0, SUBLANES - 1), (0, 0)))
    ya = _conv_branch(hist, pad_rows(u), pad_rows(sza), cw, conv_b, cln_g, cln_b)[:, 0]
    n_pool, page = cache_cmp.shape[:2]
    w_buf = cache_win.shape[1]
    fm = lambda c: jnp.transpose(c, (0, 2, 3, 4, 1)).reshape(c.shape[0], d_kv, c.shape[1])
    yb = _nsa_decode2(q.astype(F32).reshape(bs, H_B, hd), fm(cache_cmp), fm(cache_sel), fm(cache_win), page_table,
                      sel_s[:, None, :], win_s[:, None, :], gate[:, None, :], szb.reshape(bs, H_B, hd), wk, wv)
    ys = _out_proj_ln([ya, yb.reshape(bs, d_b)], [w_out_bf[:d_a], w_out_bf[d_a:]], xs.reshape(bs, d),
                      ln_g, ln_b, alpha).reshape(bs, 1, d)
    ctx = jnp.concatenate([cache_win, win_s.reshape(kv_shape(bs, 1))], axis=1)
    outs_s = (cmp_s.reshape(kv_shape(bs, 1)), sel_s.reshape(kv_shape(bs, 1)),
              ctx[:, ctx.shape[1] - min(WINDOW, ctx.shape[1]):], ext[:, 1:])
    return yp, ys, outs_p, outs_s


def _odd_group(x3, shift, s0, mu, w_rkvz, w0, w1, w2, a0, a1, a2, k_k, k_a, r_k, gn_g, gn_b, w_out_bf,
               ln_g, ln_b, alpha):
    bsz, t, d = x3.shape
    x2 = x3.reshape(bsz * t, d)
    r, k, v, lw, a, sz = _rwkv_proj(x2, shift, t, mu, w_rkvz, w0, w1, w2, a0, a1, a2)
    c_len = WKV_CHUNK if t >= WKV_CHUNK else -(-t // SUBLANES) * SUBLANES
    t_pad = -(-t // c_len) * c_len
    r3 = lambda z: jnp.pad(z.reshape(bsz, t, d), ((0, 0), (0, t_pad - t), (0, 0)))
    yz, s_fin = _wkv_scan(r3(r), r3(k), r3(v), r3(lw), r3(a), r3(sz), s0, k_k, k_a, r_k.reshape(-1), gn_g, gn_b,
                          c_len, t)
    y = _out_proj_ln([yz[:, :t].reshape(bsz * t, d)], [w_out_bf], x2, ln_g, ln_b, alpha).reshape(bsz, t, d)
    return y, s_fin, x3[:, -1]


def kernel(x_prompt, x_sample, cache_cmp_kv, cache_sel_kv, cache_win_kv, state_conv, state_wkv, state_shift,
           page_table, w_in_even, conv_w, conv_b, conv_ln_g, conv_ln_b, wk_cmp, wv_cmp, w_out_even, mu_c, w_rkvz,
           w0, w1, w2, a0, a1, a2, k_k, k_a, r_k, gn_g, gn_b, w_out_odd, ln_g, ln_b):
    depth = ln_g.shape[0]
    alpha = (2 * depth) ** 0.25
    bp, _, d = x_prompt.shape
    n_heads = d // HD_C
    xp, xs = x_prompt, x_sample
    even_p, even_s, odd_p, odd_s = [], [], [], []
    for l in range(depth):
        if l % 2 == 0:
            e = l // 2
            xp, xs, o_p, o_s = _even_layer(
                xp, xs, cache_cmp_kv[e], cache_sel_kv[e], cache_win_kv[e], state_conv[e], page_table,
                w_in_even[e], conv_w[e], conv_b[e], conv_ln_g[e], conv_ln_b[e], wk_cmp[e], wv_cmp[e],
                w_out_even[e], ln_g[l], ln_b[l], alpha)
            even_p.append(o_p)
            even_s.append(o_s)
        else:
            o = l // 2
            po = (mu_c[o], w_rkvz[o], w0[o], w1[o], w2[o], a0[o], a1[o], a2[o], k_k[o], k_a[o], r_k[o],
                  gn_g[o], gn_b[o], w_out_odd[o].astype(BF16), ln_g[l], ln_b[l], alpha)
            xp, s_p, h_p = _odd_group(xp, jnp.zeros((bp, d), F32), jnp.zeros((bp, n_heads, HD_C, HD_C), F32), *po)
            xs, s_s, h_s = _odd_group(xs, state_shift[o], state_wkv[o], *po)
            odd_p.append((s_p, h_p))
            odd_s.append((s_s, h_s))
    stack = lambda items, j: jnp.stack([it[j] for it in items])
    return (xp, xs,
            stack(even_p, 0), stack(even_s, 0), stack(even_p, 1), stack(even_s, 1),
            stack(even_p, 2), stack(even_s, 2), stack(even_p, 3), stack(even_s, 3),
            stack(odd_p, 0), stack(odd_s, 0), stack(odd_p, 1), stack(odd_s, 1))
```

```python
import functools

import jax
import jax.numpy as jnp
from jax import lax
from jax.experimental import pallas as pl
from jax.experimental.pallas import tpu as pltpu

F32 = jnp.float32
BF16 = jnp.bfloat16

H_B = 8
L_SEL = 64
K_SEL = 16
WINDOW = 512
FORCE = 1e4
HD_C = 64
GN_EPS = 64e-5
LN_EPS = 1e-5
NEG = -1e30

LANES = 128
SUBLANES = 8
VMEM_LIMIT = 56 * 1024 * 1024

NT = (((1,), (1,)), ((), ()))
TN = (((0,), (0,)), ((), ()))
NN = (((1,), (0,)), ((), ()))


def _cparams(n_axes):
    return pltpu.CompilerParams(dimension_semantics=("arbitrary",) * n_axes,
                                vmem_limit_bytes=VMEM_LIMIT)


def _sigmoid(x):
    return 1.0 / (1.0 + jnp.exp(-x))


def _silu(x):
    return x * _sigmoid(x)


def _split_bf16(x, n):
    parts, rem = [], x
    for i in range(n):
        p = rem.astype(BF16)
        parts.append(p)
        if i + 1 < n:
            rem = rem - p.astype(F32)
    return parts


def _mm(a, b, dims=NN, pa=1, pb=1):
    aa = [a] if a.dtype == BF16 else _split_bf16(a, pa)
    bb = [b] if b.dtype == BF16 else _split_bf16(b, pb)
    keep = max(len(aa), len(bb))
    out = None
    for i, ai in enumerate(aa):
        for j, bj in enumerate(bb):
            if i + j < keep:
                t = lax.dot_general(ai, bj, dims, preferred_element_type=F32)
                out = t if out is None else out + t
    return out


def _layer_norm_rows(h, g, b, eps):
    mu = jnp.mean(h, axis=-1, keepdims=True)
    d = h - mu
    var = jnp.mean(d * d, axis=-1, keepdims=True)
    return d * lax.rsqrt(var + eps) * g + b


def _even_in_proj_kernel(x_ref, w_ref, u_ref, sza_ref, q_ref, cmp_ref, sel_ref, win_ref, szb_ref, gate_ref,
                         *t_refs, d_a, d_b, d_kv, q_scale):
    xb = x_ref[...].astype(BF16)

    def seg(lo, n):
        return jnp.dot(xb, w_ref[:, lo:lo + n], preferred_element_type=F32)

    o = 0
    a_val = seg(o, d_a); o += d_a
    a_glu = seg(o, d_a); o += d_a
    u_ref[...] = a_val * _sigmoid(a_glu)
    sza_ref[...] = _silu(seg(o, d_a)); o += d_a
    q_ref[...] = (seg(o, d_b) * q_scale).astype(BF16); o += d_b
    for j, ref in enumerate((cmp_ref, sel_ref, win_ref)):
        kv = seg(o, d_kv); o += d_kv
        ref[...] = kv
        if t_refs:
            t_refs[j][0] = kv.T
    szb_ref[...] = _silu(seg(o, d_b)); o += d_b
    gate_ref[...] = _sigmoid(seg(o, gate_ref.shape[-1]))


def _prep_w_in(w, d_a, d_b, kv_h, hd):
    grp = H_B // kv_h
    c_kv6 = 3 * d_a + d_b
    c_g3 = c_kv6 + 6 * kv_h * hd
    c_zb = c_g3 + 3 * H_B
    gate_blocks = []
    for g in range(kv_h):
        blk = w[:, c_g3 + g * grp * 3: c_g3 + (g + 1) * grp * 3]
        gate_blocks.append(jnp.pad(blk, ((0, 0), (0, LANES - grp * 3))))
    wn = jnp.concatenate([w[:, :c_g3], w[:, c_zb:c_zb + d_b]] + gate_blocks, axis=1)
    return wn.astype(BF16)


def _even_in_proj(x2, w_bf, d_a, d_b, d_kv, kv_h, hd, t_len):
    m, d = x2.shape
    tm = min(512, m)
    assert m % tm == 0
    n_gate = kv_h * LANES
    row = lambda n: pl.BlockSpec((tm, n), lambda i: (i, 0))
    out_shape = (jax.ShapeDtypeStruct((m, d_a), F32), jax.ShapeDtypeStruct((m, d_a), F32),
                 jax.ShapeDtypeStruct((m, d_b), BF16),
                 jax.ShapeDtypeStruct((m, d_kv), F32), jax.ShapeDtypeStruct((m, d_kv), F32),
                 jax.ShapeDtypeStruct((m, d_kv), F32),
                 jax.ShapeDtypeStruct((m, d_b), F32), jax.ShapeDtypeStruct((m, n_gate), F32))
    out_specs = (row(d_a), row(d_a), row(d_b), row(d_kv), row(d_kv), row(d_kv), row(d_b), row(n_gate))
    if t_len % tm == 0 and tm % LANES == 0:
        per_seq = t_len // tm
        t_spec = pl.BlockSpec((1, d_kv, tm), lambda i: (i // per_seq, 0, i % per_seq))
        out_shape += (jax.ShapeDtypeStruct((m // t_len, d_kv, t_len), F32),) * 3
        out_specs += (t_spec,) * 3
    return pl.pallas_call(
        functools.partial(_even_in_proj_kernel, d_a=d_a, d_b=d_b, d_kv=d_kv, q_scale=hd ** -0.5),
        grid=(m // tm,),
        in_specs=[row(d), pl.BlockSpec(w_bf.shape, lambda i: (0, 0), pipeline_mode=pl.Buffered(1))],
        out_specs=out_specs,
        out_shape=out_shape,
        compiler_params=_cparams(1),
        name="even_in_proj",
    )(x2, w_bf)


CONV_HIST = 32


def _conv_kernel(hist_ref, u_ref, sza_ref, w_ref, cb_ref, g_ref, b_ref, ya_ref, win_ref, sh_ref, conv_ref,
                 *, tq, rb, n_taps):
    i = pl.program_id(1)
    base = pl.multiple_of(i * tq, SUBLANES)
    win_ref[CONV_HIST:, :] = u_ref[0, pl.ds(base, tq), :]

    @pl.when(i == 0)
    def _():
        win_ref[:CONV_HIST, :] = hist_ref[0]

    @pl.when(i > 0)
    def _():
        win_ref[:CONV_HIST, :] = u_ref[0, pl.ds(base - CONV_HIST, CONV_HIST), :]

    w = win_ref[...]
    n = tq + CONV_HIST
    sh_ref[0] = w
    for s in range(1, SUBLANES):
        sh_ref[s] = pltpu.roll(w, n - s, 0)

    first = CONV_HIST - (n_taps - 1)
    cb = cb_ref[...]
    g = g_ref[...]
    b = b_ref[...]

    def block(k, carry):
        r0 = pl.multiple_of(k * rb, SUBLANES)
        acc = jnp.zeros((rb, u_ref.shape[-1]), F32) + cb
        for j in range(n_taps):
            a, s = divmod(first + j, SUBLANES)
            acc = acc + sh_ref[s, pl.ds(r0 + SUBLANES * a, rb), :] * w_ref[j:j + 1, :]
        conv_ref[pl.ds(r0, rb), :] = acc
        return carry

    lax.fori_loop(0, tq // rb, block, 0)
    y = _silu(_layer_norm_rows(conv_ref[...], g, b, LN_EPS)) * sza_ref[0]
    ya_ref[0] = y.astype(ya_ref.dtype)


def _conv_branch(hist, u3, sza3, conv_w, conv_b, ln_g, ln_b):
    bsz, t, d = u3.shape
    n_taps = conv_w.shape[0]
    assert n_taps - 1 <= CONV_HIST and t % SUBLANES == 0
    tq = min(256, t)
    rb = min(32, tq)
    assert t % tq == 0 and tq % rb == 0
    vec = lambda a: a.reshape(1, d)
    full = lambda shp: pl.BlockSpec(shp, lambda b, i: (0,) * len(shp))
    return pl.pallas_call(
        functools.partial(_conv_kernel, tq=tq, rb=rb, n_taps=n_taps),
        grid=(bsz, t // tq),
        in_specs=[pl.BlockSpec((1, CONV_HIST, d), lambda b, i: (b, 0, 0)),
                  pl.BlockSpec((1, t, d), lambda b, i: (b, 0, 0)),
                  pl.BlockSpec((1, tq, d), lambda b, i: (b, i, 0)),
                  full((n_taps, d)), full((1, d)), full((1, d)), full((1, d))],
        out_specs=pl.BlockSpec((1, tq, d), lambda b, i: (b, i, 0)),
        out_shape=jax.ShapeDtypeStruct((bsz, t, d), BF16),
        scratch_shapes=[pltpu.VMEM((tq + CONV_HIST, d), F32),
                        pltpu.VMEM((SUBLANES, tq + CONV_HIST, d), F32), pltpu.VMEM((tq, d), F32)],
        compiler_params=_cparams(2),
        name="conv_branch",
    )(hist, u3, sza3, conv_w, vec(conv_b), vec(ln_g), vec(ln_b))


SEL_OFF = -(2.0 ** 100)


def _nsa_prompt_kernel(q_ref, kcmp_ref, ksel_ref, kwin_ref, gate_ref, szb_ref, wpool_ref, o_ref,
                       kc_s, vc_s, ks_s, vs_s, kw_s, vw_s, m_s, l_s, acc_s, s_scr,
                       *, tq, t_len, l_cmp, hd, grp, kv_h):
    i = pl.program_id(1)
    n_cmp = t_len // l_cmp
    n_sel = t_len // L_SEL
    half = n_cmp // 2
    gw = kv_h * hd
    heads = range(kv_h * grp)
    slopes = [2.0 ** -(h + 1) for h in heads]

    @pl.when(i == 0)
    def _prep():
        x3 = kcmp_ref[0].reshape(half, 2 * l_cmp, 2 * gw)
        pooled = jnp.concatenate([jnp.sum(x3 * wpool_ref[0][None], axis=1),
                                  jnp.sum(x3 * wpool_ref[1][None], axis=1)], axis=0)
        pos = lax.broadcasted_iota(jnp.int32, (t_len, LANES), 0)
        lane = lax.broadcasted_iota(jnp.int32, (t_len, LANES), 1)
        blk = pos // L_SEL
        feat = jnp.where(lane == blk, 1.0, 0.0)
        feat = jnp.where(lane == n_sel, blk.astype(F32), feat)
        feat = jnp.where(lane == n_sel + 1, (pos - blk * L_SEL).astype(F32), feat)
        for src, dst in ((ksel_ref, ks_s), (kwin_ref, kw_s)):
            k01 = src[0, :, :gw]
            dst[0] = jnp.where(lane < hd, feat, pltpu.roll(k01, hd, 1)).astype(BF16)
            dst[1] = jnp.where(lane < hd, feat, k01).astype(BF16)
        vc_pad = jnp.concatenate([pooled[:, gw:], jnp.zeros((LANES - n_cmp, gw), F32)], axis=0) \
            if n_cmp < LANES else pooled[:, gw:]
        vc_t = vc_pad.T
        vs_t = ksel_ref[0, :, gw:].T
        vw_t = kwin_ref[0, :, gw:].T
        for gg in range(kv_h):
            kc_s[gg] = pooled[:, gg * hd:(gg + 1) * hd].astype(BF16)
            vc_s[gg] = vc_t[gg * hd:(gg + 1) * hd, :n_cmp].astype(BF16)
            vs_s[gg] = vs_t[gg * hd:(gg + 1) * hd].astype(BF16)
            vw_s[gg] = vw_t[gg * hd:(gg + 1) * hd].astype(BF16)

    t0 = i * tq
    qt = q_ref[0]

    n_io = lax.broadcasted_iota(jnp.int32, (n_cmp, tq), 0)
    t_io = lax.broadcasted_iota(jnp.int32, (n_cmp, tq), 1) + t0
    c_end = jnp.where(n_io < half, 2 * l_cmp * n_io + (l_cmp - 1), 2 * l_cmp * (n_io - half) + (2 * l_cmp - 1))
    dist_c = t_io - c_end
    valid_c = dist_c >= 0
    dist_cf = dist_c.astype(F32)
    imp = [jnp.zeros((n_sel, tq), F32) for _ in range(kv_h)]
    o_c = []
    for g in range(kv_h):
        qs = jnp.concatenate([qt[:, h * hd:(h + 1) * hd] for h in range(g * grp, (g + 1) * grp)], axis=0)
        s_t = lax.dot_general(kc_s[g], qs, NT, preferred_element_type=F32)
        for r in range(grp):
            s = jnp.where(valid_c, s_t[:, r * tq:(r + 1) * tq] - slopes[g * grp + r] * dist_cf, NEG)
            e = jnp.exp(s - jnp.max(s, axis=0, keepdims=True))
            p = jnp.where(valid_c, e / jnp.sum(e, axis=0, keepdims=True), 0.0)
            imp[g] = imp[g] + (p[:half] + p[half:])
            o_c.append(jnp.dot(vc_s[g], p.astype(BF16), preferred_element_type=F32))

    sb = lax.broadcasted_iota(jnp.int32, (n_sel, tq), 0)
    tb = (lax.broadcasted_iota(jnp.int32, (n_sel, tq), 1) + t0) // L_SEL
    forced = (sb == 0) | (sb == tb) | (sb == tb - 1)
    feat_sel = []
    for g in range(kv_h):
        val = jnp.where(sb <= tb, jnp.where(forced, FORCE, imp[g]), -jnp.inf)
        rank = jnp.zeros((n_sel, tq), F32)
        for j in range(n_sel):
            vj = val[j:j + 1, :]
            beats = jnp.where(val > vj, 1.0, jnp.where((val == vj) & (sb < j), 1.0, 0.0))
            rank = jnp.where(sb == j, jnp.sum(beats, axis=0, keepdims=True), rank)
        off_t = jnp.where(rank < min(K_SEL, n_sel), 0.0, SEL_OFF)
        feat_sel.append(jnp.concatenate([off_t, jnp.zeros((LANES - n_sel, tq), F32)], axis=0).T)

    qf = qt.astype(F32)
    lane_q = lax.broadcasted_iota(jnp.int32, (tq, LANES), 1)

    def q_aug(h, feat):
        tile = qf[:, (h // 2) * LANES:(h // 2 + 1) * LANES]
        if h % 2 == 0:
            tile = pltpu.roll(tile, hd, 1)
        x = jnp.where(lane_q < hd, feat, tile)
        x = jnp.where(lane_q == n_sel, slopes[h] * L_SEL, x)
        return jnp.where(lane_q == n_sel + 1, slopes[h], x).astype(BF16)

    d0 = (lax.broadcasted_iota(jnp.int32, (tq, tq), 1) - lax.broadcasted_iota(jnp.int32, (tq, tq), 0))

    def reset():
        m_s[...] = jnp.full(m_s.shape, NEG, F32)
        l_s[...] = jnp.zeros(l_s.shape, F32)
        acc_s[...] = jnp.zeros(acc_s.shape, F32)

    def scores(q_list, k_s, kt):
        k0 = pl.multiple_of(kt * tq, tq)
        kk = [k_s[g, pl.ds(k0, tq), :] for g in range(kv_h)]
        return [lax.dot_general(kk[r // grp], q_list[r], NT, preferred_element_type=F32) for r in heads]

    def update(s, v_s, kt, mask):
        k0 = pl.multiple_of(kt * tq, tq)
        vts = [v_s[g, :, pl.ds(k0, tq)] for g in range(kv_h)]
        if mask is not None:
            s = [jnp.where(mask, s[r], NEG) for r in heads]
        m_old = [m_s[r] for r in heads]
        m_new = [jnp.maximum(m_old[r], jnp.max(s[r], axis=0, keepdims=True)) for r in heads]
        al = [jnp.exp(m_old[r] - m_new[r]) for r in heads]
        p = [jnp.exp(s[r] - m_new[r]) for r in heads]
        for r in heads:
            m_s[r] = m_new[r]
            l_s[r] = al[r] * l_s[r] + jnp.sum(p[r], axis=0, keepdims=True)
            acc_s[r] = al[r] * acc_s[r] + jnp.dot(vts[r // grp], p[r].astype(BF16),
                                                  preferred_element_type=F32)

    def attend(q_list, k_s, v_s, lo, interior_mask, diag_mask):
        reset()

        def put(slot, kt):
            s = scores(q_list, k_s, kt)
            for r in heads:
                s_scr[slot, r] = s[r]

        def take(slot):
            return [s_scr[slot, r] for r in heads]

        odd = (i - lo) % 2

        @pl.when(odd == 1)
        def _():
            update(scores(q_list, k_s, lo), v_s, lo, interior_mask(lo))

        lo2 = lo + odd
        put(0, lo2)

        def body(j, carry):
            kt = lo2 + 2 * j
            put(1, kt + 1)
            update(take(0), v_s, kt, interior_mask(kt))
            put(0, kt + 2)
            update(take(1), v_s, kt + 1, interior_mask(kt + 1))
            return carry

        lax.fori_loop(0, (i - lo2) // 2, body, 0)
        update(take(0), v_s, i, diag_mask)
        return [acc_s[r] / l_s[r] for r in heads]

    o_s = attend([q_aug(r, feat_sel[r // grp]) for r in heads], ks_s, vs_s, 0, lambda kt: None, d0 >= 0)

    w_tiles = -(-(WINDOW - 1) // tq)
    o_w = attend([q_aug(r, 0.0) for r in heads], kw_s, vw_s, jnp.maximum(i - w_tiles, 0),
                 lambda kt: d0 + (i - kt) * tq < WINDOW, (d0 >= 0) & (d0 < WINDOW))

    gate_t = gate_ref[0].T
    outs = []
    for h in heads:
        row = (h // grp) * LANES + 3 * (h % grp)
        outs.append(gate_t[row:row + 1] * o_c[h]
                    + gate_t[row + 1:row + 2] * o_s[h]
                    + gate_t[row + 2:row + 3] * o_w[h])
    o_ref[0] = (jnp.concatenate(outs, axis=0).T * szb_ref[0]).astype(o_ref.dtype)


def _pool_weights(wk, wv, hd, halves):
    l_cmp, kv_h = wk.shape
    row = jnp.concatenate([jnp.repeat(wk, hd, axis=1), jnp.repeat(wv, hd, axis=1)], axis=1)
    if not halves:
        return row
    z = jnp.zeros_like(row)
    return jnp.stack([jnp.concatenate([row, z], axis=0), jnp.concatenate([z, row], axis=0)])


def _nsa_prompt(q3, cmp3, sel3, win3, gate3, szb3, wk, wv):
    bsz, t, d_b = q3.shape
    l_cmp, kv_h = wk.shape
    grp = H_B // kv_h
    hd = d_b // H_B
    gw = kv_h * hd
    tq = min(256, t)
    assert t % tq == 0 and t % (2 * l_cmp) == 0 and L_SEL == 2 * l_cmp and tq % L_SEL == 0
    n_cmp = t // l_cmp
    n_sel = t // L_SEL
    assert gw == LANES and 2 * hd == LANES and n_sel + 2 <= hd and n_sel % SUBLANES == 0 and n_cmp <= LANES
    wpool = _pool_weights(wk, wv, hd, halves=True)
    kv_spec = pl.BlockSpec((1, t, 2 * gw), lambda b, i: (b, 0, 0))
    tile = lambda n: pl.BlockSpec((1, tq, n), lambda b, i: (b, i, 0))
    return pl.pallas_call(
        functools.partial(_nsa_prompt_kernel, tq=tq, t_len=t, l_cmp=l_cmp, hd=hd, grp=grp, kv_h=kv_h),
        grid=(bsz, t // tq),
        in_specs=[tile(d_b), kv_spec, kv_spec, kv_spec, tile(kv_h * LANES), tile(d_b),
                  pl.BlockSpec(wpool.shape, lambda b, i: (0, 0, 0))],
        out_specs=tile(d_b),
        out_shape=jax.ShapeDtypeStruct((bsz, t, d_b), BF16),
        scratch_shapes=[pltpu.VMEM((kv_h, n_cmp, hd), BF16), pltpu.VMEM((kv_h, hd, n_cmp), BF16),
                        pltpu.VMEM((kv_h, t, LANES), BF16), pltpu.VMEM((kv_h, hd, t), BF16),
                        pltpu.VMEM((kv_h, t, LANES), BF16), pltpu.VMEM((kv_h, hd, t), BF16),
                        pltpu.VMEM((H_B, 1, tq), F32), pltpu.VMEM((H_B, 1, tq), F32),
                        pltpu.VMEM((H_B, hd, tq), F32), pltpu.VMEM((2, H_B, tq, tq), F32)],
        compiler_params=_cparams(2),
        name="nsa_prompt",
    )(q3, cmp3, sel3, win3, gate3, szb3, wpool)


def _decode_slopes():
    head = lax.broadcasted_iota(jnp.int32, (H_B, 1), 0)
    slope = jnp.zeros((H_B, 1), F32)
    for hh in range(H_B):
        slope = jnp.where(head == hh, 2.0 ** -(hh + 1), slope)
    return head, slope


def _nsa_decode_cmp_kernel(pt_ref, q_ref, *refs, pg, n_pages, page, l_cmp, hd, grp, kv_h, n_slot):
    del pt_ref
    cmp_refs = refs[:pg]
    wrow_ref, seg_ref, oc_ref, idx_ref, kcv_s = refs[pg:]
    p = pl.program_id(1)
    n_chunks = n_pages // pg
    past = n_pages * page
    n_cmp = past // l_cmp
    n_sel = past // L_SEL
    gw = kv_h * hd
    nc_pad = idx_ref.shape[-1]
    prod = jnp.concatenate([cmp_refs[j][0] * wrow_ref[...] for j in range(pg)], axis=1)
    kcv_s[p] = _mm(prod, seg_ref[...], NN)

    @pl.when(p == n_chunks - 1)
    def _compressed():
        head, slope = _decode_slopes()
        q = q_ref[0].astype(BF16)
        n_io = lax.broadcasted_iota(jnp.int32, (1, n_cmp), 1)
        dist = (past - (l_cmp * n_io + (l_cmp - 1))).astype(F32)
        pair = jnp.where(lax.broadcasted_iota(jnp.int32, (n_cmp, nc_pad), 0) // (L_SEL // l_cmp)
                         == lax.broadcasted_iota(jnp.int32, (n_cmp, nc_pad), 1), 1.0, 0.0).astype(BF16)
        c_io = lax.broadcasted_iota(jnp.int32, (1, nc_pad), 1)
        i_r = lax.broadcasted_iota(jnp.int32, (nc_pad, nc_pad), 1)
        i_c = lax.broadcasted_iota(jnp.int32, (nc_pad, nc_pad), 0)
        before = jnp.where(i_c < i_r, 1.0, 0.0).astype(BF16)
        slot = lax.broadcasted_iota(jnp.int32, (n_slot, nc_pad), 0).astype(F32)
        for g in range(kv_h):
            kc_t = jnp.concatenate([kcv_s[c, g * hd:(g + 1) * hd, :] for c in range(n_chunks)], axis=1)
            vc_t = jnp.concatenate([kcv_s[c, gw + g * hd: gw + (g + 1) * hd, :] for c in range(n_chunks)], axis=1)
            s = jnp.dot(q, kc_t.astype(BF16), preferred_element_type=F32) - slope * dist
            e = jnp.exp(s - jnp.max(s, axis=-1, keepdims=True))
            pc = e / jnp.sum(e, axis=-1, keepdims=True)
            oc_ref[0, g] = lax.dot_general(pc.astype(BF16), vc_t.astype(BF16), NT, preferred_element_type=F32)
            in_grp = (head >= g * grp) & (head < (g + 1) * grp)
            imp = _mm(jnp.sum(jnp.where(in_grp, pc, 0.0), axis=0, keepdims=True), pair, NN, pa=3)
            forced = (c_io == 0) | (c_io == n_sel - 1)
            imp = jnp.where(forced, FORCE, imp)
            imp = jnp.where(c_io < n_sel, imp, -jnp.inf)
            v_r = jnp.broadcast_to(imp, (nc_pad, nc_pad))
            v_c = v_r.T
            beats = jnp.where(v_c > v_r, 1.0, jnp.where((v_c == v_r) & (i_c < i_r), 1.0, 0.0))
            rank = jnp.sum(beats, axis=0, keepdims=True) + jnp.where(imp < FORCE, 1.0, 0.0)
            chosen = jnp.where((rank < min(K_SEL, n_sel + 1)) & (c_io < n_sel), 1.0, 0.0)
            n_before = jnp.dot(jnp.broadcast_to(chosen, (SUBLANES, nc_pad)).astype(BF16), before,
                               preferred_element_type=F32)[:1]
            hit = jnp.where((n_before == slot) & (chosen > 0.5), 1.0, 0.0)
            ids = jnp.sum(hit * c_io.astype(F32), axis=-1, keepdims=True)
            ids = jnp.where(jnp.sum(hit, axis=-1, keepdims=True) > 0.5, ids, -1.0)
            idx_ref[0, g] = jnp.broadcast_to(ids, (n_slot, nc_pad)).astype(jnp.int32)


def _nsa_decode_sel_kernel(pt_ref, idx_ref, q_ref, *refs, n_slot, n_pages, page, hd, grp, kv_h, w_buf):
    del pt_ref
    n_blk = kv_h * n_slot
    k_refs = refs[:n_blk]
    v_refs = refs[n_blk:2 * n_blk]
    kwin_ref, seln_ref, winn_ref, gate_ref, szb_ref, oc_ref, o_ref = refs[2 * n_blk:]
    b = pl.program_id(0)
    past = n_pages * page
    gw = kv_h * hd
    per_page = page // L_SEL
    head, slope = _decode_slopes()
    qf = q_ref[0]
    q = qf.astype(BF16)
    gate = gate_ref[0]
    lane = lax.broadcasted_iota(jnp.int32, (1, page), 1)
    i_io = lax.broadcasted_iota(jnp.int32, (1, w_buf), 1)
    dist_w = w_buf - i_io
    valid_w = dist_w < WINDOW
    o = jnp.zeros((H_B, hd), F32)
    for g in range(kv_h):
        dists, valids = [], []
        for j in range(n_slot):
            blk = idx_ref[b, g, j]
            safe = jnp.maximum(blk, 0)
            dists.append(past - ((safe // per_page) * page + lane))
            valids.append(lane // L_SEL == jnp.where(blk >= 0, safe % per_page, -1))
        dist = jnp.concatenate(dists, axis=1).astype(F32)
        valid = jnp.concatenate(valids, axis=1)
        k_t = jnp.concatenate([k_refs[g * n_slot + j][0] for j in range(n_slot)], axis=1).astype(BF16)
        v_t = jnp.concatenate([v_refs[g * n_slot + j][0] for j in range(n_slot)], axis=1).astype(BF16)
        s = jnp.where(valid, jnp.dot(q, k_t, preferred_element_type=F32) - slope * dist, NEG)
        kn = seln_ref[0][:, g * hd:(g + 1) * hd]
        vn = seln_ref[0][:, gw + g * hd: gw + (g + 1) * hd]
        s_n = jnp.sum(qf * kn, axis=-1, keepdims=True)
        m = jnp.maximum(jnp.max(s, axis=-1, keepdims=True), s_n)
        pe = jnp.where(valid, jnp.exp(s - m), 0.0)
        pn = jnp.exp(s_n - m)
        o_s = ((lax.dot_general(pe.astype(BF16), v_t, NT, preferred_element_type=F32) + pn * vn)
               / (jnp.sum(pe, axis=-1, keepdims=True) + pn))
        kw_t = kwin_ref[0, g * hd:(g + 1) * hd, :].astype(BF16)
        vw_t = kwin_ref[0, gw + g * hd: gw + (g + 1) * hd, :].astype(BF16)
        s = jnp.dot(q, kw_t, preferred_element_type=F32) - slope * dist_w.astype(F32)
        s = jnp.where(valid_w, s, NEG)
        kwn = winn_ref[0][:, g * hd:(g + 1) * hd]
        vwn = winn_ref[0][:, gw + g * hd: gw + (g + 1) * hd]
        s_wn = jnp.sum(qf * kwn, axis=-1, keepdims=True)
        mw = jnp.maximum(jnp.max(s, axis=-1, keepdims=True), s_wn)
        pw = jnp.where(valid_w, jnp.exp(s - mw), 0.0)
        pwn = jnp.exp(s_wn - mw)
        o_w = ((lax.dot_general(pw.astype(BF16), vw_t, NT, preferred_element_type=F32) + pwn * vwn)
               / (jnp.sum(pw, axis=-1, keepdims=True) + pwn))
        gcol = []
        for c in range(3):
            col = jnp.zeros((H_B, 1), F32)
            for r in range(grp):
                ln = g * LANES + 3 * r + c
                col = jnp.where(head == g * grp + r, gate[:, ln:ln + 1], col)
            gcol.append(col)
        og = gcol[0] * oc_ref[0, g] + gcol[1] * o_s + gcol[2] * o_w
        o = jnp.where((head >= g * grp) & (head < (g + 1) * grp), og, o)
    o_ref[0] = o * szb_ref[0]


def _nsa_decode2(q3, cache_cmp_t, cache_sel_t, cache_win_t, page_table, sel_new, win_new, gate, szb3, wk, wv):
    bs, _, hd = q3.shape
    l_cmp, kv_h = wk.shape
    grp = H_B // kv_h
    gw = kv_h * hd
    page = cache_cmp_t.shape[-1]
    n_pages = page_table.shape[1]
    w_buf = cache_win_t.shape[-1]
    per_page = page // l_cmp
    pg = LANES // per_page
    assert page == LANES and page % L_SEL == 0 and page % l_cmp == 0 and n_pages % pg == 0
    n_chunks = n_pages // pg
    n_sel = n_pages * page // L_SEL
    n_slot = min(K_SEL, n_sel + 1) - 1
    assert n_slot >= 1
    nc_pad = -(-n_sel // LANES) * LANES
    wrow = jnp.tile(_pool_weights(wk, wv, hd, halves=False), (per_page, 1)).T
    pos = lax.broadcasted_iota(jnp.int32, (pg, page, LANES), 1)
    pj = lax.broadcasted_iota(jnp.int32, (pg, page, LANES), 0)
    ln = lax.broadcasted_iota(jnp.int32, (pg, page, LANES), 2)
    seg = (ln == pj * per_page + pos // l_cmp).astype(BF16).reshape(pg * page, LANES)

    per_b = lambda shp: pl.BlockSpec((1,) + shp, lambda b, p, pt: (b,) + (0,) * len(shp))
    oc, idx = pl.pallas_call(
        functools.partial(_nsa_decode_cmp_kernel, pg=pg, n_pages=n_pages, page=page, l_cmp=l_cmp, hd=hd, grp=grp,
                          kv_h=kv_h, n_slot=n_slot),
        grid_spec=pltpu.PrefetchScalarGridSpec(
            num_scalar_prefetch=1,
            grid=(bs, n_chunks),
            in_specs=([per_b((H_B, hd))]
                      + [pl.BlockSpec((1, 2 * gw, page), (lambda j: lambda b, p, pt: (pt[b, p * pg + j], 0, 0))(j))
                         for j in range(pg)]
                      + [pl.BlockSpec((2 * gw, page), lambda b, p, pt: (0, 0)),
                         pl.BlockSpec(seg.shape, lambda b, p, pt: (0, 0))]),
            out_specs=(per_b((kv_h, H_B, hd)), per_b((kv_h, n_slot, nc_pad))),
            scratch_shapes=[pltpu.VMEM((n_chunks, 2 * gw, LANES), F32)]),
        out_shape=(jax.ShapeDtypeStruct((bs, kv_h, H_B, hd), F32),
                   jax.ShapeDtypeStruct((bs, kv_h, n_slot, nc_pad), jnp.int32)),
        compiler_params=_cparams(2),
        name="nsa_decode_cmp",
    )(page_table, q3, *([cache_cmp_t] * pg), wrow, seg)
    blk_ids = idx[:, :, :, 0]

    sel_per_page = page // L_SEL

    def blk_map(g, j, row_blk):
        return lambda b, pt, ids: (pt[b, jnp.maximum(ids[b, g, j], 0) // sel_per_page], row_blk, 0)

    per_b2 = lambda shp: pl.BlockSpec((1,) + shp, lambda b, pt, ids: (b,) + (0,) * len(shp))
    slots = [(g, j) for g in range(kv_h) for j in range(n_slot)]
    return pl.pallas_call(
        functools.partial(_nsa_decode_sel_kernel, n_slot=n_slot, n_pages=n_pages, page=page, hd=hd, grp=grp,
                          kv_h=kv_h, w_buf=w_buf),
        grid_spec=pltpu.PrefetchScalarGridSpec(
            num_scalar_prefetch=2,
            grid=(bs,),
            in_specs=([per_b2((H_B, hd))]
                      + [pl.BlockSpec((1, hd, page), blk_map(g, j, g)) for g, j in slots]
                      + [pl.BlockSpec((1, hd, page), blk_map(g, j, kv_h + g)) for g, j in slots]
                      + [per_b2((2 * gw, w_buf)), per_b2((1, 2 * gw)), per_b2((1, 2 * gw)),
                         per_b2((1, kv_h * LANES)), per_b2((H_B, hd)), per_b2((kv_h, H_B, hd))]),
            out_specs=per_b2((H_B, hd))),
        out_shape=jax.ShapeDtypeStruct((bs, H_B, hd), F32),
        compiler_params=_cparams(1),
        name="nsa_decode_sel",
    )(page_table, blk_ids, q3, *([cache_sel_t] * (2 * len(slots))), cache_win_t, sel_new, win_new, gate, szb3, oc)


def _out_proj_ln_kernel(*refs, n_in, alpha):
    a_refs, w_refs = refs[:n_in], refs[n_in:2 * n_in]
    x_ref, g_ref, b_ref, o_ref = refs[2 * n_in:]
    y = None
    for a_ref, w_ref in zip(a_refs, w_refs):
        t = jnp.dot(a_ref[...].astype(BF16), w_ref[...], preferred_element_type=F32)
        y = t if y is None else y + t
    o_ref[...] = _layer_norm_rows(alpha * x_ref[...] + y, g_ref[...], b_ref[...], LN_EPS)


def _out_proj_ln(a_list, w_list, x2, ln_g, ln_b, alpha):
    m, d = x2.shape
    tm = min(1024, m)
    assert m % tm == 0
    n_in = len(a_list)
    return pl.pallas_call(
        functools.partial(_out_proj_ln_kernel, n_in=n_in, alpha=alpha),
        grid=(m // tm,),
        in_specs=([pl.BlockSpec((tm, a.shape[1]), lambda i: (i, 0)) for a in a_list]
                  + [pl.BlockSpec(w.shape, lambda i: (0, 0), pipeline_mode=pl.Buffered(1)) for w in w_list]
                  + [pl.BlockSpec((tm, d), lambda i: (i, 0)),
                     pl.BlockSpec((1, d), lambda i: (0, 0)), pl.BlockSpec((1, d), lambda i: (0, 0))]),
        out_specs=pl.BlockSpec((tm, d), lambda i: (i, 0)),
        out_shape=jax.ShapeDtypeStruct((m, d), F32),
        compiler_params=_cparams(1),
        name="out_proj_ln",
    )(*a_list, *w_list, x2, ln_g.reshape(1, d), ln_b.reshape(1, d))


def _rwkv_proj_kernel(*refs, seq, tiles_per_seq):
    if seq:
        x_ref, tail_ref, shift_ref = refs[:3]
        rest = refs[3:]
    else:
        x_ref, shift_ref = refs[:2]
        rest = refs[2:]
    (mu_ref, w_ref, w1_ref, w2_ref, a1_ref, a2_ref, w0_ref, a0_ref,
     r_ref, k_ref, v_ref, lw_ref, a_ref, sz_ref) = rest
    x = x_ref[...]
    if seq:
        i = pl.program_id(0)
        first = jnp.where(i % tiles_per_seq == 0, shift_ref[0], tail_ref[SUBLANES - 1:SUBLANES, :])
        row = lax.broadcasted_iota(jnp.int32, x.shape, 0)
        x_prev = jnp.where(row == 0, first, pltpu.roll(x, 1, 0))
    else:
        x_prev = shift_ref[...]
    dx = x_prev - x
    mix = lambda n: (x + dx * mu_ref[n:n + 1, :]).astype(BF16)
    r_ref[...] = jnp.dot(mix(0), w_ref[0], preferred_element_type=F32)
    k_ref[...] = jnp.dot(mix(1), w_ref[1], preferred_element_type=F32)
    v_ref[...] = jnp.dot(mix(2), w_ref[2], preferred_element_type=F32)
    sz_ref[...] = _silu(jnp.dot(mix(3), w_ref[3], preferred_element_type=F32))
    hw = jnp.tanh(jnp.dot(mix(4), w1_ref[...], preferred_element_type=F32)).astype(BF16)
    y = -(w0_ref[...] + jnp.dot(hw, w2_ref[...], preferred_element_type=F32))
    softplus = jnp.maximum(y, 0.0) + jnp.log(1.0 + jnp.exp(-jnp.abs(y)))
    lw_ref[...] = -jnp.exp(-softplus - 0.5)
    ha = jnp.dot(mix(5), a1_ref[...], preferred_element_type=F32).astype(BF16)
    a_ref[...] = _sigmoid(a0_ref[...] + jnp.dot(ha, a2_ref[...], preferred_element_type=F32))


def _rwkv_proj(x2, shift, t_len, mu, w_rkvz, w0, w1, w2, a0, a1, a2):
    m, d = x2.shape
    seq = t_len > 1
    tm = min(512, t_len) if seq else m
    assert m % tm == 0 and (not seq or (t_len % tm == 0 and tm % SUBLANES == 0))
    tiles_per_seq = t_len // tm if seq else 1
    full = lambda a: pl.BlockSpec(a.shape, lambda i: (0,) * a.ndim, pipeline_mode=pl.Buffered(1))
    row = pl.BlockSpec((tm, d), lambda i: (i, 0))
    if seq:
        blk = tm // SUBLANES
        lead = [x2, x2, shift.reshape(-1, 1, d)]
        lead_specs = [row, pl.BlockSpec((SUBLANES, d), lambda i: (jnp.maximum(i * blk - 1, 0), 0)),
                      pl.BlockSpec((1, 1, d), lambda i: (i // tiles_per_seq, 0, 0))]
    else:
        lead = [x2, shift]
        lead_specs = [row, row]
    ws = [mu, w_rkvz.astype(BF16), w1.astype(BF16), w2.astype(BF16), a1.astype(BF16), a2.astype(BF16),
          w0.reshape(1, d), a0.reshape(1, d)]
    return pl.pallas_call(
        functools.partial(_rwkv_proj_kernel, seq=seq, tiles_per_seq=tiles_per_seq),
        grid=(m // tm,),
        in_specs=lead_specs + [full(a) for a in ws],
        out_specs=(row,) * 6,
        out_shape=(jax.ShapeDtypeStruct((m, d), F32),) * 6,
        compiler_params=_cparams(1),
        name="rwkv_proj",
    )(*lead, *ws)


WKV_CHUNK = 64
WKV_PASSES = 1
WKV_SUB = 8


def _wkv_kernel(r_ref, k_ref, v_ref, lw_ref, a_ref, sz_ref, s0_ref, kk_ref, ka_ref, rk_ref, gg_ref, gb_ref,
                yz_ref, sfin_ref, h_s, *, c_len, n_sub, n_real, n_heads, hd):
    c = pl.program_id(1)
    mm = functools.partial(_mm, pa=WKV_PASSES, pb=WKV_PASSES)
    n_pairs = n_heads // 2
    pw = 2 * hd
    d = n_heads * hd
    pairs = range(n_pairs)
    iota = lambda shape, axis: lax.broadcasted_iota(jnp.int32, shape, axis)
    bd_mask = (iota((pw, pw), 0) // hd) == (iota((pw, pw), 1) // hd)
    bd_ones = jnp.where(bd_mask, 1.0, 0.0).astype(BF16)
    eye_mask = iota((pw, pw), 0) == iota((pw, pw), 1)
    eye_pb = jnp.where(eye_mask, 1.0, 0.0).astype(BF16)
    lane_lo = iota((1, pw), 1) < hd

    @pl.when(c == 0)
    def _():
        z = jnp.zeros((hd, hd), F32)
        for p in pairs:
            h_s[p] = jnp.concatenate([jnp.concatenate([s0_ref[0, 2 * p].T, z], axis=1),
                                      jnp.concatenate([z, s0_ref[0, 2 * p + 1].T], axis=1)], axis=0)

    def segsum(x):
        xb = x.astype(BF16)
        return jnp.concatenate([jnp.dot(xb[:, t * pw:(t + 1) * pw], bd_ones, preferred_element_type=F32)
                                for t in range(d // pw)], axis=1)

    r = r_ref[0]
    k = k_ref[0]
    v = v_ref[0]
    lw = lw_ref[0]
    a = a_ref[0]
    tri = jnp.where(iota((c_len, c_len), 0) >= iota((c_len, c_len), 1), 1.0, 0.0).astype(BF16)
    col2 = iota((c_len, 2 * c_len), 1) % c_len
    low_exc2 = iota((c_len, 2 * c_len), 0) > col2
    eye2 = jnp.where(iota((c_len, 2 * c_len), 0) == col2, 1.0, 0.0)
    low_inc4 = iota((c_len, 4 * c_len), 0) >= iota((c_len, 4 * c_len), 1) % c_len
    cc_mask = (iota((2 * c_len, 2 * c_len), 0) // c_len) == (iota((2 * c_len, 2 * c_len), 1) // c_len)

    def by_head(x):
        return jnp.concatenate([jnp.where(lane_lo, x, 0.0), jnp.where(lane_lo, 0.0, x)], axis=0).astype(BF16)
    kk = k * kk_ref[...]
    kap = kk / jnp.maximum(jnp.sqrt(segsum(kk * kk)), 1e-12)
    ka = kap * a
    kmod = k * (1.0 + (a - 1.0) * ka_ref[...])
    rkr = r * kmod * rk_ref[...]
    levels = max((n_real - 1).bit_length() - 1, 0)

    lhs_b, lhs2_t, vb, v_st, l_b, l_k, m_r = [], [], [], [], [], [], []
    xlu_transpose = (2 * c_len + pw) % LANES == 0
    for s in range(n_sub):
        rs = slice(s * c_len, (s + 1) * c_len)
        lw_s = lw[rs]
        cum = _mm(tri, lw_s, NN, pb=3)
        cum_end = cum[c_len - 1:c_len, :]
        p_inv = jnp.exp(-cum)
        p_rest = jnp.exp(cum_end - cum)
        p_end = jnp.exp(cum_end)
        lhs_top = kap[rs] * jnp.exp(cum - lw_s)
        r_t = r[rs] * jnp.exp(cum)
        rhs_top = ka[rs] * p_inv
        k_t = kmod[rs] * p_inv
        ke_top = ka[rs] * p_rest
        k_e = kmod[rs] * p_rest
        for p in pairs:
            ls = slice(p * pw, (p + 1) * pw)
            lhs_b.append(jnp.concatenate([lhs_top[:, ls], r_t[:, ls]], axis=0).astype(BF16))
            v_p = v[rs, ls]
            vb.append(v_p.astype(BF16))
            v_st.append(by_head(v_p))
            lhs2 = jnp.concatenate([ke_top[:, ls], k_e[:, ls], jnp.where(eye_mask, p_end[:, ls], 0.0)],
                                   axis=0)
            if xlu_transpose:
                lhs2_t.append(lhs2.T.astype(BF16))
            else:
                lhs2_t.append(lax.dot_general(eye_pb, lhs2.astype(BF16), NT,
                                              preferred_element_type=F32).astype(BF16))
            rhs_st = jnp.concatenate([by_head(rhs_top[:, ls]), by_head(k_t[:, ls])], axis=0)
            am = mm(lhs_b[-1], rhs_st, NT)
            l_b.append(jnp.where(low_exc2, am[:c_len, :2 * c_len], 0.0))
            l_k.append(jnp.where(low_exc2, am[:c_len, 2 * c_len:], 0.0).astype(BF16))
            m_r.append(jnp.where(low_inc4, am[c_len:], 0.0).astype(BF16))
    n_sp = n_sub * n_pairs
    lkv = [mm(l_k[i], v_st[i], NN) for i in range(n_sp)]
    blockdiag = lambda x: jnp.where(cc_mask, jnp.concatenate([x, x], axis=0), 0.0).astype(BF16)
    t_inv = [eye2 - x for x in l_b]
    pwr = l_b
    for _ in range(levels):
        pwr = [mm(x, blockdiag(x), NN) for x in pwr]
        t_inv = [t + mm(t, blockdiag(x), NN) for t, x in zip(t_inv, pwr)]

    h_cur = [h_s[p] for p in pairs]
    y_rows = []
    for s in range(n_sub):
        idx = [s * n_pairs + p for p in pairs]
        hb = [h_cur[p].astype(BF16) for p in pairs]
        gh = [mm(lhs_b[i], hb[p], NN) for p, i in zip(pairs, idx)]
        u = [-mm(t_inv[i], by_head(gh[p][:c_len] + lkv[i]), NN) for p, i in zip(pairs, idx)]
        uv = [jnp.concatenate([u[p].astype(BF16), vb[i]], axis=0) for p, i in zip(pairs, idx)]
        y = [gh[p][c_len:] + mm(m_r[i], jnp.concatenate([by_head(u[p]), v_st[i]], axis=0), NN)
             for p, i in zip(pairs, idx)]
        h_cur = [jnp.where(bd_mask, mm(lhs2_t[i], jnp.concatenate([uv[p], hb[p]], axis=0), NN), 0.0)
                 for p, i in zip(pairs, idx)]
        y_rows.append(jnp.concatenate(y, axis=1))
    for p in pairs:
        h_s[p] = h_cur[p]

    y_all = jnp.concatenate(y_rows, axis=0) if n_sub > 1 else y_rows[0]
    dy = y_all - segsum(y_all) * (1.0 / hd)
    var = segsum(dy * dy) * (1.0 / hd)
    yn = dy * lax.rsqrt(var + GN_EPS) * gg_ref[...] + gb_ref[...]
    yz_ref[0] = ((yn + segsum(rkr) * v) * sz_ref[0]).astype(yz_ref.dtype)

    @pl.when(c == pl.num_programs(1) - 1)
    def _():
        for p in pairs:
            sfin_ref[0, 2 * p] = h_s[p, :hd, :hd].T
            sfin_ref[0, 2 * p + 1] = h_s[p, hd:, hd:].T


def _wkv_scan(r3, k3, v3, lw3, a3, sz3, s0, k_k, k_a, r_k, gn_g, gn_b, c_len, t_real):
    bsz, t, d = r3.shape
    n_heads = d // HD_C
    n_sub = WKV_SUB if t % (WKV_SUB * c_len) == 0 else 1
    rows = n_sub * c_len
    assert t % rows == 0 and n_heads % 2 == 0 and 2 * HD_C == LANES
    vec = lambda x: x.reshape(1, d)
    seq = pl.BlockSpec((1, rows, d), lambda b, c: (b, c, 0))
    st = pl.BlockSpec((1, n_heads, HD_C, HD_C), lambda b, c: (b, 0, 0, 0))
    par = pl.BlockSpec((1, d), lambda b, c: (0, 0))
    return pl.pallas_call(
        functools.partial(_wkv_kernel, c_len=c_len, n_sub=n_sub, n_real=min(t_real, c_len), n_heads=n_heads,
                          hd=HD_C),
        grid=(bsz, t // rows),
        in_specs=[seq] * 6 + [st] + [par] * 5,
        out_specs=(seq, st),
        out_shape=(jax.ShapeDtypeStruct((bsz, t, d), BF16),
                   jax.ShapeDtypeStruct((bsz, n_heads, HD_C, HD_C), F32)),
        scratch_shapes=[pltpu.VMEM((n_heads // 2, LANES, LANES), F32)],
        compiler_params=_cparams(2),
        name="wkv_scan",
    )(r3, k3, v3, lw3, a3, sz3, s0, vec(k_k), vec(k_a), vec(r_k), vec(gn_g), vec(gn_b))


def _even_layer(xp, xs, cache_cmp, cache_sel, cache_win, state_conv, page_table,
                w_in, conv_w, conv_b, cln_g, cln_b, wk, wv, w_out, ln_g, ln_b, alpha):
    bp, tp, d = xp.shape
    bs, ts, _ = xs.shape
    assert ts == 1
    d_a = conv_w.shape[-1]
    n_taps = conv_w.shape[0]
    l_cmp, kv_h = wk.shape
    hd = cache_cmp.shape[-1]
    d_b = H_B * hd
    d_kv = 2 * kv_h * hd
    w_bf = _prep_w_in(w_in, d_a, d_b, kv_h, hd)
    cw = conv_w.reshape(n_taps, d_a)
    w_out_bf = w_out.astype(BF16)
    kv_shape = lambda b, t: (b, t, 2, kv_h, hd)

    proj = _even_in_proj(xp.reshape(bp * tp, d), w_bf, d_a, d_b, d_kv, kv_h, hd, tp)
    u, sza, q, cmp_n, sel_n, win_n, szb, gate = proj[:8]
    if len(proj) > 8:
        new_kv = [a.reshape(bp, 2, kv_h, hd, tp).transpose(0, 4, 1, 2, 3) for a in proj[8:]]
    else:
        new_kv = [a.reshape(bp, tp, 2, kv_h, hd) for a in (cmp_n, sel_n, win_n)]
    r3 = lambda a: a.reshape(bp, tp, a.shape[-1])
    u3 = r3(u)
    ya = _conv_branch(jnp.zeros((bp, CONV_HIST, d_a), F32), u3, r3(sza), cw, conv_b, cln_g, cln_b)
    yb = _nsa_prompt(r3(q), r3(cmp_n), r3(sel_n), r3(win_n), r3(gate), r3(szb), wk, wv)
    yp = _out_proj_ln([ya.reshape(bp * tp, d_a), yb.reshape(bp * tp, d_b)], [w_out_bf[:d_a], w_out_bf[d_a:]],
                      xp.reshape(bp * tp, d), ln_g, ln_b, alpha).reshape(bp, tp, d)
    w_keep = min(WINDOW, tp)
    outs_p = (new_kv[0], new_kv[1], new_kv[2][:, tp - w_keep:], u3[:, tp - (n_taps - 1):])

    u, sza, q, cmp_s, sel_s, win_s, szb, gate = _even_in_proj(xs.reshape(bs, d), w_bf, d_a, d_b, d_kv, kv_h, hd, 1)
    ext = jnp.concatenate([state_conv, u[:, None, :]], axis=1)
    hist = jnp.pad(state_conv, ((0, 0), (CONV_HIST - (n_taps - 1), 0), (0, 0)))
    pad_rows = lambda a: jnp.pad(a[:, None, :], ((0, 0), (0, SUBLANES - 1), (0, 0)))
    ya = _conv_branch(hist, pad_rows(u), pad_rows(sza), cw, conv_b, cln_g, cln_b)[:, 0]
    n_pool, page = cache_cmp.shape[:2]
    w_buf = cache_win.shape[1]
    fm = lambda c: jnp.transpose(c, (0, 2, 3, 4, 1)).reshape(c.shape[0], d_kv, c.shape[1])
    yb = _nsa_decode2(q.astype(F32).reshape(bs, H_B, hd), fm(cache_cmp), fm(cache_sel), fm(cache_win), page_table,
                      sel_s[:, None, :], win_s[:, None, :], gate[:, None, :], szb.reshape(bs, H_B, hd), wk, wv)
    ys = _out_proj_ln([ya, yb.reshape(bs, d_b)], [w_out_bf[:d_a], w_out_bf[d_a:]], xs.reshape(bs, d),
                      ln_g, ln_b, alpha).reshape(bs, 1, d)
    ctx = jnp.concatenate([cache_win, win_s.reshape(kv_shape(bs, 1))], axis=1)
    outs_s = (cmp_s.reshape(kv_shape(bs, 1)), sel_s.reshape(kv_shape(bs, 1)),
              ctx[:, ctx.shape[1] - min(WINDOW, ctx.shape[1]):], ext[:, 1:])
    return yp, ys, outs_p, outs_s


def _odd_group(x3, shift, s0, mu, w_rkvz, w0, w1, w2, a0, a1, a2, k_k, k_a, r_k, gn_g, gn_b, w_out_bf,
               ln_g, ln_b, alpha):
    bsz, t, d = x3.shape
    x2 = x3.reshape(bsz * t, d)
    r, k, v, lw, a, sz = _rwkv_proj(x2, shift, t, mu, w_rkvz, w0, w1, w2, a0, a1, a2)
    c_len = WKV_CHUNK if t >= WKV_CHUNK else -(-t // SUBLANES) * SUBLANES
    t_pad = -(-t // c_len) * c_len
    r3 = lambda z: jnp.pad(z.reshape(bsz, t, d), ((0, 0), (0, t_pad - t), (0, 0)))
    yz, s_fin = _wkv_scan(r3(r), r3(k), r3(v), r3(lw), r3(a), r3(sz), s0, k_k, k_a, r_k.reshape(-1), gn_g, gn_b,
                          c_len, t)
    y = _out_proj_ln([yz[:, :t].reshape(bsz * t, d)], [w_out_bf], x2, ln_g, ln_b, alpha).reshape(bsz, t, d)
    return y, s_fin, x3[:, -1]


def kernel(x_prompt, x_sample, cache_cmp_kv, cache_sel_kv, cache_win_kv, state_conv, state_wkv, state_shift,
           page_table, w_in_even, conv_w, conv_b, conv_ln_g, conv_ln_b, wk_cmp, wv_cmp, w_out_even, mu_c, w_rkvz,
           w0, w1, w2, a0, a1, a2, k_k, k_a, r_k, gn_g, gn_b, w_out_odd, ln_g, ln_b):
    depth = ln_g.shape[0]
    alpha = (2 * depth) ** 0.25
    bp, _, d = x_prompt.shape
    n_heads = d // HD_C
    xp, xs = x_prompt, x_sample
    even_p, even_s, odd_p, odd_s = [], [], [], []
    for l in range(depth):
        if l % 2 == 0:
            e = l // 2
            xp, xs, o_p, o_s = _even_layer(
                xp, xs, cache_cmp_kv[e], cache_sel_kv[e], cache_win_kv[e], state_conv[e], page_table,
                w_in_even[e], conv_w[e], conv_b[e], conv_ln_g[e], conv_ln_b[e], wk_cmp[e], wv_cmp[e],
                w_out_even[e], ln_g[l], ln_b[l], alpha)
            even_p.append(o_p)
            even_s.append(o_s)
        else:
            o = l // 2
            po = (mu_c[o], w_rkvz[o], w0[o], w1[o], w2[o], a0[o], a1[o], a2[o], k_k[o], k_a[o], r_k[o],
                  gn_g[o], gn_b[o], w_out_odd[o].astype(BF16), ln_g[l], ln_b[l], alpha)
            xp, s_p, h_p = _odd_group(xp, jnp.zeros((bp, d), F32), jnp.zeros((bp, n_heads, HD_C, HD_C), F32), *po)
            xs, s_s, h_s = _odd_group(xs, state_shift[o], state_wkv[o], *po)
            odd_p.append((s_p, h_p))
            odd_s.append((s_s, h_s))
    stack = lambda items, j: jnp.stack([it[j] for it in items])
    return (xp, xs,
            stack(even_p, 0), stack(even_s, 0), stack(even_p, 1), stack(even_s, 1),
            stack(even_p, 2), stack(even_s, 2), stack(even_p, 3), stack(even_s, 3),
            stack(odd_p, 0), stack(odd_s, 0), stack(odd_p, 1), stack(odd_s, 1))
```

```python
import functools

import jax
import jax.numpy as jnp
from jax import lax
from jax.experimental import pallas as pl
from jax.experimental.pallas import tpu as pltpu

F32 = jnp.float32
BF16 = jnp.bfloat16

H_B = 8
L_SEL = 64
K_SEL = 16
WINDOW = 512
FORCE = 1e4
HD_C = 64
GN_EPS = 64e-5
LN_EPS = 1e-5
NEG = -1e30

LANES = 128
SUBLANES = 8
VMEM_LIMIT = 56 * 1024 * 1024

NT = (((1,), (1,)), ((), ()))
TN = (((0,), (0,)), ((), ()))
NN = (((1,), (0,)), ((), ()))


def _cparams(n_axes):
    return pltpu.CompilerParams(dimension_semantics=("arbitrary",) * n_axes,
                                vmem_limit_bytes=VMEM_LIMIT)


def _sigmoid(x):
    return 1.0 / (1.0 + jnp.exp(-x))


def _silu(x):
    return x * _sigmoid(x)


def _split_bf16(x, n):
    parts, rem = [], x
    for i in range(n):
        p = rem.astype(BF16)
        parts.append(p)
        if i + 1 < n:
            rem = rem - p.astype(F32)
    return parts


def _mm(a, b, dims=NN, pa=1, pb=1):
    aa = [a] if a.dtype == BF16 else _split_bf16(a, pa)
    bb = [b] if b.dtype == BF16 else _split_bf16(b, pb)
    keep = max(len(aa), len(bb))
    out = None
    for i, ai in enumerate(aa):
        for j, bj in enumerate(bb):
            if i + j < keep:
                t = lax.dot_general(ai, bj, dims, preferred_element_type=F32)
                out = t if out is None else out + t
    return out


def _layer_norm_rows(h, g, b, eps):
    mu = jnp.mean(h, axis=-1, keepdims=True)
    d = h - mu
    var = jnp.mean(d * d, axis=-1, keepdims=True)
    return d * lax.rsqrt(var + eps) * g + b


def _even_in_proj_kernel(x_ref, w_ref, u_ref, sza_ref, q_ref, cmp_ref, sel_ref, win_ref, szb_ref, gate_ref,
                         *t_refs, d_a, d_b, d_kv, q_scale):
    xb = x_ref[...].astype(BF16)

    def seg(lo, n):
        return jnp.dot(xb, w_ref[:, lo:lo + n], preferred_element_type=F32)

    o = 0
    a_val = seg(o, d_a); o += d_a
    a_glu = seg(o, d_a); o += d_a
    u_ref[...] = a_val * _sigmoid(a_glu)
    sza_ref[...] = _silu(seg(o, d_a)); o += d_a
    q_ref[...] = (seg(o, d_b) * q_scale).astype(BF16); o += d_b
    for j, ref in enumerate((cmp_ref, sel_ref, win_ref)):
        kv = seg(o, d_kv); o += d_kv
        ref[...] = kv
        if t_refs:
            t_refs[j][0] = kv.T
    szb_ref[...] = _silu(seg(o, d_b)); o += d_b
    gate_ref[...] = _sigmoid(seg(o, gate_ref.shape[-1]))


def _prep_w_in(w, d_a, d_b, kv_h, hd):
    grp = H_B // kv_h
    c_kv6 = 3 * d_a + d_b
    c_g3 = c_kv6 + 6 * kv_h * hd
    c_zb = c_g3 + 3 * H_B
    gate_blocks = []
    for g in range(kv_h):
        blk = w[:, c_g3 + g * grp * 3: c_g3 + (g + 1) * grp * 3]
        gate_blocks.append(jnp.pad(blk, ((0, 0), (0, LANES - grp * 3))))
    wn = jnp.concatenate([w[:, :c_g3], w[:, c_zb:c_zb + d_b]] + gate_blocks, axis=1)
    return wn.astype(BF16)


def _even_in_proj(x2, w_bf, d_a, d_b, d_kv, kv_h, hd, t_len):
    m, d = x2.shape
    tm = min(512, m)
    assert m % tm == 0
    n_gate = kv_h * LANES
    row = lambda n: pl.BlockSpec((tm, n), lambda i: (i, 0))
    out_shape = (jax.ShapeDtypeStruct((m, d_a), F32), jax.ShapeDtypeStruct((m, d_a), F32),
                 jax.ShapeDtypeStruct((m, d_b), BF16),
                 jax.ShapeDtypeStruct((m, d_kv), F32), jax.ShapeDtypeStruct((m, d_kv), F32),
                 jax.ShapeDtypeStruct((m, d_kv), F32),
                 jax.ShapeDtypeStruct((m, d_b), F32), jax.ShapeDtypeStruct((m, n_gate), F32))
    out_specs = (row(d_a), row(d_a), row(d_b), row(d_kv), row(d_kv), row(d_kv), row(d_b), row(n_gate))
    if t_len % tm == 0 and tm % LANES == 0:
        per_seq = t_len // tm
        t_spec = pl.BlockSpec((1, d_kv, tm), lambda i: (i // per_seq, 0, i % per_seq))
        out_shape += (jax.ShapeDtypeStruct((m // t_len, d_kv, t_len), F32),) * 3
        out_specs += (t_spec,) * 3
    return pl.pallas_call(
        functools.partial(_even_in_proj_kernel, d_a=d_a, d_b=d_b, d_kv=d_kv, q_scale=hd ** -0.5),
        grid=(m // tm,),
        in_specs=[row(d), pl.BlockSpec(w_bf.shape, lambda i: (0, 0), pipeline_mode=pl.Buffered(1))],
        out_specs=out_specs,
        out_shape=out_shape,
        compiler_params=_cparams(1),
        name="even_in_proj",
    )(x2, w_bf)


CONV_HIST = 32


def _conv_kernel(hist_ref, u_ref, sza_ref, w_ref, cb_ref, g_ref, b_ref, ya_ref, win_ref, sh_ref, conv_ref,
                 *, tq, rb, n_taps):
    i = pl.program_id(1)
    base = pl.multiple_of(i * tq, SUBLANES)
    win_ref[CONV_HIST:, :] = u_ref[0, pl.ds(base, tq), :]

    @pl.when(i == 0)
    def _():
        win_ref[:CONV_HIST, :] = hist_ref[0]

    @pl.when(i > 0)
    def _():
        win_ref[:CONV_HIST, :] = u_ref[0, pl.ds(base - CONV_HIST, CONV_HIST), :]

    w = win_ref[...]
    n = tq + CONV_HIST
    sh_ref[0] = w
    for s in range(1, SUBLANES):
        sh_ref[s] = pltpu.roll(w, n - s, 0)

    first = CONV_HIST - (n_taps - 1)
    cb = cb_ref[...]
    g = g_ref[...]
    b = b_ref[...]

    def block(k, carry):
        r0 = pl.multiple_of(k * rb, SUBLANES)
        acc = jnp.zeros((rb, u_ref.shape[-1]), F32) + cb
        for j in range(n_taps):
            a, s = divmod(first + j, SUBLANES)
            acc = acc + sh_ref[s, pl.ds(r0 + SUBLANES * a, rb), :] * w_ref[j:j + 1, :]
        conv_ref[pl.ds(r0, rb), :] = acc
        return carry

    lax.fori_loop(0, tq // rb, block, 0)
    y = _silu(_layer_norm_rows(conv_ref[...], g, b, LN_EPS)) * sza_ref[0]
    ya_ref[0] = y.astype(ya_ref.dtype)


def _conv_branch(hist, u3, sza3, conv_w, conv_b, ln_g, ln_b):
    bsz, t, d = u3.shape
    n_taps = conv_w.shape[0]
    assert n_taps - 1 <= CONV_HIST and t % SUBLANES == 0
    tq = min(512, t)
    rb = min(32, tq)
    assert t % tq == 0 and tq % rb == 0
    vec = lambda a: a.reshape(1, d)
    full = lambda shp: pl.BlockSpec(shp, lambda b, i: (0,) * len(shp))
    return pl.pallas_call(
        functools.partial(_conv_kernel, tq=tq, rb=rb, n_taps=n_taps),
        grid=(bsz, t // tq),
        in_specs=[pl.BlockSpec((1, CONV_HIST, d), lambda b, i: (b, 0, 0)),
                  pl.BlockSpec((1, t, d), lambda b, i: (b, 0, 0)),
                  pl.BlockSpec((1, tq, d), lambda b, i: (b, i, 0)),
                  full((n_taps, d)), full((1, d)), full((1, d)), full((1, d))],
        out_specs=pl.BlockSpec((1, tq, d), lambda b, i: (b, i, 0)),
        out_shape=jax.ShapeDtypeStruct((bsz, t, d), BF16),
        scratch_shapes=[pltpu.VMEM((tq + CONV_HIST, d), F32),
                        pltpu.VMEM((SUBLANES, tq + CONV_HIST, d), F32), pltpu.VMEM((tq, d), F32)],
        compiler_params=_cparams(2),
        name="conv_branch",
    )(hist, u3, sza3, conv_w, vec(conv_b), vec(ln_g), vec(ln_b))


SEL_OFF = -(2.0 ** 100)


def _nsa_prompt_kernel(q_ref, kcmp_ref, ksel_ref, kwin_ref, gate_ref, szb_ref, wpool_ref, o_ref,
                       kc_s, vc_s, ks_s, vs_s, kw_s, vw_s, m_s, l_s, acc_s, s_scr,
                       *, tq, t_len, l_cmp, hd, grp, kv_h):
    i = pl.program_id(1)
    n_cmp = t_len // l_cmp
    n_sel = t_len // L_SEL
    half = n_cmp // 2
    gw = kv_h * hd
    heads = range(kv_h * grp)
    slopes = [2.0 ** -(h + 1) for h in heads]

    @pl.when(i == 0)
    def _prep():
        x3 = kcmp_ref[0].reshape(half, 2 * l_cmp, 2 * gw)
        pooled = jnp.concatenate([jnp.sum(x3 * wpool_ref[0][None], axis=1),
                                  jnp.sum(x3 * wpool_ref[1][None], axis=1)], axis=0)
        pos = lax.broadcasted_iota(jnp.int32, (t_len, LANES), 0)
        lane = lax.broadcasted_iota(jnp.int32, (t_len, LANES), 1)
        blk = pos // L_SEL
        feat = jnp.where(lane == blk, 1.0, 0.0)
        feat = jnp.where(lane == n_sel, blk.astype(F32), feat)
        feat = jnp.where(lane == n_sel + 1, (pos - blk * L_SEL).astype(F32), feat)
        for src, dst in ((ksel_ref, ks_s), (kwin_ref, kw_s)):
            k01 = src[0, :, :gw]
            dst[0] = jnp.where(lane < hd, feat, pltpu.roll(k01, hd, 1)).astype(BF16)
            dst[1] = jnp.where(lane < hd, feat, k01).astype(BF16)
        vc_pad = jnp.concatenate([pooled[:, gw:], jnp.zeros((LANES - n_cmp, gw), F32)], axis=0) \
            if n_cmp < LANES else pooled[:, gw:]
        vc_t = vc_pad.T
        vs_t = ksel_ref[0, :, gw:].T
        vw_t = kwin_ref[0, :, gw:].T
        for gg in range(kv_h):
            kc_s[gg] = pooled[:, gg * hd:(gg + 1) * hd].astype(BF16)
            vc_s[gg] = vc_t[gg * hd:(gg + 1) * hd, :n_cmp].astype(BF16)
            vs_s[gg] = vs_t[gg * hd:(gg + 1) * hd].astype(BF16)
            vw_s[gg] = vw_t[gg * hd:(gg + 1) * hd].astype(BF16)

    t0 = i * tq
    qt = q_ref[0]

    n_io = lax.broadcasted_iota(jnp.int32, (n_cmp, tq), 0)
    t_io = lax.broadcasted_iota(jnp.int32, (n_cmp, tq), 1) + t0
    c_end = jnp.where(n_io < half, 2 * l_cmp * n_io + (l_cmp - 1), 2 * l_cmp * (n_io - half) + (2 * l_cmp - 1))
    dist_c = t_io - c_end
    valid_c = dist_c >= 0
    dist_cf = dist_c.astype(F32)
    imp = [jnp.zeros((n_sel, tq), F32) for _ in range(kv_h)]
    o_c = []
    for g in range(kv_h):
        qs = jnp.concatenate([qt[:, h * hd:(h + 1) * hd] for h in range(g * grp, (g + 1) * grp)], axis=0)
        s_t = lax.dot_general(kc_s[g], qs, NT, preferred_element_type=F32)
        for r in range(grp):
            s = jnp.where(valid_c, s_t[:, r * tq:(r + 1) * tq] - slopes[g * grp + r] * dist_cf, NEG)
            e = jnp.exp(s - jnp.max(s, axis=0, keepdims=True))
            p = jnp.where(valid_c, e / jnp.sum(e, axis=0, keepdims=True), 0.0)
            imp[g] = imp[g] + (p[:half] + p[half:])
            o_c.append(jnp.dot(vc_s[g], p.astype(BF16), preferred_element_type=F32))

    sb = lax.broadcasted_iota(jnp.int32, (n_sel, tq), 0)
    tb = (lax.broadcasted_iota(jnp.int32, (n_sel, tq), 1) + t0) // L_SEL
    forced = (sb == 0) | (sb == tb) | (sb == tb - 1)

    def select():
        feats = []
        for g in range(kv_h):
            val = jnp.where(sb <= tb, jnp.where(forced, FORCE, imp[g]), -jnp.inf)
            rank = jnp.zeros((n_sel, tq), F32)
            for j in range(n_sel):
                vj = val[j:j + 1, :]
                beats = jnp.where(val > vj, 1.0, jnp.where((val == vj) & (sb < j), 1.0, 0.0))
                rank = jnp.where(sb == j, jnp.sum(beats, axis=0, keepdims=True), rank)
            off_t = jnp.where(rank < min(K_SEL, n_sel), 0.0, SEL_OFF)
            feats.append(jnp.concatenate([off_t, jnp.zeros((LANES - n_sel, tq), F32)], axis=0).T)
        return feats

    qf = qt.astype(F32)
    lane_q = lax.broadcasted_iota(jnp.int32, (tq, LANES), 1)

    def q_aug(h, feat):
        tile = qf[:, (h // 2) * LANES:(h // 2 + 1) * LANES]
        if h % 2 == 0:
            tile = pltpu.roll(tile, hd, 1)
        x = jnp.where(lane_q < hd, feat, tile)
        x = jnp.where(lane_q == n_sel, slopes[h] * L_SEL, x)
        return jnp.where(lane_q == n_sel + 1, slopes[h], x).astype(BF16)

    d0 = (lax.broadcasted_iota(jnp.int32, (tq, tq), 1) - lax.broadcasted_iota(jnp.int32, (tq, tq), 0))

    def reset():
        m_s[...] = jnp.full(m_s.shape, NEG, F32)
        l_s[...] = jnp.zeros(l_s.shape, F32)
        acc_s[...] = jnp.zeros(acc_s.shape, F32)

    def scores(q_list, k_s, kt):
        k0 = pl.multiple_of(kt * tq, tq)
        kk = [k_s[g, pl.ds(k0, tq), :] for g in range(kv_h)]
        return [lax.dot_general(kk[r // grp], q_list[r], NT, preferred_element_type=F32) for r in heads]

    def update(s, v_s, kt, mask):
        k0 = pl.multiple_of(kt * tq, tq)
        vts = [v_s[g, :, pl.ds(k0, tq)] for g in range(kv_h)]
        if mask is not None:
            s = [jnp.where(mask, s[r], NEG) for r in heads]
        m_old = [m_s[r] for r in heads]
        m_new = [jnp.maximum(m_old[r], jnp.max(s[r], axis=0, keepdims=True)) for r in heads]
        al = [jnp.exp(m_old[r] - m_new[r]) for r in heads]
        p = [jnp.exp(s[r] - m_new[r]) for r in heads]
        for r in heads:
            m_s[r] = m_new[r]
            l_s[r] = al[r] * l_s[r] + jnp.sum(p[r], axis=0, keepdims=True)
            acc_s[r] = al[r] * acc_s[r] + jnp.dot(vts[r // grp], p[r].astype(BF16),
                                                  preferred_element_type=F32)

    def attend(q_list, k_s, v_s, lo, interior_mask, diag_mask, overlap=None):
        reset()

        def put(slot, kt):
            s = scores(q_list, k_s, kt)
            for r in heads:
                s_scr[slot, r] = s[r]

        def take(slot):
            return [s_scr[slot, r] for r in heads]

        odd = (i - lo) % 2
        lo2 = lo + odd
        put(0, lo2)
        extra = overlap() if overlap is not None else None

        def body(j, carry):
            kt = lo2 + 2 * j
            put(1, kt + 1)
            update(take(0), v_s, kt, interior_mask(kt))
            put(0, kt + 2)
            update(take(1), v_s, kt + 1, interior_mask(kt + 1))
            return carry

        lax.fori_loop(0, (i - lo2) // 2, body, 0)
        update(take(0), v_s, i, diag_mask)

        @pl.when(odd == 1)
        def _():
            update(scores(q_list, k_s, lo), v_s, lo, interior_mask(lo))

        return [acc_s[r] / l_s[r] for r in heads], extra

    w_tiles = -(-(WINDOW - 1) // tq)
    o_w, feat_sel = attend([q_aug(r, 0.0) for r in heads], kw_s, vw_s, jnp.maximum(i - w_tiles, 0),
                           lambda kt: d0 + (i - kt) * tq < WINDOW, (d0 >= 0) & (d0 < WINDOW), overlap=select)

    o_s, _ = attend([q_aug(r, feat_sel[r // grp]) for r in heads], ks_s, vs_s, 0, lambda kt: None, d0 >= 0)

    gate_t = gate_ref[0].T
    outs = []
    for h in heads:
        row = (h // grp) * LANES + 3 * (h % grp)
        outs.append(gate_t[row:row + 1] * o_c[h]
                    + gate_t[row + 1:row + 2] * o_s[h]
                    + gate_t[row + 2:row + 3] * o_w[h])
    o_ref[0] = (jnp.concatenate(outs, axis=0).T * szb_ref[0]).astype(o_ref.dtype)


def _pool_weights(wk, wv, hd, halves):
    l_cmp, kv_h = wk.shape
    row = jnp.concatenate([jnp.repeat(wk, hd, axis=1), jnp.repeat(wv, hd, axis=1)], axis=1)
    if not halves:
        return row
    z = jnp.zeros_like(row)
    return jnp.stack([jnp.concatenate([row, z], axis=0), jnp.concatenate([z, row], axis=0)])


def _nsa_prompt(q3, cmp3, sel3, win3, gate3, szb3, wk, wv):
    bsz, t, d_b = q3.shape
    l_cmp, kv_h = wk.shape
    grp = H_B // kv_h
    hd = d_b // H_B
    gw = kv_h * hd
    tq = min(256, t)
    assert t % tq == 0 and t % (2 * l_cmp) == 0 and L_SEL == 2 * l_cmp and tq % L_SEL == 0
    n_cmp = t // l_cmp
    n_sel = t // L_SEL
    assert gw == LANES and 2 * hd == LANES and n_sel + 2 <= hd and n_sel % SUBLANES == 0 and n_cmp <= LANES
    wpool = _pool_weights(wk, wv, hd, halves=True)
    kv_spec = pl.BlockSpec((1, t, 2 * gw), lambda b, i: (b, 0, 0))
    tile = lambda n: pl.BlockSpec((1, tq, n), lambda b, i: (b, i, 0))
    return pl.pallas_call(
        functools.partial(_nsa_prompt_kernel, tq=tq, t_len=t, l_cmp=l_cmp, hd=hd, grp=grp, kv_h=kv_h),
        grid=(bsz, t // tq),
        in_specs=[tile(d_b), kv_spec, kv_spec, kv_spec, tile(kv_h * LANES), tile(d_b),
                  pl.BlockSpec(wpool.shape, lambda b, i: (0, 0, 0))],
        out_specs=tile(d_b),
        out_shape=jax.ShapeDtypeStruct((bsz, t, d_b), BF16),
        scratch_shapes=[pltpu.VMEM((kv_h, n_cmp, hd), BF16), pltpu.VMEM((kv_h, hd, n_cmp), BF16),
                        pltpu.VMEM((kv_h, t, LANES), BF16), pltpu.VMEM((kv_h, hd, t), BF16),
                        pltpu.VMEM((kv_h, t, LANES), BF16), pltpu.VMEM((kv_h, hd, t), BF16),
                        pltpu.VMEM((H_B, 1, tq), F32), pltpu.VMEM((H_B, 1, tq), F32),
                        pltpu.VMEM((H_B, hd, tq), F32), pltpu.VMEM((2, H_B, tq, tq), F32)],
        compiler_params=_cparams(2),
        name="nsa_prompt",
    )(q3, cmp3, sel3, win3, gate3, szb3, wpool)


def _decode_slopes():
    head = lax.broadcasted_iota(jnp.int32, (H_B, 1), 0)
    slope = jnp.zeros((H_B, 1), F32)
    for hh in range(H_B):
        slope = jnp.where(head == hh, 2.0 ** -(hh + 1), slope)
    return head, slope


def _nsa_decode_cmp_kernel(pt_ref, q_ref, *refs, pg, n_pages, page, l_cmp, hd, grp, kv_h, n_slot):
    del pt_ref
    cmp_refs = refs[:pg]
    wrow_ref, seg_ref, oc_ref, idx_ref, kcv_s = refs[pg:]
    p = pl.program_id(1)
    n_chunks = n_pages // pg
    past = n_pages * page
    n_cmp = past // l_cmp
    n_sel = past // L_SEL
    gw = kv_h * hd
    nc_pad = idx_ref.shape[-1]
    prod = jnp.concatenate([cmp_refs[j][0] * wrow_ref[...] for j in range(pg)], axis=1)
    kcv_s[p] = _mm(prod, seg_ref[...], NN)

    @pl.when(p == n_chunks - 1)
    def _compressed():
        head, slope = _decode_slopes()
        q = q_ref[0].astype(BF16)
        n_io = lax.broadcasted_iota(jnp.int32, (1, n_cmp), 1)
        dist = (past - (l_cmp * n_io + (l_cmp - 1))).astype(F32)
        pair = jnp.where(lax.broadcasted_iota(jnp.int32, (n_cmp, nc_pad), 0) // (L_SEL // l_cmp)
                         == lax.broadcasted_iota(jnp.int32, (n_cmp, nc_pad), 1), 1.0, 0.0).astype(BF16)
        c_io = lax.broadcasted_iota(jnp.int32, (1, nc_pad), 1)
        i_r = lax.broadcasted_iota(jnp.int32, (nc_pad, nc_pad), 1)
        i_c = lax.broadcasted_iota(jnp.int32, (nc_pad, nc_pad), 0)
        before = jnp.where(i_c < i_r, 1.0, 0.0).astype(BF16)
        slot = lax.broadcasted_iota(jnp.int32, (n_slot, nc_pad), 0).astype(F32)
        for g in range(kv_h):
            kc_t = jnp.concatenate([kcv_s[c, g * hd:(g + 1) * hd, :] for c in range(n_chunks)], axis=1)
            vc_t = jnp.concatenate([kcv_s[c, gw + g * hd: gw + (g + 1) * hd, :] for c in range(n_chunks)], axis=1)
            s = jnp.dot(q, kc_t.astype(BF16), preferred_element_type=F32) - slope * dist
            e = jnp.exp(s - jnp.max(s, axis=-1, keepdims=True))
            pc = e / jnp.sum(e, axis=-1, keepdims=True)
            oc_ref[0, g] = lax.dot_general(pc.astype(BF16), vc_t.astype(BF16), NT, preferred_element_type=F32)
            in_grp = (head >= g * grp) & (head < (g + 1) * grp)
            imp = _mm(jnp.sum(jnp.where(in_grp, pc, 0.0), axis=0, keepdims=True), pair, NN, pa=3)
            forced = (c_io == 0) | (c_io == n_sel - 1)
            imp = jnp.where(forced, FORCE, imp)
            imp = jnp.where(c_io < n_sel, imp, -jnp.inf)
            v_r = jnp.broadcast_to(imp, (nc_pad, nc_pad))
            v_c = v_r.T
            beats = jnp.where(v_c > v_r, 1.0, jnp.where((v_c == v_r) & (i_c < i_r), 1.0, 0.0))
            rank = jnp.sum(beats, axis=0, keepdims=True) + jnp.where(imp < FORCE, 1.0, 0.0)
            chosen = jnp.where((rank < min(K_SEL, n_sel + 1)) & (c_io < n_sel), 1.0, 0.0)
            n_before = jnp.dot(jnp.broadcast_to(chosen, (SUBLANES, nc_pad)).astype(BF16), before,
                               preferred_element_type=F32)[:1]
            hit = jnp.where((n_before == slot) & (chosen > 0.5), 1.0, 0.0)
            ids = jnp.sum(hit * c_io.astype(F32), axis=-1, keepdims=True)
            ids = jnp.where(jnp.sum(hit, axis=-1, keepdims=True) > 0.5, ids, -1.0)
            idx_ref[0, g] = jnp.broadcast_to(ids, (n_slot, nc_pad)).astype(jnp.int32)


def _nsa_decode_sel_kernel(pt_ref, idx_ref, q_ref, *refs, n_slot, n_pages, page, hd, grp, kv_h, w_buf):
    del pt_ref
    n_blk = kv_h * n_slot
    k_refs = refs[:n_blk]
    v_refs = refs[n_blk:2 * n_blk]
    kwin_ref, seln_ref, winn_ref, gate_ref, szb_ref, oc_ref, o_ref = refs[2 * n_blk:]
    b = pl.program_id(0)
    past = n_pages * page
    gw = kv_h * hd
    per_page = page // L_SEL
    head, slope = _decode_slopes()
    qf = q_ref[0]
    q = qf.astype(BF16)
    gate = gate_ref[0]
    lane = lax.broadcasted_iota(jnp.int32, (1, page), 1)
    i_io = lax.broadcasted_iota(jnp.int32, (1, w_buf), 1)
    dist_w = w_buf - i_io
    valid_w = dist_w < WINDOW
    o = jnp.zeros((H_B, hd), F32)
    for g in range(kv_h):
        dists, valids = [], []
        for j in range(n_slot):
            blk = idx_ref[b, g, j]
            safe = jnp.maximum(blk, 0)
            dists.append(past - ((safe // per_page) * page + lane))
            valids.append(lane // L_SEL == jnp.where(blk >= 0, safe % per_page, -1))
        dist = jnp.concatenate(dists, axis=1).astype(F32)
        valid = jnp.concatenate(valids, axis=1)
        k_t = jnp.concatenate([k_refs[g * n_slot + j][0] for j in range(n_slot)], axis=1).astype(BF16)
        v_t = jnp.concatenate([v_refs[g * n_slot + j][0] for j in range(n_slot)], axis=1).astype(BF16)
        s = jnp.where(valid, jnp.dot(q, k_t, preferred_element_type=F32) - slope * dist, NEG)
        kn = seln_ref[0][:, g * hd:(g + 1) * hd]
        vn = seln_ref[0][:, gw + g * hd: gw + (g + 1) * hd]
        s_n = jnp.sum(qf * kn, axis=-1, keepdims=True)
        m = jnp.maximum(jnp.max(s, axis=-1, keepdims=True), s_n)
        pe = jnp.where(valid, jnp.exp(s - m), 0.0)
        pn = jnp.exp(s_n - m)
        o_s = ((lax.dot_general(pe.astype(BF16), v_t, NT, preferred_element_type=F32) + pn * vn)
               / (jnp.sum(pe, axis=-1, keepdims=True) + pn))
        kw_t = kwin_ref[0, g * hd:(g + 1) * hd, :].astype(BF16)
        vw_t = kwin_ref[0, gw + g * hd: gw + (g + 1) * hd, :].astype(BF16)
        s = jnp.dot(q, kw_t, preferred_element_type=F32) - slope * dist_w.astype(F32)
        s = jnp.where(valid_w, s, NEG)
        kwn = winn_ref[0][:, g * hd:(g + 1) * hd]
        vwn = winn_ref[0][:, gw + g * hd: gw + (g + 1) * hd]
        s_wn = jnp.sum(qf * kwn, axis=-1, keepdims=True)
        mw = jnp.maximum(jnp.max(s, axis=-1, keepdims=True), s_wn)
        pw = jnp.where(valid_w, jnp.exp(s - mw), 0.0)
        pwn = jnp.exp(s_wn - mw)
        o_w = ((lax.dot_general(pw.astype(BF16), vw_t, NT, preferred_element_type=F32) + pwn * vwn)
               / (jnp.sum(pw, axis=-1, keepdims=True) + pwn))
        gcol = []
        for c in range(3):
            col = jnp.zeros((H_B, 1), F32)
            for r in range(grp):
                ln = g * LANES + 3 * r + c
                col = jnp.where(head == g * grp + r, gate[:, ln:ln + 1], col)
            gcol.append(col)
        og = gcol[0] * oc_ref[0, g] + gcol[1] * o_s + gcol[2] * o_w
        o = jnp.where((head >= g * grp) & (head < (g + 1) * grp), og, o)
    o_ref[0] = o * szb_ref[0]


def _nsa_decode2(q3, cache_cmp_t, cache_sel_t, cache_win_t, page_table, sel_new, win_new, gate, szb3, wk, wv):
    bs, _, hd = q3.shape
    l_cmp, kv_h = wk.shape
    grp = H_B // kv_h
    gw = kv_h * hd
    page = cache_cmp_t.shape[-1]
    n_pages = page_table.shape[1]
    w_buf = cache_win_t.shape[-1]
    per_page = page // l_cmp
    pg = LANES // per_page
    assert page == LANES and page % L_SEL == 0 and page % l_cmp == 0 and n_pages % pg == 0
    n_chunks = n_pages // pg
    n_sel = n_pages * page // L_SEL
    n_slot = min(K_SEL, n_sel + 1) - 1
    assert n_slot >= 1
    nc_pad = -(-n_sel // LANES) * LANES
    wrow = jnp.tile(_pool_weights(wk, wv, hd, halves=False), (per_page, 1)).T
    pos = lax.broadcasted_iota(jnp.int32, (pg, page, LANES), 1)
    pj = lax.broadcasted_iota(jnp.int32, (pg, page, LANES), 0)
    ln = lax.broadcasted_iota(jnp.int32, (pg, page, LANES), 2)
    seg = (ln == pj * per_page + pos // l_cmp).astype(BF16).reshape(pg * page, LANES)

    per_b = lambda shp: pl.BlockSpec((1,) + shp, lambda b, p, pt: (b,) + (0,) * len(shp))
    oc, idx = pl.pallas_call(
        functools.partial(_nsa_decode_cmp_kernel, pg=pg, n_pages=n_pages, page=page, l_cmp=l_cmp, hd=hd, grp=grp,
                          kv_h=kv_h, n_slot=n_slot),
        grid_spec=pltpu.PrefetchScalarGridSpec(
            num_scalar_prefetch=1,
            grid=(bs, n_chunks),
            in_specs=([per_b((H_B, hd))]
                      + [pl.BlockSpec((1, 2 * gw, page), (lambda j: lambda b, p, pt: (pt[b, p * pg + j], 0, 0))(j))
                         for j in range(pg)]
                      + [pl.BlockSpec((2 * gw, page), lambda b, p, pt: (0, 0)),
                         pl.BlockSpec(seg.shape, lambda b, p, pt: (0, 0))]),
            out_specs=(per_b((kv_h, H_B, hd)), per_b((kv_h, n_slot, nc_pad))),
            scratch_shapes=[pltpu.VMEM((n_chunks, 2 * gw, LANES), F32)]),
        out_shape=(jax.ShapeDtypeStruct((bs, kv_h, H_B, hd), F32),
                   jax.ShapeDtypeStruct((bs, kv_h, n_slot, nc_pad), jnp.int32)),
        compiler_params=_cparams(2),
        name="nsa_decode_cmp",
    )(page_table, q3, *([cache_cmp_t] * pg), wrow, seg)
    blk_ids = idx[:, :, :, 0]

    sel_per_page = page // L_SEL

    def blk_map(g, j, row_blk):
        return lambda b, pt, ids: (pt[b, jnp.maximum(ids[b, g, j], 0) // sel_per_page], row_blk, 0)

    per_b2 = lambda shp: pl.BlockSpec((1,) + shp, lambda b, pt, ids: (b,) + (0,) * len(shp))
    slots = [(g, j) for g in range(kv_h) for j in range(n_slot)]
    return pl.pallas_call(
        functools.partial(_nsa_decode_sel_kernel, n_slot=n_slot, n_pages=n_pages, page=page, hd=hd, grp=grp,
                          kv_h=kv_h, w_buf=w_buf),
        grid_spec=pltpu.PrefetchScalarGridSpec(
            num_scalar_prefetch=2,
            grid=(bs,),
            in_specs=([per_b2((H_B, hd))]
                      + [pl.BlockSpec((1, hd, page), blk_map(g, j, g)) for g, j in slots]
                      + [pl.BlockSpec((1, hd, page), blk_map(g, j, kv_h + g)) for g, j in slots]
                      + [per_b2((2 * gw, w_buf)), per_b2((1, 2 * gw)), per_b2((1, 2 * gw)),
                         per_b2((1, kv_h * LANES)), per_b2((H_B, hd)), per_b2((kv_h, H_B, hd))]),
            out_specs=per_b2((H_B, hd))),
        out_shape=jax.ShapeDtypeStruct((bs, H_B, hd), F32),
        compiler_params=_cparams(1),
        name="nsa_decode_sel",
    )(page_table, blk_ids, q3, *([cache_sel_t] * (2 * len(slots))), cache_win_t, sel_new, win_new, gate, szb3, oc)


def _out_proj_ln_kernel(*refs, n_in, alpha):
    a_refs, w_refs = refs[:n_in], refs[n_in:2 * n_in]
    x_ref, g_ref, b_ref, o_ref = refs[2 * n_in:]
    y = None
    for a_ref, w_ref in zip(a_refs, w_refs):
        t = jnp.dot(a_ref[...].astype(BF16), w_ref[...], preferred_element_type=F32)
        y = t if y is None else y + t
    o_ref[...] = _layer_norm_rows(alpha * x_ref[...] + y, g_ref[...], b_ref[...], LN_EPS)


def _out_proj_ln(a_list, w_list, x2, ln_g, ln_b, alpha):
    m, d = x2.shape
    tm = min(1024, m)
    assert m % tm == 0
    n_in = len(a_list)
    return pl.pallas_call(
        functools.partial(_out_proj_ln_kernel, n_in=n_in, alpha=alpha),
        grid=(m // tm,),
        in_specs=([pl.BlockSpec((tm, a.shape[1]), lambda i: (i, 0)) for a in a_list]
                  + [pl.BlockSpec(w.shape, lambda i: (0, 0), pipeline_mode=pl.Buffered(1)) for w in w_list]
                  + [pl.BlockSpec((tm, d), lambda i: (i, 0)),
                     pl.BlockSpec((1, d), lambda i: (0, 0)), pl.BlockSpec((1, d), lambda i: (0, 0))]),
        out_specs=pl.BlockSpec((tm, d), lambda i: (i, 0)),
        out_shape=jax.ShapeDtypeStruct((m, d), F32),
        compiler_params=_cparams(1),
        name="out_proj_ln",
    )(*a_list, *w_list, x2, ln_g.reshape(1, d), ln_b.reshape(1, d))


def _rwkv_proj_kernel(*refs, seq, tiles_per_seq):
    if seq:
        x_ref, tail_ref, shift_ref = refs[:3]
        rest = refs[3:]
    else:
        x_ref, shift_ref = refs[:2]
        rest = refs[2:]
    (mu_ref, w_ref, w1_ref, w2_ref, a1_ref, a2_ref, w0_ref, a0_ref,
     r_ref, k_ref, v_ref, lw_ref, a_ref, sz_ref) = rest
    x = x_ref[...]
    if seq:
        i = pl.program_id(0)
        first = jnp.where(i % tiles_per_seq == 0, shift_ref[0], tail_ref[SUBLANES - 1:SUBLANES, :])
        row = lax.broadcasted_iota(jnp.int32, x.shape, 0)
        x_prev = jnp.where(row == 0, first, pltpu.roll(x, 1, 0))
    else:
        x_prev = shift_ref[...]
    dx = x_prev - x
    mix = lambda n: (x + dx * mu_ref[n:n + 1, :]).astype(BF16)
    r_ref[...] = jnp.dot(mix(0), w_ref[0], preferred_element_type=F32)
    k_ref[...] = jnp.dot(mix(1), w_ref[1], preferred_element_type=F32)
    v_ref[...] = jnp.dot(mix(2), w_ref[2], preferred_element_type=F32)
    sz_ref[...] = _silu(jnp.dot(mix(3), w_ref[3], preferred_element_type=F32))
    hw = jnp.tanh(jnp.dot(mix(4), w1_ref[...], preferred_element_type=F32)).astype(BF16)
    y = -(w0_ref[...] + jnp.dot(hw, w2_ref[...], preferred_element_type=F32))
    softplus = jnp.maximum(y, 0.0) + jnp.log(1.0 + jnp.exp(-jnp.abs(y)))
    lw_ref[...] = -jnp.exp(-softplus - 0.5)
    ha = jnp.dot(mix(5), a1_ref[...], preferred_element_type=F32).astype(BF16)
    a_ref[...] = _sigmoid(a0_ref[...] + jnp.dot(ha, a2_ref[...], preferred_element_type=F32))


def _rwkv_proj(x2, shift, t_len, mu, w_rkvz, w0, w1, w2, a0, a1, a2):
    m, d = x2.shape
    seq = t_len > 1
    tm = min(512, t_len) if seq else m
    assert m % tm == 0 and (not seq or (t_len % tm == 0 and tm % SUBLANES == 0))
    tiles_per_seq = t_len // tm if seq else 1
    full = lambda a: pl.BlockSpec(a.shape, lambda i: (0,) * a.ndim, pipeline_mode=pl.Buffered(1))
    row = pl.BlockSpec((tm, d), lambda i: (i, 0))
    if seq:
        blk = tm // SUBLANES
        lead = [x2, x2, shift.reshape(-1, 1, d)]
        lead_specs = [row, pl.BlockSpec((SUBLANES, d), lambda i: (jnp.maximum(i * blk - 1, 0), 0)),
                      pl.BlockSpec((1, 1, d), lambda i: (i // tiles_per_seq, 0, 0))]
    else:
        lead = [x2, shift]
        lead_specs = [row, row]
    ws = [mu, w_rkvz.astype(BF16), w1.astype(BF16), w2.astype(BF16), a1.astype(BF16), a2.astype(BF16),
          w0.reshape(1, d), a0.reshape(1, d)]
    return pl.pallas_call(
        functools.partial(_rwkv_proj_kernel, seq=seq, tiles_per_seq=tiles_per_seq),
        grid=(m // tm,),
        in_specs=lead_specs + [full(a) for a in ws],
        out_specs=(row,) * 6,
        out_shape=(jax.ShapeDtypeStruct((m, d), F32),) * 6,
        compiler_params=_cparams(1),
        name="rwkv_proj",
    )(*lead, *ws)


WKV_CHUNK = 64
WKV_PASSES = 1
WKV_SUB = 8


def _wkv_kernel(r_ref, k_ref, v_ref, lw_ref, a_ref, sz_ref, s0_ref, kk_ref, ka_ref, rk_ref, gg_ref, gb_ref,
                yz_ref, sfin_ref, h_s, *, c_len, n_sub, n_real, n_heads, hd):
    c = pl.program_id(1)
    mm = functools.partial(_mm, pa=WKV_PASSES, pb=WKV_PASSES)
    n_pairs = n_heads // 2
    pw = 2 * hd
    d = n_heads * hd
    pairs = range(n_pairs)
    iota = lambda shape, axis: lax.broadcasted_iota(jnp.int32, shape, axis)
    bd_mask = (iota((pw, pw), 0) // hd) == (iota((pw, pw), 1) // hd)
    bd_ones = jnp.where(bd_mask, 1.0, 0.0).astype(BF16)
    eye_mask = iota((pw, pw), 0) == iota((pw, pw), 1)
    eye_pb = jnp.where(eye_mask, 1.0, 0.0).astype(BF16)
    lane_lo = iota((1, pw), 1) < hd

    @pl.when(c == 0)
    def _():
        z = jnp.zeros((hd, hd), F32)
        for p in pairs:
            h_s[p] = jnp.concatenate([jnp.concatenate([s0_ref[0, 2 * p].T, z], axis=1),
                                      jnp.concatenate([z, s0_ref[0, 2 * p + 1].T], axis=1)], axis=0)

    def segsum(x):
        xb = x.astype(BF16)
        return jnp.concatenate([jnp.dot(xb[:, t * pw:(t + 1) * pw], bd_ones, preferred_element_type=F32)
                                for t in range(d // pw)], axis=1)

    r = r_ref[0]
    k = k_ref[0]
    v = v_ref[0]
    lw = lw_ref[0]
    a = a_ref[0]
    tri = jnp.where(iota((c_len, c_len), 0) >= iota((c_len, c_len), 1), 1.0, 0.0).astype(BF16)
    col2 = iota((c_len, 2 * c_len), 1) % c_len
    low_exc2 = iota((c_len, 2 * c_len), 0) > col2
    eye2 = jnp.where(iota((c_len, 2 * c_len), 0) == col2, 1.0, 0.0)
    low_inc4 = iota((c_len, 4 * c_len), 0) >= iota((c_len, 4 * c_len), 1) % c_len
    cc_mask = (iota((2 * c_len, 2 * c_len), 0) // c_len) == (iota((2 * c_len, 2 * c_len), 1) // c_len)

    def by_head(x):
        return jnp.concatenate([jnp.where(lane_lo, x, 0.0), jnp.where(lane_lo, 0.0, x)], axis=0).astype(BF16)
    kk = k * kk_ref[...]
    kap = kk / jnp.maximum(jnp.sqrt(segsum(kk * kk)), 1e-12)
    ka = kap * a
    kmod = k * (1.0 + (a - 1.0) * ka_ref[...])
    rkr = r * kmod * rk_ref[...]
    levels = max((n_real - 1).bit_length() - 1, 0)

    lhs_b, lhs2_t, vb, v_st, l_b, l_k, m_r = [], [], [], [], [], [], []
    for s in range(n_sub):
        rs = slice(s * c_len, (s + 1) * c_len)
        lw_s = lw[rs]
        cum = _mm(tri, lw_s, NN, pb=3)
        cum_end = cum[c_len - 1:c_len, :]
        p_inv = jnp.exp(-cum)
        p_rest = jnp.exp(cum_end - cum)
        p_end = jnp.exp(cum_end)
        lhs_top = kap[rs] * jnp.exp(cum - lw_s)
        r_t = r[rs] * jnp.exp(cum)
        rhs_top = ka[rs] * p_inv
        k_t = kmod[rs] * p_inv
        ke_top = ka[rs] * p_rest
        k_e = kmod[rs] * p_rest
        for p in pairs:
            ls = slice(p * pw, (p + 1) * pw)
            lhs_b.append(jnp.concatenate([lhs_top[:, ls], r_t[:, ls]], axis=0).astype(BF16))
            v_p = v[rs, ls]
            vb.append(v_p.astype(BF16))
            v_st.append(by_head(v_p))
            lhs2 = jnp.concatenate([ke_top[:, ls], k_e[:, ls], jnp.where(eye_mask, p_end[:, ls], 0.0)],
                                   axis=0).astype(BF16)
            lhs2_t.append(lax.dot_general(eye_pb, lhs2, NT, preferred_element_type=F32).astype(BF16))
            rhs_st = jnp.concatenate([by_head(rhs_top[:, ls]), by_head(k_t[:, ls])], axis=0)
            am = mm(lhs_b[-1], rhs_st, NT)
            l_b.append(jnp.where(low_exc2, am[:c_len, :2 * c_len], 0.0))
            l_k.append(jnp.where(low_exc2, am[:c_len, 2 * c_len:], 0.0).astype(BF16))
            m_r.append(jnp.where(low_inc4, am[c_len:], 0.0).astype(BF16))
    n_sp = n_sub * n_pairs
    lkv = [mm(l_k[i], v_st[i], NN) for i in range(n_sp)]
    blockdiag = lambda x: jnp.where(cc_mask, jnp.concatenate([x, x], axis=0), 0.0).astype(BF16)
    t_inv = [eye2 - x for x in l_b]
    pwr = l_b
    for _ in range(levels):
        pwr = [mm(x, blockdiag(x), NN) for x in pwr]
        t_inv = [t + mm(t, blockdiag(x), NN) for t, x in zip(t_inv, pwr)]

    h_cur = [h_s[p] for p in pairs]
    y_rows = []
    for s in range(n_sub):
        idx = [s * n_pairs + p for p in pairs]
        hb = [h_cur[p].astype(BF16) for p in pairs]
        gh = [mm(lhs_b[i], hb[p], NN) for p, i in zip(pairs, idx)]
        u = [-mm(t_inv[i], by_head(gh[p][:c_len] + lkv[i]), NN) for p, i in zip(pairs, idx)]
        uv = [jnp.concatenate([u[p].astype(BF16), vb[i]], axis=0) for p, i in zip(pairs, idx)]
        y = [gh[p][c_len:] + mm(m_r[i], jnp.concatenate([by_head(u[p]), v_st[i]], axis=0), NN)
             for p, i in zip(pairs, idx)]
        h_cur = [jnp.where(bd_mask, mm(lhs2_t[i], jnp.concatenate([uv[p], hb[p]], axis=0), NN), 0.0)
                 for p, i in zip(pairs, idx)]
        y_rows.append(jnp.concatenate(y, axis=1))
    for p in pairs:
        h_s[p] = h_cur[p]

    y_all = jnp.concatenate(y_rows, axis=0) if n_sub > 1 else y_rows[0]
    dy = y_all - segsum(y_all) * (1.0 / hd)
    var = segsum(dy * dy) * (1.0 / hd)
    yn = dy * lax.rsqrt(var + GN_EPS) * gg_ref[...] + gb_ref[...]
    yz_ref[0] = ((yn + segsum(rkr) * v) * sz_ref[0]).astype(yz_ref.dtype)

    @pl.when(c == pl.num_programs(1) - 1)
    def _():
        for p in pairs:
            sfin_ref[0, 2 * p] = h_s[p, :hd, :hd].T
            sfin_ref[0, 2 * p + 1] = h_s[p, hd:, hd:].T


def _wkv_scan(r3, k3, v3, lw3, a3, sz3, s0, k_k, k_a, r_k, gn_g, gn_b, c_len, t_real):
    bsz, t, d = r3.shape
    n_heads = d // HD_C
    n_sub = WKV_SUB if t % (WKV_SUB * c_len) == 0 else 1
    rows = n_sub * c_len
    assert t % rows == 0 and n_heads % 2 == 0 and 2 * HD_C == LANES
    vec = lambda x: x.reshape(1, d)
    seq = pl.BlockSpec((1, rows, d), lambda b, c: (b, c, 0))
    st = pl.BlockSpec((1, n_heads, HD_C, HD_C), lambda b, c: (b, 0, 0, 0))
    par = pl.BlockSpec((1, d), lambda b, c: (0, 0))
    return pl.pallas_call(
        functools.partial(_wkv_kernel, c_len=c_len, n_sub=n_sub, n_real=min(t_real, c_len), n_heads=n_heads,
                          hd=HD_C),
        grid=(bsz, t // rows),
        in_specs=[seq] * 6 + [st] + [par] * 5,
        out_specs=(seq, st),
        out_shape=(jax.ShapeDtypeStruct((bsz, t, d), BF16),
                   jax.ShapeDtypeStruct((bsz, n_heads, HD_C, HD_C), F32)),
        scratch_shapes=[pltpu.VMEM((n_heads // 2, LANES, LANES), F32)],
        compiler_params=_cparams(2),
        name="wkv_scan",
    )(r3, k3, v3, lw3, a3, sz3, s0, vec(k_k), vec(k_a), vec(r_k), vec(gn_g), vec(gn_b))


def _even_layer(xp, xs, cache_cmp, cache_sel, cache_win, state_conv, page_table,
                w_in, conv_w, conv_b, cln_g, cln_b, wk, wv, w_out, ln_g, ln_b, alpha):
    bp, tp, d = xp.shape
    bs, ts, _ = xs.shape
    assert ts == 1
    d_a = conv_w.shape[-1]
    n_taps = conv_w.shape[0]
    l_cmp, kv_h = wk.shape
    hd = cache_cmp.shape[-1]
    d_b = H_B * hd
    d_kv = 2 * kv_h * hd
    w_bf = _prep_w_in(w_in, d_a, d_b, kv_h, hd)
    cw = conv_w.reshape(n_taps, d_a)
    w_out_bf = w_out.astype(BF16)
    kv_shape = lambda b, t: (b, t, 2, kv_h, hd)

    proj = _even_in_proj(xp.reshape(bp * tp, d), w_bf, d_a, d_b, d_kv, kv_h, hd, tp)
    u, sza, q, cmp_n, sel_n, win_n, szb, gate = proj[:8]
    if len(proj) > 8:
        new_kv = [a.reshape(bp, 2, kv_h, hd, tp).transpose(0, 4, 1, 2, 3) for a in proj[8:]]
    else:
        new_kv = [a.reshape(bp, tp, 2, kv_h, hd) for a in (cmp_n, sel_n, win_n)]
    r3 = lambda a: a.reshape(bp, tp, a.shape[-1])
    u3 = r3(u)
    ya = _conv_branch(jnp.zeros((bp, CONV_HIST, d_a), F32), u3, r3(sza), cw, conv_b, cln_g, cln_b)
    yb = _nsa_prompt(r3(q), r3(cmp_n), r3(sel_n), r3(win_n), r3(gate), r3(szb), wk, wv)
    yp = _out_proj_ln([ya.reshape(bp * tp, d_a), yb.reshape(bp * tp, d_b)], [w_out_bf[:d_a], w_out_bf[d_a:]],
                      xp.reshape(bp * tp, d), ln_g, ln_b, alpha).reshape(bp, tp, d)
    w_keep = min(WINDOW, tp)
    outs_p = (new_kv[0], new_kv[1], new_kv[2][:, tp - w_keep:], u3[:, tp - (n_taps - 1):])

    u, sza, q, cmp_s, sel_s, win_s, szb, gate = _even_in_proj(xs.reshape(bs, d), w_bf, d_a, d_b, d_kv, kv_h, hd, 1)
    ext = jnp.concatenate([state_conv, u[:, None, :]], axis=1)
    hist = jnp.pad(state_conv, ((0, 0), (CONV_HIST - (n_taps - 1), 0), (0, 0)))
    pad_rows = lambda a: jnp.pad(a[:, None, :], ((0, 0), (0, SUBLANES - 1), (0, 0)))
    ya = _conv_branch(hist, pad_rows(u), pad_rows(sza), cw, conv_b, cln_g, cln_b)[:, 0]
    n_pool, page = cache_cmp.shape[:2]
    w_buf = cache_win.shape[1]
    fm = lambda c: jnp.transpose(c, (0, 2, 3, 4, 1)).reshape(c.shape[0], d_kv, c.shape[1])
    yb = _nsa_decode2(q.astype(F32).reshape(bs, H_B, hd), fm(cache_cmp), fm(cache_sel), fm(cache_win), page_table,
                      sel_s[:, None, :], win_s[:, None, :], gate[:, None, :], szb.reshape(bs, H_B, hd), wk, wv)
    ys = _out_proj_ln([ya, yb.reshape(bs, d_b)], [w_out_bf[:d_a], w_out_bf[d_a:]], xs.reshape(bs, d),
                      ln_g, ln_b, alpha).reshape(bs, 1, d)
    ctx = jnp.concatenate([cache_win, win_s.reshape(kv_shape(bs, 1))], axis=1)
    outs_s = (cmp_s.reshape(kv_shape(bs, 1)), sel_s.reshape(kv_shape(bs, 1)),
              ctx[:, ctx.shape[1] - min(WINDOW, ctx.shape[1]):], ext[:, 1:])
    return yp, ys, outs_p, outs_s


def _odd_group(x3, shift, s0, mu, w_rkvz, w0, w1, w2, a0, a1, a2, k_k, k_a, r_k, gn_g, gn_b, w_out_bf,
               ln_g, ln_b, alpha):
    bsz, t, d = x3.shape
    x2 = x3.reshape(bsz * t, d)
    r, k, v, lw, a, sz = _rwkv_proj(x2, shift, t, mu, w_rkvz, w0, w1, w2, a0, a1, a2)
    c_len = WKV_CHUNK if t >= WKV_CHUNK else -(-t // SUBLANES) * SUBLANES
    t_pad = -(-t // c_len) * c_len
    r3 = lambda z: jnp.pad(z.reshape(bsz, t, d), ((0, 0), (0, t_pad - t), (0, 0)))
    yz, s_fin = _wkv_scan(r3(r), r3(k), r3(v), r3(lw), r3(a), r3(sz), s0, k_k, k_a, r_k.reshape(-1), gn_g, gn_b,
                          c_len, t)
    y = _out_proj_ln([yz[:, :t].reshape(bsz * t, d)], [w_out_bf], x2, ln_g, ln_b, alpha).reshape(bsz, t, d)
    return y, s_fin, x3[:, -1]


def kernel(x_prompt, x_sample, cache_cmp_kv, cache_sel_kv, cache_win_kv, state_conv, state_wkv, state_shift,
           page_table, w_in_even, conv_w, conv_b, conv_ln_g, conv_ln_b, wk_cmp, wv_cmp, w_out_even, mu_c, w_rkvz,
           w0, w1, w2, a0, a1, a2, k_k, k_a, r_k, gn_g, gn_b, w_out_odd, ln_g, ln_b):
    depth = ln_g.shape[0]
    alpha = (2 * depth) ** 0.25
    bp, _, d = x_prompt.shape
    n_heads = d // HD_C
    xp, xs = x_prompt, x_sample
    even_p, even_s, odd_p, odd_s = [], [], [], []
    for l in range(depth):
        if l % 2 == 0:
            e = l // 2
            xp, xs, o_p, o_s = _even_layer(
                xp, xs, cache_cmp_kv[e], cache_sel_kv[e], cache_win_kv[e], state_conv[e], page_table,
                w_in_even[e], conv_w[e], conv_b[e], conv_ln_g[e], conv_ln_b[e], wk_cmp[e], wv_cmp[e],
                w_out_even[e], ln_g[l], ln_b[l], alpha)
            even_p.append(o_p)
            even_s.append(o_s)
        else:
            o = l // 2
            po = (mu_c[o], w_rkvz[o], w0[o], w1[o], w2[o], a0[o], a1[o], a2[o], k_k[o], k_a[o], r_k[o],
                  gn_g[o], gn_b[o], w_out_odd[o].astype(BF16), ln_g[l], ln_b[l], alpha)
            xp, s_p, h_p = _odd_group(xp, jnp.zeros((bp, d), F32), jnp.zeros((bp, n_heads, HD_C, HD_C), F32), *po)
            xs, s_s, h_s = _odd_group(xs, state_shift[o], state_wkv[o], *po)
            odd_p.append((s_p, h_p))
            odd_s.append((s_s, h_s))
    stack = lambda items, j: jnp.stack([it[j] for it in items])
    return (xp, xs,
            stack(even_p, 0), stack(even_s, 0), stack(even_p, 1), stack(even_s, 1),
            stack(even_p, 2), stack(even_s, 2), stack(even_p, 3), stack(even_s, 3),
            stack(odd_p, 0), stack(odd_s, 0), stack(odd_p, 1), stack(odd_s, 1))
```
